```python
import jax, jax.numpy as jnp
from jax import lax
import numpy as np

D_MODEL = 1024
BATCH = 8
SEQ = 4096
DEPTH = 2

GRID_W = 64
CTX_LEN = 256
EPS = 1e-6
NEG_INF = -1e30
HEAD_DIM = 64
N_Q_HEADS = 8
N_KV_HEADS = 2
Q_PER_KV = N_Q_HEADS // N_KV_HEADS
ATTN_WIDTH = N_Q_HEADS * HEAD_DIM
KV_WIDTH = N_KV_HEADS * HEAD_DIM
WINDOW = 128
BLOCK = 128
ROPE_BASE = 10000.0
CONV_B_WIDTH = D_MODEL // 2
CONV_B_GROUPS = 8
CONV_B_K = 3
CONV_B_LEFT = 1
EVEN_SPLITS = (ATTN_WIDTH, ATTN_WIDTH + KV_WIDTH, ATTN_WIDTH + 2 * KV_WIDTH,
               ATTN_WIDTH + 2 * KV_WIDTH + CONV_B_WIDTH, ATTN_WIDTH + 2 * KV_WIDTH + 2 * CONV_B_WIDTH)
EVEN_IN = ATTN_WIDTH + 2 * KV_WIDTH + 3 * CONV_B_WIDTH
EVEN_OUT = ATTN_WIDTH + CONV_B_WIDTH
LRU_WIDTH = D_MODEL
LRU_HEADS = 8
LRU_BLOCK = LRU_WIDTH // LRU_HEADS
LRU_C = 8.0
LRU_CONV_K = 4
LRU_CONV_LEFT = 2
N_EXPERTS = 16
CAPACITY_FACTOR = 2
EXPERT_FF = 1408
N_EVEN = (DEPTH + 1) // 2
N_ODD = DEPTH // 2

kernel_name = "hybrid_swa_shortconv_rglru_ecmoe_dit"


def rmsnorm(x, g):
    xf = x.astype(jnp.float32)
    y = xf * lax.rsqrt(jnp.mean(xf * xf, axis=-1, keepdims=True) + EPS)
    return (y * g.astype(jnp.float32)).astype(x.dtype)


def modulate(h, shift, scale):
    return h * (1.0 + scale) + shift


def axial_rope(x, rows, cols):
    dh = x.shape[-1]
    half = dh // 2
    nf = half // 2
    inv = ROPE_BASE ** (-jnp.arange(nf, dtype=jnp.float32) / nf)
    out = []
    for part, pos in ((x[..., :half], rows), (x[..., half:], cols)):
        ang = pos.astype(jnp.float32)[:, None] * inv[None, :]
        cos = jnp.cos(ang)[None, :, None, :]
        sin = jnp.sin(ang)[None, :, None, :]
        p1 = part[..., :nf].astype(jnp.float32)
        p2 = part[..., nf:].astype(jnp.float32)
        out += [p1 * cos - p2 * sin, p2 * cos + p1 * sin]
    return jnp.concatenate(out, axis=-1).astype(x.dtype)


def dwconv(u, w, bias, left):
    k_w = w.shape[0]
    n = u.shape[1]
    up = jnp.pad(u, ((0, 0), (left, k_w - 1 - left), (0, 0)))
    return sum(w[k] * up[:, k:k + n] for k in range(k_w)) + bias


def sink_logits(sink, shape):
    return jnp.broadcast_to(sink.astype(jnp.float32).reshape(1, N_KV_HEADS, Q_PER_KV, *([1] * (len(shape) - 3))), shape)


def windowed_gqa_with_context(q, k, v, kc, vc, sink):
    b, n, hk, g, dh = q.shape
    nb = n // BLOCK
    scale = dh ** -0.5
    qb = q.reshape(b, nb, BLOCK, hk, g, dh)
    pad = ((0, 0), (BLOCK, BLOCK), (0, 0), (0, 0))
    kp = jnp.pad(k, pad).reshape(b, nb + 2, BLOCK, hk, dh)
    vp = jnp.pad(v, pad).reshape(b, nb + 2, BLOCK, hk, dh)
    kb = jnp.concatenate([kp[:, :-2], kp[:, 1:-1], kp[:, 2:]], axis=2)
    vb = jnp.concatenate([vp[:, :-2], vp[:, 1:-1], vp[:, 2:]], axis=2)
    s_loc = jnp.einsum('bnqkgd,bnjkd->bkgnqj', qb, kb).astype(jnp.float32) * scale
    qpos = jnp.arange(nb)[:, None] * BLOCK + jnp.arange(BLOCK)[None, :]
    kpos = jnp.arange(nb)[:, None] * BLOCK - BLOCK + jnp.arange(3 * BLOCK)[None, :]
    diff = kpos[:, None, :] - qpos[:, :, None]
    valid = (jnp.abs(diff) <= WINDOW) & (kpos[:, None, :] >= 0) & (kpos[:, None, :] < n)
    s_loc = jnp.where(valid, s_loc, NEG_INF)
    s_ctx = jnp.einsum('bnqkgd,bjkd->bkgnqj', qb, kc).astype(jnp.float32) * scale
    s_snk = sink_logits(sink, s_loc.shape[:-1] + (1,))
    p = jax.nn.softmax(jnp.concatenate([s_loc, s_ctx, s_snk], axis=-1), axis=-1)
    n_loc = 3 * BLOCK
    n_ctx = kc.shape[1]
    p_loc = p[..., :n_loc].astype(v.dtype)
    p_ctx = p[..., n_loc:n_loc + n_ctx].astype(v.dtype)
    o = (jnp.einsum('bkgnqj,bnjkd->bnqkgd', p_loc, vb)
         + jnp.einsum('bkgnqj,bjkd->bnqkgd', p_ctx, vc))
    return o.reshape(b, n, hk * g * dh)


def context_attention(qc, kc, vc, sink):
    b, L, hk, g, dh = qc.shape
    s = jnp.einsum('bqkgd,bjkd->bkgqj', qc, kc).astype(jnp.float32) * dh ** -0.5
    s_snk = sink_logits(sink, s.shape[:-1] + (1,))
    p = jax.nn.softmax(jnp.concatenate([s, s_snk], axis=-1), axis=-1)
    o = jnp.einsum('bkgqj,bjkd->bqkgd', p[..., :L].astype(vc.dtype), vc)
    return o.reshape(b, L, hk * g * dh)


def gated_short_conv(gb, gc, u, w, bias):
    return gb * dwconv(gc * u, w, bias, CONV_B_LEFT)


def even_mixer(hl, hc, rows, cols, w_in, w_out, sink, conv_w, conv_b, need_ctx_out):
    def project(h):
        b, n = h.shape[:2]
        q, k, v, gb, gc, u = jnp.split(h @ w_in, EVEN_SPLITS, axis=-1)
        return (q.reshape(b, n, N_Q_HEADS, HEAD_DIM), k.reshape(b, n, N_KV_HEADS, HEAD_DIM),
                v.reshape(b, n, N_KV_HEADS, HEAD_DIM), gb, gc, u)
    b, n = hl.shape[:2]
    ql, kl, vl, bl, cl, ul = project(hl)
    qc, kc, vc, bc, cc, uc = project(hc)
    ql = axial_rope(ql, rows, cols).reshape(b, n, N_KV_HEADS, Q_PER_KV, HEAD_DIM)
    kl = axial_rope(kl, rows, cols)
    att_l = windowed_gqa_with_context(ql, kl, vl, kc, vc, sink)
    conv_l = gated_short_conv(bl, cl, ul, conv_w, conv_b)
    yl = jnp.concatenate([att_l, conv_l], axis=-1) @ w_out
    if not need_ctx_out:
        return yl, None
    L = hc.shape[1]
    att_c = context_attention(qc.reshape(b, L, N_KV_HEADS, Q_PER_KV, HEAD_DIM), kc, vc, sink)
    conv_c = gated_short_conv(bc, cc, uc, conv_w, conv_b)
    yc = jnp.concatenate([att_c, conv_c], axis=-1) @ w_out
    return yl, yc


def rglru_coeffs(u, wa, ba, wx, bx, lam):
    b, n, w = u.shape
    ub = u.reshape(b, n, LRU_HEADS, LRU_BLOCK)
    r = jax.nn.sigmoid(jnp.einsum('bnhi,hij->bnhj', ub, wa.astype(jnp.float32)).reshape(b, n, w) + ba.astype(jnp.float32))
    i = jax.nn.sigmoid(jnp.einsum('bnhi,hij->bnhj', ub, wx.astype(jnp.float32)).reshape(b, n, w) + bx.astype(jnp.float32))
    log_a = -LRU_C * r * jax.nn.softplus(-lam.astype(jnp.float32))
    a = jnp.exp(log_a)
    mult = jnp.sqrt(-jnp.expm1(2.0 * log_a))
    return a, mult * (i * u)


def linear_scan(a, b, reverse):
    def combine(e1, e2):
        a1, b1 = e1
        a2, b2 = e2
        return a1 * a2, a2 * b1 + b2
    return lax.associative_scan(combine, (a, b), reverse=reverse, axis=1)[1]


def odd_mixer(hl, hc, w_in, w_out, conv_w, conv_b, wa, ba, wx, bx, lam, need_ctx_out):
    gl, ul = jnp.split(hl @ w_in, 2, axis=-1)
    gc, uc = jnp.split(hc @ w_in, 2, axis=-1)
    ul = dwconv(ul, conv_w, conv_b, LRU_CONV_LEFT).astype(jnp.float32)
    uc = dwconv(uc, conv_w, conv_b, LRU_CONV_LEFT).astype(jnp.float32)
    hs_l, hs_c = [], []
    for d, reverse in ((0, False), (1, True)):
        a_c, b_c = rglru_coeffs(uc, wa[d], ba[d], wx[d], bx[d], lam[d])
        h_c = linear_scan(a_c, b_c, reverse)
        end = 0 if reverse else -1
        h0 = h_c[:, 0] if reverse else h_c[:, -1]
        a_l, b_l = rglru_coeffs(ul, wa[d], ba[d], wx[d], bx[d], lam[d])
        b_l = b_l.at[:, -1 if reverse else 0].add(a_l[:, -1 if reverse else 0] * h0)
        hs_l.append(linear_scan(a_l, b_l, reverse))
        hs_c.append(h_c)
    yl = (jax.nn.gelu(gl) * (hs_l[0] + hs_l[1]).astype(gl.dtype)) @ w_out
    if not need_ctx_out:
        return yl, None
    yc = (jax.nn.gelu(gc) * (hs_c[0] + hs_c[1]).astype(gc.dtype)) @ w_out
    return yl, yc


def expert_choice_ffn(h, router_w, w_gate, w_up, w_down):
    b, n, d = h.shape
    cap = max(1, CAPACITY_FACTOR * n // N_EXPERTS)
    aff = jax.nn.softmax((h @ router_w).astype(jnp.float32), axis=-1)
    g, idx = lax.top_k(jnp.swapaxes(aff, 1, 2), cap)
    flat = (jnp.arange(b)[:, None, None] * n + idx).reshape(-1)
    hf = h.reshape(b * n, d)
    xe = hf[flat].reshape(b, N_EXPERTS, cap, d)
    a = jnp.einsum('becd,edf->becf', xe, w_gate)
    u = jnp.einsum('becd,edf->becf', xe, w_up)
    y = jnp.einsum('becf,efd->becd', jax.nn.silu(a) * u, w_down)
    y = y * g[..., None].astype(y.dtype)
    out = jnp.zeros((b * n, d), y.dtype).at[flat].add(y.reshape(-1, d))
    return out.reshape(b, n, d)


def setup_inputs(seed: int = 0) -> dict:
    key = jax.random.key(seed)
    ks = jax.random.split(key, 32)
    f32 = jnp.float32
    D = D_MODEL

    def nrm(k, shape, s):
        return jax.random.normal(k, shape, f32) * s

    u = jax.random.uniform(ks[22], (N_ODD, 2, LRU_WIDTH), f32, 0.9, 0.999)
    s = u ** (1.0 / LRU_C)
    lam = jnp.log(s) - jnp.log1p(-s)
    return {
        "x": nrm(ks[0], (BATCH, SEQ, D), 1.0),
        "c": nrm(ks[1], (BATCH, D), 1.0),
        "ctx": nrm(ks[2], (BATCH, CTX_LEN, D), 1.0),
        "c_ctx": nrm(ks[3], (D,), 1.0),
        "ada_w": nrm(ks[4], (DEPTH, D, 6 * D), 0.02),
        "ada_b": nrm(ks[5], (DEPTH, 6 * D), 0.01),
        "norm_mix_g": 1.0 + nrm(ks[6], (DEPTH, D), 0.02),
        "norm_ffn_g": 1.0 + nrm(ks[7], (DEPTH, D), 0.02),
        "ev_w_in": nrm(ks[8], (N_EVEN, D, EVEN_IN), D ** -0.5),
        "ev_w_out": nrm(ks[9], (N_EVEN, EVEN_OUT, D), EVEN_OUT ** -0.5),
        "ev_sink": nrm(ks[10], (N_EVEN, N_Q_HEADS), 0.5),
        "ev_conv_w": nrm(ks[11], (N_EVEN, CONV_B_K, CONV_B_WIDTH), CONV_B_K ** -0.5),
        "ev_conv_b": nrm(ks[12], (N_EVEN, CONV_B_WIDTH), 0.01),
        "od_w_in": nrm(ks[13], (N_ODD, D, 2 * LRU_WIDTH), D ** -0.5),
        "od_w_out": nrm(ks[14], (N_ODD, LRU_WIDTH, D), LRU_WIDTH ** -0.5),
        "od_conv_w": nrm(ks[15], (N_ODD, LRU_CONV_K, LRU_WIDTH), LRU_CONV_K ** -0.5),
        "od_conv_b": nrm(ks[16], (N_ODD, LRU_WIDTH), 0.01),
        "od_wa": nrm(ks[17], (N_ODD, 2, LRU_HEADS, LRU_BLOCK, LRU_BLOCK), LRU_BLOCK ** -0.5),
        "od_ba": nrm(ks[18], (N_ODD, 2, LRU_WIDTH), 0.01),
        "od_wx": nrm(ks[19], (N_ODD, 2, LRU_HEADS, LRU_BLOCK, LRU_BLOCK), LRU_BLOCK ** -0.5),
        "od_bx": nrm(ks[20], (N_ODD, 2, LRU_WIDTH), 0.01),
        "od_lambda": lam,
        "router_w": nrm(ks[23], (DEPTH, D, N_EXPERTS), D ** -0.5),
        "w_gate": nrm(ks[24], (DEPTH, N_EXPERTS, D, EXPERT_FF), D ** -0.5),
        "w_up": nrm(ks[25], (DEPTH, N_EXPERTS, D, EXPERT_FF), D ** -0.5),
        "w_down": nrm(ks[26], (DEPTH, N_EXPERTS, EXPERT_FF, D), EXPERT_FF ** -0.5),
        "final_g": 1.0 + nrm(ks[27], (D,), 0.02),
    }


def reference(x, c, ctx, c_ctx, ada_w, ada_b, norm_mix_g, norm_ffn_g, ev_w_in, ev_w_out, ev_sink,
              ev_conv_w, ev_conv_b, od_w_in, od_w_out, od_conv_w, od_conv_b, od_wa, od_ba, od_wx, od_bx,
              od_lambda, router_w, w_gate, w_up, w_down, final_g):
    n = x.shape[1]
    n_rows = n // GRID_W
    rows = jnp.broadcast_to(jnp.arange(n_rows)[:, None], (n_rows, GRID_W)).reshape(-1)
    cols = jnp.broadcast_to(jnp.arange(GRID_W)[None, :], (n_rows, GRID_W)).reshape(-1)
    s_lat = jax.nn.silu(c)
    s_ctx = jax.nn.silu(c_ctx)[None, :]
    xl, xc = x, ctx
    for l in range(DEPTH):
        last = l == DEPTH - 1
        m_l = [m[:, None, :] for m in jnp.split(s_lat @ ada_w[l] + ada_b[l], 6, axis=-1)]
        m_c = [m[:, None, :] for m in jnp.split(s_ctx @ ada_w[l] + ada_b[l], 6, axis=-1)]
        hl = modulate(rmsnorm(xl, norm_mix_g[l]), m_l[0], m_l[1])
        hc = modulate(rmsnorm(xc, norm_mix_g[l]), m_c[0], m_c[1])
        j = l // 2
        if l % 2 == 0:
            yl, yc = even_mixer(hl, hc, rows, cols, ev_w_in[j], ev_w_out[j], ev_sink[j],
                                ev_conv_w[j], ev_conv_b[j], not last)
        else:
            yl, yc = odd_mixer(hl, hc, od_w_in[j], od_w_out[j], od_conv_w[j], od_conv_b[j],
                               od_wa[j], od_ba[j], od_wx[j], od_bx[j], od_lambda[j], not last)
        xl = xl + m_l[2] * yl
        if not last:
            xc = xc + m_c[2] * yc
        hl = modulate(rmsnorm(xl, norm_ffn_g[l]), m_l[3], m_l[4])
        xl = xl + m_l[5] * expert_choice_ffn(hl, router_w[l], w_gate[l], w_up[l], w_down[l])
        if not last:
            hc = modulate(rmsnorm(xc, norm_ffn_g[l]), m_c[3], m_c[4])
            xc = xc + m_c[5] * expert_choice_ffn(hc, router_w[l], w_gate[l], w_up[l], w_down[l])
    return rmsnorm(xl, final_g)
```

```python
import functools

import jax
import jax.numpy as jnp
from jax import lax
from jax.experimental import pallas as pl
from jax.experimental.pallas import tpu as pltpu

F32 = jnp.float32
MXU_DTYPE = jnp.bfloat16

D_MODEL = 1024
GRID_W = 64
EPS = 1e-6
NEG_INF = -1e30
HEAD_DIM = 64
N_Q_HEADS = 8
N_KV_HEADS = 2
Q_PER_KV = N_Q_HEADS // N_KV_HEADS
ATTN_WIDTH = N_Q_HEADS * HEAD_DIM
KV_WIDTH = N_KV_HEADS * HEAD_DIM
WINDOW = 128
BLOCK = 128
ROPE_BASE = 10000.0
CONV_B_WIDTH = D_MODEL // 2
EVEN_IN = ATTN_WIDTH + 2 * KV_WIDTH + 3 * CONV_B_WIDTH
LRU_WIDTH = D_MODEL
LRU_HEADS = 8
LRU_BLOCK = LRU_WIDTH // LRU_HEADS
LRU_C = 8.0
N_EXPERTS = 16
CAPACITY_FACTOR = 2
EXPERT_FF = 1408
MOD_ROWS = 16
HALO = 8
V7X_VMEM_LIMIT = 56 * 1024 * 1024


def _params(*sem):
    return pltpu.CompilerParams(dimension_semantics=sem, vmem_limit_bytes=V7X_VMEM_LIMIT)


def _split_hi_lo(a):
    hi = a.astype(MXU_DTYPE)
    lo = (a - hi.astype(F32)).astype(MXU_DTYPE)
    return hi, lo


def _sigmoid(z):
    return 0.5 * (1.0 + jnp.tanh(0.5 * z))


def _norm_mod(x, g, shift, scale):
    y = x * lax.rsqrt(jnp.mean(x * x, axis=-1, keepdims=True) + EPS) * g
    return y * (1.0 + scale) + shift


def _ada_kernel(c_ref, w_ref, b_ref, o_ref):
    c = c_ref[...]
    s_hi, s_lo = _split_hi_lo(c * _sigmoid(c))
    w_hi, w_lo = _split_hi_lo(w_ref[0])
    acc = jnp.dot(s_hi, w_hi, preferred_element_type=F32)
    acc += jnp.dot(s_hi, w_lo, preferred_element_type=F32)
    acc += jnp.dot(s_lo, w_hi, preferred_element_type=F32)
    o_ref[0] = acc + b_ref[0]


def _ada(cvec, ada_w, ada_b):
    depth, d, n6 = ada_w.shape
    tn = 1536
    return pl.pallas_call(
        _ada_kernel,
        grid=(depth, n6 // tn),
        in_specs=[
            pl.BlockSpec((MOD_ROWS, d), lambda l, j: (0, 0)),
            pl.BlockSpec((1, d, tn), lambda l, j: (l, 0, j)),
            pl.BlockSpec((1, 1, tn), lambda l, j: (l, 0, j)),
        ],
        out_specs=pl.BlockSpec((1, MOD_ROWS, tn), lambda l, j: (l, 0, j)),
        out_shape=jax.ShapeDtypeStruct((depth, MOD_ROWS, n6), F32),
        compiler_params=_params("arbitrary", "arbitrary"),
        name="ada",
    )(cvec, ada_w, ada_b.reshape(depth, 1, n6))


def _even_in_kernel(*refs, rope):
    if rope:
        (x_ref, g_ref, sh_ref, sc_ref, w_ref, cos_ref, sa_ref, sb_ref, q_ref, kv_ref, gb_ref, cu_ref) = refs
    else:
        (x_ref, g_ref, sh_ref, sc_ref, w_ref, q_ref, kv_ref, gb_ref, cu_ref) = refs
    h = _norm_mod(x_ref[0], g_ref[...], sh_ref[0], sc_ref[0]).astype(MXU_DTYPE)
    y = jnp.dot(h, w_ref[...], preferred_element_type=F32)
    q = y[:, :ATTN_WIDTH]
    k = y[:, ATTN_WIDTH:ATTN_WIDTH + KV_WIDTH]
    v = y[:, ATTN_WIDTH + KV_WIDTH:ATTN_WIDTH + 2 * KV_WIDTH]
    c0 = ATTN_WIDTH + 2 * KV_WIDTH
    if rope:
        cos, sa, sb = cos_ref[...], sa_ref[...], sb_ref[...]

        def rot(z):
            return z * cos + pltpu.roll(z, 16, 1) * sa + pltpu.roll(z, 112, 1) * sb

        q = jnp.concatenate([rot(q[:, j * 128:(j + 1) * 128]) for j in range(ATTN_WIDTH // 128)], axis=1)
        k = rot(k)
    q_ref[0] = (q * (HEAD_DIM ** -0.5)).astype(q_ref.dtype)
    kv_ref[0] = jnp.concatenate([k, v], axis=1).astype(kv_ref.dtype)
    gb_ref[0] = y[:, c0:c0 + CONV_B_WIDTH]
    cu_ref[0] = y[:, c0 + CONV_B_WIDTH:c0 + 2 * CONV_B_WIDTH] * y[:, c0 + 2 * CONV_B_WIDTH:]


def _even_in(x, g, shift, scale, mod_row, w, tables, tile):
    b, n, d = x.shape
    nt = n // tile
    rope = tables is not None
    row = (lambda bi: bi) if mod_row is None else (lambda bi: mod_row)
    in_specs = [
        pl.BlockSpec((1, tile, d), lambda bi, t: (bi, t, 0)),
        pl.BlockSpec((1, d), lambda bi, t: (0, 0)),
        pl.BlockSpec((1, 1, d), lambda bi, t: (row(bi), 0, 0)),
        pl.BlockSpec((1, 1, d), lambda bi, t: (row(bi), 0, 0)),
        pl.BlockSpec(w.shape, lambda bi, t: (0, 0)),
    ]
    args = [x, g, shift, scale, w]
    if rope:
        in_specs += [pl.BlockSpec((tile, 128), lambda bi, t: (t, 0))] * 3
        args += list(tables)
    out_shape = (
        jax.ShapeDtypeStruct((b, n, ATTN_WIDTH), MXU_DTYPE),
        jax.ShapeDtypeStruct((b, n, 2 * KV_WIDTH), MXU_DTYPE),
        jax.ShapeDtypeStruct((b, n, CONV_B_WIDTH), F32),
        jax.ShapeDtypeStruct((b, n, CONV_B_WIDTH), F32),
    )
    out_specs = tuple(pl.BlockSpec((1, tile, s.shape[-1]), lambda bi, t: (bi, t, 0)) for s in out_shape)
    return pl.pallas_call(
        functools.partial(_even_in_kernel, rope=rope),
        grid=(b, nt),
        in_specs=in_specs,
        out_specs=out_specs,
        out_shape=out_shape,
        compiler_params=_params("arbitrary", "arbitrary"),
        name="even_in_rope" if rope else "even_in",
    )(*args)


def _rope_tables(n):
    nf = HEAD_DIM // 4
    pos = jnp.arange(n)
    rows = (pos // GRID_W).astype(F32)
    cols = (pos % GRID_W).astype(F32)
    lane = jnp.arange(128)
    inv = ROPE_BASE ** (-(lane % nf).astype(F32) / nf)
    use_col = (lane % HEAD_DIM) >= HEAD_DIM // 2
    ang = jnp.where(use_col[None, :], cols[:, None], rows[:, None]) * inv[None, :]
    cos, sin = jnp.cos(ang), jnp.sin(ang)
    second = ((lane % (2 * nf)) >= nf)[None, :]
    return cos, jnp.where(second, sin, 0.0), jnp.where(second, 0.0, -sin)


def _attn_kernel(*refs, n, has_local):
    if has_local:
        sink_ref, q_ref, kv_ref, kvc_ref, o_ref = refs
    else:
        sink_ref, q_ref, kvc_ref, o_ref = refs
    i = pl.program_id(1)
    q = q_ref[0]
    kvall = kvc_ref[0]
    n_loc = 3 * BLOCK
    if has_local:
        start = pl.multiple_of(jnp.clip(i * BLOCK - BLOCK, 0, n - n_loc), BLOCK)
        kvall = jnp.concatenate([kv_ref[0, pl.ds(start, n_loc), :], kvall], axis=0)
        qpos = i * BLOCK + lax.broadcasted_iota(jnp.int32, (Q_PER_KV * BLOCK, n_loc), 0) % BLOCK
        kpos = start + lax.broadcasted_iota(jnp.int32, (Q_PER_KV * BLOCK, n_loc), 1)
        valid = jnp.abs(kpos - qpos) <= WINDOW
    grp = lax.broadcasted_iota(jnp.int32, (Q_PER_KV * BLOCK, 1), 0) // BLOCK
    outs = []
    for hk in range(N_KV_HEADS):
        kh = kvall[:, hk * HEAD_DIM:(hk + 1) * HEAD_DIM]
        vh = kvall[:, KV_WIDTH + hk * HEAD_DIM:KV_WIDTH + (hk + 1) * HEAD_DIM]
        qg = jnp.concatenate(
            [q[:, (hk * Q_PER_KV + g) * HEAD_DIM:(hk * Q_PER_KV + g + 1) * HEAD_DIM] for g in range(Q_PER_KV)], axis=0)
        s = lax.dot_general(qg, kh, (((1,), (1,)), ((), ())), preferred_element_type=F32)
        if has_local:
            s = jnp.concatenate([jnp.where(valid, s[:, :n_loc], NEG_INF), s[:, n_loc:]], axis=1)
        snk = jnp.zeros((Q_PER_KV * BLOCK, 1), F32)
        for g in range(Q_PER_KV):
            snk = jnp.where(grp == g, sink_ref[hk * Q_PER_KV + g], snk)
        m = jnp.maximum(jnp.max(s, axis=1, keepdims=True), snk)
        p = jnp.exp(s - m)
        denom = jnp.sum(p, axis=1, keepdims=True) + jnp.exp(snk - m)
        o = jnp.dot(p.astype(MXU_DTYPE), vh, preferred_element_type=F32) / denom
        outs += [o[g * BLOCK:(g + 1) * BLOCK] for g in range(Q_PER_KV)]
    o_ref[0] = jnp.concatenate(outs, axis=1).astype(o_ref.dtype)


def _attention(sink, q, kv, kvc):
    b, n, _ = q.shape
    lc = kvc.shape[1]
    has_local = kv is not None
    in_specs = [pl.BlockSpec(memory_space=pltpu.SMEM), pl.BlockSpec((1, BLOCK, ATTN_WIDTH), lambda bi, i: (bi, i, 0))]
    args = [sink, q]
    if has_local:
        in_specs.append(pl.BlockSpec((1, n, 2 * KV_WIDTH), lambda bi, i: (bi, 0, 0)))
        args.append(kv)
    in_specs.append(pl.BlockSpec((1, lc, 2 * KV_WIDTH), lambda bi, i: (bi, 0, 0)))
    args.append(kvc)
    return pl.pallas_call(
        functools.partial(_attn_kernel, n=n, has_local=has_local),
        grid=(b, n // BLOCK),
        in_specs=in_specs,
        out_specs=pl.BlockSpec((1, BLOCK, ATTN_WIDTH), lambda bi, i: (bi, i, 0)),
        out_shape=jax.ShapeDtypeStruct((b, n, ATTN_WIDTH), MXU_DTYPE),
        compiler_params=_params("arbitrary", "arbitrary"),
        name="attn_local" if has_local else "attn_ctx",
    )(*args)


def _mix_out_kernel(*refs, conv, first, last):
    if conv:
        (att_ref, gb_ref, cu_ref, cup_ref, cun_ref, cw_ref, cb_ref,
         w_ref, x_ref, gate_ref, g2_ref, sh2_ref, sc2_ref, rwh_ref, rwl_ref, xo_ref, h_ref, aff_ref) = refs
        tile = cu_ref.shape[1]
        t = pl.program_id(1)
        prev = jnp.where(t == first, 0.0, cup_ref[0])
        nxt = jnp.where(t == last, 0.0, cun_ref[0])
        ext = jnp.concatenate([prev, cu_ref[0], nxt], axis=0)
        cw = cw_ref[...]
        cv = (cw[0:1] * ext[HALO - 1:HALO - 1 + tile] + cw[1:2] * ext[HALO:HALO + tile]
              + cw[2:3] * ext[HALO + 1:HALO + 1 + tile] + cb_ref[...])
        cat = jnp.concatenate([att_ref[0], (gb_ref[0] * cv).astype(MXU_DTYPE)], axis=1)
    else:
        (y_ref, w_ref, x_ref, gate_ref, g2_ref, sh2_ref, sc2_ref, rwh_ref, rwl_ref, xo_ref, h_ref, aff_ref) = refs
        cat = y_ref[0]
    y = jnp.dot(cat, w_ref[...], preferred_element_type=F32)
    x = x_ref[0] + gate_ref[0] * y
    xo_ref[0] = x
    h = _norm_mod(x, g2_ref[...], sh2_ref[0], sc2_ref[0])
    h_hi, h_lo = _split_hi_lo(h)
    h_ref[0] = h_hi
    nt_dims = (((1,), (1,)), ((), ()))
    logits = lax.dot_general(rwh_ref[...], h_hi, nt_dims, preferred_element_type=F32)
    logits += lax.dot_general(rwl_ref[...], h_hi, nt_dims, preferred_element_type=F32)
    logits += lax.dot_general(rwh_ref[...], h_lo, nt_dims, preferred_element_type=F32)
    e = jnp.exp(logits - jnp.max(logits, axis=0, keepdims=True))
    aff_ref[0] = e / jnp.sum(e, axis=0, keepdims=True)


def _mix_out(mix_in, conv_params, w, x, gate, g2, sh2, sc2, mod_row, rw_hi, rw_lo, tile):
    b, n, d = x.shape
    nt = n // tile
    conv = conv_params is not None
    row = (lambda bi: bi) if mod_row is None else (lambda bi: mod_row)
    tok = lambda width: pl.BlockSpec((1, tile, width), lambda bi, t: (bi, t, 0))
    modspec = pl.BlockSpec((1, 1, d), lambda bi, t: (row(bi), 0, 0))
    full = lambda a: pl.BlockSpec(a.shape, lambda bi, t: (0,) * a.ndim)
    if conv:
        att, gb, cu = mix_in
        cw, cb = conv_params
        per = tile // HALO
        in_specs = [tok(ATTN_WIDTH), tok(CONV_B_WIDTH), tok(CONV_B_WIDTH),
                    pl.BlockSpec((1, HALO, CONV_B_WIDTH), lambda bi, t: (bi, jnp.maximum(t * per - 1, 0), 0)),
                    pl.BlockSpec((1, HALO, CONV_B_WIDTH), lambda bi, t: (bi, jnp.minimum((t + 1) * per, n // HALO - 1), 0)),
                    full(cw), full(cb)]
        args = [att, gb, cu, cu, cu, cw, cb]
    else:
        in_specs = [tok(d)]
        args = [mix_in]
    in_specs += [full(w), tok(d), modspec, full(g2), modspec, modspec, full(rw_hi), full(rw_lo)]
    args += [w, x, gate, g2, sh2, sc2, rw_hi, rw_lo]
    out_shape = (jax.ShapeDtypeStruct((b, n, d), F32), jax.ShapeDtypeStruct((b, n, d), MXU_DTYPE),
                 jax.ShapeDtypeStruct((b, N_EXPERTS, n), F32))
    out_specs = (tok(d), tok(d), pl.BlockSpec((1, N_EXPERTS, tile), lambda bi, t: (bi, 0, t)))
    return pl.pallas_call(
        functools.partial(_mix_out_kernel, conv=conv, first=0, last=nt - 1),
        grid=(b, nt),
        in_specs=in_specs,
        out_specs=out_specs,
        out_shape=out_shape,
        compiler_params=_params("arbitrary", "arbitrary"),
        name="even_out" if conv else "odd_out",
    )(*args)


def _moe_ffn_kernel(x_ref, g_ref, wg_ref, wu_ref, wd_ref, o_ref):
    bb, _, cap, d = x_ref.shape
    x = x_ref[...].reshape(bb * cap, d)
    a = jnp.dot(x, wg_ref[0], preferred_element_type=F32)
    u = jnp.dot(x, wu_ref[0], preferred_element_type=F32)
    mid = (a * _sigmoid(a) * u).astype(MXU_DTYPE)
    y = jnp.dot(mid, wd_ref[0], preferred_element_type=F32)
    o_ref[...] = (y * g_ref[...].reshape(bb * cap, 1)).reshape(bb, 1, cap, d)


def _moe_ffn(xe, g, wg, wu, wd, bblk):
    b, e, cap, d = xe.shape
    ff = wg.shape[-1]
    tokspec = lambda width: pl.BlockSpec((bblk, 1, cap, width), lambda ei, bi: (bi, ei, 0, 0))
    return pl.pallas_call(
        _moe_ffn_kernel,
        grid=(e, b // bblk),
        in_specs=[tokspec(d), tokspec(1),
                  pl.BlockSpec((1, d, ff), lambda ei, bi: (ei, 0, 0)),
                  pl.BlockSpec((1, d, ff), lambda ei, bi: (ei, 0, 0)),
                  pl.BlockSpec((1, ff, d), lambda ei, bi: (ei, 0, 0))],
        out_specs=tokspec(d),
        out_shape=jax.ShapeDtypeStruct((b, e, cap, d), F32),
        compiler_params=_params("arbitrary", "arbitrary"),
        name="moe_ffn",
    )(xe, g.reshape(b, e, cap, 1), wg, wu, wd)


def _expert_choice(h, aff_t, wg, wu, wd, bblk):
    b, n, d = h.shape
    cap = max(1, CAPACITY_FACTOR * n // N_EXPERTS)
    g, idx = lax.top_k(aff_t, cap)
    flat = (jnp.arange(b)[:, None, None] * n + idx).reshape(-1)
    xe = h.reshape(b * n, d)[flat].reshape(b, N_EXPERTS, cap, d)
    y = _moe_ffn(xe, g, wg, wu, wd, bblk)
    out = jnp.zeros((b * n, d), F32).at[flat].add(y.reshape(-1, d))
    return out.reshape(b, n, d)


def _odd_in_kernel(x_ref, moe_ref, gate_ref, g_ref, sh_ref, sc_ref, w_ref, xo_ref, gg_ref, u_ref):
    x = x_ref[0] + gate_ref[0] * moe_ref[0]
    xo_ref[0] = x
    h = _norm_mod(x, g_ref[...], sh_ref[0], sc_ref[0]).astype(MXU_DTYPE)
    y = jnp.dot(h, w_ref[...], preferred_element_type=F32)
    gl = y[:, :LRU_WIDTH]
    gg_ref[0] = 0.5 * gl * (1.0 + jnp.tanh(0.7978845608028654 * (gl + 0.044715 * gl * gl * gl)))
    u_ref[0] = y[:, LRU_WIDTH:]


def _odd_in(x, moe, gate, g, shift, scale, mod_row, w, tile):
    b, n, d = x.shape
    row = (lambda bi: bi) if mod_row is None else (lambda bi: mod_row)
    tok = pl.BlockSpec((1, tile, d), lambda bi, t: (bi, t, 0))
    modspec = pl.BlockSpec((1, 1, d), lambda bi, t: (row(bi), 0, 0))
    return pl.pallas_call(
        _odd_in_kernel,
        grid=(b, n // tile),
        in_specs=[tok, tok, modspec, pl.BlockSpec((1, d), lambda bi, t: (0, 0)), modspec, modspec,
                  pl.BlockSpec(w.shape, lambda bi, t: (0, 0))],
        out_specs=(tok, tok, tok),
        out_shape=(jax.ShapeDtypeStruct((b, n, d), F32),) * 3,
        compiler_params=_params("arbitrary", "arbitrary"),
        name="odd_in",
    )(x, moe, gate, g, shift, scale, w)


def _lru_kernel(*refs, reverse, nt):
    if reverse:
        (u_ref, up_ref, un_ref, cw_ref, cb_ref, wax_ref, ba_ref, bx_ref, lam_ref, h0_ref, hf_ref, gg_ref,
         out_ref, hlast_ref, a_scr, b_scr, carry_scr) = refs
    else:
        (u_ref, up_ref, un_ref, cw_ref, cb_ref, wax_ref, ba_ref, bx_ref, lam_ref, h0_ref,
         out_ref, hlast_ref, a_scr, b_scr, carry_scr) = refs
    tile, w = u_ref.shape[1], u_ref.shape[2]
    t = pl.program_id(1)
    pos = (nt - 1 - t) if reverse else t
    prev = jnp.where(pos == 0, 0.0, up_ref[0])
    nxt = jnp.where(pos == nt - 1, 0.0, un_ref[0])
    ext = jnp.concatenate([prev, u_ref[0], nxt], axis=0)
    cw = cw_ref[...]
    uc = cb_ref[...]
    for k in range(cw.shape[0]):
        uc = uc + cw[k:k + 1] * ext[HALO - 2 + k:HALO - 2 + k + tile]
    ucb = uc.astype(MXU_DTYPE)
    lam = lam_ref[0]
    softplus_neg = jnp.maximum(-lam, 0.0) + jnp.log1p(jnp.exp(-jnp.abs(lam)))
    for hd in range(LRU_HEADS):
        sl = slice(hd * LRU_BLOCK, (hd + 1) * LRU_BLOCK)
        z = jnp.dot(ucb[:, sl], wax_ref[0, hd], preferred_element_type=F32)
        r = _sigmoid(z[:, :LRU_BLOCK] + ba_ref[0][:, sl])
        ig = _sigmoid(z[:, LRU_BLOCK:] + bx_ref[0][:, sl])
        log_a = -LRU_C * r * softplus_neg[:, sl]
        a = jnp.exp(log_a)
        mult = jnp.sqrt(-jnp.tanh(log_a) * (a * a + 1.0))
        a_scr[:, sl] = a
        b_scr[:, sl] = mult * (ig * uc[:, sl])

    @pl.when(t == 0)
    def _():
        carry_scr[...] = jnp.broadcast_to(h0_ref[0], carry_scr.shape)

    rowid = lax.broadcasted_iota(jnp.int32, (HALO, w), 0)

    def body(j, carry):
        jj = (tile // HALO - 1 - j) if reverse else j
        off = pl.multiple_of(jj * HALO, HALO)
        a = a_scr[pl.ds(off, HALO), :]
        bcoef = b_scr[pl.ds(off, HALO), :]
        for dist in (1, 2, 4):
            shift = (HALO - dist) if reverse else dist
            m = (rowid < HALO - dist) if reverse else (rowid >= dist)
            a_s = pltpu.roll(a, shift, 0)
            b_s = pltpu.roll(bcoef, shift, 0)
            bcoef = jnp.where(m, a * b_s + bcoef, bcoef)
            a = jnp.where(m, a * a_s, a)
        hcur = a * carry + bcoef
        if reverse:
            out_ref[0, pl.ds(off, HALO), :] = (
                gg_ref[0, pl.ds(off, HALO), :] * (hf_ref[0, pl.ds(off, HALO), :] + hcur)).astype(out_ref.dtype)
            return jnp.broadcast_to(hcur[0:1], carry.shape)
        out_ref[0, pl.ds(off, HALO), :] = hcur
        return jnp.broadcast_to(hcur[HALO - 1:HALO], carry.shape)

    carry = lax.fori_loop(0, tile // HALO, body, carry_scr[...])
    carry_scr[...] = carry
    hlast_ref[0] = carry[0:1]


def _lru(u, cw, cb, wax, ba, bx, lam, h0, direction, tile, hf=None, gg=None):
    b, n, w = u.shape
    nt = n // tile
    per = tile // HALO
    reverse = direction == 1
    pos = (lambda t: nt - 1 - t) if reverse else (lambda t: t)
    tok = pl.BlockSpec((1, tile, w), lambda bi, t: (bi, pos(t), 0))
    rowspec = pl.BlockSpec((1, 1, w), lambda bi, t: (direction, 0, 0))
    in_specs = [
        tok,
        pl.BlockSpec((1, HALO, w), lambda bi, t: (bi, jnp.maximum(pos(t) * per - 1, 0), 0)),
        pl.BlockSpec((1, HALO, w), lambda bi, t: (bi, jnp.minimum((pos(t) + 1) * per, n // HALO - 1), 0)),
        pl.BlockSpec(cw.shape, lambda bi, t: (0, 0)),
        pl.BlockSpec(cb.shape, lambda bi, t: (0, 0)),
        pl.BlockSpec((1,) + wax.shape[1:], lambda bi, t: (direction, 0, 0, 0)),
        rowspec, rowspec, rowspec,
        pl.BlockSpec((1, 1, w), lambda bi, t: (bi, 0, 0)),
    ]
    args = [u, u, u, cw, cb, wax, ba, bx, lam, h0]
    if reverse:
        in_specs += [tok, tok]
        args += [hf, gg]
    out_dtype = MXU_DTYPE if reverse else F32
    return pl.pallas_call(
        functools.partial(_lru_kernel, reverse=reverse, nt=nt),
        grid=(b, nt),
        in_specs=in_specs,
        out_specs=(tok, pl.BlockSpec((1, 1, w), lambda bi, t: (bi, 0, 0))),
        out_shape=(jax.ShapeDtypeStruct((b, n, w), out_dtype), jax.ShapeDtypeStruct((b, 1, w), F32)),
        scratch_shapes=[pltpu.VMEM((tile, w), F32), pltpu.VMEM((tile, w), F32), pltpu.VMEM((HALO, w), F32)],
        compiler_params=_params("arbitrary", "arbitrary"),
        name="lru_bwd" if reverse else "lru_fwd",
    )(*args)


def _final_kernel(x_ref, moe_ref, gate_ref, g_ref, o_ref):
    x = x_ref[0] + gate_ref[0] * moe_ref[0]
    o_ref[0] = x * lax.rsqrt(jnp.mean(x * x, axis=-1, keepdims=True) + EPS) * g_ref[...]


def _final(x, moe, gate, g, tile):
    b, n, d = x.shape
    tok = pl.BlockSpec((1, tile, d), lambda bi, t: (bi, t, 0))
    return pl.pallas_call(
        _final_kernel,
        grid=(b, n // tile),
        in_specs=[tok, tok, pl.BlockSpec((1, 1, d), lambda bi, t: (bi, 0, 0)), pl.BlockSpec((1, d), lambda bi, t: (0, 0))],
        out_specs=tok,
        out_shape=jax.ShapeDtypeStruct((b, n, d), F32),
        compiler_params=_params("arbitrary", "arbitrary"),
        name="final_norm",
    )(x, moe, gate, g)


def kernel(x, c, ctx, c_ctx, ada_w, ada_b, norm_mix_g, norm_ffn_g, ev_w_in, ev_w_out, ev_sink, ev_conv_w, ev_conv_b, od_w_in, od_w_out, od_conv_w, od_conv_b, od_wa, od_ba, od_wx, od_bx, od_lambda, router_w, w_gate, w_up, w_down, final_g):
    b, n, d = x.shape
    lc = ctx.shape[1]
    depth = ada_w.shape[0]
    assert depth == 2 and d == D_MODEL and b < MOD_ROWS
    tile_l = min(512, n)
    tile_c = lc
    ctx_row = b

    cvec = jnp.zeros((MOD_ROWS, d), F32).at[:b].set(c).at[b].set(c_ctx)
    mods = _ada(cvec, ada_w, ada_b).reshape(depth, MOD_ROWS, 6, 1, d)
    mod = lambda l, j: mods[l, :, j]

    def router_split(l):
        return _split_hi_lo(router_w[l].T)

    bf = lambda a: a.astype(MXU_DTYPE)

    g_mix = norm_mix_g[0].reshape(1, d)
    g_ffn = norm_ffn_g[0].reshape(1, d)
    w_in = bf(ev_w_in[0])
    w_out = bf(ev_w_out[0])
    conv_p = (ev_conv_w[0], ev_conv_b[0].reshape(1, -1))
    tables = _rope_tables(n)
    ql, kvl, gbl, cul = _even_in(x, g_mix, mod(0, 0), mod(0, 1), None, w_in, tables, tile_l)
    qc, kvc, gbc, cuc = _even_in(ctx, g_mix, mod(0, 0), mod(0, 1), ctx_row, w_in, None, tile_c)
    att_l = _attention(ev_sink[0], ql, kvl, kvc)
    att_c = _attention(ev_sink[0], qc, None, kvc)
    rw_hi, rw_lo = router_split(0)
    xl, hl, aff_l = _mix_out((att_l, gbl, cul), conv_p, w_out, x, mod(0, 2), g_ffn, mod(0, 3), mod(0, 4), None,
                             rw_hi, rw_lo, tile_l)
    xc, hc, aff_c = _mix_out((att_c, gbc, cuc), conv_p, w_out, ctx, mod(0, 2), g_ffn, mod(0, 3), mod(0, 4), ctx_row,
                             rw_hi, rw_lo, tile_c)
    wg, wu, wd = bf(w_gate[0]), bf(w_up[0]), bf(w_down[0])
    moe_l = _expert_choice(hl, aff_l, wg, wu, wd, 1)
    moe_c = _expert_choice(hc, aff_c, wg, wu, wd, b)

    g_mix = norm_mix_g[1].reshape(1, d)
    g_ffn = norm_ffn_g[1].reshape(1, d)
    w_in = bf(od_w_in[0])
    w_out = bf(od_w_out[0])
    xl, ggl, ul = _odd_in(xl, moe_l, mod(0, 5), g_mix, mod(1, 0), mod(1, 1), None, w_in, tile_l)
    _, _, uc = _odd_in(xc, moe_c, mod(0, 5), g_mix, mod(1, 0), mod(1, 1), ctx_row, w_in, tile_c)
    cw, cb = od_conv_w[0], od_conv_b[0].reshape(1, -1)
    wax = bf(jnp.concatenate([od_wa[0], od_wx[0]], axis=-1))
    ba, bx, lam = (a[0].reshape(2, 1, -1) for a in (od_ba, od_bx, od_lambda))
    zero_state = jnp.zeros((b, 1, LRU_WIDTH), F32)
    hf_c, h0_f = _lru(uc, cw, cb, wax, ba, bx, lam, zero_state, 0, tile_c)
    _, h0_b = _lru(uc, cw, cb, wax, ba, bx, lam, zero_state, 1, tile_c, hf=hf_c, gg=hf_c)
    hf_l, _ = _lru(ul, cw, cb, wax, ba, bx, lam, h0_f, 0, tile_l)
    yl, _ = _lru(ul, cw, cb, wax, ba, bx, lam, h0_b, 1, tile_l, hf=hf_l, gg=ggl)
    rw_hi, rw_lo = router_split(1)
    xl, hl, aff_l = _mix_out(yl, None, w_out, xl, mod(1, 2), g_ffn, mod(1, 3), mod(1, 4), None, rw_hi, rw_lo, tile_l)
    moe_l = _expert_choice(hl, aff_l, bf(w_gate[1]), bf(w_up[1]), bf(w_down[1]), 1)
    return _final(xl, moe_l, mod(1, 5), final_g.reshape(1, d), tile_l)
```

```python
import functools

import jax
import jax.numpy as jnp
from jax import lax
from jax.experimental import pallas as pl
from jax.experimental.pallas import tpu as pltpu

F32 = jnp.float32
MXU_DTYPE = jnp.bfloat16

D_MODEL = 1024
GRID_W = 64
EPS = 1e-6
NEG_INF = -1e30
HEAD_DIM = 64
N_Q_HEADS = 8
N_KV_HEADS = 2
Q_PER_KV = N_Q_HEADS // N_KV_HEADS
ATTN_WIDTH = N_Q_HEADS * HEAD_DIM
KV_WIDTH = N_KV_HEADS * HEAD_DIM
WINDOW = 128
BLOCK = 128
ROPE_BASE = 10000.0
CONV_B_WIDTH = D_MODEL // 2
EVEN_IN = ATTN_WIDTH + 2 * KV_WIDTH + 3 * CONV_B_WIDTH
LRU_WIDTH = D_MODEL
LRU_HEADS = 8
LRU_BLOCK = LRU_WIDTH // LRU_HEADS
LRU_C = 8.0
N_EXPERTS = 16
CAPACITY_FACTOR = 2
EXPERT_FF = 1408
MOD_ROWS = 16
LANES = 128
SUBLANES = 8
HALO = SUBLANES
CUMSUM_TILE = 256
V7X_VMEM_LIMIT = 56 * 1024 * 1024


def _params(*sem):
    return pltpu.CompilerParams(dimension_semantics=sem, vmem_limit_bytes=V7X_VMEM_LIMIT)


def _split_hi_lo(a):
    hi = a.astype(MXU_DTYPE)
    lo = (a - hi.astype(F32)).astype(MXU_DTYPE)
    return hi, lo


def _sigmoid(z):
    return 0.5 * (1.0 + jnp.tanh(0.5 * z))


def _norm_mod(x, g, shift, scale):
    y = x * lax.rsqrt(jnp.mean(x * x, axis=-1, keepdims=True) + EPS) * g
    return y * (1.0 + scale) + shift


def _ada_kernel(c_ref, w_ref, b_ref, o_ref):
    c = c_ref[...]
    s_hi, s_lo = _split_hi_lo(c * _sigmoid(c))
    w_hi, w_lo = _split_hi_lo(w_ref[0])
    acc = jnp.dot(s_hi, w_hi, preferred_element_type=F32)
    acc += jnp.dot(s_hi, w_lo, preferred_element_type=F32)
    acc += jnp.dot(s_lo, w_hi, preferred_element_type=F32)
    o_ref[0] = acc + b_ref[0]


def _ada(cvec, ada_w, ada_b):
    depth, d, n6 = ada_w.shape
    tn = 1536
    return pl.pallas_call(
        _ada_kernel,
        grid=(depth, n6 // tn),
        in_specs=[
            pl.BlockSpec((MOD_ROWS, d), lambda l, j: (0, 0)),
            pl.BlockSpec((1, d, tn), lambda l, j: (l, 0, j)),
            pl.BlockSpec((1, 1, tn), lambda l, j: (l, 0, j)),
        ],
        out_specs=pl.BlockSpec((1, MOD_ROWS, tn), lambda l, j: (l, 0, j)),
        out_shape=jax.ShapeDtypeStruct((depth, MOD_ROWS, n6), F32),
        compiler_params=_params("arbitrary", "arbitrary"),
        name="ada",
    )(cvec, ada_w, ada_b.reshape(depth, 1, n6))


def _even_in_kernel(*refs, rope):
    if rope:
        (x_ref, g_ref, sh_ref, sc_ref, w_ref, cos_ref, sa_ref, sb_ref, q_ref, kv_ref, gb_ref, cu_ref) = refs
    else:
        (x_ref, g_ref, sh_ref, sc_ref, w_ref, q_ref, kv_ref, gb_ref, cu_ref) = refs
    h = _norm_mod(x_ref[0], g_ref[...], sh_ref[0], sc_ref[0]).astype(MXU_DTYPE)
    y = jnp.dot(h, w_ref[...], preferred_element_type=F32)
    q = y[:, :ATTN_WIDTH]
    k = y[:, ATTN_WIDTH:ATTN_WIDTH + KV_WIDTH]
    v = y[:, ATTN_WIDTH + KV_WIDTH:ATTN_WIDTH + 2 * KV_WIDTH]
    c0 = ATTN_WIDTH + 2 * KV_WIDTH
    if rope:
        cos, sa, sb = cos_ref[...], sa_ref[...], sb_ref[...]

        def rot(z):
            return z * cos + pltpu.roll(z, 16, 1) * sa + pltpu.roll(z, 112, 1) * sb

        q = jnp.concatenate([rot(q[:, j * 128:(j + 1) * 128]) for j in range(ATTN_WIDTH // 128)], axis=1)
        k = rot(k)
    q_ref[0] = (q * (HEAD_DIM ** -0.5)).astype(q_ref.dtype)
    kv_ref[0] = jnp.concatenate([k, v], axis=1).astype(kv_ref.dtype)
    gb_ref[0] = y[:, c0:c0 + CONV_B_WIDTH]
    cu_ref[0] = y[:, c0 + CONV_B_WIDTH:c0 + 2 * CONV_B_WIDTH] * y[:, c0 + 2 * CONV_B_WIDTH:]


def _even_in(x, g, shift, scale, mod_row, w, tables, tile):
    b, n, d = x.shape
    nt = n // tile
    rope = tables is not None
    row = (lambda bi: bi) if mod_row is None else (lambda bi: mod_row)
    in_specs = [
        pl.BlockSpec((1, tile, d), lambda bi, t: (bi, t, 0)),
        pl.BlockSpec((1, d), lambda bi, t: (0, 0)),
        pl.BlockSpec((1, 1, d), lambda bi, t: (row(bi), 0, 0)),
        pl.BlockSpec((1, 1, d), lambda bi, t: (row(bi), 0, 0)),
        pl.BlockSpec(w.shape, lambda bi, t: (0, 0)),
    ]
    args = [x, g, shift, scale, w]
    if rope:
        in_specs += [pl.BlockSpec((tile, 128), lambda bi, t: (t, 0))] * 3
        args += list(tables)
    out_shape = (
        jax.ShapeDtypeStruct((b, n, ATTN_WIDTH), MXU_DTYPE),
        jax.ShapeDtypeStruct((b, n, 2 * KV_WIDTH), MXU_DTYPE),
        jax.ShapeDtypeStruct((b, n, CONV_B_WIDTH), F32),
        jax.ShapeDtypeStruct((b, n, CONV_B_WIDTH), F32),
    )
    out_specs = tuple(pl.BlockSpec((1, tile, s.shape[-1]), lambda bi, t: (bi, t, 0)) for s in out_shape)
    return pl.pallas_call(
        functools.partial(_even_in_kernel, rope=rope),
        grid=(b, nt),
        in_specs=in_specs,
        out_specs=out_specs,
        out_shape=out_shape,
        compiler_params=_params("arbitrary", "arbitrary"),
        name="even_in_rope" if rope else "even_in",
    )(*args)


def _rope_tables(n):
    nf = HEAD_DIM // 4
    pos = jnp.arange(n)
    rows = (pos // GRID_W).astype(F32)
    cols = (pos % GRID_W).astype(F32)
    lane = jnp.arange(128)
    inv = ROPE_BASE ** (-(lane % nf).astype(F32) / nf)
    use_col = (lane % HEAD_DIM) >= HEAD_DIM // 2
    ang = jnp.where(use_col[None, :], cols[:, None], rows[:, None]) * inv[None, :]
    cos, sin = jnp.cos(ang), jnp.sin(ang)
    second = ((lane % (2 * nf)) >= nf)[None, :]
    return cos, jnp.where(second, sin, 0.0), jnp.where(second, 0.0, -sin)


def _attn_kernel(*refs, n, has_local):
    if has_local:
        sink_ref, q_ref, kv_ref, kvc_ref, o_ref = refs
    else:
        sink_ref, q_ref, kvc_ref, o_ref = refs
    i = pl.program_id(1)
    q = q_ref[0]
    kvall = kvc_ref[0]
    n_loc = 3 * BLOCK
    if has_local:
        start = pl.multiple_of(jnp.clip(i * BLOCK - BLOCK, 0, n - n_loc), BLOCK)
        kvall = jnp.concatenate([kv_ref[0, pl.ds(start, n_loc), :], kvall], axis=0)
        qpos = i * BLOCK + lax.broadcasted_iota(jnp.int32, (Q_PER_KV * BLOCK, n_loc), 0) % BLOCK
        kpos = start + lax.broadcasted_iota(jnp.int32, (Q_PER_KV * BLOCK, n_loc), 1)
        valid = jnp.abs(kpos - qpos) <= WINDOW
    grp = lax.broadcasted_iota(jnp.int32, (Q_PER_KV * BLOCK, 1), 0) // BLOCK
    outs = []
    for hk in range(N_KV_HEADS):
        kh = kvall[:, hk * HEAD_DIM:(hk + 1) * HEAD_DIM]
        vh = kvall[:, KV_WIDTH + hk * HEAD_DIM:KV_WIDTH + (hk + 1) * HEAD_DIM]
        qg = jnp.concatenate(
            [q[:, (hk * Q_PER_KV + g) * HEAD_DIM:(hk * Q_PER_KV + g + 1) * HEAD_DIM] for g in range(Q_PER_KV)], axis=0)
        s = lax.dot_general(qg, kh, (((1,), (1,)), ((), ())), preferred_element_type=F32)
        if has_local:
            s = jnp.concatenate([jnp.where(valid, s[:, :n_loc], NEG_INF), s[:, n_loc:]], axis=1)
        snk = jnp.zeros((Q_PER_KV * BLOCK, 1), F32)
        for g in range(Q_PER_KV):
            snk = jnp.where(grp == g, sink_ref[hk * Q_PER_KV + g], snk)
        m = jnp.maximum(jnp.max(s, axis=1, keepdims=True), snk)
        p = jnp.exp(s - m)
        denom = jnp.sum(p, axis=1, keepdims=True) + jnp.exp(snk - m)
        o = jnp.dot(p.astype(MXU_DTYPE), vh, preferred_element_type=F32) / denom
        outs += [o[g * BLOCK:(g + 1) * BLOCK] for g in range(Q_PER_KV)]
    o_ref[0] = jnp.concatenate(outs, axis=1).astype(o_ref.dtype)


def _attention(sink, q, kv, kvc):
    b, n, _ = q.shape
    lc = kvc.shape[1]
    has_local = kv is not None
    in_specs = [pl.BlockSpec(memory_space=pltpu.SMEM), pl.BlockSpec((1, BLOCK, ATTN_WIDTH), lambda bi, i: (bi, i, 0))]
    args = [sink, q]
    if has_local:
        in_specs.append(pl.BlockSpec((1, n, 2 * KV_WIDTH), lambda bi, i: (bi, 0, 0)))
        args.append(kv)
    in_specs.append(pl.BlockSpec((1, lc, 2 * KV_WIDTH), lambda bi, i: (bi, 0, 0)))
    args.append(kvc)
    return pl.pallas_call(
        functools.partial(_attn_kernel, n=n, has_local=has_local),
        grid=(b, n // BLOCK),
        in_specs=in_specs,
        out_specs=pl.BlockSpec((1, BLOCK, ATTN_WIDTH), lambda bi, i: (bi, i, 0)),
        out_shape=jax.ShapeDtypeStruct((b, n, ATTN_WIDTH), MXU_DTYPE),
        compiler_params=_params("arbitrary", "arbitrary"),
        name="attn_local" if has_local else "attn_ctx",
    )(*args)


def _mix_out_kernel(*refs, conv, first, last):
    if conv:
        (att_ref, gb_ref, cu_ref, cup_ref, cun_ref, cw_ref, cb_ref,
         w_ref, x_ref, gate_ref, g2_ref, sh2_ref, sc2_ref, rwh_ref, rwl_ref, xo_ref, h_ref, aff_ref, afft_ref) = refs
        tile = cu_ref.shape[1]
        t = pl.program_id(1)
        prev = jnp.where(t == first, 0.0, cup_ref[0])
        nxt = jnp.where(t == last, 0.0, cun_ref[0])
        ext = jnp.concatenate([prev, cu_ref[0], nxt], axis=0)
        cw = cw_ref[...]
        cv = (cw[0:1] * ext[HALO - 1:HALO - 1 + tile] + cw[1:2] * ext[HALO:HALO + tile]
              + cw[2:3] * ext[HALO + 1:HALO + 1 + tile] + cb_ref[...])
        cat = jnp.concatenate([att_ref[0], (gb_ref[0] * cv).astype(MXU_DTYPE)], axis=1)
    else:
        (y_ref, w_ref, x_ref, gate_ref, g2_ref, sh2_ref, sc2_ref, rwh_ref, rwl_ref, xo_ref, h_ref, aff_ref, afft_ref) = refs
        cat = y_ref[0]
    y = jnp.dot(cat, w_ref[...], preferred_element_type=F32)
    x = x_ref[0] + gate_ref[0] * y
    xo_ref[0] = x
    h = _norm_mod(x, g2_ref[...], sh2_ref[0], sc2_ref[0])
    h_ref[0] = h
    h_hi, h_lo = _split_hi_lo(h)
    logits = jnp.dot(h_hi, rwh_ref[...], preferred_element_type=F32)
    logits += jnp.dot(h_hi, rwl_ref[...], preferred_element_type=F32)
    logits += jnp.dot(h_lo, rwh_ref[...], preferred_element_type=F32)
    lane = lax.broadcasted_iota(jnp.int32, logits.shape, 1)
    logits = jnp.where(lane < N_EXPERTS, logits, NEG_INF)
    e = jnp.exp(logits - jnp.max(logits, axis=1, keepdims=True))
    aff = e / jnp.sum(e, axis=1, keepdims=True)
    aff_ref[0] = aff
    afft_ref[0] = aff.T[:N_EXPERTS]


def _mix_out(mix_in, conv_params, w, x, gate, g2, sh2, sc2, mod_row, rw_hi, rw_lo, tile):
    b, n, d = x.shape
    nt = n // tile
    conv = conv_params is not None
    row = (lambda bi: bi) if mod_row is None else (lambda bi: mod_row)
    tok = lambda width: pl.BlockSpec((1, tile, width), lambda bi, t: (bi, t, 0))
    modspec = pl.BlockSpec((1, 1, d), lambda bi, t: (row(bi), 0, 0))
    full = lambda a: pl.BlockSpec(a.shape, lambda bi, t: (0,) * a.ndim)
    if conv:
        att, gb, cu = mix_in
        cw, cb = conv_params
        per = tile // HALO
        in_specs = [tok(ATTN_WIDTH), tok(CONV_B_WIDTH), tok(CONV_B_WIDTH),
                    pl.BlockSpec((1, HALO, CONV_B_WIDTH), lambda bi, t: (bi, jnp.maximum(t * per - 1, 0), 0)),
                    pl.BlockSpec((1, HALO, CONV_B_WIDTH), lambda bi, t: (bi, jnp.minimum((t + 1) * per, n // HALO - 1), 0)),
                    full(cw), full(cb)]
        args = [att, gb, cu, cu, cu, cw, cb]
    else:
        in_specs = [tok(d)]
        args = [mix_in]
    in_specs += [full(w), tok(d), modspec, full(g2), modspec, modspec, full(rw_hi), full(rw_lo)]
    args += [w, x, gate, g2, sh2, sc2, rw_hi, rw_lo]
    out_shape = (jax.ShapeDtypeStruct((b, n, d), F32), jax.ShapeDtypeStruct((b, n, d), F32),
                 jax.ShapeDtypeStruct((b, n, LANES), F32), jax.ShapeDtypeStruct((b, N_EXPERTS, n), F32))
    out_specs = (tok(d), tok(d), tok(LANES), pl.BlockSpec((1, N_EXPERTS, tile), lambda bi, t: (bi, 0, t)))
    return pl.pallas_call(
        functools.partial(_mix_out_kernel, conv=conv, first=0, last=nt - 1),
        grid=(b, nt),
        in_specs=in_specs,
        out_specs=out_specs,
        out_shape=out_shape,
        compiler_params=_params("arbitrary", "arbitrary"),
        name="even_out" if conv else "odd_out",
    )(*args)


def _cumsum_rows(x):
    n = x.shape[0]
    r = lax.broadcasted_iota(jnp.int32, (CUMSUM_TILE, CUMSUM_TILE), 0)
    c = lax.broadcasted_iota(jnp.int32, (CUMSUM_TILE, CUMSUM_TILE), 1)
    tri = jnp.where(r >= c, 1.0, 0.0).astype(MXU_DTYPE)
    carry = jnp.zeros((1, x.shape[1]), F32)
    outs = []
    for k in range(n // CUMSUM_TILE):
        blk = x[k * CUMSUM_TILE:(k + 1) * CUMSUM_TILE].astype(MXU_DTYPE)
        loc = jnp.dot(tri, blk, preferred_element_type=F32) + carry
        outs.append(loc)
        carry = loc[CUMSUM_TILE - 1:CUMSUM_TILE]
    return jnp.concatenate(outs, axis=0)


def _select_kernel(at_ref, am_ref, idx_ref, *, cap):
    at = at_ref[0]
    am = am_ref[0]
    n_exp, n = at.shape
    capf = float(cap)

    def count_ge(thr):
        return jnp.sum(jnp.where(at >= thr, 1.0, 0.0), axis=1, keepdims=True)

    def bit_body(_, c):
        lo_i, hi_i = c
        mid = lo_i + ((hi_i - lo_i) >> 1)
        ge = count_ge(lax.bitcast_convert_type(mid, F32)) >= capf
        return jnp.where(ge, mid, lo_i), jnp.where(ge, hi_i, mid)

    lo_i, hi_i = lax.fori_loop(
        0, 31, bit_body, (jnp.zeros((n_exp, 1), jnp.int32), jnp.full((n_exp, 1), 0x3F800001, jnp.int32)))

    def val_body(_, c):
        lo, hi = c
        mid = 0.5 * (lo + hi)
        ge = count_ge(mid) >= capf
        return jnp.where(ge, mid, lo), jnp.where(ge, hi, mid)

    lo, hi = lax.fori_loop(
        0, 24, val_body, (lax.bitcast_convert_type(lo_i, F32), lax.bitcast_convert_type(hi_i, F32)))
    need = capf - count_ge(hi)

    eye = lax.broadcasted_iota(jnp.int32, (n_exp, LANES), 0) == lax.broadcasted_iota(jnp.int32, (n_exp, LANES), 1)
    to_row = lambda col: jnp.sum(jnp.where(eye, col, 0.0), axis=0, keepdims=True)
    lo_r, hi_r, need_r = to_row(lo), to_row(hi), to_row(need)
    above = jnp.where(am >= hi_r, 1.0, 0.0)
    band = jnp.where(am >= lo_r, 1.0, 0.0) - above
    sel = above + band * jnp.where(_cumsum_rows(band) <= need_r, 1.0, 0.0)
    rank = _cumsum_rows(sel)

    slot = lax.broadcasted_iota(jnp.int32, (1, LANES), 1).astype(F32)
    rows = []
    for ei in range(n_exp):
        col = rank[:, ei:ei + 1]
        parts = [jnp.sum(jnp.where(col <= slot + float(q * LANES), 1.0, 0.0), axis=0, keepdims=True)
                 for q in range(pl.cdiv(cap, LANES))]
        rows.append(jnp.concatenate(parts, axis=1)[:, :cap])
    idx_ref[0] = jnp.minimum(jnp.concatenate(rows, axis=0).astype(jnp.int32), n - 1)


def _select(aff_t, aff_m, cap):
    b, n_exp, n = aff_t.shape
    return pl.pallas_call(
        functools.partial(_select_kernel, cap=cap),
        grid=(b,),
        in_specs=[pl.BlockSpec((1, n_exp, n), lambda bi: (bi, 0, 0)), pl.BlockSpec((1, n, LANES), lambda bi: (bi, 0, 0))],
        out_specs=pl.BlockSpec((1, n_exp, cap), lambda bi: (bi, 0, 0)),
        out_shape=jax.ShapeDtypeStruct((b, n_exp, cap), jnp.int32),
        compiler_params=_params("arbitrary"),
        name="moe_select",
    )(aff_t, aff_m)


def _gather_kernel(idx_ref, src_ref, aff_ref, xe_ref, ge_ref, xcm_ref, *, cap, n_exp):
    base = (pl.program_id(0) * n_exp + pl.program_id(1)) * cap
    stride = cap + SUBLANES

    def body(gi, c):
        for i in range(SUBLANES):
            r = gi * SUBLANES + i
            t = idx_ref[base + r]
            xcm_ref[pl.ds(r, SUBLANES, stride=stride), :] = src_ref[0, pl.ds(pl.multiple_of(t * SUBLANES, SUBLANES), SUBLANES), :]
            ge_ref[0, 0, pl.ds(r, 1), :] = aff_ref[0, pl.ds(t, 1), :]
        return c

    lax.fori_loop(0, cap // SUBLANES, body, 0)
    xe_ref[0, 0] = jnp.concatenate(
        [xcm_ref[pl.ds(j * stride, cap), :] for j in range(SUBLANES)], axis=1).astype(xe_ref.dtype)


def _gather(idx_flat, h, aff_m, cap):
    b, n, d = h.shape
    assert d == SUBLANES * LANES and cap % SUBLANES == 0
    src = h.reshape(b, n * SUBLANES, LANES)
    return pl.pallas_call(
        functools.partial(_gather_kernel, cap=cap, n_exp=N_EXPERTS),
        grid=(b, N_EXPERTS),
        in_specs=[pl.BlockSpec(memory_space=pltpu.SMEM),
                  pl.BlockSpec((1, n * SUBLANES, LANES), lambda bi, ei: (bi, 0, 0)),
                  pl.BlockSpec((1, n, LANES), lambda bi, ei: (bi, 0, 0))],
        out_specs=(pl.BlockSpec((1, 1, cap, d), lambda bi, ei: (bi, ei, 0, 0)),
                   pl.BlockSpec((1, 1, cap, LANES), lambda bi, ei: (bi, ei, 0, 0))),
        out_shape=(jax.ShapeDtypeStruct((b, N_EXPERTS, cap, d), MXU_DTYPE),
                   jax.ShapeDtypeStruct((b, N_EXPERTS, cap, LANES), F32)),
        scratch_shapes=[pltpu.VMEM((SUBLANES * (cap + SUBLANES), LANES), F32)],
        compiler_params=_params("arbitrary", "arbitrary"),
        name="moe_gather",
    )(idx_flat, src, aff_m)


def _moe_ffn_kernel(x_ref, g_ref, wg_ref, wu_ref, wd_ref, o_ref):
    bb, _, cap, d = x_ref.shape
    x = x_ref[...].reshape(bb * cap, d)
    a = jnp.dot(x, wg_ref[0], preferred_element_type=F32)
    u = jnp.dot(x, wu_ref[0], preferred_element_type=F32)
    mid = (a * _sigmoid(a) * u).astype(MXU_DTYPE)
    y = jnp.dot(mid, wd_ref[0], preferred_element_type=F32)
    aff = g_ref[...].reshape(bb * cap, LANES)
    lane = lax.broadcasted_iota(jnp.int32, aff.shape, 1)
    g = jnp.sum(jnp.where(lane == pl.program_id(0), aff, 0.0), axis=1, keepdims=True)
    o_ref[...] = (y * g).reshape(bb, 1, cap, d)


def _moe_ffn(xe, ge, wg, wu, wd, bblk):
    b, e, cap, d = xe.shape
    ff = wg.shape[-1]
    tokspec = lambda width: pl.BlockSpec((bblk, 1, cap, width), lambda ei, bi: (bi, ei, 0, 0))
    return pl.pallas_call(
        _moe_ffn_kernel,
        grid=(e, b // bblk),
        in_specs=[tokspec(d), tokspec(LANES),
                  pl.BlockSpec((1, d, ff), lambda ei, bi: (ei, 0, 0)),
                  pl.BlockSpec((1, d, ff), lambda ei, bi: (ei, 0, 0)),
                  pl.BlockSpec((1, ff, d), lambda ei, bi: (ei, 0, 0))],
        out_specs=tokspec(d),
        out_shape=jax.ShapeDtypeStruct((b, e, cap, d), F32),
        compiler_params=_params("arbitrary", "arbitrary"),
        name="moe_ffn",
    )(xe, ge, wg, wu, wd)


COMBINE_UNROLL = 8


def _combine_kernel(idx_ref, y_ref, acc_ref, ycm_ref, *, cap, n_exp):
    ei = pl.program_id(1)
    base = (pl.program_id(0) * n_exp + ei) * cap
    stride = cap + SUBLANES

    @pl.when(ei == 0)
    def _():
        acc_ref[...] = jnp.zeros_like(acc_ref)

    y = y_ref[0, 0]
    for j in range(SUBLANES):
        ycm_ref[pl.ds(j * stride, cap), :] = y[:, j * LANES:(j + 1) * LANES]

    def body(gi, c):
        toks = [pl.multiple_of(idx_ref[base + gi * COMBINE_UNROLL + i] * SUBLANES, SUBLANES) for i in range(COMBINE_UNROLL)]
        new = [acc_ref[0, pl.ds(toks[i], SUBLANES), :] + ycm_ref[pl.ds(gi * COMBINE_UNROLL + i, SUBLANES, stride=stride), :]
               for i in range(COMBINE_UNROLL)]
        for i in range(COMBINE_UNROLL):
            acc_ref[0, pl.ds(toks[i], SUBLANES), :] = new[i]
        return c

    lax.fori_loop(0, cap // COMBINE_UNROLL, body, 0)


def _combine(idx_flat, y, n):
    b, n_exp, cap, d = y.shape
    assert cap % COMBINE_UNROLL == 0
    out = pl.pallas_call(
        functools.partial(_combine_kernel, cap=cap, n_exp=n_exp),
        grid=(b, n_exp),
        in_specs=[pl.BlockSpec(memory_space=pltpu.SMEM),
                  pl.BlockSpec((1, 1, cap, d), lambda bi, ei: (bi, ei, 0, 0))],
        out_specs=pl.BlockSpec((1, n * SUBLANES, LANES), lambda bi, ei: (bi, 0, 0)),
        out_shape=jax.ShapeDtypeStruct((b, n * SUBLANES, LANES), F32),
        scratch_shapes=[pltpu.VMEM((SUBLANES * (cap + SUBLANES), LANES), F32)],
        compiler_params=_params("arbitrary", "arbitrary"),
        name="moe_combine",
    )(idx_flat, y)
    return out.reshape(b, n, d)


def _expert_choice(h, aff_t, aff_m, wg, wu, wd, bblk):
    b, n, d = h.shape
    cap = max(1, CAPACITY_FACTOR * n // N_EXPERTS)
    idx_flat = _select(aff_t, aff_m, cap).reshape(-1)
    xe, ge = _gather(idx_flat, h, aff_m, cap)
    y = _moe_ffn(xe, ge, wg, wu, wd, bblk)
    return _combine(idx_flat, y, n)


def _odd_in_kernel(x_ref, moe_ref, gate_ref, g_ref, sh_ref, sc_ref, w_ref, xo_ref, gg_ref, u_ref):
    x = x_ref[0] + gate_ref[0] * moe_ref[0]
    xo_ref[0] = x
    h = _norm_mod(x, g_ref[...], sh_ref[0], sc_ref[0]).astype(MXU_DTYPE)
    y = jnp.dot(h, w_ref[...], preferred_element_type=F32)
    gl = y[:, :LRU_WIDTH]
    gg_ref[0] = 0.5 * gl * (1.0 + jnp.tanh(0.7978845608028654 * (gl + 0.044715 * gl * gl * gl)))
    u_ref[0] = y[:, LRU_WIDTH:]


def _odd_in(x, moe, gate, g, shift, scale, mod_row, w, tile):
    b, n, d = x.shape
    row = (lambda bi: bi) if mod_row is None else (lambda bi: mod_row)
    tok = pl.BlockSpec((1, tile, d), lambda bi, t: (bi, t, 0))
    modspec = pl.BlockSpec((1, 1, d), lambda bi, t: (row(bi), 0, 0))
    return pl.pallas_call(
        _odd_in_kernel,
        grid=(b, n // tile),
        in_specs=[tok, tok, modspec, pl.BlockSpec((1, d), lambda bi, t: (0, 0)), modspec, modspec,
                  pl.BlockSpec(w.shape, lambda bi, t: (0, 0))],
        out_specs=(tok, tok, tok),
        out_shape=(jax.ShapeDtypeStruct((b, n, d), F32),) * 3,
        compiler_params=_params("arbitrary", "arbitrary"),
        name="odd_in",
    )(x, moe, gate, g, shift, scale, w)


def _lru_kernel(*refs, reverse, nt):
    if reverse:
        (u_ref, up_ref, un_ref, cw_ref, cb_ref, wax_ref, ba_ref, bx_ref, lam_ref, h0_ref, hf_ref, gg_ref,
         out_ref, hlast_ref, a_scr, b_scr, carry_scr) = refs
    else:
        (u_ref, up_ref, un_ref, cw_ref, cb_ref, wax_ref, ba_ref, bx_ref, lam_ref, h0_ref,
         out_ref, hlast_ref, a_scr, b_scr, carry_scr) = refs
    tile, w = u_ref.shape[1], u_ref.shape[2]
    t = pl.program_id(1)
    pos = (nt - 1 - t) if reverse else t
    prev = jnp.where(pos == 0, 0.0, up_ref[0])
    nxt = jnp.where(pos == nt - 1, 0.0, un_ref[0])
    ext = jnp.concatenate([prev, u_ref[0], nxt], axis=0)
    cw = cw_ref[...]
    uc = cb_ref[...]
    for k in range(cw.shape[0]):
        uc = uc + cw[k:k + 1] * ext[HALO - 2 + k:HALO - 2 + k + tile]
    ucb = uc.astype(MXU_DTYPE)
    lam = lam_ref[0]
    softplus_neg = jnp.maximum(-lam, 0.0) + jnp.log1p(jnp.exp(-jnp.abs(lam)))
    for hd in range(LRU_HEADS):
        sl = slice(hd * LRU_BLOCK, (hd + 1) * LRU_BLOCK)
        z = jnp.dot(ucb[:, sl], wax_ref[0, hd], preferred_element_type=F32)
        r = _sigmoid(z[:, :LRU_BLOCK] + ba_ref[0][:, sl])
        ig = _sigmoid(z[:, LRU_BLOCK:] + bx_ref[0][:, sl])
        log_a = -LRU_C * r * softplus_neg[:, sl]
        a = jnp.exp(log_a)
        mult = jnp.sqrt(-jnp.tanh(log_a) * (a * a + 1.0))
        a_scr[:, sl] = a
        b_scr[:, sl] = mult * (ig * uc[:, sl])

    @pl.when(t == 0)
    def _():
        carry_scr[...] = jnp.broadcast_to(h0_ref[0], carry_scr.shape)

    rowid = lax.broadcasted_iota(jnp.int32, (HALO, w), 0)

    def body(j, carry):
        jj = (tile // HALO - 1 - j) if reverse else j
        off = pl.multiple_of(jj * HALO, HALO)
        a = a_scr[pl.ds(off, HALO), :]
        bcoef = b_scr[pl.ds(off, HALO), :]
        for dist in (1, 2, 4):
            shift = (HALO - dist) if reverse else dist
            m = (rowid < HALO - dist) if reverse else (rowid >= dist)
            a_s = pltpu.roll(a, shift, 0)
            b_s = pltpu.roll(bcoef, shift, 0)
            bcoef = jnp.where(m, a * b_s + bcoef, bcoef)
            a = jnp.where(m, a * a_s, a)
        hcur = a * carry + bcoef
        if reverse:
            out_ref[0, pl.ds(off, HALO), :] = (
                gg_ref[0, pl.ds(off, HALO), :] * (hf_ref[0, pl.ds(off, HALO), :] + hcur)).astype(out_ref.dtype)
            return jnp.broadcast_to(hcur[0:1], carry.shape)
        out_ref[0, pl.ds(off, HALO), :] = hcur
        return jnp.broadcast_to(hcur[HALO - 1:HALO], carry.shape)

    carry = lax.fori_loop(0, tile // HALO, body, carry_scr[...])
    carry_scr[...] = carry
    hlast_ref[0] = carry[0:1]


def _lru(u, cw, cb, wax, ba, bx, lam, h0, direction, tile, hf=None, gg=None):
    b, n, w = u.shape
    nt = n // tile
    per = tile // HALO
    reverse = direction == 1
    pos = (lambda t: nt - 1 - t) if reverse else (lambda t: t)
    tok = pl.BlockSpec((1, tile, w), lambda bi, t: (bi, pos(t), 0))
    rowspec = pl.BlockSpec((1, 1, w), lambda bi, t: (direction, 0, 0))
    in_specs = [
        tok,
        pl.BlockSpec((1, HALO, w), lambda bi, t: (bi, jnp.maximum(pos(t) * per - 1, 0), 0)),
        pl.BlockSpec((1, HALO, w), lambda bi, t: (bi, jnp.minimum((pos(t) + 1) * per, n // HALO - 1), 0)),
        pl.BlockSpec(cw.shape, lambda bi, t: (0, 0)),
        pl.BlockSpec(cb.shape, lambda bi, t: (0, 0)),
        pl.BlockSpec((1,) + wax.shape[1:], lambda bi, t: (direction, 0, 0, 0)),
        rowspec, rowspec, rowspec,
        pl.BlockSpec((1, 1, w), lambda bi, t: (bi, 0, 0)),
    ]
    args = [u, u, u, cw, cb, wax, ba, bx, lam, h0]
    if reverse:
        in_specs += [tok, tok]
        args += [hf, gg]
    out_dtype = MXU_DTYPE if reverse else F32
    return pl.pallas_call(
        functools.partial(_lru_kernel, reverse=reverse, nt=nt),
        grid=(b, nt),
        in_specs=in_specs,
        out_specs=(tok, pl.BlockSpec((1, 1, w), lambda bi, t: (bi, 0, 0))),
        out_shape=(jax.ShapeDtypeStruct((b, n, w), out_dtype), jax.ShapeDtypeStruct((b, 1, w), F32)),
        scratch_shapes=[pltpu.VMEM((tile, w), F32), pltpu.VMEM((tile, w), F32), pltpu.VMEM((HALO, w), F32)],
        compiler_params=_params("arbitrary", "arbitrary"),
        name="lru_bwd" if reverse else "lru_fwd",
    )(*args)


def _final_kernel(x_ref, moe_ref, gate_ref, g_ref, o_ref):
    x = x_ref[0] + gate_ref[0] * moe_ref[0]
    o_ref[0] = x * lax.rsqrt(jnp.mean(x * x, axis=-1, keepdims=True) + EPS) * g_ref[...]


def _final(x, moe, gate, g, tile):
    b, n, d = x.shape
    tok = pl.BlockSpec((1, tile, d), lambda bi, t: (bi, t, 0))
    return pl.pallas_call(
        _final_kernel,
        grid=(b, n // tile),
        in_specs=[tok, tok, pl.BlockSpec((1, 1, d), lambda bi, t: (bi, 0, 0)), pl.BlockSpec((1, d), lambda bi, t: (0, 0))],
        out_specs=tok,
        out_shape=jax.ShapeDtypeStruct((b, n, d), F32),
        compiler_params=_params("arbitrary", "arbitrary"),
        name="final_norm",
    )(x, moe, gate, g)


def kernel(x, c, ctx, c_ctx, ada_w, ada_b, norm_mix_g, norm_ffn_g, ev_w_in, ev_w_out, ev_sink, ev_conv_w, ev_conv_b, od_w_in, od_w_out, od_conv_w, od_conv_b, od_wa, od_ba, od_wx, od_bx, od_lambda, router_w, w_gate, w_up, w_down, final_g):
    b, n, d = x.shape
    lc = ctx.shape[1]
    depth = ada_w.shape[0]
    assert depth == 2 and d == D_MODEL and b < MOD_ROWS
    tile_l = min(512, n)
    tile_c = lc
    ctx_row = b

    cvec = jnp.zeros((MOD_ROWS, d), F32).at[:b].set(c).at[b].set(c_ctx)
    mods = _ada(cvec, ada_w, ada_b).reshape(depth, MOD_ROWS, 6, 1, d)
    mod = lambda l, j: mods[l, :, j]

    def router_split(l):
        return _split_hi_lo(jnp.pad(router_w[l], ((0, 0), (0, LANES - N_EXPERTS))))

    bf = lambda a: a.astype(MXU_DTYPE)

    g_mix = norm_mix_g[0].reshape(1, d)
    g_ffn = norm_ffn_g[0].reshape(1, d)
    w_in = bf(ev_w_in[0])
    w_out = bf(ev_w_out[0])
    conv_p = (ev_conv_w[0], ev_conv_b[0].reshape(1, -1))
    tables = _rope_tables(n)
    ql, kvl, gbl, cul = _even_in(x, g_mix, mod(0, 0), mod(0, 1), None, w_in, tables, tile_l)
    qc, kvc, gbc, cuc = _even_in(ctx, g_mix, mod(0, 0), mod(0, 1), ctx_row, w_in, None, tile_c)
    att_l = _attention(ev_sink[0], ql, kvl, kvc)
    att_c = _attention(ev_sink[0], qc, None, kvc)
    rw_hi, rw_lo = router_split(0)
    xl, hl, am_l, at_l = _mix_out((att_l, gbl, cul), conv_p, w_out, x, mod(0, 2), g_ffn, mod(0, 3), mod(0, 4), None,
                                  rw_hi, rw_lo, tile_l)
    xc, hc, am_c, at_c = _mix_out((att_c, gbc, cuc), conv_p, w_out, ctx, mod(0, 2), g_ffn, mod(0, 3), mod(0, 4),
                                  ctx_row, rw_hi, rw_lo, tile_c)
    wg, wu, wd = bf(w_gate[0]), bf(w_up[0]), bf(w_down[0])
    moe_l = _expert_choice(hl, at_l, am_l, wg, wu, wd, 1)
    moe_c = _expert_choice(hc, at_c, am_c, wg, wu, wd, b)

    g_mix = norm_mix_g[1].reshape(1, d)
    g_ffn = norm_ffn_g[1].reshape(1, d)
    w_in = bf(od_w_in[0])
    w_out = bf(od_w_out[0])
    xl, ggl, ul = _odd_in(xl, moe_l, mod(0, 5), g_mix, mod(1, 0), mod(1, 1), None, w_in, tile_l)
    _, _, uc = _odd_in(xc, moe_c, mod(0, 5), g_mix, mod(1, 0), mod(1, 1), ctx_row, w_in, tile_c)
    cw, cb = od_conv_w[0], od_conv_b[0].reshape(1, -1)
    wax = bf(jnp.concatenate([od_wa[0], od_wx[0]], axis=-1))
    ba, bx, lam = (a[0].reshape(2, 1, -1) for a in (od_ba, od_bx, od_lambda))
    zero_state = jnp.zeros((b, 1, LRU_WIDTH), F32)
    hf_c, h0_f = _lru(uc, cw, cb, wax, ba, bx, lam, zero_state, 0, tile_c)
    _, h0_b = _lru(uc, cw, cb, wax, ba, bx, lam, zero_state, 1, tile_c, hf=hf_c, gg=hf_c)
    hf_l, _ = _lru(ul, cw, cb, wax, ba, bx, lam, h0_f, 0, tile_l)
    yl, _ = _lru(ul, cw, cb, wax, ba, bx, lam, h0_b, 1, tile_l, hf=hf_l, gg=ggl)
    rw_hi, rw_lo = router_split(1)
    xl, hl, am_l, at_l = _mix_out(yl, None, w_out, xl, mod(1, 2), g_ffn, mod(1, 3), mod(1, 4), None,
                                  rw_hi, rw_lo, tile_l)
    moe_l = _expert_choice(hl, at_l, am_l, bf(w_gate[1]), bf(w_up[1]), bf(w_down[1]), 1)
    return _final(xl, moe_l, mod(1, 5), final_g.reshape(1, d), tile_l)
```

```python
import functools

import jax
import jax.numpy as jnp
from jax import lax
from jax.experimental import pallas as pl
from jax.experimental.pallas import tpu as pltpu

F32 = jnp.float32
MXU_DTYPE = jnp.bfloat16

D_MODEL = 1024
GRID_W = 64
EPS = 1e-6
NEG_INF = -1e30
HEAD_DIM = 64
N_Q_HEADS = 8
N_KV_HEADS = 2
Q_PER_KV = N_Q_HEADS // N_KV_HEADS
ATTN_WIDTH = N_Q_HEADS * HEAD_DIM
KV_WIDTH = N_KV_HEADS * HEAD_DIM
WINDOW = 128
BLOCK = 128
ROPE_BASE = 10000.0
CONV_B_WIDTH = D_MODEL // 2
EVEN_IN = ATTN_WIDTH + 2 * KV_WIDTH + 3 * CONV_B_WIDTH
LRU_WIDTH = D_MODEL
LRU_HEADS = 8
LRU_BLOCK = LRU_WIDTH // LRU_HEADS
LRU_C = 8.0
N_EXPERTS = 16
CAPACITY_FACTOR = 2
EXPERT_FF = 1408
MOD_ROWS = 16
LANES = 128
SUBLANES = 8
HALO = SUBLANES
CUMSUM_TILE = 256
V7X_VMEM_LIMIT = 56 * 1024 * 1024


def _params(*sem):
    return pltpu.CompilerParams(dimension_semantics=sem, vmem_limit_bytes=V7X_VMEM_LIMIT)


def _split_hi_lo(a):
    hi = a.astype(MXU_DTYPE)
    lo = (a - hi.astype(F32)).astype(MXU_DTYPE)
    return hi, lo


def _sigmoid(z):
    return 0.5 * (1.0 + jnp.tanh(0.5 * z))


def _store_token_tiles(ref, val):
    tile = val.shape[0]
    for j in range(SUBLANES):
        ref[0, pl.ds(j, tile, stride=SUBLANES), :] = val[:, j * LANES:(j + 1) * LANES]


def _load_token_tiles(ref):
    tile = ref.shape[1] // SUBLANES
    return jnp.concatenate([ref[0, pl.ds(j, tile, stride=SUBLANES), :] for j in range(SUBLANES)], axis=1)


def _slab_spec(rows, index_map, width=LRU_WIDTH):
    return pl.BlockSpec((1, width // LANES, rows, LANES), index_map)


def _store_slabs(ref, val):
    for c in range(val.shape[1] // LANES):
        ref[0, c] = val[:, c * LANES:(c + 1) * LANES]


def _load_slabs(ref):
    return jnp.concatenate([ref[0, c] for c in range(ref.shape[1])], axis=1)


def _norm_mod(x, g, shift, scale):
    y = x * lax.rsqrt(jnp.mean(x * x, axis=-1, keepdims=True) + EPS) * g
    return y * (1.0 + scale) + shift


def _ada_kernel(c_ref, w_ref, b_ref, o_ref):
    c = c_ref[...]
    s_hi, s_lo = _split_hi_lo(c * _sigmoid(c))
    w_hi, w_lo = _split_hi_lo(w_ref[0])
    acc = jnp.dot(s_hi, w_hi, preferred_element_type=F32)
    acc += jnp.dot(s_hi, w_lo, preferred_element_type=F32)
    acc += jnp.dot(s_lo, w_hi, preferred_element_type=F32)
    o_ref[0] = acc + b_ref[0]


def _ada(cvec, ada_w, ada_b):
    depth, d, n6 = ada_w.shape
    tn = 1536
    return pl.pallas_call(
        _ada_kernel,
        grid=(depth, n6 // tn),
        in_specs=[
            pl.BlockSpec((MOD_ROWS, d), lambda l, j: (0, 0)),
            pl.BlockSpec((1, d, tn), lambda l, j: (l, 0, j)),
            pl.BlockSpec((1, 1, tn), lambda l, j: (l, 0, j)),
        ],
        out_specs=pl.BlockSpec((1, MOD_ROWS, tn), lambda l, j: (l, 0, j)),
        out_shape=jax.ShapeDtypeStruct((depth, MOD_ROWS, n6), F32),
        compiler_params=_params("arbitrary", "arbitrary"),
        name="ada",
    )(cvec, ada_w, ada_b.reshape(depth, 1, n6))


def _even_in_kernel(*refs, rope):
    if rope:
        (x_ref, g_ref, sh_ref, sc_ref, w_ref, cos_ref, sa_ref, sb_ref, q_ref, kv_ref, gb_ref, cu_ref) = refs
    else:
        (x_ref, g_ref, sh_ref, sc_ref, w_ref, q_ref, kv_ref, gb_ref, cu_ref) = refs
    h = _norm_mod(x_ref[0], g_ref[...], sh_ref[0], sc_ref[0]).astype(MXU_DTYPE)
    y = jnp.dot(h, w_ref[...], preferred_element_type=F32)
    q = y[:, :ATTN_WIDTH]
    k = y[:, ATTN_WIDTH:ATTN_WIDTH + KV_WIDTH]
    v = y[:, ATTN_WIDTH + KV_WIDTH:ATTN_WIDTH + 2 * KV_WIDTH]
    c0 = ATTN_WIDTH + 2 * KV_WIDTH
    if rope:
        cos, sa, sb = cos_ref[...], sa_ref[...], sb_ref[...]

        def rot(z):
            return z * cos + pltpu.roll(z, 16, 1) * sa + pltpu.roll(z, 112, 1) * sb

        q = jnp.concatenate([rot(q[:, j * 128:(j + 1) * 128]) for j in range(ATTN_WIDTH // 128)], axis=1)
        k = rot(k)
    q_ref[0] = (q * (HEAD_DIM ** -0.5)).astype(q_ref.dtype)
    kv_ref[0] = jnp.concatenate([k, v], axis=1).astype(kv_ref.dtype)
    gb_ref[0] = y[:, c0:c0 + CONV_B_WIDTH]
    cu_ref[0] = y[:, c0 + CONV_B_WIDTH:c0 + 2 * CONV_B_WIDTH] * y[:, c0 + 2 * CONV_B_WIDTH:]


def _even_in(x, g, shift, scale, mod_row, w, tables, tile):
    b, n, d = x.shape
    nt = n // tile
    rope = tables is not None
    row = (lambda bi: bi) if mod_row is None else (lambda bi: mod_row)
    in_specs = [
        pl.BlockSpec((1, tile, d), lambda bi, t: (bi, t, 0)),
        pl.BlockSpec((1, d), lambda bi, t: (0, 0)),
        pl.BlockSpec((1, 1, d), lambda bi, t: (row(bi), 0, 0)),
        pl.BlockSpec((1, 1, d), lambda bi, t: (row(bi), 0, 0)),
        pl.BlockSpec(w.shape, lambda bi, t: (0, 0)),
    ]
    args = [x, g, shift, scale, w]
    if rope:
        in_specs += [pl.BlockSpec((tile, 128), lambda bi, t: (t, 0))] * 3
        args += list(tables)
    out_shape = (
        jax.ShapeDtypeStruct((b, n, ATTN_WIDTH), MXU_DTYPE),
        jax.ShapeDtypeStruct((b, n, 2 * KV_WIDTH), MXU_DTYPE),
        jax.ShapeDtypeStruct((b, n, CONV_B_WIDTH), F32),
        jax.ShapeDtypeStruct((b, n, CONV_B_WIDTH), F32),
    )
    out_specs = tuple(pl.BlockSpec((1, tile, s.shape[-1]), lambda bi, t: (bi, t, 0)) for s in out_shape)
    return pl.pallas_call(
        functools.partial(_even_in_kernel, rope=rope),
        grid=(b, nt),
        in_specs=in_specs,
        out_specs=out_specs,
        out_shape=out_shape,
        compiler_params=_params("arbitrary", "arbitrary"),
        name="even_in_rope" if rope else "even_in",
    )(*args)


def _rope_tables(n):
    nf = HEAD_DIM // 4
    pos = jnp.arange(n)
    rows = (pos // GRID_W).astype(F32)
    cols = (pos % GRID_W).astype(F32)
    lane = jnp.arange(128)
    inv = ROPE_BASE ** (-(lane % nf).astype(F32) / nf)
    use_col = (lane % HEAD_DIM) >= HEAD_DIM // 2
    ang = jnp.where(use_col[None, :], cols[:, None], rows[:, None]) * inv[None, :]
    cos, sin = jnp.cos(ang), jnp.sin(ang)
    second = ((lane % (2 * nf)) >= nf)[None, :]
    return cos, jnp.where(second, sin, 0.0), jnp.where(second, 0.0, -sin)


def _attn_kernel(*refs, n, has_local):
    if has_local:
        sink_ref, q_ref, kv_ref, kvc_ref, o_ref = refs
    else:
        sink_ref, q_ref, kvc_ref, o_ref = refs
    i = pl.program_id(1)
    q = q_ref[0]
    kvall = kvc_ref[0]
    n_loc = 3 * BLOCK
    if has_local:
        start = pl.multiple_of(jnp.clip(i * BLOCK - BLOCK, 0, n - n_loc), BLOCK)
        kvall = jnp.concatenate([kv_ref[0, pl.ds(start, n_loc), :], kvall], axis=0)
        qpos = i * BLOCK + lax.broadcasted_iota(jnp.int32, (Q_PER_KV * BLOCK, n_loc), 0) % BLOCK
        kpos = start + lax.broadcasted_iota(jnp.int32, (Q_PER_KV * BLOCK, n_loc), 1)
        valid = jnp.abs(kpos - qpos) <= WINDOW
    grp = lax.broadcasted_iota(jnp.int32, (Q_PER_KV * BLOCK, 1), 0) // BLOCK
    outs = []
    for hk in range(N_KV_HEADS):
        kh = kvall[:, hk * HEAD_DIM:(hk + 1) * HEAD_DIM]
        vh = kvall[:, KV_WIDTH + hk * HEAD_DIM:KV_WIDTH + (hk + 1) * HEAD_DIM]
        qg = jnp.concatenate(
            [q[:, (hk * Q_PER_KV + g) * HEAD_DIM:(hk * Q_PER_KV + g + 1) * HEAD_DIM] for g in range(Q_PER_KV)], axis=0)
        s = lax.dot_general(qg, kh, (((1,), (1,)), ((), ())), preferred_element_type=F32)
        if has_local:
            s = jnp.concatenate([jnp.where(valid, s[:, :n_loc], NEG_INF), s[:, n_loc:]], axis=1)
        snk = jnp.zeros((Q_PER_KV * BLOCK, 1), F32)
        for g in range(Q_PER_KV):
            snk = jnp.where(grp == g, sink_ref[hk * Q_PER_KV + g], snk)
        m = jnp.maximum(jnp.max(s, axis=1, keepdims=True), snk)
        p = jnp.exp(s - m)
        denom = jnp.sum(p, axis=1, keepdims=True) + jnp.exp(snk - m)
        o = jnp.dot(p.astype(MXU_DTYPE), vh, preferred_element_type=F32) / denom
        outs += [o[g * BLOCK:(g + 1) * BLOCK] for g in range(Q_PER_KV)]
    o_ref[0] = jnp.concatenate(outs, axis=1).astype(o_ref.dtype)


def _attention(sink, q, kv, kvc):
    b, n, _ = q.shape
    lc = kvc.shape[1]
    has_local = kv is not None
    in_specs = [pl.BlockSpec(memory_space=pltpu.SMEM), pl.BlockSpec((1, BLOCK, ATTN_WIDTH), lambda bi, i: (bi, i, 0))]
    args = [sink, q]
    if has_local:
        in_specs.append(pl.BlockSpec((1, n, 2 * KV_WIDTH), lambda bi, i: (bi, 0, 0)))
        args.append(kv)
    in_specs.append(pl.BlockSpec((1, lc, 2 * KV_WIDTH), lambda bi, i: (bi, 0, 0)))
    args.append(kvc)
    return pl.pallas_call(
        functools.partial(_attn_kernel, n=n, has_local=has_local),
        grid=(b, n // BLOCK),
        in_specs=in_specs,
        out_specs=pl.BlockSpec((1, BLOCK, ATTN_WIDTH), lambda bi, i: (bi, i, 0)),
        out_shape=jax.ShapeDtypeStruct((b, n, ATTN_WIDTH), MXU_DTYPE),
        compiler_params=_params("arbitrary", "arbitrary"),
        name="attn_local" if has_local else "attn_ctx",
    )(*args)


def _mix_out_kernel(*refs, conv, first, last):
    if conv:
        (att_ref, gb_ref, cu_ref, cup_ref, cun_ref, cw_ref, cb_ref,
         w_ref, x_ref, gate_ref, g2_ref, sh2_ref, sc2_ref, rwh_ref, rwl_ref, xo_ref, h_ref, aff_ref, afft_ref) = refs
        tile = cu_ref.shape[1]
        t = pl.program_id(1)
        prev = jnp.where(t == first, 0.0, cup_ref[0])
        nxt = jnp.where(t == last, 0.0, cun_ref[0])
        ext = jnp.concatenate([prev, cu_ref[0], nxt], axis=0)
        cw = cw_ref[...]
        cv = (cw[0:1] * ext[HALO - 1:HALO - 1 + tile] + cw[1:2] * ext[HALO:HALO + tile]
              + cw[2:3] * ext[HALO + 1:HALO + 1 + tile] + cb_ref[...])
        cat = jnp.concatenate([att_ref[0], (gb_ref[0] * cv).astype(MXU_DTYPE)], axis=1)
    else:
        (y_ref, w_ref, x_ref, gate_ref, g2_ref, sh2_ref, sc2_ref, rwh_ref, rwl_ref, xo_ref, h_ref, aff_ref, afft_ref) = refs
        cat = _load_slabs(y_ref).astype(MXU_DTYPE)
    y = jnp.dot(cat, w_ref[...], preferred_element_type=F32)
    x = x_ref[0] + gate_ref[0] * y
    xo_ref[0] = x
    h = _norm_mod(x, g2_ref[...], sh2_ref[0], sc2_ref[0])
    _store_token_tiles(h_ref, h)
    h_hi, h_lo = _split_hi_lo(h)
    logits = jnp.dot(h_hi, rwh_ref[...], preferred_element_type=F32)
    logits += jnp.dot(h_hi, rwl_ref[...], preferred_element_type=F32)
    logits += jnp.dot(h_lo, rwh_ref[...], preferred_element_type=F32)
    lane = lax.broadcasted_iota(jnp.int32, logits.shape, 1)
    logits = jnp.where(lane < N_EXPERTS, logits, NEG_INF)
    e = jnp.exp(logits - jnp.max(logits, axis=1, keepdims=True))
    aff = e / jnp.sum(e, axis=1, keepdims=True)
    aff_ref[0] = aff
    afft_ref[0] = aff.T[:N_EXPERTS]


def _mix_out(mix_in, conv_params, w, x, gate, g2, sh2, sc2, mod_row, rw_hi, rw_lo, tile):
    b, n, d = x.shape
    nt = n // tile
    conv = conv_params is not None
    row = (lambda bi: bi) if mod_row is None else (lambda bi: mod_row)
    tok = lambda width: pl.BlockSpec((1, tile, width), lambda bi, t: (bi, t, 0))
    modspec = pl.BlockSpec((1, 1, d), lambda bi, t: (row(bi), 0, 0))
    full = lambda a: pl.BlockSpec(a.shape, lambda bi, t: (0,) * a.ndim)
    if conv:
        att, gb, cu = mix_in
        cw, cb = conv_params
        per = tile // HALO
        in_specs = [tok(ATTN_WIDTH), tok(CONV_B_WIDTH), tok(CONV_B_WIDTH),
                    pl.BlockSpec((1, HALO, CONV_B_WIDTH), lambda bi, t: (bi, jnp.maximum(t * per - 1, 0), 0)),
                    pl.BlockSpec((1, HALO, CONV_B_WIDTH), lambda bi, t: (bi, jnp.minimum((t + 1) * per, n // HALO - 1), 0)),
                    full(cw), full(cb)]
        args = [att, gb, cu, cu, cu, cw, cb]
    else:
        in_specs = [_slab_spec(tile, lambda bi, t: (bi, 0, t, 0), d)]
        args = [mix_in]
    in_specs += [full(w), tok(d), modspec, full(g2), modspec, modspec, full(rw_hi), full(rw_lo)]
    args += [w, x, gate, g2, sh2, sc2, rw_hi, rw_lo]
    out_shape = (jax.ShapeDtypeStruct((b, n, d), F32), jax.ShapeDtypeStruct((b, n * SUBLANES, LANES), F32),
                 jax.ShapeDtypeStruct((b, n, LANES), F32), jax.ShapeDtypeStruct((b, N_EXPERTS, n), F32))
    out_specs = (tok(d), pl.BlockSpec((1, tile * SUBLANES, LANES), lambda bi, t: (bi, t, 0)), tok(LANES),
                 pl.BlockSpec((1, N_EXPERTS, tile), lambda bi, t: (bi, 0, t)))
    return pl.pallas_call(
        functools.partial(_mix_out_kernel, conv=conv, first=0, last=nt - 1),
        grid=(b, nt),
        in_specs=in_specs,
        out_specs=out_specs,
        out_shape=out_shape,
        compiler_params=_params("arbitrary", "arbitrary"),
        name="even_out" if conv else "odd_out",
    )(*args)


def _cumsum_rows(x):
    n = x.shape[0]
    r = lax.broadcasted_iota(jnp.int32, (CUMSUM_TILE, CUMSUM_TILE), 0)
    c = lax.broadcasted_iota(jnp.int32, (CUMSUM_TILE, CUMSUM_TILE), 1)
    tri = jnp.where(r >= c, 1.0, 0.0).astype(MXU_DTYPE)
    carry = jnp.zeros((1, x.shape[1]), F32)
    outs = []
    for k in range(n // CUMSUM_TILE):
        blk = x[k * CUMSUM_TILE:(k + 1) * CUMSUM_TILE].astype(MXU_DTYPE)
        loc = jnp.dot(tri, blk, preferred_element_type=F32) + carry
        outs.append(loc)
        carry = loc[CUMSUM_TILE - 1:CUMSUM_TILE]
    return jnp.concatenate(outs, axis=0)


def _select_kernel(at_ref, am_ref, idx_ref, *, cap):
    at = at_ref[0]
    am = am_ref[0]
    n_exp, n = at.shape
    capf = float(cap)

    def count_ge(thr):
        return jnp.sum(jnp.where(at >= thr, 1.0, 0.0), axis=1, keepdims=True)

    def bit_body(_, c):
        lo_i, hi_i = c
        mid = lo_i + ((hi_i - lo_i) >> 1)
        ge = count_ge(lax.bitcast_convert_type(mid, F32)) >= capf
        return jnp.where(ge, mid, lo_i), jnp.where(ge, hi_i, mid)

    lo_i, hi_i = lax.fori_loop(
        0, 31, bit_body, (jnp.zeros((n_exp, 1), jnp.int32), jnp.full((n_exp, 1), 0x3F800001, jnp.int32)))

    def val_body(_, c):
        lo, hi = c
        mid = 0.5 * (lo + hi)
        ge = count_ge(mid) >= capf
        return jnp.where(ge, mid, lo), jnp.where(ge, hi, mid)

    lo, hi = lax.fori_loop(
        0, 24, val_body, (lax.bitcast_convert_type(lo_i, F32), lax.bitcast_convert_type(hi_i, F32)))
    need = capf - count_ge(hi)

    eye = lax.broadcasted_iota(jnp.int32, (n_exp, LANES), 0) == lax.broadcasted_iota(jnp.int32, (n_exp, LANES), 1)
    to_row = lambda col: jnp.sum(jnp.where(eye, col, 0.0), axis=0, keepdims=True)
    lo_r, hi_r, need_r = to_row(lo), to_row(hi), to_row(need)
    above = jnp.where(am >= hi_r, 1.0, 0.0)
    band = jnp.where(am >= lo_r, 1.0, 0.0) - above
    sel = above + band * jnp.where(_cumsum_rows(band) <= need_r, 1.0, 0.0)
    rank = _cumsum_rows(sel)

    slot = lax.broadcasted_iota(jnp.int32, (1, LANES), 1).astype(F32)
    rows = []
    for ei in range(n_exp):
        col = rank[:, ei:ei + 1]
        parts = [jnp.sum(jnp.where(col <= slot + float(q * LANES), 1.0, 0.0), axis=0, keepdims=True)
                 for q in range(pl.cdiv(cap, LANES))]
        rows.append(jnp.concatenate(parts, axis=1)[:, :cap])
    idx_ref[0] = jnp.minimum(jnp.concatenate(rows, axis=0).astype(jnp.int32), n - 1)


def _select(aff_t, aff_m, cap):
    b, n_exp, n = aff_t.shape
    return pl.pallas_call(
        functools.partial(_select_kernel, cap=cap),
        grid=(b,),
        in_specs=[pl.BlockSpec((1, n_exp, n), lambda bi: (bi, 0, 0)), pl.BlockSpec((1, n, LANES), lambda bi: (bi, 0, 0))],
        out_specs=pl.BlockSpec((1, n_exp, cap), lambda bi: (bi, 0, 0)),
        out_shape=jax.ShapeDtypeStruct((b, n_exp, cap), jnp.int32),
        compiler_params=_params("arbitrary"),
        name="moe_select",
    )(aff_t, aff_m)


def _gather_kernel(idx_ref, src_ref, aff_ref, xe_ref, ge_ref, xcm_ref, *, cap, n_exp):
    base = (pl.program_id(0) * n_exp + pl.program_id(1)) * cap
    stride = cap + SUBLANES

    def body(gi, c):
        for i in range(SUBLANES):
            r = gi * SUBLANES + i
            t = idx_ref[base + r]
            xcm_ref[pl.ds(r, SUBLANES, stride=stride), :] = src_ref[0, pl.ds(pl.multiple_of(t * SUBLANES, SUBLANES), SUBLANES), :]
            ge_ref[0, 0, pl.ds(r, 1), :] = aff_ref[0, pl.ds(t, 1), :]
        return c

    lax.fori_loop(0, cap // SUBLANES, body, 0)
    xe_ref[0, 0] = jnp.concatenate(
        [xcm_ref[pl.ds(j * stride, cap), :] for j in range(SUBLANES)], axis=1).astype(xe_ref.dtype)


def _gather(idx_flat, src, aff_m, cap):
    b, n, _ = aff_m.shape
    d = SUBLANES * LANES
    assert src.shape == (b, n * SUBLANES, LANES) and cap % SUBLANES == 0
    return pl.pallas_call(
        functools.partial(_gather_kernel, cap=cap, n_exp=N_EXPERTS),
        grid=(b, N_EXPERTS),
        in_specs=[pl.BlockSpec(memory_space=pltpu.SMEM),
                  pl.BlockSpec((1, n * SUBLANES, LANES), lambda bi, ei: (bi, 0, 0)),
                  pl.BlockSpec((1, n, LANES), lambda bi, ei: (bi, 0, 0))],
        out_specs=(pl.BlockSpec((1, 1, cap, d), lambda bi, ei: (bi, ei, 0, 0)),
                   pl.BlockSpec((1, 1, cap, LANES), lambda bi, ei: (bi, ei, 0, 0))),
        out_shape=(jax.ShapeDtypeStruct((b, N_EXPERTS, cap, d), MXU_DTYPE),
                   jax.ShapeDtypeStruct((b, N_EXPERTS, cap, LANES), F32)),
        scratch_shapes=[pltpu.VMEM((SUBLANES * (cap + SUBLANES), LANES), F32)],
        compiler_params=_params("arbitrary", "arbitrary"),
        name="moe_gather",
    )(idx_flat, src, aff_m)


def _moe_ffn_kernel(x_ref, g_ref, wg_ref, wu_ref, wd_ref, o_ref, wg_s, wu_s, wd_s):
    bb, _, cap, d = x_ref.shape

    @pl.when(pl.program_id(1) == 0)
    def _():
        wg_s[...] = wg_ref[0, 0].astype(wg_s.dtype)
        wu_s[...] = wu_ref[0, 0].astype(wu_s.dtype)
        wd_s[...] = wd_ref[0, 0].astype(wd_s.dtype)

    x = x_ref[...].reshape(bb * cap, d)
    a = jnp.dot(x, wg_s[...], preferred_element_type=F32)
    u = jnp.dot(x, wu_s[...], preferred_element_type=F32)
    mid = (a * _sigmoid(a) * u).astype(MXU_DTYPE)
    y = jnp.dot(mid, wd_s[...], preferred_element_type=F32)
    aff = g_ref[...].reshape(bb * cap, LANES)
    lane = lax.broadcasted_iota(jnp.int32, aff.shape, 1)
    g = jnp.sum(jnp.where(lane == pl.program_id(0), aff, 0.0), axis=1, keepdims=True)
    o_ref[...] = (y * g).reshape(bb, 1, cap, d)


def _moe_ffn(xe, ge, layer, wg, wu, wd, bblk):
    b, e, cap, d = xe.shape
    ff = wg.shape[-1]
    tokspec = lambda width: pl.BlockSpec((bblk, 1, cap, width), lambda ei, bi: (bi, ei, 0, 0))
    wspec = lambda r, c: pl.BlockSpec((1, 1, r, c), lambda ei, bi: (layer, ei, 0, 0), pipeline_mode=pl.Buffered(1))
    return pl.pallas_call(
        _moe_ffn_kernel,
        grid=(e, b // bblk),
        in_specs=[tokspec(d), tokspec(LANES), wspec(d, ff), wspec(d, ff), wspec(ff, d)],
        out_specs=tokspec(d),
        out_shape=jax.ShapeDtypeStruct((b, e, cap, d), F32),
        scratch_shapes=[pltpu.VMEM((d, ff), MXU_DTYPE), pltpu.VMEM((d, ff), MXU_DTYPE), pltpu.VMEM((ff, d), MXU_DTYPE)],
        compiler_params=_params("arbitrary", "arbitrary"),
        name="moe_ffn",
    )(xe, ge, wg, wu, wd)


COMBINE_UNROLL = 8


def _combine_kernel(idx_ref, y_ref, acc_ref, ycm_ref, *, cap, n_exp):
    ei = pl.program_id(1)
    base = (pl.program_id(0) * n_exp + ei) * cap
    stride = cap + SUBLANES

    @pl.when(ei == 0)
    def _():
        acc_ref[...] = jnp.zeros_like(acc_ref)

    y = y_ref[0, 0]
    for j in range(SUBLANES):
        ycm_ref[pl.ds(j * stride, cap), :] = y[:, j * LANES:(j + 1) * LANES]

    def body(gi, c):
        toks = [pl.multiple_of(idx_ref[base + gi * COMBINE_UNROLL + i] * SUBLANES, SUBLANES) for i in range(COMBINE_UNROLL)]
        new = [acc_ref[0, pl.ds(toks[i], SUBLANES), :] + ycm_ref[pl.ds(gi * COMBINE_UNROLL + i, SUBLANES, stride=stride), :]
               for i in range(COMBINE_UNROLL)]
        for i in range(COMBINE_UNROLL):
            acc_ref[0, pl.ds(toks[i], SUBLANES), :] = new[i]
        return c

    lax.fori_loop(0, cap // COMBINE_UNROLL, body, 0)


def _combine(idx_flat, y, n):
    b, n_exp, cap, d = y.shape
    assert cap % COMBINE_UNROLL == 0 and d == SUBLANES * LANES
    return pl.pallas_call(
        functools.partial(_combine_kernel, cap=cap, n_exp=n_exp),
        grid=(b, n_exp),
        in_specs=[pl.BlockSpec(memory_space=pltpu.SMEM),
                  pl.BlockSpec((1, 1, cap, d), lambda bi, ei: (bi, ei, 0, 0))],
        out_specs=pl.BlockSpec((1, n * SUBLANES, LANES), lambda bi, ei: (bi, 0, 0)),
        out_shape=jax.ShapeDtypeStruct((b, n * SUBLANES, LANES), F32),
        scratch_shapes=[pltpu.VMEM((SUBLANES * (cap + SUBLANES), LANES), F32)],
        compiler_params=_params("arbitrary", "arbitrary"),
        name="moe_combine",
    )(idx_flat, y)


def _expert_choice(h_tiles, aff_t, aff_m, layer, wg, wu, wd, bblk):
    b, n, _ = aff_m.shape
    cap = max(1, CAPACITY_FACTOR * n // N_EXPERTS)
    idx_flat = _select(aff_t, aff_m, cap).reshape(-1)
    xe, ge = _gather(idx_flat, h_tiles, aff_m, cap)
    y = _moe_ffn(xe, ge, layer, wg, wu, wd, bblk)
    return _combine(idx_flat, y, n)


def _odd_in_kernel(x_ref, moe_ref, gate_ref, g_ref, sh_ref, sc_ref, w_ref, xo_ref, gg_ref, u_ref):
    x = x_ref[0] + gate_ref[0] * _load_token_tiles(moe_ref)
    xo_ref[0] = x
    h = _norm_mod(x, g_ref[...], sh_ref[0], sc_ref[0]).astype(MXU_DTYPE)
    y = jnp.dot(h, w_ref[...], preferred_element_type=F32)
    gl = y[:, :LRU_WIDTH]
    _store_slabs(gg_ref, 0.5 * gl * (1.0 + jnp.tanh(0.7978845608028654 * (gl + 0.044715 * gl * gl * gl))))
    _store_slabs(u_ref, y[:, LRU_WIDTH:])


def _odd_in(x, moe, gate, g, shift, scale, mod_row, w, tile):
    b, n, d = x.shape
    row = (lambda bi: bi) if mod_row is None else (lambda bi: mod_row)
    tok = pl.BlockSpec((1, tile, d), lambda bi, t: (bi, t, 0))
    tiles = pl.BlockSpec((1, tile * SUBLANES, LANES), lambda bi, t: (bi, t, 0))
    modspec = pl.BlockSpec((1, 1, d), lambda bi, t: (row(bi), 0, 0))
    return pl.pallas_call(
        _odd_in_kernel,
        grid=(b, n // tile),
        in_specs=[tok, tiles, modspec, pl.BlockSpec((1, d), lambda bi, t: (0, 0)), modspec, modspec,
                  pl.BlockSpec(w.shape, lambda bi, t: (0, 0))],
        out_specs=(tok, _slab_spec(tile, lambda bi, t: (bi, 0, t, 0)), _slab_spec(tile, lambda bi, t: (bi, 0, t, 0))),
        out_shape=(jax.ShapeDtypeStruct((b, n, d), F32),) + (jax.ShapeDtypeStruct((b, d // LANES, n, LANES), F32),) * 2,
        compiler_params=_params("arbitrary", "arbitrary"),
        name="odd_in",
    )(x, moe, gate, g, shift, scale, w)


def _lru_kernel(*refs, reverse, nt):
    if reverse:
        (u_ref, up_ref, un_ref, cw_ref, cb_ref, wax_ref, ba_ref, bx_ref, lam_ref, h0_ref, hf_ref, gg_ref,
         out_ref, hlast_ref, a_scr, b_scr, uc_scr, carry_scr) = refs
    else:
        (u_ref, up_ref, un_ref, cw_ref, cb_ref, wax_ref, ba_ref, bx_ref, lam_ref, h0_ref,
         out_ref, hlast_ref, a_scr, b_scr, uc_scr, carry_scr) = refs
    n_slab, tile = u_ref.shape[1], u_ref.shape[2]
    w = n_slab * LANES
    per = tile // SUBLANES
    phase = lambda ref, j: jnp.concatenate(
        [ref[0, c, pl.ds(j, per, stride=SUBLANES), :] for c in range(n_slab)], axis=1)
    rows = lambda j: slice(j * per, (j + 1) * per)
    t = pl.program_id(1)
    pos = (nt - 1 - t) if reverse else t
    prev = jnp.where(pos == 0, 0.0, _load_slabs(up_ref))
    nxt = jnp.where(pos == nt - 1, 0.0, _load_slabs(un_ref))

    rowid = lax.broadcasted_iota(jnp.int32, (per, w), 0)
    shift_down = lambda x, first: jnp.where(rowid == 0, first, pltpu.roll(x, 1, 0))
    shift_up = lambda x, last: jnp.where(rowid == per - 1, last, pltpu.roll(x, per - 1, 0))
    u = [phase(u_ref, j) for j in range(SUBLANES)]
    um1 = [shift_down(u[7], prev[7:8])] + u[:7]
    um2 = [shift_down(u[6], prev[6:7]), um1[0]] + u[:6]
    up1 = u[1:] + [shift_up(u[0], nxt[0:1])]
    cw = cw_ref[...]
    for j in range(SUBLANES):
        uc_scr[rows(j), :] = (cb_ref[...] + cw[0:1] * um2[j] + cw[1:2] * um1[j] + cw[2:3] * u[j] + cw[3:4] * up1[j])

    lam = lam_ref[0]
    decay = LRU_C * (jnp.maximum(-lam, 0.0) + jnp.log1p(jnp.exp(-jnp.abs(lam))))
    for hd in range(LRU_HEADS):
        sl = slice(hd * LRU_BLOCK, (hd + 1) * LRU_BLOCK)
        uc = uc_scr[:, sl]
        z = jnp.dot(uc.astype(MXU_DTYPE), wax_ref[0, hd], preferred_element_type=F32)
        r = _sigmoid(z[:, :LRU_BLOCK] + ba_ref[0][:, sl])
        ig = _sigmoid(z[:, LRU_BLOCK:] + bx_ref[0][:, sl])
        neg_log_a = r * decay[:, sl]
        a = jnp.exp(-neg_log_a)
        mult = jnp.sqrt(jnp.tanh(neg_log_a) * (a * a + 1.0))
        a_scr[:, sl] = a
        b_scr[:, sl] = mult * (ig * uc)

    @pl.when(t == 0)
    def _():
        carry_scr[...] = jnp.broadcast_to(h0_ref[0], carry_scr.shape)

    order = list(range(SUBLANES))[::-1] if reverse else list(range(SUBLANES))
    hrun = b_scr[rows(order[0]), :]
    prun = a_scr[rows(order[0]), :]
    for j in order[1:]:
        aj = a_scr[rows(j), :]
        hrun = aj * hrun + b_scr[rows(j), :]
        prun = aj * prun
        b_scr[rows(j), :] = hrun
        a_scr[rows(j), :] = prun

    lane_row = lax.broadcasted_iota(jnp.int32, (SUBLANES, w), 0)
    carry = carry_scr[...]
    groups = list(range(per // SUBLANES))
    entering = [None] * len(groups)
    for m in (groups[::-1] if reverse else groups):
        a = prun[m * SUBLANES:(m + 1) * SUBLANES]
        bcoef = hrun[m * SUBLANES:(m + 1) * SUBLANES]
        for dist in (1, 2, 4):
            shift = (SUBLANES - dist) if reverse else dist
            msk = (lane_row < SUBLANES - dist) if reverse else (lane_row >= dist)
            a_s = pltpu.roll(a, shift, 0)
            b_s = pltpu.roll(bcoef, shift, 0)
            bcoef = jnp.where(msk, a * b_s + bcoef, bcoef)
            a = jnp.where(msk, a * a_s, a)
        after = a * carry + bcoef
        if reverse:
            entering[m] = jnp.where(lane_row == SUBLANES - 1, carry, pltpu.roll(after, SUBLANES - 1, 0))
            carry = jnp.broadcast_to(after[0:1], carry.shape)
        else:
            entering[m] = jnp.where(lane_row == 0, carry, pltpu.roll(after, 1, 0))
            carry = jnp.broadcast_to(after[SUBLANES - 1:SUBLANES], carry.shape)
    carry_scr[...] = carry
    hlast_ref[0] = carry[0:1]
    h_in = jnp.concatenate(entering, axis=0)

    for j in range(SUBLANES):
        hcur = b_scr[rows(j), :] + a_scr[rows(j), :] * h_in
        if reverse:
            hcur = phase(gg_ref, j) * (phase(hf_ref, j) + hcur)
        for c in range(n_slab):
            out_ref[0, c, pl.ds(j, per, stride=SUBLANES), :] = hcur[:, c * LANES:(c + 1) * LANES]


def _lru(u, cw, cb, wax, ba, bx, lam, h0, direction, tile, hf=None, gg=None):
    b, n_slab, n, _ = u.shape
    w = n_slab * LANES
    nt = n // tile
    per = tile // HALO
    reverse = direction == 1
    pos = (lambda t: nt - 1 - t) if reverse else (lambda t: t)
    tok = _slab_spec(tile, lambda bi, t: (bi, 0, pos(t), 0), w)
    rowspec = pl.BlockSpec((1, 1, w), lambda bi, t: (direction, 0, 0))
    in_specs = [
        tok,
        _slab_spec(HALO, lambda bi, t: (bi, 0, jnp.maximum(pos(t) * per - 1, 0), 0), w),
        _slab_spec(HALO, lambda bi, t: (bi, 0, jnp.minimum((pos(t) + 1) * per, n // HALO - 1), 0), w),
        pl.BlockSpec(cw.shape, lambda bi, t: (0, 0)),
        pl.BlockSpec(cb.shape, lambda bi, t: (0, 0)),
        pl.BlockSpec((1,) + wax.shape[1:], lambda bi, t: (direction, 0, 0, 0)),
        rowspec, rowspec, rowspec,
        pl.BlockSpec((1, 1, w), lambda bi, t: (bi, 0, 0)),
    ]
    args = [u, u, u, cw, cb, wax, ba, bx, lam, h0]
    if reverse:
        in_specs += [tok, tok]
        args += [hf, gg]
    assert tile % (SUBLANES * SUBLANES) == 0
    return pl.pallas_call(
        functools.partial(_lru_kernel, reverse=reverse, nt=nt),
        grid=(b, nt),
        in_specs=in_specs,
        out_specs=(tok, pl.BlockSpec((1, 1, w), lambda bi, t: (bi, 0, 0))),
        out_shape=(jax.ShapeDtypeStruct(u.shape, F32), jax.ShapeDtypeStruct((b, 1, w), F32)),
        scratch_shapes=[pltpu.VMEM((tile, w), F32), pltpu.VMEM((tile, w), F32), pltpu.VMEM((tile, w), F32),
                        pltpu.VMEM((SUBLANES, w), F32)],
        compiler_params=_params("arbitrary", "arbitrary"),
        name="lru_bwd" if reverse else "lru_fwd",
    )(*args)


def _final_kernel(x_ref, moe_ref, gate_ref, g_ref, o_ref):
    x = x_ref[0] + gate_ref[0] * _load_token_tiles(moe_ref)
    o_ref[0] = x * lax.rsqrt(jnp.mean(x * x, axis=-1, keepdims=True) + EPS) * g_ref[...]


def _final(x, moe, gate, g, tile):
    b, n, d = x.shape
    tok = pl.BlockSpec((1, tile, d), lambda bi, t: (bi, t, 0))
    tiles = pl.BlockSpec((1, tile * SUBLANES, LANES), lambda bi, t: (bi, t, 0))
    return pl.pallas_call(
        _final_kernel,
        grid=(b, n // tile),
        in_specs=[tok, tiles, pl.BlockSpec((1, 1, d), lambda bi, t: (bi, 0, 0)), pl.BlockSpec((1, d), lambda bi, t: (0, 0))],
        out_specs=tok,
        out_shape=jax.ShapeDtypeStruct((b, n, d), F32),
        compiler_params=_params("arbitrary", "arbitrary"),
        name="final_norm",
    )(x, moe, gate, g)


def kernel(x, c, ctx, c_ctx, ada_w, ada_b, norm_mix_g, norm_ffn_g, ev_w_in, ev_w_out, ev_sink, ev_conv_w, ev_conv_b, od_w_in, od_w_out, od_conv_w, od_conv_b, od_wa, od_ba, od_wx, od_bx, od_lambda, router_w, w_gate, w_up, w_down, final_g):
    b, n, d = x.shape
    lc = ctx.shape[1]
    depth = ada_w.shape[0]
    assert depth == 2 and d == D_MODEL and b < MOD_ROWS
    tile_l = min(512, n)
    tile_c = lc
    ctx_row = b

    cvec = jnp.zeros((MOD_ROWS, d), F32).at[:b].set(c).at[b].set(c_ctx)
    mods = _ada(cvec, ada_w, ada_b).reshape(depth, MOD_ROWS, 6, 1, d)
    mod = lambda l, j: mods[l, :, j]

    def router_split(l):
        return _split_hi_lo(jnp.pad(router_w[l], ((0, 0), (0, LANES - N_EXPERTS))))

    bf = lambda a: a.astype(MXU_DTYPE)

    g_mix = norm_mix_g[0].reshape(1, d)
    g_ffn = norm_ffn_g[0].reshape(1, d)
    w_in = bf(ev_w_in[0])
    w_out = bf(ev_w_out[0])
    conv_p = (ev_conv_w[0], ev_conv_b[0].reshape(1, -1))
    tables = _rope_tables(n)
    ql, kvl, gbl, cul = _even_in(x, g_mix, mod(0, 0), mod(0, 1), None, w_in, tables, tile_l)
    qc, kvc, gbc, cuc = _even_in(ctx, g_mix, mod(0, 0), mod(0, 1), ctx_row, w_in, None, tile_c)
    att_l = _attention(ev_sink[0], ql, kvl, kvc)
    att_c = _attention(ev_sink[0], qc, None, kvc)
    rw_hi, rw_lo = router_split(0)
    xl, hl, am_l, at_l = _mix_out((att_l, gbl, cul), conv_p, w_out, x, mod(0, 2), g_ffn, mod(0, 3), mod(0, 4), None,
                                  rw_hi, rw_lo, tile_l)
    xc, hc, am_c, at_c = _mix_out((att_c, gbc, cuc), conv_p, w_out, ctx, mod(0, 2), g_ffn, mod(0, 3), mod(0, 4),
                                  ctx_row, rw_hi, rw_lo, tile_c)
    moe_l = _expert_choice(hl, at_l, am_l, 0, w_gate, w_up, w_down, 1)
    moe_c = _expert_choice(hc, at_c, am_c, 0, w_gate, w_up, w_down, b)

    g_mix = norm_mix_g[1].reshape(1, d)
    g_ffn = norm_ffn_g[1].reshape(1, d)
    w_in = bf(od_w_in[0])
    w_out = bf(od_w_out[0])
    xl, ggl, ul = _odd_in(xl, moe_l, mod(0, 5), g_mix, mod(1, 0), mod(1, 1), None, w_in, tile_l)
    _, _, uc = _odd_in(xc, moe_c, mod(0, 5), g_mix, mod(1, 0), mod(1, 1), ctx_row, w_in, tile_c)
    cw, cb = od_conv_w[0], od_conv_b[0].reshape(1, -1)
    wax = bf(jnp.concatenate([od_wa[0], od_wx[0]], axis=-1))
    ba, bx, lam = (a[0].reshape(2, 1, -1) for a in (od_ba, od_bx, od_lambda))
    zero_state = jnp.zeros((b, 1, LRU_WIDTH), F32)
    hf_c, h0_f = _lru(uc, cw, cb, wax, ba, bx, lam, zero_state, 0, tile_c)
    _, h0_b = _lru(uc, cw, cb, wax, ba, bx, lam, zero_state, 1, tile_c, hf=hf_c, gg=hf_c)
    hf_l, _ = _lru(ul, cw, cb, wax, ba, bx, lam, h0_f, 0, tile_l)
    yl, _ = _lru(ul, cw, cb, wax, ba, bx, lam, h0_b, 1, tile_l, hf=hf_l, gg=ggl)
    rw_hi, rw_lo = router_split(1)
    xl, hl, am_l, at_l = _mix_out(yl, None, w_out, xl, mod(1, 2), g_ffn, mod(1, 3), mod(1, 4), None,
                                  rw_hi, rw_lo, tile_l)
    moe_l = _expert_choice(hl, at_l, am_l, 1, w_gate, w_up, w_down, 1)
    return _final(xl, moe_l, mod(1, 5), final_g.reshape(1, d), tile_l)
```

```python
import functools

import jax
import jax.numpy as jnp
from jax import lax
from jax.experimental import pallas as pl
from jax.experimental.pallas import tpu as pltpu

F32 = jnp.float32
MXU_DTYPE = jnp.bfloat16

D_MODEL = 1024
GRID_W = 64
EPS = 1e-6
NEG_INF = -1e30
HEAD_DIM = 64
N_Q_HEADS = 8
N_KV_HEADS = 2
Q_PER_KV = N_Q_HEADS // N_KV_HEADS
ATTN_WIDTH = N_Q_HEADS * HEAD_DIM
KV_WIDTH = N_KV_HEADS * HEAD_DIM
WINDOW = 128
BLOCK = 128
ROPE_BASE = 10000.0
CONV_B_WIDTH = D_MODEL // 2
EVEN_IN = ATTN_WIDTH + 2 * KV_WIDTH + 3 * CONV_B_WIDTH
LRU_WIDTH = D_MODEL
LRU_HEADS = 8
LRU_BLOCK = LRU_WIDTH // LRU_HEADS
LRU_C = 8.0
N_EXPERTS = 16
CAPACITY_FACTOR = 2
EXPERT_FF = 1408
MOD_ROWS = 16
LANES = 128
SUBLANES = 8
HALO = SUBLANES
CUMSUM_TILE = 256
MIX_OUT_SPLIT = 2
V7X_VMEM_LIMIT = 56 * 1024 * 1024


def _params(*sem):
    return pltpu.CompilerParams(dimension_semantics=sem, vmem_limit_bytes=V7X_VMEM_LIMIT)


def _split_hi_lo(a):
    hi = a.astype(MXU_DTYPE)
    lo = (a - hi.astype(F32)).astype(MXU_DTYPE)
    return hi, lo


def _sigmoid(z):
    return 0.5 * (1.0 + jnp.tanh(0.5 * z))


def _store_token_tiles(ref, val, row0=0):
    tile = val.shape[0]
    for j in range(SUBLANES):
        ref[0, pl.ds(row0 * SUBLANES + j, tile, stride=SUBLANES), :] = val[:, j * LANES:(j + 1) * LANES]


def _load_token_tiles(ref):
    tile = ref.shape[1] // SUBLANES
    return jnp.concatenate([ref[0, pl.ds(j, tile, stride=SUBLANES), :] for j in range(SUBLANES)], axis=1)


def _slab_spec(rows, index_map, width=LRU_WIDTH):
    return pl.BlockSpec((1, width // LANES, rows, LANES), index_map)


def _store_slabs(ref, val):
    for c in range(val.shape[1] // LANES):
        ref[0, c] = val[:, c * LANES:(c + 1) * LANES]


def _load_slabs(ref):
    return jnp.concatenate([ref[0, c] for c in range(ref.shape[1])], axis=1)


def _norm_mod(x, g, shift, scale):
    y = x * lax.rsqrt(jnp.mean(x * x, axis=-1, keepdims=True) + EPS) * g
    return y * (1.0 + scale) + shift


def _ada_kernel(c_ref, w_ref, b_ref, o_ref):
    c = c_ref[...]
    s_hi, s_lo = _split_hi_lo(c * _sigmoid(c))
    w_hi, w_lo = _split_hi_lo(w_ref[0])
    acc = jnp.dot(s_hi, w_hi, preferred_element_type=F32)
    acc += jnp.dot(s_hi, w_lo, preferred_element_type=F32)
    acc += jnp.dot(s_lo, w_hi, preferred_element_type=F32)
    o_ref[0] = acc + b_ref[0]


def _ada(cvec, ada_w, ada_b):
    depth, d, n6 = ada_w.shape
    tn = 1536
    return pl.pallas_call(
        _ada_kernel,
        grid=(depth, n6 // tn),
        in_specs=[
            pl.BlockSpec((MOD_ROWS, d), lambda l, j: (0, 0)),
            pl.BlockSpec((1, d, tn), lambda l, j: (l, 0, j)),
            pl.BlockSpec((1, 1, tn), lambda l, j: (l, 0, j)),
        ],
        out_specs=pl.BlockSpec((1, MOD_ROWS, tn), lambda l, j: (l, 0, j)),
        out_shape=jax.ShapeDtypeStruct((depth, MOD_ROWS, n6), F32),
        compiler_params=_params("arbitrary", "arbitrary"),
        name="ada",
    )(cvec, ada_w, ada_b.reshape(depth, 1, n6))


def _even_in_kernel(*refs, rope):
    if rope:
        (x_ref, g_ref, sh_ref, sc_ref, w_ref, cos_ref, sa_ref, sb_ref, q_ref, kv_ref, gb_ref, cu_ref) = refs
    else:
        (x_ref, g_ref, sh_ref, sc_ref, w_ref, q_ref, kv_ref, gb_ref, cu_ref) = refs
    h = _norm_mod(x_ref[0], g_ref[...], sh_ref[0], sc_ref[0]).astype(MXU_DTYPE)
    y = jnp.dot(h, w_ref[...], preferred_element_type=F32)
    q = y[:, :ATTN_WIDTH]
    k = y[:, ATTN_WIDTH:ATTN_WIDTH + KV_WIDTH]
    v = y[:, ATTN_WIDTH + KV_WIDTH:ATTN_WIDTH + 2 * KV_WIDTH]
    c0 = ATTN_WIDTH + 2 * KV_WIDTH
    if rope:
        cos, sa, sb = cos_ref[...], sa_ref[...], sb_ref[...]

        def rot(z):
            return z * cos + pltpu.roll(z, 16, 1) * sa + pltpu.roll(z, 112, 1) * sb

        q = jnp.concatenate([rot(q[:, j * 128:(j + 1) * 128]) for j in range(ATTN_WIDTH // 128)], axis=1)
        k = rot(k)
    q_ref[0] = (q * (HEAD_DIM ** -0.5)).astype(q_ref.dtype)
    kv_ref[0] = jnp.concatenate([k, v], axis=1).astype(kv_ref.dtype)
    gb_ref[0] = y[:, c0:c0 + CONV_B_WIDTH]
    cu_ref[0] = y[:, c0 + CONV_B_WIDTH:c0 + 2 * CONV_B_WIDTH] * y[:, c0 + 2 * CONV_B_WIDTH:]


def _even_in(x, g, shift, scale, mod_row, w, tables, tile):
    b, n, d = x.shape
    nt = n // tile
    rope = tables is not None
    row = (lambda bi: bi) if mod_row is None else (lambda bi: mod_row)
    in_specs = [
        pl.BlockSpec((1, tile, d), lambda bi, t: (bi, t, 0)),
        pl.BlockSpec((1, d), lambda bi, t: (0, 0)),
        pl.BlockSpec((1, 1, d), lambda bi, t: (row(bi), 0, 0)),
        pl.BlockSpec((1, 1, d), lambda bi, t: (row(bi), 0, 0)),
        pl.BlockSpec(w.shape, lambda bi, t: (0, 0)),
    ]
    args = [x, g, shift, scale, w]
    if rope:
        in_specs += [pl.BlockSpec((tile, 128), lambda bi, t: (t, 0))] * 3
        args += list(tables)
    out_shape = (
        jax.ShapeDtypeStruct((b, n, ATTN_WIDTH), MXU_DTYPE),
        jax.ShapeDtypeStruct((b, n, 2 * KV_WIDTH), MXU_DTYPE),
        jax.ShapeDtypeStruct((b, n, CONV_B_WIDTH), F32),
        jax.ShapeDtypeStruct((b, n, CONV_B_WIDTH), F32),
    )
    out_specs = tuple(pl.BlockSpec((1, tile, s.shape[-1]), lambda bi, t: (bi, t, 0)) for s in out_shape)
    return pl.pallas_call(
        functools.partial(_even_in_kernel, rope=rope),
        grid=(b, nt),
        in_specs=in_specs,
        out_specs=out_specs,
        out_shape=out_shape,
        compiler_params=_params("arbitrary", "arbitrary"),
        name="even_in_rope" if rope else "even_in",
    )(*args)


def _rope_tables(n):
    nf = HEAD_DIM // 4
    pos = jnp.arange(n)
    rows = (pos // GRID_W).astype(F32)
    cols = (pos % GRID_W).astype(F32)
    lane = jnp.arange(128)
    inv = ROPE_BASE ** (-(lane % nf).astype(F32) / nf)
    use_col = (lane % HEAD_DIM) >= HEAD_DIM // 2
    ang = jnp.where(use_col[None, :], cols[:, None], rows[:, None]) * inv[None, :]
    cos, sin = jnp.cos(ang), jnp.sin(ang)
    second = ((lane % (2 * nf)) >= nf)[None, :]
    return cos, jnp.where(second, sin, 0.0), jnp.where(second, 0.0, -sin)


def _attn_kernel(*refs, n, has_local):
    if has_local:
        sink_ref, q_ref, kv_ref, kvc_ref, o_ref = refs
    else:
        sink_ref, q_ref, kvc_ref, o_ref = refs
    i = pl.program_id(1)
    q = q_ref[0]
    kvall = kvc_ref[0]
    n_loc = 3 * BLOCK
    if has_local:
        start = pl.multiple_of(jnp.clip(i * BLOCK - BLOCK, 0, n - n_loc), BLOCK)
        kvall = jnp.concatenate([kv_ref[0, pl.ds(start, n_loc), :], kvall], axis=0)
        qpos = i * BLOCK + lax.broadcasted_iota(jnp.int32, (Q_PER_KV * BLOCK, n_loc), 0) % BLOCK
        kpos = start + lax.broadcasted_iota(jnp.int32, (Q_PER_KV * BLOCK, n_loc), 1)
        valid = jnp.abs(kpos - qpos) <= WINDOW
    grp = lax.broadcasted_iota(jnp.int32, (Q_PER_KV * BLOCK, 1), 0) // BLOCK
    outs = []
    for hk in range(N_KV_HEADS):
        kh = kvall[:, hk * HEAD_DIM:(hk + 1) * HEAD_DIM]
        vh = kvall[:, KV_WIDTH + hk * HEAD_DIM:KV_WIDTH + (hk + 1) * HEAD_DIM]
        qg = jnp.concatenate(
            [q[:, (hk * Q_PER_KV + g) * HEAD_DIM:(hk * Q_PER_KV + g + 1) * HEAD_DIM] for g in range(Q_PER_KV)], axis=0)
        s = lax.dot_general(qg, kh, (((1,), (1,)), ((), ())), preferred_element_type=F32)
        if has_local:
            s = jnp.concatenate([jnp.where(valid, s[:, :n_loc], NEG_INF), s[:, n_loc:]], axis=1)
        snk = jnp.zeros((Q_PER_KV * BLOCK, 1), F32)
        for g in range(Q_PER_KV):
            snk = jnp.where(grp == g, sink_ref[hk * Q_PER_KV + g], snk)
        m = jnp.maximum(jnp.max(s, axis=1, keepdims=True), snk)
        p = jnp.exp(s - m)
        denom = jnp.sum(p, axis=1, keepdims=True) + jnp.exp(snk - m)
        o = jnp.dot(p.astype(MXU_DTYPE), vh, preferred_element_type=F32) / denom
        outs += [o[g * BLOCK:(g + 1) * BLOCK] for g in range(Q_PER_KV)]
    o_ref[0] = jnp.concatenate(outs, axis=1).astype(o_ref.dtype)


def _attention(sink, q, kv, kvc):
    b, n, _ = q.shape
    lc = kvc.shape[1]
    has_local = kv is not None
    in_specs = [pl.BlockSpec(memory_space=pltpu.SMEM), pl.BlockSpec((1, BLOCK, ATTN_WIDTH), lambda bi, i: (bi, i, 0))]
    args = [sink, q]
    if has_local:
        in_specs.append(pl.BlockSpec((1, n, 2 * KV_WIDTH), lambda bi, i: (bi, 0, 0)))
        args.append(kv)
    in_specs.append(pl.BlockSpec((1, lc, 2 * KV_WIDTH), lambda bi, i: (bi, 0, 0)))
    args.append(kvc)
    return pl.pallas_call(
        functools.partial(_attn_kernel, n=n, has_local=has_local),
        grid=(b, n // BLOCK),
        in_specs=in_specs,
        out_specs=pl.BlockSpec((1, BLOCK, ATTN_WIDTH), lambda bi, i: (bi, i, 0)),
        out_shape=jax.ShapeDtypeStruct((b, n, ATTN_WIDTH), MXU_DTYPE),
        compiler_params=_params("arbitrary", "arbitrary"),
        name="attn_local" if has_local else "attn_ctx",
    )(*args)


def _mix_out_kernel(*refs, conv, first, last):
    if conv:
        (att_ref, gb_ref, cu_ref, cup_ref, cun_ref, cw_ref, cb_ref,
         w_ref, x_ref, gate_ref, g2_ref, sh2_ref, sc2_ref, rwh_ref, rwl_ref, xo_ref, h_ref, afft_ref) = refs
        t = pl.program_id(1)
        prev = jnp.where(t == first, 0.0, cup_ref[0])
        nxt = jnp.where(t == last, 0.0, cun_ref[0])
        ext = jnp.concatenate([prev, cu_ref[0], nxt], axis=0)
        cw = cw_ref[...]
    else:
        (y_ref, w_ref, x_ref, gate_ref, g2_ref, sh2_ref, sc2_ref, rwh_ref, rwl_ref, xo_ref, h_ref, afft_ref) = refs
    tile = x_ref.shape[1]
    rows = tile // MIX_OUT_SPLIT
    for s in range(MIX_OUT_SPLIT):
        r0 = s * rows
        rs = slice(r0, r0 + rows)
        if conv:
            cv = (cw[0:1] * ext[HALO - 1 + r0:HALO - 1 + r0 + rows] + cw[1:2] * ext[HALO + r0:HALO + r0 + rows]
                  + cw[2:3] * ext[HALO + 1 + r0:HALO + 1 + r0 + rows] + cb_ref[...])
            cat = jnp.concatenate([att_ref[0, rs, :], (gb_ref[0, rs, :] * cv).astype(MXU_DTYPE)], axis=1)
        else:
            cat = jnp.concatenate([y_ref[0, c, rs, :] for c in range(y_ref.shape[1])], axis=1).astype(MXU_DTYPE)
        y = jnp.dot(cat, w_ref[...], preferred_element_type=F32)
        x = x_ref[0, rs, :] + gate_ref[0] * y
        xo_ref[0, rs, :] = x
        h = _norm_mod(x, g2_ref[...], sh2_ref[0], sc2_ref[0])
        _store_token_tiles(h_ref, h, r0)
        h_hi, h_lo = _split_hi_lo(h)
        logits = jnp.dot(h_hi, rwh_ref[...], preferred_element_type=F32)
        logits += jnp.dot(h_hi, rwl_ref[...], preferred_element_type=F32)
        logits += jnp.dot(h_lo, rwh_ref[...], preferred_element_type=F32)
        lane = lax.broadcasted_iota(jnp.int32, logits.shape, 1)
        logits = jnp.where(lane < N_EXPERTS, logits, NEG_INF)
        e = jnp.exp(logits - jnp.max(logits, axis=1, keepdims=True))
        aff = e / jnp.sum(e, axis=1, keepdims=True)
        afft_ref[0, :, rs] = aff.T[:N_EXPERTS]


def _mix_out(mix_in, conv_params, w, x, gate, g2, sh2, sc2, mod_row, rw_hi, rw_lo, tile):
    b, n, d = x.shape
    nt = n // tile
    conv = conv_params is not None
    row = (lambda bi: bi) if mod_row is None else (lambda bi: mod_row)
    tok = lambda width: pl.BlockSpec((1, tile, width), lambda bi, t: (bi, t, 0))
    modspec = pl.BlockSpec((1, 1, d), lambda bi, t: (row(bi), 0, 0))
    full = lambda a: pl.BlockSpec(a.shape, lambda bi, t: (0,) * a.ndim)
    if conv:
        att, gb, cu = mix_in
        cw, cb = conv_params
        per = tile // HALO
        in_specs = [tok(ATTN_WIDTH), tok(CONV_B_WIDTH), tok(CONV_B_WIDTH),
                    pl.BlockSpec((1, HALO, CONV_B_WIDTH), lambda bi, t: (bi, jnp.maximum(t * per - 1, 0), 0)),
                    pl.BlockSpec((1, HALO, CONV_B_WIDTH), lambda bi, t: (bi, jnp.minimum((t + 1) * per, n // HALO - 1), 0)),
                    full(cw), full(cb)]
        args = [att, gb, cu, cu, cu, cw, cb]
    else:
        in_specs = [_slab_spec(tile, lambda bi, t: (bi, 0, t, 0), d)]
        args = [mix_in]
    in_specs += [full(w), tok(d), modspec, full(g2), modspec, modspec, full(rw_hi), full(rw_lo)]
    args += [w, x, gate, g2, sh2, sc2, rw_hi, rw_lo]
    out_shape = (jax.ShapeDtypeStruct((b, n, d), F32), jax.ShapeDtypeStruct((b, n * SUBLANES, LANES), F32),
                 jax.ShapeDtypeStruct((b, N_EXPERTS, n), F32))
    out_specs = (tok(d), pl.BlockSpec((1, tile * SUBLANES, LANES), lambda bi, t: (bi, t, 0)),
                 pl.BlockSpec((1, N_EXPERTS, tile), lambda bi, t: (bi, 0, t)))
    return pl.pallas_call(
        functools.partial(_mix_out_kernel, conv=conv, first=0, last=nt - 1),
        grid=(b, nt),
        in_specs=in_specs,
        out_specs=out_specs,
        out_shape=out_shape,
        compiler_params=_params("arbitrary", "arbitrary"),
        name="even_out" if conv else "odd_out",
    )(*args)


def _cumsum_lanes(x):
    n = x.shape[1]
    r = lax.broadcasted_iota(jnp.int32, (CUMSUM_TILE, CUMSUM_TILE), 0)
    c = lax.broadcasted_iota(jnp.int32, (CUMSUM_TILE, CUMSUM_TILE), 1)
    tri = jnp.where(r <= c, 1.0, 0.0).astype(MXU_DTYPE)
    carry = jnp.zeros((x.shape[0], 1), F32)
    outs = []
    for k in range(n // CUMSUM_TILE):
        blk = x[:, k * CUMSUM_TILE:(k + 1) * CUMSUM_TILE].astype(MXU_DTYPE)
        loc = jnp.dot(blk, tri, preferred_element_type=F32) + carry
        outs.append(loc)
        carry = loc[:, CUMSUM_TILE - 1:CUMSUM_TILE]
    return jnp.concatenate(outs, axis=1)


SLOT_EMPTY = 1 << 20


def _select_kernel(at_ref, idx_ref, g_ref, *, cap):
    at = at_ref[0]
    n_exp, n = at.shape
    capf = float(cap)

    def count_ge(thr):
        return jnp.sum(jnp.where(at >= thr, 1.0, 0.0), axis=1, keepdims=True)

    def bit_body(_, c):
        lo_i, hi_i = c
        mid = lo_i + ((hi_i - lo_i) >> 1)
        ge = count_ge(lax.bitcast_convert_type(mid, F32)) >= capf
        return jnp.where(ge, mid, lo_i), jnp.where(ge, hi_i, mid)

    lo_i, hi_i = lax.fori_loop(
        0, 31, bit_body, (jnp.zeros((n_exp, 1), jnp.int32), jnp.full((n_exp, 1), 0x3F800001, jnp.int32)))

    def val_body(_, c):
        lo, hi = c
        mid = 0.5 * (lo + hi)
        ge = count_ge(mid) >= capf
        return jnp.where(ge, mid, lo), jnp.where(ge, hi, mid)

    lo, hi = lax.fori_loop(
        0, 24, val_body, (lax.bitcast_convert_type(lo_i, F32), lax.bitcast_convert_type(hi_i, F32)))
    need = capf - count_ge(hi)

    above = jnp.where(at >= hi, 1.0, 0.0)
    band = jnp.where(at >= lo, 1.0, 0.0) - above
    sel = above + band * jnp.where(_cumsum_lanes(band) <= need, 1.0, 0.0)
    rank = _cumsum_lanes(sel)

    lane = lax.broadcasted_iota(jnp.int32, (n_exp, n), 1)
    disp = jnp.where(sel > 0.5, lane + 1 - rank.astype(jnp.int32), SLOT_EMPTY)
    g = at
    for k in range(n.bit_length() - 1):
        step = 1 << k
        moving = ((disp >> k) & 1) == 1
        disp_in = pltpu.roll(disp, n - step, 1)
        arriving = ((disp_in >> k) & 1) == 1
        g = jnp.where(arriving, pltpu.roll(g, n - step, 1), g)
        disp = jnp.where(arriving, disp_in, jnp.where(moving, SLOT_EMPTY, disp))

    capp = pl.cdiv(cap, LANES) * LANES
    slot = lax.broadcasted_iota(jnp.int32, (n_exp, cap), 1)
    idx_ref[0] = jnp.clip(slot + disp[:, :cap], 0, n - 1)
    g_pad = jnp.concatenate([g[:, :capp], jnp.zeros((LANES - n_exp, capp), F32)], axis=0)
    g_ref[0] = g_pad.T[:cap]


def _select(aff_t, cap):
    b, n_exp, n = aff_t.shape
    assert n & (n - 1) == 0 and n % CUMSUM_TILE == 0 and n < SLOT_EMPTY
    return pl.pallas_call(
        functools.partial(_select_kernel, cap=cap),
        grid=(b,),
        in_specs=[pl.BlockSpec((1, n_exp, n), lambda bi: (bi, 0, 0))],
        out_specs=(pl.BlockSpec((1, n_exp, cap), lambda bi: (bi, 0, 0)), pl.BlockSpec((1, cap, LANES), lambda bi: (bi, 0, 0))),
        out_shape=(jax.ShapeDtypeStruct((b, n_exp, cap), jnp.int32), jax.ShapeDtypeStruct((b, cap, LANES), F32)),
        compiler_params=_params("arbitrary"),
        name="moe_select",
    )(aff_t)


def _gather_kernel(idx_ref, src_ref, xe_ref, xcm_ref, *, cap, n_exp):
    base = (pl.program_id(0) * n_exp + pl.program_id(1)) * cap
    stride = cap + SUBLANES

    def body(gi, c):
        for i in range(SUBLANES):
            r = gi * SUBLANES + i
            t = idx_ref[base + r]
            xcm_ref[pl.ds(r, SUBLANES, stride=stride), :] = src_ref[0, pl.ds(pl.multiple_of(t * SUBLANES, SUBLANES), SUBLANES), :]
        return c

    lax.fori_loop(0, cap // SUBLANES, body, 0)
    xe_ref[0, 0] = jnp.concatenate(
        [xcm_ref[pl.ds(j * stride, cap), :] for j in range(SUBLANES)], axis=1).astype(xe_ref.dtype)


def _gather(idx_flat, src, cap):
    b, rows, _ = src.shape
    d = SUBLANES * LANES
    assert cap % SUBLANES == 0
    return pl.pallas_call(
        functools.partial(_gather_kernel, cap=cap, n_exp=N_EXPERTS),
        grid=(b, N_EXPERTS),
        in_specs=[pl.BlockSpec(memory_space=pltpu.SMEM),
                  pl.BlockSpec((1, rows, LANES), lambda bi, ei: (bi, 0, 0))],
        out_specs=pl.BlockSpec((1, 1, cap, d), lambda bi, ei: (bi, ei, 0, 0)),
        out_shape=jax.ShapeDtypeStruct((b, N_EXPERTS, cap, d), MXU_DTYPE),
        scratch_shapes=[pltpu.VMEM((SUBLANES * (cap + SUBLANES), LANES), F32)],
        compiler_params=_params("arbitrary", "arbitrary"),
        name="moe_gather",
    )(idx_flat, src)


def _moe_ffn_kernel(x_ref, g_ref, wg_ref, wu_ref, wd_ref, o_ref, wg_s, wu_s, wd_s):
    bb, _, cap, d = x_ref.shape

    @pl.when(pl.program_id(1) == 0)
    def _():
        wg_s[...] = wg_ref[0, 0].astype(wg_s.dtype)
        wu_s[...] = wu_ref[0, 0].astype(wu_s.dtype)
        wd_s[...] = wd_ref[0, 0].astype(wd_s.dtype)

    x = x_ref[...].reshape(bb * cap, d)
    a = jnp.dot(x, wg_s[...], preferred_element_type=F32)
    u = jnp.dot(x, wu_s[...], preferred_element_type=F32)
    mid = (a * _sigmoid(a) * u).astype(MXU_DTYPE)
    y = jnp.dot(mid, wd_s[...], preferred_element_type=F32)
    aff = g_ref[...].reshape(bb * cap, LANES)
    lane = lax.broadcasted_iota(jnp.int32, aff.shape, 1)
    g = jnp.sum(jnp.where(lane == pl.program_id(0), aff, 0.0), axis=1, keepdims=True)
    o_ref[...] = (y * g).reshape(bb, 1, cap, d)


def _moe_ffn(xe, ge, layer, wg, wu, wd, bblk):
    b, e, cap, d = xe.shape
    ff = wg.shape[-1]
    tokspec = lambda width: pl.BlockSpec((bblk, 1, cap, width), lambda ei, bi: (bi, ei, 0, 0))
    wspec = lambda r, c: pl.BlockSpec((1, 1, r, c), lambda ei, bi: (layer, ei, 0, 0), pipeline_mode=pl.Buffered(1))
    return pl.pallas_call(
        _moe_ffn_kernel,
        grid=(e, b // bblk),
        in_specs=[tokspec(d), pl.BlockSpec((bblk, cap, LANES), lambda ei, bi: (bi, 0, 0)),
                  wspec(d, ff), wspec(d, ff), wspec(ff, d)],
        out_specs=tokspec(d),
        out_shape=jax.ShapeDtypeStruct((b, e, cap, d), F32),
        scratch_shapes=[pltpu.VMEM((d, ff), MXU_DTYPE), pltpu.VMEM((d, ff), MXU_DTYPE), pltpu.VMEM((ff, d), MXU_DTYPE)],
        compiler_params=_params("arbitrary", "arbitrary"),
        name="moe_ffn",
    )(xe, ge, wg, wu, wd)


COMBINE_UNROLL = 8


def _combine_kernel(idx_ref, y_ref, acc_ref, ycm_ref, *, cap, n_exp):
    ei = pl.program_id(1)
    base = (pl.program_id(0) * n_exp + ei) * cap
    stride = cap + SUBLANES

    @pl.when(ei == 0)
    def _():
        acc_ref[...] = jnp.zeros_like(acc_ref)

    y = y_ref[0, 0]
    for j in range(SUBLANES):
        ycm_ref[pl.ds(j * stride, cap), :] = y[:, j * LANES:(j + 1) * LANES]

    def body(gi, c):
        toks = [pl.multiple_of(idx_ref[base + gi * COMBINE_UNROLL + i] * SUBLANES, SUBLANES) for i in range(COMBINE_UNROLL)]
        new = [acc_ref[0, pl.ds(toks[i], SUBLANES), :] + ycm_ref[pl.ds(gi * COMBINE_UNROLL + i, SUBLANES, stride=stride), :]
               for i in range(COMBINE_UNROLL)]
        for i in range(COMBINE_UNROLL):
            acc_ref[0, pl.ds(toks[i], SUBLANES), :] = new[i]
        return c

    lax.fori_loop(0, cap // COMBINE_UNROLL, body, 0)


def _combine(idx_flat, y, n):
    b, n_exp, cap, d = y.shape
    assert cap % COMBINE_UNROLL == 0 and d == SUBLANES * LANES
    return pl.pallas_call(
        functools.partial(_combine_kernel, cap=cap, n_exp=n_exp),
        grid=(b, n_exp),
        in_specs=[pl.BlockSpec(memory_space=pltpu.SMEM),
                  pl.BlockSpec((1, 1, cap, d), lambda bi, ei: (bi, ei, 0, 0))],
        out_specs=pl.BlockSpec((1, n * SUBLANES, LANES), lambda bi, ei: (bi, 0, 0)),
        out_shape=jax.ShapeDtypeStruct((b, n * SUBLANES, LANES), F32),
        scratch_shapes=[pltpu.VMEM((SUBLANES * (cap + SUBLANES), LANES), F32)],
        compiler_params=_params("arbitrary", "arbitrary"),
        name="moe_combine",
    )(idx_flat, y)


def _expert_choice(h_tiles, aff_t, layer, wg, wu, wd, bblk):
    n = aff_t.shape[2]
    cap = max(1, CAPACITY_FACTOR * n // N_EXPERTS)
    idx, g = _select(aff_t, cap)
    idx_flat = idx.reshape(-1)
    xe = _gather(idx_flat, h_tiles, cap)
    y = _moe_ffn(xe, g, layer, wg, wu, wd, bblk)
    return _combine(idx_flat, y, n)


def _odd_in_kernel(x_ref, moe_ref, gate_ref, g_ref, sh_ref, sc_ref, w_ref, xo_ref, gg_ref, u_ref):
    x = x_ref[0] + gate_ref[0] * _load_token_tiles(moe_ref)
    xo_ref[0] = x
    h = _norm_mod(x, g_ref[...], sh_ref[0], sc_ref[0]).astype(MXU_DTYPE)
    y = jnp.dot(h, w_ref[...], preferred_element_type=F32)
    gl = y[:, :LRU_WIDTH]
    _store_slabs(gg_ref, 0.5 * gl * (1.0 + jnp.tanh(0.7978845608028654 * (gl + 0.044715 * gl * gl * gl))))
    _store_slabs(u_ref, y[:, LRU_WIDTH:])


def _odd_in(x, moe, gate, g, shift, scale, mod_row, w, tile):
    b, n, d = x.shape
    row = (lambda bi: bi) if mod_row is None else (lambda bi: mod_row)
    tok = pl.BlockSpec((1, tile, d), lambda bi, t: (bi, t, 0))
    tiles = pl.BlockSpec((1, tile * SUBLANES, LANES), lambda bi, t: (bi, t, 0))
    modspec = pl.BlockSpec((1, 1, d), lambda bi, t: (row(bi), 0, 0))
    return pl.pallas_call(
        _odd_in_kernel,
        grid=(b, n // tile),
        in_specs=[tok, tiles, modspec, pl.BlockSpec((1, d), lambda bi, t: (0, 0)), modspec, modspec,
                  pl.BlockSpec(w.shape, lambda bi, t: (0, 0))],
        out_specs=(tok, _slab_spec(tile, lambda bi, t: (bi, 0, t, 0)), _slab_spec(tile, lambda bi, t: (bi, 0, t, 0))),
        out_shape=(jax.ShapeDtypeStruct((b, n, d), F32),) + (jax.ShapeDtypeStruct((b, d // LANES, n, LANES), F32),) * 2,
        compiler_params=_params("arbitrary", "arbitrary"),
        name="odd_in",
    )(x, moe, gate, g, shift, scale, w)


def _lru_kernel(*refs, reverse, nt):
    if reverse:
        (u_ref, up_ref, un_ref, cw_ref, cb_ref, wax_ref, ba_ref, bx_ref, lam_ref, h0_ref, hf_ref, gg_ref,
         out_ref, hlast_ref, a_scr, b_scr, uc_scr, carry_scr) = refs
    else:
        (u_ref, up_ref, un_ref, cw_ref, cb_ref, wax_ref, ba_ref, bx_ref, lam_ref, h0_ref,
         out_ref, hlast_ref, a_scr, b_scr, uc_scr, carry_scr) = refs
    n_slab, tile = u_ref.shape[1], u_ref.shape[2]
    w = n_slab * LANES
    per = tile // SUBLANES
    phase = lambda ref, j: jnp.concatenate(
        [ref[0, c, pl.ds(j, per, stride=SUBLANES), :] for c in range(n_slab)], axis=1)
    rows = lambda j: slice(j * per, (j + 1) * per)
    t = pl.program_id(1)
    pos = (nt - 1 - t) if reverse else t
    prev = jnp.where(pos == 0, 0.0, _load_slabs(up_ref))
    nxt = jnp.where(pos == nt - 1, 0.0, _load_slabs(un_ref))

    rowid = lax.broadcasted_iota(jnp.int32, (per, w), 0)
    shift_down = lambda x, first: jnp.where(rowid == 0, first, pltpu.roll(x, 1, 0))
    shift_up = lambda x, last: jnp.where(rowid == per - 1, last, pltpu.roll(x, per - 1, 0))
    u = [phase(u_ref, j) for j in range(SUBLANES)]
    um1 = [shift_down(u[7], prev[7:8])] + u[:7]
    um2 = [shift_down(u[6], prev[6:7]), um1[0]] + u[:6]
    up1 = u[1:] + [shift_up(u[0], nxt[0:1])]
    cw = cw_ref[...]
    for j in range(SUBLANES):
        uc_scr[rows(j), :] = (cb_ref[...] + cw[0:1] * um2[j] + cw[1:2] * um1[j] + cw[2:3] * u[j] + cw[3:4] * up1[j])

    lam = lam_ref[0]
    half_decay = (0.5 * LRU_C) * (jnp.maximum(-lam, 0.0) + jnp.log1p(jnp.exp(-jnp.abs(lam))))
    for hd in range(LRU_HEADS):
        sl = slice(hd * LRU_BLOCK, (hd + 1) * LRU_BLOCK)
        uc = uc_scr[:, sl]
        z = jnp.dot(uc.astype(MXU_DTYPE), wax_ref[0, hd], preferred_element_type=F32)
        hd_row = half_decay[:, sl]
        neg_log_a = hd_row * jnp.tanh(z[:, :LRU_BLOCK] + ba_ref[0][:, sl]) + hd_row
        gate2 = 1.0 + jnp.tanh(z[:, LRU_BLOCK:] + bx_ref[0][:, sl])
        a = jnp.exp(-neg_log_a)
        m2 = jnp.tanh(neg_log_a) * (a * a + 1.0)
        mult = jnp.where(m2 > 0.0, m2 * lax.rsqrt(m2), 0.0)
        a_scr[:, sl] = a
        b_scr[:, sl] = (0.5 * mult) * (gate2 * uc)

    @pl.when(t == 0)
    def _():
        carry_scr[...] = jnp.broadcast_to(h0_ref[0], carry_scr.shape)

    order = list(range(SUBLANES))[::-1] if reverse else list(range(SUBLANES))
    hrun = b_scr[rows(order[0]), :]
    prun = a_scr[rows(order[0]), :]
    for j in order[1:]:
        aj = a_scr[rows(j), :]
        hrun = aj * hrun + b_scr[rows(j), :]
        prun = aj * prun
        b_scr[rows(j), :] = hrun
        a_scr[rows(j), :] = prun

    lane_row = lax.broadcasted_iota(jnp.int32, (SUBLANES, w), 0)
    carry = carry_scr[...]
    groups = list(range(per // SUBLANES))
    entering = [None] * len(groups)
    for m in (groups[::-1] if reverse else groups):
        a = prun[m * SUBLANES:(m + 1) * SUBLANES]
        bcoef = hrun[m * SUBLANES:(m + 1) * SUBLANES]
        for dist in (1, 2, 4):
            shift = (SUBLANES - dist) if reverse else dist
            msk = (lane_row < SUBLANES - dist) if reverse else (lane_row >= dist)
            a_s = pltpu.roll(a, shift, 0)
            b_s = pltpu.roll(bcoef, shift, 0)
            bcoef = jnp.where(msk, a * b_s + bcoef, bcoef)
            a = jnp.where(msk, a * a_s, a)
        after = a * carry + bcoef
        if reverse:
            entering[m] = jnp.where(lane_row == SUBLANES - 1, carry, pltpu.roll(after, SUBLANES - 1, 0))
            carry = jnp.broadcast_to(after[0:1], carry.shape)
        else:
            entering[m] = jnp.where(lane_row == 0, carry, pltpu.roll(after, 1, 0))
            carry = jnp.broadcast_to(after[SUBLANES - 1:SUBLANES], carry.shape)
    carry_scr[...] = carry
    hlast_ref[0] = carry[0:1]
    h_in = jnp.concatenate(entering, axis=0)

    for j in range(SUBLANES):
        hcur = b_scr[rows(j), :] + a_scr[rows(j), :] * h_in
        if reverse:
            hcur = phase(gg_ref, j) * (phase(hf_ref, j) + hcur)
        for c in range(n_slab):
            out_ref[0, c, pl.ds(j, per, stride=SUBLANES), :] = hcur[:, c * LANES:(c + 1) * LANES]


def _lru(u, cw, cb, wax, ba, bx, lam, h0, direction, tile, hf=None, gg=None):
    b, n_slab, n, _ = u.shape
    w = n_slab * LANES
    nt = n // tile
    per = tile // HALO
    reverse = direction == 1
    pos = (lambda t: nt - 1 - t) if reverse else (lambda t: t)
    tok = _slab_spec(tile, lambda bi, t: (bi, 0, pos(t), 0), w)
    rowspec = pl.BlockSpec((1, 1, w), lambda bi, t: (direction, 0, 0))
    in_specs = [
        tok,
        _slab_spec(HALO, lambda bi, t: (bi, 0, jnp.maximum(pos(t) * per - 1, 0), 0), w),
        _slab_spec(HALO, lambda bi, t: (bi, 0, jnp.minimum((pos(t) + 1) * per, n // HALO - 1), 0), w),
        pl.BlockSpec(cw.shape, lambda bi, t: (0, 0)),
        pl.BlockSpec(cb.shape, lambda bi, t: (0, 0)),
        pl.BlockSpec((1,) + wax.shape[1:], lambda bi, t: (direction, 0, 0, 0)),
        rowspec, rowspec, rowspec,
        pl.BlockSpec((1, 1, w), lambda bi, t: (bi, 0, 0)),
    ]
    args = [u, u, u, cw, cb, wax, ba, bx, lam, h0]
    if reverse:
        in_specs += [tok, tok]
        args += [hf, gg]
    assert tile % (SUBLANES * SUBLANES) == 0
    return pl.pallas_call(
        functools.partial(_lru_kernel, reverse=reverse, nt=nt),
        grid=(b, nt),
        in_specs=in_specs,
        out_specs=(tok, pl.BlockSpec((1, 1, w), lambda bi, t: (bi, 0, 0))),
        out_shape=(jax.ShapeDtypeStruct(u.shape, F32), jax.ShapeDtypeStruct((b, 1, w), F32)),
        scratch_shapes=[pltpu.VMEM((tile, w), F32), pltpu.VMEM((tile, w), F32), pltpu.VMEM((tile, w), F32),
                        pltpu.VMEM((SUBLANES, w), F32)],
        compiler_params=_params("arbitrary", "arbitrary"),
        name="lru_bwd" if reverse else "lru_fwd",
    )(*args)


def _final_kernel(x_ref, moe_ref, gate_ref, g_ref, o_ref):
    x = x_ref[0] + gate_ref[0] * _load_token_tiles(moe_ref)
    o_ref[0] = x * lax.rsqrt(jnp.mean(x * x, axis=-1, keepdims=True) + EPS) * g_ref[...]


def _final(x, moe, gate, g, tile):
    b, n, d = x.shape
    tok = pl.BlockSpec((1, tile, d), lambda bi, t: (bi, t, 0))
    tiles = pl.BlockSpec((1, tile * SUBLANES, LANES), lambda bi, t: (bi, t, 0))
    return pl.pallas_call(
        _final_kernel,
        grid=(b, n // tile),
        in_specs=[tok, tiles, pl.BlockSpec((1, 1, d), lambda bi, t: (bi, 0, 0)), pl.BlockSpec((1, d), lambda bi, t: (0, 0))],
        out_specs=tok,
        out_shape=jax.ShapeDtypeStruct((b, n, d), F32),
        compiler_params=_params("arbitrary", "arbitrary"),
        name="final_norm",
    )(x, moe, gate, g)


def kernel(x, c, ctx, c_ctx, ada_w, ada_b, norm_mix_g, norm_ffn_g, ev_w_in, ev_w_out, ev_sink, ev_conv_w, ev_conv_b, od_w_in, od_w_out, od_conv_w, od_conv_b, od_wa, od_ba, od_wx, od_bx, od_lambda, router_w, w_gate, w_up, w_down, final_g):
    b, n, d = x.shape
    lc = ctx.shape[1]
    depth = ada_w.shape[0]
    assert depth == 2 and d == D_MODEL and b < MOD_ROWS
    tile_l = min(512, n)
    tile_c = lc
    ctx_row = b

    cvec = jnp.zeros((MOD_ROWS, d), F32).at[:b].set(c).at[b].set(c_ctx)
    mods = _ada(cvec, ada_w, ada_b).reshape(depth, MOD_ROWS, 6, 1, d)
    mod = lambda l, j: mods[l, :, j]

    def router_split(l):
        return _split_hi_lo(jnp.pad(router_w[l], ((0, 0), (0, LANES - N_EXPERTS))))

    bf = lambda a: a.astype(MXU_DTYPE)

    g_mix = norm_mix_g[0].reshape(1, d)
    g_ffn = norm_ffn_g[0].reshape(1, d)
    w_in = bf(ev_w_in[0])
    w_out = bf(ev_w_out[0])
    conv_p = (ev_conv_w[0], ev_conv_b[0].reshape(1, -1))
    tables = _rope_tables(n)
    ql, kvl, gbl, cul = _even_in(x, g_mix, mod(0, 0), mod(0, 1), None, w_in, tables, tile_l)
    qc, kvc, gbc, cuc = _even_in(ctx, g_mix, mod(0, 0), mod(0, 1), ctx_row, w_in, None, tile_c)
    att_l = _attention(ev_sink[0], ql, kvl, kvc)
    att_c = _attention(ev_sink[0], qc, None, kvc)
    rw_hi, rw_lo = router_split(0)
    xl, hl, at_l = _mix_out((att_l, gbl, cul), conv_p, w_out, x, mod(0, 2), g_ffn, mod(0, 3), mod(0, 4), None,
                            rw_hi, rw_lo, tile_l)
    xc, hc, at_c = _mix_out((att_c, gbc, cuc), conv_p, w_out, ctx, mod(0, 2), g_ffn, mod(0, 3), mod(0, 4),
                            ctx_row, rw_hi, rw_lo, tile_c)
    moe_l = _expert_choice(hl, at_l, 0, w_gate, w_up, w_down, 1)
    moe_c = _expert_choice(hc, at_c, 0, w_gate, w_up, w_down, b)

    g_mix = norm_mix_g[1].reshape(1, d)
    g_ffn = norm_ffn_g[1].reshape(1, d)
    w_in = bf(od_w_in[0])
    w_out = bf(od_w_out[0])
    xl, ggl, ul = _odd_in(xl, moe_l, mod(0, 5), g_mix, mod(1, 0), mod(1, 1), None, w_in, tile_l)
    _, _, uc = _odd_in(xc, moe_c, mod(0, 5), g_mix, mod(1, 0), mod(1, 1), ctx_row, w_in, tile_c)
    cw, cb = od_conv_w[0], od_conv_b[0].reshape(1, -1)
    wax = bf(0.5 * jnp.concatenate([od_wa[0], od_wx[0]], axis=-1))
    ba, bx = (0.5 * a[0].reshape(2, 1, -1) for a in (od_ba, od_bx))
    lam = od_lambda[0].reshape(2, 1, -1)
    zero_state = jnp.zeros((b, 1, LRU_WIDTH), F32)
    hf_c, h0_f = _lru(uc, cw, cb, wax, ba, bx, lam, zero_state, 0, tile_c)
    _, h0_b = _lru(uc, cw, cb, wax, ba, bx, lam, zero_state, 1, tile_c, hf=hf_c, gg=hf_c)
    hf_l, _ = _lru(ul, cw, cb, wax, ba, bx, lam, h0_f, 0, tile_l)
    yl, _ = _lru(ul, cw, cb, wax, ba, bx, lam, h0_b, 1, tile_l, hf=hf_l, gg=ggl)
    rw_hi, rw_lo = router_split(1)
    xl, hl, at_l = _mix_out(yl, None, w_out, xl, mod(1, 2), g_ffn, mod(1, 3), mod(1, 4), None,
                            rw_hi, rw_lo, tile_l)
    moe_l = _expert_choice(hl, at_l, 1, w_gate, w_up, w_down, 1)
    return _final(xl, moe_l, mod(1, 5), final_g.reshape(1, d), tile_l)
```

```python
import functools

import jax
import jax.numpy as jnp
from jax import lax
from jax.experimental import pallas as pl
from jax.experimental.pallas import tpu as pltpu

F32 = jnp.float32
MXU_DTYPE = jnp.bfloat16

D_MODEL = 1024
GRID_W = 64
EPS = 1e-6
NEG_INF = -1e30
HEAD_DIM = 64
N_Q_HEADS = 8
N_KV_HEADS = 2
Q_PER_KV = N_Q_HEADS // N_KV_HEADS
ATTN_WIDTH = N_Q_HEADS * HEAD_DIM
KV_WIDTH = N_KV_HEADS * HEAD_DIM
WINDOW = 128
BLOCK = 128
ROPE_BASE = 10000.0
CONV_B_WIDTH = D_MODEL // 2
EVEN_IN = ATTN_WIDTH + 2 * KV_WIDTH + 3 * CONV_B_WIDTH
LRU_WIDTH = D_MODEL
LRU_HEADS = 8
LRU_BLOCK = LRU_WIDTH // LRU_HEADS
LRU_C = 8.0
N_EXPERTS = 16
CAPACITY_FACTOR = 2
EXPERT_FF = 1408
MOD_ROWS = 16
LANES = 128
SUBLANES = 8
HALO = SUBLANES
CUMSUM_TILE = 256
MIX_OUT_SPLIT = 2
V7X_VMEM_LIMIT = 56 * 1024 * 1024


def _params(*sem):
    return pltpu.CompilerParams(dimension_semantics=sem, vmem_limit_bytes=V7X_VMEM_LIMIT)


def _split_hi_lo(a):
    hi = a.astype(MXU_DTYPE)
    lo = (a - hi.astype(F32)).astype(MXU_DTYPE)
    return hi, lo


def _sigmoid(z):
    return 0.5 * (1.0 + jnp.tanh(0.5 * z))


def _store_token_tiles(ref, val, row0=0):
    tile = val.shape[0]
    for j in range(SUBLANES):
        ref[0, pl.ds(row0 * SUBLANES + j, tile, stride=SUBLANES), :] = val[:, j * LANES:(j + 1) * LANES]


def _load_token_tiles(ref):
    tile = ref.shape[1] // SUBLANES
    return jnp.concatenate([ref[0, pl.ds(j, tile, stride=SUBLANES), :] for j in range(SUBLANES)], axis=1)


def _slab_spec(rows, index_map, width=LRU_WIDTH):
    return pl.BlockSpec((1, width // LANES, rows, LANES), index_map)


def _store_slabs(ref, val):
    for c in range(val.shape[1] // LANES):
        ref[0, c] = val[:, c * LANES:(c + 1) * LANES]


def _load_slabs(ref):
    return jnp.concatenate([ref[0, c] for c in range(ref.shape[1])], axis=1)


def _norm_mod(x, g, shift, scale):
    y = x * lax.rsqrt(jnp.mean(x * x, axis=-1, keepdims=True) + EPS) * g
    return y * (1.0 + scale) + shift


def _ada_kernel(c_ref, w_ref, b_ref, o_ref):
    c = c_ref[...]
    s_hi, s_lo = _split_hi_lo(c * _sigmoid(c))
    w_hi, w_lo = _split_hi_lo(w_ref[0])
    acc = jnp.dot(s_hi, w_hi, preferred_element_type=F32)
    acc += jnp.dot(s_hi, w_lo, preferred_element_type=F32)
    acc += jnp.dot(s_lo, w_hi, preferred_element_type=F32)
    o_ref[0] = acc + b_ref[0]


def _ada(cvec, ada_w, ada_b):
    depth, d, n6 = ada_w.shape
    tn = 1536
    return pl.pallas_call(
        _ada_kernel,
        grid=(depth, n6 // tn),
        in_specs=[
            pl.BlockSpec((MOD_ROWS, d), lambda l, j: (0, 0)),
            pl.BlockSpec((1, d, tn), lambda l, j: (l, 0, j)),
            pl.BlockSpec((1, 1, tn), lambda l, j: (l, 0, j)),
        ],
        out_specs=pl.BlockSpec((1, MOD_ROWS, tn), lambda l, j: (l, 0, j)),
        out_shape=jax.ShapeDtypeStruct((depth, MOD_ROWS, n6), F32),
        compiler_params=_params("arbitrary", "arbitrary"),
        name="ada",
    )(cvec, ada_w, ada_b.reshape(depth, 1, n6))


def _even_in_kernel(*refs, rope):
    if rope:
        (x_ref, g_ref, sh_ref, sc_ref, w_ref, cos_ref, sa_ref, sb_ref, q_ref, kv_ref, gb_ref, cu_ref) = refs
    else:
        (x_ref, g_ref, sh_ref, sc_ref, w_ref, q_ref, kv_ref, gb_ref, cu_ref) = refs
    h = _norm_mod(x_ref[0], g_ref[...], sh_ref[0], sc_ref[0]).astype(MXU_DTYPE)
    y = jnp.dot(h, w_ref[...], preferred_element_type=F32)
    q = y[:, :ATTN_WIDTH]
    k = y[:, ATTN_WIDTH:ATTN_WIDTH + KV_WIDTH]
    v = y[:, ATTN_WIDTH + KV_WIDTH:ATTN_WIDTH + 2 * KV_WIDTH]
    c0 = ATTN_WIDTH + 2 * KV_WIDTH
    if rope:
        cos, sa, sb = cos_ref[...], sa_ref[...], sb_ref[...]

        def rot(z):
            return z * cos + pltpu.roll(z, 16, 1) * sa + pltpu.roll(z, 112, 1) * sb

        q = jnp.concatenate([rot(q[:, j * 128:(j + 1) * 128]) for j in range(ATTN_WIDTH // 128)], axis=1)
        k = rot(k)
    q_ref[0] = (q * (HEAD_DIM ** -0.5)).astype(q_ref.dtype)
    kv_ref[0] = jnp.concatenate([k, v], axis=1).astype(kv_ref.dtype)
    gb_ref[0] = y[:, c0:c0 + CONV_B_WIDTH]
    cu_ref[0] = y[:, c0 + CONV_B_WIDTH:c0 + 2 * CONV_B_WIDTH] * y[:, c0 + 2 * CONV_B_WIDTH:]


def _even_in(x, g, shift, scale, mod_row, w, tables, tile):
    b, n, d = x.shape
    nt = n // tile
    rope = tables is not None
    row = (lambda bi: bi) if mod_row is None else (lambda bi: mod_row)
    in_specs = [
        pl.BlockSpec((1, tile, d), lambda bi, t: (bi, t, 0)),
        pl.BlockSpec((1, d), lambda bi, t: (0, 0)),
        pl.BlockSpec((1, 1, d), lambda bi, t: (row(bi), 0, 0)),
        pl.BlockSpec((1, 1, d), lambda bi, t: (row(bi), 0, 0)),
        pl.BlockSpec(w.shape, lambda bi, t: (0, 0)),
    ]
    args = [x, g, shift, scale, w]
    if rope:
        in_specs += [pl.BlockSpec((tile, 128), lambda bi, t: (t, 0))] * 3
        args += list(tables)
    out_shape = (
        jax.ShapeDtypeStruct((b, n, ATTN_WIDTH), MXU_DTYPE),
        jax.ShapeDtypeStruct((b, n, 2 * KV_WIDTH), MXU_DTYPE),
        jax.ShapeDtypeStruct((b, n, CONV_B_WIDTH), F32),
        jax.ShapeDtypeStruct((b, n, CONV_B_WIDTH), F32),
    )
    out_specs = tuple(pl.BlockSpec((1, tile, s.shape[-1]), lambda bi, t: (bi, t, 0)) for s in out_shape)
    return pl.pallas_call(
        functools.partial(_even_in_kernel, rope=rope),
        grid=(b, nt),
        in_specs=in_specs,
        out_specs=out_specs,
        out_shape=out_shape,
        compiler_params=_params("arbitrary", "arbitrary"),
        name="even_in_rope" if rope else "even_in",
    )(*args)


def _rope_tables(n):
    nf = HEAD_DIM // 4
    pos = jnp.arange(n)
    rows = (pos // GRID_W).astype(F32)
    cols = (pos % GRID_W).astype(F32)
    lane = jnp.arange(128)
    inv = ROPE_BASE ** (-(lane % nf).astype(F32) / nf)
    use_col = (lane % HEAD_DIM) >= HEAD_DIM // 2
    ang = jnp.where(use_col[None, :], cols[:, None], rows[:, None]) * inv[None, :]
    cos, sin = jnp.cos(ang), jnp.sin(ang)
    second = ((lane % (2 * nf)) >= nf)[None, :]
    return cos, jnp.where(second, sin, 0.0), jnp.where(second, 0.0, -sin)


def _attn_kernel(*refs, n, has_local):
    if has_local:
        sink_ref, q_ref, kv_ref, kvc_ref, o_ref = refs
    else:
        sink_ref, q_ref, kvc_ref, o_ref = refs
    i = pl.program_id(1)
    q = q_ref[0]
    kvall = kvc_ref[0]
    n_loc = 3 * BLOCK
    if has_local:
        start = pl.multiple_of(jnp.clip(i * BLOCK - BLOCK, 0, n - n_loc), BLOCK)
        kvall = jnp.concatenate([kv_ref[0, pl.ds(start, n_loc), :], kvall], axis=0)
        qpos = i * BLOCK + lax.broadcasted_iota(jnp.int32, (Q_PER_KV * BLOCK, n_loc), 0) % BLOCK
        kpos = start + lax.broadcasted_iota(jnp.int32, (Q_PER_KV * BLOCK, n_loc), 1)
        valid = jnp.abs(kpos - qpos) <= WINDOW
    grp = lax.broadcasted_iota(jnp.int32, (Q_PER_KV * BLOCK, 1), 0) // BLOCK
    outs = []
    for hk in range(N_KV_HEADS):
        kh = kvall[:, hk * HEAD_DIM:(hk + 1) * HEAD_DIM]
        vh = kvall[:, KV_WIDTH + hk * HEAD_DIM:KV_WIDTH + (hk + 1) * HEAD_DIM]
        qg = jnp.concatenate(
            [q[:, (hk * Q_PER_KV + g) * HEAD_DIM:(hk * Q_PER_KV + g + 1) * HEAD_DIM] for g in range(Q_PER_KV)], axis=0)
        s = lax.dot_general(qg, kh, (((1,), (1,)), ((), ())), preferred_element_type=F32)
        if has_local:
            s = jnp.concatenate([jnp.where(valid, s[:, :n_loc], NEG_INF), s[:, n_loc:]], axis=1)
        snk = jnp.zeros((Q_PER_KV * BLOCK, 1), F32)
        for g in range(Q_PER_KV):
            snk = jnp.where(grp == g, sink_ref[hk * Q_PER_KV + g], snk)
        m = jnp.maximum(jnp.max(s, axis=1, keepdims=True), snk)
        p = jnp.exp(s - m)
        denom = jnp.sum(p, axis=1, keepdims=True) + jnp.exp(snk - m)
        o = jnp.dot(p.astype(MXU_DTYPE), vh, preferred_element_type=F32) / denom
        outs += [o[g * BLOCK:(g + 1) * BLOCK] for g in range(Q_PER_KV)]
    o_ref[0] = jnp.concatenate(outs, axis=1).astype(o_ref.dtype)


def _attention(sink, q, kv, kvc):
    b, n, _ = q.shape
    lc = kvc.shape[1]
    has_local = kv is not None
    in_specs = [pl.BlockSpec(memory_space=pltpu.SMEM), pl.BlockSpec((1, BLOCK, ATTN_WIDTH), lambda bi, i: (bi, i, 0))]
    args = [sink, q]
    if has_local:
        in_specs.append(pl.BlockSpec((1, n, 2 * KV_WIDTH), lambda bi, i: (bi, 0, 0)))
        args.append(kv)
    in_specs.append(pl.BlockSpec((1, lc, 2 * KV_WIDTH), lambda bi, i: (bi, 0, 0)))
    args.append(kvc)
    return pl.pallas_call(
        functools.partial(_attn_kernel, n=n, has_local=has_local),
        grid=(b, n // BLOCK),
        in_specs=in_specs,
        out_specs=pl.BlockSpec((1, BLOCK, ATTN_WIDTH), lambda bi, i: (bi, i, 0)),
        out_shape=jax.ShapeDtypeStruct((b, n, ATTN_WIDTH), MXU_DTYPE),
        compiler_params=_params("arbitrary", "arbitrary"),
        name="attn_local" if has_local else "attn_ctx",
    )(*args)


def _mix_out_kernel(*refs, conv, first, last):
    if conv:
        (att_ref, gb_ref, cu_ref, cup_ref, cun_ref, cw_ref, cb_ref,
         w_ref, x_ref, gate_ref, g2_ref, sh2_ref, sc2_ref, rwh_ref, rwl_ref, xo_ref, h_ref, afft_ref) = refs
        t = pl.program_id(1)
        prev = jnp.where(t == first, 0.0, cup_ref[0])
        nxt = jnp.where(t == last, 0.0, cun_ref[0])
        ext = jnp.concatenate([prev, cu_ref[0], nxt], axis=0)
        cw = cw_ref[...]
    else:
        (y_ref, w_ref, x_ref, gate_ref, g2_ref, sh2_ref, sc2_ref, rwh_ref, rwl_ref, xo_ref, h_ref, afft_ref) = refs
    tile = x_ref.shape[1]
    rows = tile // MIX_OUT_SPLIT
    for s in range(MIX_OUT_SPLIT):
        r0 = s * rows
        rs = slice(r0, r0 + rows)
        if conv:
            cv = (cw[0:1] * ext[HALO - 1 + r0:HALO - 1 + r0 + rows] + cw[1:2] * ext[HALO + r0:HALO + r0 + rows]
                  + cw[2:3] * ext[HALO + 1 + r0:HALO + 1 + r0 + rows] + cb_ref[...])
            cat = jnp.concatenate([att_ref[0, rs, :], (gb_ref[0, rs, :] * cv).astype(MXU_DTYPE)], axis=1)
        else:
            cat = jnp.concatenate([y_ref[0, c, rs, :] for c in range(y_ref.shape[1])], axis=1).astype(MXU_DTYPE)
        y = jnp.dot(cat, w_ref[...], preferred_element_type=F32)
        x = x_ref[0, rs, :] + gate_ref[0] * y
        xo_ref[0, rs, :] = x
        h = _norm_mod(x, g2_ref[...], sh2_ref[0], sc2_ref[0])
        _store_token_tiles(h_ref, h, r0)
        h_hi, h_lo = _split_hi_lo(h)
        logits = jnp.dot(h_hi, rwh_ref[...], preferred_element_type=F32)
        logits += jnp.dot(h_hi, rwl_ref[...], preferred_element_type=F32)
        logits += jnp.dot(h_lo, rwh_ref[...], preferred_element_type=F32)
        lane = lax.broadcasted_iota(jnp.int32, logits.shape, 1)
        logits = jnp.where(lane < N_EXPERTS, logits, NEG_INF)
        e = jnp.exp(logits - jnp.max(logits, axis=1, keepdims=True))
        aff = e / jnp.sum(e, axis=1, keepdims=True)
        afft_ref[0, :, rs] = aff.T[:N_EXPERTS]


def _mix_out(mix_in, conv_params, w, x, gate, g2, sh2, sc2, mod_row, rw_hi, rw_lo, tile):
    b, n, d = x.shape
    nt = n // tile
    conv = conv_params is not None
    row = (lambda bi: bi) if mod_row is None else (lambda bi: mod_row)
    tok = lambda width: pl.BlockSpec((1, tile, width), lambda bi, t: (bi, t, 0))
    modspec = pl.BlockSpec((1, 1, d), lambda bi, t: (row(bi), 0, 0))
    full = lambda a: pl.BlockSpec(a.shape, lambda bi, t: (0,) * a.ndim)
    if conv:
        att, gb, cu = mix_in
        cw, cb = conv_params
        per = tile // HALO
        in_specs = [tok(ATTN_WIDTH), tok(CONV_B_WIDTH), tok(CONV_B_WIDTH),
                    pl.BlockSpec((1, HALO, CONV_B_WIDTH), lambda bi, t: (bi, jnp.maximum(t * per - 1, 0), 0)),
                    pl.BlockSpec((1, HALO, CONV_B_WIDTH), lambda bi, t: (bi, jnp.minimum((t + 1) * per, n // HALO - 1), 0)),
                    full(cw), full(cb)]
        args = [att, gb, cu, cu, cu, cw, cb]
    else:
        in_specs = [_slab_spec(tile, lambda bi, t: (bi, 0, t, 0), d)]
        args = [mix_in]
    in_specs += [full(w), tok(d), modspec, full(g2), modspec, modspec, full(rw_hi), full(rw_lo)]
    args += [w, x, gate, g2, sh2, sc2, rw_hi, rw_lo]
    out_shape = (jax.ShapeDtypeStruct((b, n, d), F32), jax.ShapeDtypeStruct((b, n * SUBLANES, LANES), F32),
                 jax.ShapeDtypeStruct((b, N_EXPERTS, n), F32))
    out_specs = (tok(d), pl.BlockSpec((1, tile * SUBLANES, LANES), lambda bi, t: (bi, t, 0)),
                 pl.BlockSpec((1, N_EXPERTS, tile), lambda bi, t: (bi, 0, t)))
    return pl.pallas_call(
        functools.partial(_mix_out_kernel, conv=conv, first=0, last=nt - 1),
        grid=(b, nt),
        in_specs=in_specs,
        out_specs=out_specs,
        out_shape=out_shape,
        compiler_params=_params("arbitrary", "arbitrary"),
        name="even_out" if conv else "odd_out",
    )(*args)


def _cumsum_lanes(x):
    n = x.shape[1]
    r = lax.broadcasted_iota(jnp.int32, (CUMSUM_TILE, CUMSUM_TILE), 0)
    c = lax.broadcasted_iota(jnp.int32, (CUMSUM_TILE, CUMSUM_TILE), 1)
    tri = jnp.where(r <= c, 1.0, 0.0).astype(MXU_DTYPE)
    carry = jnp.zeros((x.shape[0], 1), F32)
    outs = []
    for k in range(n // CUMSUM_TILE):
        blk = x[:, k * CUMSUM_TILE:(k + 1) * CUMSUM_TILE].astype(MXU_DTYPE)
        loc = jnp.dot(blk, tri, preferred_element_type=F32) + carry
        outs.append(loc)
        carry = loc[:, CUMSUM_TILE - 1:CUMSUM_TILE]
    return jnp.concatenate(outs, axis=1)


SLOT_EMPTY = 1 << 20


def _select_kernel(at_ref, idx_ref, g_ref, *, cap):
    at = at_ref[0]
    n_exp, n = at.shape
    capf = float(cap)

    def count_ge(thr):
        return jnp.sum(jnp.where(at >= thr, 1.0, 0.0), axis=1, keepdims=True)

    def bit_body(_, c):
        lo_i, hi_i = c
        mid = lo_i + ((hi_i - lo_i) >> 1)
        ge = count_ge(lax.bitcast_convert_type(mid, F32)) >= capf
        return jnp.where(ge, mid, lo_i), jnp.where(ge, hi_i, mid)

    lo_i, hi_i = lax.fori_loop(
        0, 31, bit_body, (jnp.zeros((n_exp, 1), jnp.int32), jnp.full((n_exp, 1), 0x3F800001, jnp.int32)))

    def val_body(_, c):
        lo, hi = c
        mid = 0.5 * (lo + hi)
        ge = count_ge(mid) >= capf
        return jnp.where(ge, mid, lo), jnp.where(ge, hi, mid)

    lo, hi = lax.fori_loop(
        0, 24, val_body, (lax.bitcast_convert_type(lo_i, F32), lax.bitcast_convert_type(hi_i, F32)))
    need = capf - count_ge(hi)

    above = jnp.where(at >= hi, 1.0, 0.0)
    band = jnp.where(at >= lo, 1.0, 0.0) - above
    sel = above + band * jnp.where(_cumsum_lanes(band) <= need, 1.0, 0.0)
    rank = _cumsum_lanes(sel)

    lane = lax.broadcasted_iota(jnp.int32, (n_exp, n), 1)
    disp = jnp.where(sel > 0.5, lane + 1 - rank.astype(jnp.int32), SLOT_EMPTY)
    g = at
    for k in range(n.bit_length() - 1):
        step = 1 << k
        moving = ((disp >> k) & 1) == 1
        disp_in = pltpu.roll(disp, n - step, 1)
        arriving = ((disp_in >> k) & 1) == 1
        g = jnp.where(arriving, pltpu.roll(g, n - step, 1), g)
        disp = jnp.where(arriving, disp_in, jnp.where(moving, SLOT_EMPTY, disp))

    capp = pl.cdiv(cap, LANES) * LANES
    slot = lax.broadcasted_iota(jnp.int32, (n_exp, cap), 1)
    idx_ref[0] = jnp.clip(slot + disp[:, :cap], 0, n - 1)
    g_pad = jnp.concatenate([g[:, :capp], jnp.zeros((LANES - n_exp, capp), F32)], axis=0)
    g_ref[0] = g_pad.T[:cap]


def _select(aff_t, cap):
    b, n_exp, n = aff_t.shape
    assert n & (n - 1) == 0 and n % CUMSUM_TILE == 0 and n < SLOT_EMPTY
    return pl.pallas_call(
        functools.partial(_select_kernel, cap=cap),
        grid=(b,),
        in_specs=[pl.BlockSpec((1, n_exp, n), lambda bi: (bi, 0, 0))],
        out_specs=(pl.BlockSpec((1, n_exp, cap), lambda bi: (bi, 0, 0)), pl.BlockSpec((1, cap, LANES), lambda bi: (bi, 0, 0))),
        out_shape=(jax.ShapeDtypeStruct((b, n_exp, cap), jnp.int32), jax.ShapeDtypeStruct((b, cap, LANES), F32)),
        compiler_params=_params("arbitrary"),
        name="moe_select",
    )(aff_t)


def _gather_kernel(idx_ref, src_ref, xe_ref, xcm_ref, *, cap, n_exp):
    base = (pl.program_id(0) * n_exp + pl.program_id(1)) * cap
    stride = cap + SUBLANES

    def body(gi, c):
        for i in range(SUBLANES):
            r = gi * SUBLANES + i
            t = idx_ref[base + r]
            xcm_ref[pl.ds(r, SUBLANES, stride=stride), :] = src_ref[0, pl.ds(pl.multiple_of(t * SUBLANES, SUBLANES), SUBLANES), :]
        return c

    lax.fori_loop(0, cap // SUBLANES, body, 0)
    xe_ref[0, 0] = jnp.concatenate(
        [xcm_ref[pl.ds(j * stride, cap), :] for j in range(SUBLANES)], axis=1).astype(xe_ref.dtype)


def _gather(idx_flat, src, cap):
    b, rows, _ = src.shape
    d = SUBLANES * LANES
    assert cap % SUBLANES == 0
    return pl.pallas_call(
        functools.partial(_gather_kernel, cap=cap, n_exp=N_EXPERTS),
        grid=(b, N_EXPERTS),
        in_specs=[pl.BlockSpec(memory_space=pltpu.SMEM),
                  pl.BlockSpec((1, rows, LANES), lambda bi, ei: (bi, 0, 0))],
        out_specs=pl.BlockSpec((1, 1, cap, d), lambda bi, ei: (bi, ei, 0, 0)),
        out_shape=jax.ShapeDtypeStruct((b, N_EXPERTS, cap, d), MXU_DTYPE),
        scratch_shapes=[pltpu.VMEM((SUBLANES * (cap + SUBLANES), LANES), F32)],
        compiler_params=_params("arbitrary", "arbitrary"),
        name="moe_gather",
    )(idx_flat, src)


WEIGHT_CHUNKS = 2


def _moe_ffn_kernel(x_ref, g_ref, wg_hbm, wu_hbm, wd_hbm, o_ref, wg_s, wu_s, wd_s, stage_in, stage_out, sem,
                    *, layer, n_steps):
    bb, _, cap, d = x_ref.shape
    n_exp = pl.num_programs(0)
    e = pl.program_id(0)
    s = pl.program_id(1)
    slot = e % 2
    chunks = []
    for hbm, dst, stage in ((wg_hbm, wg_s, stage_in), (wu_hbm, wu_s, stage_in), (wd_hbm, wd_s, stage_out)):
        rows = hbm.shape[2] // WEIGHT_CHUNKS
        chunks += [(hbm, dst, stage, c * rows, rows) for c in range(WEIGHT_CHUNKS)]
    per_step = pl.cdiv(len(chunks), n_steps)

    def copy(c, expert):
        hbm, _, stage, r0, rows = chunks[c]
        return pltpu.make_async_copy(hbm.at[layer, expert, pl.ds(r0, rows), :], stage, sem.at[0])

    def cast(c, to_slot):
        _, dst, stage, r0, rows = chunks[c]
        dst[to_slot, pl.ds(r0, rows), :] = stage[...].astype(dst.dtype)

    @pl.when((e == 0) & (s == 0))
    def _():
        for c in range(len(chunks)):
            copy(c, 0).start()
            copy(c, 0).wait()
            cast(c, 0)

    has_next = e + 1 < n_exp
    for c in range(len(chunks)):
        if c % per_step == 0:
            @pl.when(has_next & (s == c // per_step))
            def _():
                copy(c, e + 1).start()

    x = x_ref[...].reshape(bb * cap, d)
    a = jnp.dot(x, wg_s[slot], preferred_element_type=F32)
    u = jnp.dot(x, wu_s[slot], preferred_element_type=F32)
    mid = (a * _sigmoid(a) * u).astype(MXU_DTYPE)
    y = jnp.dot(mid, wd_s[slot], preferred_element_type=F32)
    aff = g_ref[...].reshape(bb * cap, LANES)
    lane = lax.broadcasted_iota(jnp.int32, aff.shape, 1)
    y = y * jnp.sum(jnp.where(lane == e, aff, 0.0), axis=1, keepdims=True)
    stride = cap + SUBLANES
    for bi in range(bb):
        for j in range(SUBLANES):
            o_ref[bi, 0, pl.ds(j * stride, cap), :] = y[bi * cap:(bi + 1) * cap, j * LANES:(j + 1) * LANES]
            o_ref[bi, 0, pl.ds(j * stride + cap, SUBLANES), :] = jnp.zeros((SUBLANES, LANES), F32)

    for c in range(len(chunks)):
        @pl.when(has_next & (s == c // per_step))
        def _():
            copy(c, e + 1).wait()
            cast(c, 1 - slot)
            if (c + 1) % per_step != 0 and c + 1 < len(chunks):
                copy(c + 1, e + 1).start()


def _moe_ffn(xe, ge, layer, wg, wu, wd, bblk):
    b, e, cap, d = xe.shape
    ff = wg.shape[-1]
    assert d % WEIGHT_CHUNKS == 0 and ff % (WEIGHT_CHUNKS * SUBLANES) == 0
    n_steps = b // bblk
    hbm = pl.BlockSpec(memory_space=pl.ANY)
    return pl.pallas_call(
        functools.partial(_moe_ffn_kernel, layer=layer, n_steps=n_steps),
        grid=(e, n_steps),
        in_specs=[pl.BlockSpec((bblk, 1, cap, d), lambda ei, bi: (bi, ei, 0, 0)),
                  pl.BlockSpec((bblk, cap, LANES), lambda ei, bi: (bi, 0, 0)), hbm, hbm, hbm],
        out_specs=pl.BlockSpec((bblk, 1, SUBLANES * (cap + SUBLANES), LANES), lambda ei, bi: (bi, ei, 0, 0)),
        out_shape=jax.ShapeDtypeStruct((b, e, SUBLANES * (cap + SUBLANES), LANES), F32),
        scratch_shapes=[pltpu.VMEM((2, d, ff), MXU_DTYPE), pltpu.VMEM((2, d, ff), MXU_DTYPE),
                        pltpu.VMEM((2, ff, d), MXU_DTYPE),
                        pltpu.VMEM((d // WEIGHT_CHUNKS, ff), F32), pltpu.VMEM((ff // WEIGHT_CHUNKS, d), F32),
                        pltpu.SemaphoreType.DMA((1,))],
        compiler_params=_params("arbitrary", "arbitrary"),
        name="moe_ffn",
    )(xe, ge, wg, wu, wd)


COMBINE_UNROLL = 8


def _combine_kernel(idx_ref, y_ref, acc_ref, *, cap, n_exp):
    ei = pl.program_id(1)
    base = (pl.program_id(0) * n_exp + ei) * cap
    stride = cap + SUBLANES

    @pl.when(ei == 0)
    def _():
        acc_ref[...] = jnp.zeros_like(acc_ref)

    def body(gi, c):
        toks = [pl.multiple_of(idx_ref[base + gi * COMBINE_UNROLL + i] * SUBLANES, SUBLANES) for i in range(COMBINE_UNROLL)]
        new = [acc_ref[0, pl.ds(toks[i], SUBLANES), :]
               + y_ref[0, 0, pl.ds(gi * COMBINE_UNROLL + i, SUBLANES, stride=stride), :] for i in range(COMBINE_UNROLL)]
        for i in range(COMBINE_UNROLL):
            acc_ref[0, pl.ds(toks[i], SUBLANES), :] = new[i]
        return c

    lax.fori_loop(0, cap // COMBINE_UNROLL, body, 0)


def _combine(idx_flat, y, n, cap):
    b, n_exp, rows, _ = y.shape
    assert cap % COMBINE_UNROLL == 0 and rows == SUBLANES * (cap + SUBLANES)
    return pl.pallas_call(
        functools.partial(_combine_kernel, cap=cap, n_exp=n_exp),
        grid=(b, n_exp),
        in_specs=[pl.BlockSpec(memory_space=pltpu.SMEM),
                  pl.BlockSpec((1, 1, rows, LANES), lambda bi, ei: (bi, ei, 0, 0))],
        out_specs=pl.BlockSpec((1, n * SUBLANES, LANES), lambda bi, ei: (bi, 0, 0)),
        out_shape=jax.ShapeDtypeStruct((b, n * SUBLANES, LANES), F32),
        compiler_params=_params("arbitrary", "arbitrary"),
        name="moe_combine",
    )(idx_flat, y)


def _expert_choice(h_tiles, aff_t, layer, wg, wu, wd, bblk):
    n = aff_t.shape[2]
    cap = max(1, CAPACITY_FACTOR * n // N_EXPERTS)
    idx, g = _select(aff_t, cap)
    idx_flat = idx.reshape(-1)
    xe = _gather(idx_flat, h_tiles, cap)
    y = _moe_ffn(xe, g, layer, wg, wu, wd, bblk)
    return _combine(idx_flat, y, n, cap)


def _odd_in_kernel(x_ref, moe_ref, gate_ref, g_ref, sh_ref, sc_ref, w_ref, xo_ref, gg_ref, u_ref):
    x = x_ref[0] + gate_ref[0] * _load_token_tiles(moe_ref)
    xo_ref[0] = x
    h = _norm_mod(x, g_ref[...], sh_ref[0], sc_ref[0]).astype(MXU_DTYPE)
    y = jnp.dot(h, w_ref[...], preferred_element_type=F32)
    gl = y[:, :LRU_WIDTH]
    _store_slabs(gg_ref, 0.5 * gl * (1.0 + jnp.tanh(0.7978845608028654 * (gl + 0.044715 * gl * gl * gl))))
    _store_slabs(u_ref, y[:, LRU_WIDTH:])


def _odd_in(x, moe, gate, g, shift, scale, mod_row, w, tile):
    b, n, d = x.shape
    row = (lambda bi: bi) if mod_row is None else (lambda bi: mod_row)
    tok = pl.BlockSpec((1, tile, d), lambda bi, t: (bi, t, 0))
    tiles = pl.BlockSpec((1, tile * SUBLANES, LANES), lambda bi, t: (bi, t, 0))
    modspec = pl.BlockSpec((1, 1, d), lambda bi, t: (row(bi), 0, 0))
    return pl.pallas_call(
        _odd_in_kernel,
        grid=(b, n // tile),
        in_specs=[tok, tiles, modspec, pl.BlockSpec((1, d), lambda bi, t: (0, 0)), modspec, modspec,
                  pl.BlockSpec(w.shape, lambda bi, t: (0, 0))],
        out_specs=(tok, _slab_spec(tile, lambda bi, t: (bi, 0, t, 0)), _slab_spec(tile, lambda bi, t: (bi, 0, t, 0))),
        out_shape=(jax.ShapeDtypeStruct((b, n, d), F32),) + (jax.ShapeDtypeStruct((b, d // LANES, n, LANES), F32),) * 2,
        compiler_params=_params("arbitrary", "arbitrary"),
        name="odd_in",
    )(x, moe, gate, g, shift, scale, w)


def _lru_kernel(*refs, reverse, nt):
    if reverse:
        (u_ref, up_ref, un_ref, cw_ref, cb_ref, wax_ref, ba_ref, bx_ref, lam_ref, h0_ref, hf_ref, gg_ref,
         out_ref, hlast_ref, a_scr, b_scr, uc_scr, carry_scr) = refs
    else:
        (u_ref, up_ref, un_ref, cw_ref, cb_ref, wax_ref, ba_ref, bx_ref, lam_ref, h0_ref,
         out_ref, hlast_ref, a_scr, b_scr, uc_scr, carry_scr) = refs
    n_slab, tile = u_ref.shape[1], u_ref.shape[2]
    w = n_slab * LANES
    per = tile // SUBLANES
    phase = lambda ref, j: jnp.concatenate(
        [ref[0, c, pl.ds(j, per, stride=SUBLANES), :] for c in range(n_slab)], axis=1)
    rows = lambda j: slice(j * per, (j + 1) * per)
    t = pl.program_id(1)
    pos = (nt - 1 - t) if reverse else t
    prev = jnp.where(pos == 0, 0.0, _load_slabs(up_ref))
    nxt = jnp.where(pos == nt - 1, 0.0, _load_slabs(un_ref))

    rowid = lax.broadcasted_iota(jnp.int32, (per, w), 0)
    shift_down = lambda x, first: jnp.where(rowid == 0, first, pltpu.roll(x, 1, 0))
    shift_up = lambda x, last: jnp.where(rowid == per - 1, last, pltpu.roll(x, per - 1, 0))
    u = [phase(u_ref, j) for j in range(SUBLANES)]
    um1 = [shift_down(u[7], prev[7:8])] + u[:7]
    um2 = [shift_down(u[6], prev[6:7]), um1[0]] + u[:6]
    up1 = u[1:] + [shift_up(u[0], nxt[0:1])]
    cw = cw_ref[...]
    for j in range(SUBLANES):
        uc_scr[rows(j), :] = (cb_ref[...] + cw[0:1] * um2[j] + cw[1:2] * um1[j] + cw[2:3] * u[j] + cw[3:4] * up1[j])

    lam = lam_ref[0]
    half_decay = (0.5 * LRU_C) * (jnp.maximum(-lam, 0.0) + jnp.log1p(jnp.exp(-jnp.abs(lam))))
    for hd in range(LRU_HEADS):
        sl = slice(hd * LRU_BLOCK, (hd + 1) * LRU_BLOCK)
        uc = uc_scr[:, sl]
        z = jnp.dot(uc.astype(MXU_DTYPE), wax_ref[0, hd], preferred_element_type=F32)
        hd_row = half_decay[:, sl]
        neg_log_a = hd_row * jnp.tanh(z[:, :LRU_BLOCK] + ba_ref[0][:, sl]) + hd_row
        gate2 = 1.0 + jnp.tanh(z[:, LRU_BLOCK:] + bx_ref[0][:, sl])
        a = jnp.exp(-neg_log_a)
        m2 = jnp.tanh(neg_log_a) * (a * a + 1.0)
        mult = jnp.where(m2 > 0.0, m2 * lax.rsqrt(m2), 0.0)
        a_scr[:, sl] = a
        b_scr[:, sl] = (0.5 * mult) * (gate2 * uc)

    @pl.when(t == 0)
    def _():
        carry_scr[...] = jnp.broadcast_to(h0_ref[0], carry_scr.shape)

    order = list(range(SUBLANES))[::-1] if reverse else list(range(SUBLANES))
    hrun = b_scr[rows(order[0]), :]
    prun = a_scr[rows(order[0]), :]
    for j in order[1:]:
        aj = a_scr[rows(j), :]
        hrun = aj * hrun + b_scr[rows(j), :]
        prun = aj * prun
        b_scr[rows(j), :] = hrun
        a_scr[rows(j), :] = prun

    lane_row = lax.broadcasted_iota(jnp.int32, (SUBLANES, w), 0)
    carry = carry_scr[...]
    groups = list(range(per // SUBLANES))
    entering = [None] * len(groups)
    for m in (groups[::-1] if reverse else groups):
        a = prun[m * SUBLANES:(m + 1) * SUBLANES]
        bcoef = hrun[m * SUBLANES:(m + 1) * SUBLANES]
        for dist in (1, 2, 4):
            shift = (SUBLANES - dist) if reverse else dist
            msk = (lane_row < SUBLANES - dist) if reverse else (lane_row >= dist)
            a_s = pltpu.roll(a, shift, 0)
            b_s = pltpu.roll(bcoef, shift, 0)
            bcoef = jnp.where(msk, a * b_s + bcoef, bcoef)
            a = jnp.where(msk, a * a_s, a)
        after = a * carry + bcoef
        if reverse:
            entering[m] = jnp.where(lane_row == SUBLANES - 1, carry, pltpu.roll(after, SUBLANES - 1, 0))
            carry = jnp.broadcast_to(after[0:1], carry.shape)
        else:
            entering[m] = jnp.where(lane_row == 0, carry, pltpu.roll(after, 1, 0))
            carry = jnp.broadcast_to(after[SUBLANES - 1:SUBLANES], carry.shape)
    carry_scr[...] = carry
    hlast_ref[0] = carry[0:1]
    h_in = jnp.concatenate(entering, axis=0)

    for j in range(SUBLANES):
        hcur = b_scr[rows(j), :] + a_scr[rows(j), :] * h_in
        if reverse:
            hcur = phase(gg_ref, j) * (phase(hf_ref, j) + hcur)
        for c in range(n_slab):
            out_ref[0, c, pl.ds(j, per, stride=SUBLANES), :] = hcur[:, c * LANES:(c + 1) * LANES]


def _lru(u, cw, cb, wax, ba, bx, lam, h0, direction, tile, hf=None, gg=None):
    b, n_slab, n, _ = u.shape
    w = n_slab * LANES
    nt = n // tile
    per = tile // HALO
    reverse = direction == 1
    pos = (lambda t: nt - 1 - t) if reverse else (lambda t: t)
    tok = _slab_spec(tile, lambda bi, t: (bi, 0, pos(t), 0), w)
    rowspec = pl.BlockSpec((1, 1, w), lambda bi, t: (direction, 0, 0))
    in_specs = [
        tok,
        _slab_spec(HALO, lambda bi, t: (bi, 0, jnp.maximum(pos(t) * per - 1, 0), 0), w),
        _slab_spec(HALO, lambda bi, t: (bi, 0, jnp.minimum((pos(t) + 1) * per, n // HALO - 1), 0), w),
        pl.BlockSpec(cw.shape, lambda bi, t: (0, 0)),
        pl.BlockSpec(cb.shape, lambda bi, t: (0, 0)),
        pl.BlockSpec((1,) + wax.shape[1:], lambda bi, t: (direction, 0, 0, 0)),
        rowspec, rowspec, rowspec,
        pl.BlockSpec((1, 1, w), lambda bi, t: (bi, 0, 0)),
    ]
    args = [u, u, u, cw, cb, wax, ba, bx, lam, h0]
    if reverse:
        in_specs += [tok, tok]
        args += [hf, gg]
    assert tile % (SUBLANES * SUBLANES) == 0
    return pl.pallas_call(
        functools.partial(_lru_kernel, reverse=reverse, nt=nt),
        grid=(b, nt),
        in_specs=in_specs,
        out_specs=(tok, pl.BlockSpec((1, 1, w), lambda bi, t: (bi, 0, 0))),
        out_shape=(jax.ShapeDtypeStruct(u.shape, F32), jax.ShapeDtypeStruct((b, 1, w), F32)),
        scratch_shapes=[pltpu.VMEM((tile, w), F32), pltpu.VMEM((tile, w), F32), pltpu.VMEM((tile, w), F32),
                        pltpu.VMEM((SUBLANES, w), F32)],
        compiler_params=_params("arbitrary", "arbitrary"),
        name="lru_bwd" if reverse else "lru_fwd",
    )(*args)


def _final_kernel(x_ref, moe_ref, gate_ref, g_ref, o_ref):
    x = x_ref[0] + gate_ref[0] * _load_token_tiles(moe_ref)
    o_ref[0] = x * lax.rsqrt(jnp.mean(x * x, axis=-1, keepdims=True) + EPS) * g_ref[...]


def _final(x, moe, gate, g, tile):
    b, n, d = x.shape
    tok = pl.BlockSpec((1, tile, d), lambda bi, t: (bi, t, 0))
    tiles = pl.BlockSpec((1, tile * SUBLANES, LANES), lambda bi, t: (bi, t, 0))
    return pl.pallas_call(
        _final_kernel,
        grid=(b, n // tile),
        in_specs=[tok, tiles, pl.BlockSpec((1, 1, d), lambda bi, t: (bi, 0, 0)), pl.BlockSpec((1, d), lambda bi, t: (0, 0))],
        out_specs=tok,
        out_shape=jax.ShapeDtypeStruct((b, n, d), F32),
        compiler_params=_params("arbitrary", "arbitrary"),
        name="final_norm",
    )(x, moe, gate, g)


def kernel(x, c, ctx, c_ctx, ada_w, ada_b, norm_mix_g, norm_ffn_g, ev_w_in, ev_w_out, ev_sink, ev_conv_w, ev_conv_b, od_w_in, od_w_out, od_conv_w, od_conv_b, od_wa, od_ba, od_wx, od_bx, od_lambda, router_w, w_gate, w_up, w_down, final_g):
    b, n, d = x.shape
    lc = ctx.shape[1]
    depth = ada_w.shape[0]
    assert depth == 2 and d == D_MODEL and b < MOD_ROWS
    tile_l = min(512, n)
    tile_c = lc
    ctx_row = b

    cvec = jnp.zeros((MOD_ROWS, d), F32).at[:b].set(c).at[b].set(c_ctx)
    mods = _ada(cvec, ada_w, ada_b).reshape(depth, MOD_ROWS, 6, 1, d)
    mod = lambda l, j: mods[l, :, j]

    def router_split(l):
        return _split_hi_lo(jnp.pad(router_w[l], ((0, 0), (0, LANES - N_EXPERTS))))

    bf = lambda a: a.astype(MXU_DTYPE)

    g_mix = norm_mix_g[0].reshape(1, d)
    g_ffn = norm_ffn_g[0].reshape(1, d)
    w_in = bf(ev_w_in[0])
    w_out = bf(ev_w_out[0])
    conv_p = (ev_conv_w[0], ev_conv_b[0].reshape(1, -1))
    tables = _rope_tables(n)
    ql, kvl, gbl, cul = _even_in(x, g_mix, mod(0, 0), mod(0, 1), None, w_in, tables, tile_l)
    qc, kvc, gbc, cuc = _even_in(ctx, g_mix, mod(0, 0), mod(0, 1), ctx_row, w_in, None, tile_c)
    att_l = _attention(ev_sink[0], ql, kvl, kvc)
    att_c = _attention(ev_sink[0], qc, None, kvc)
    rw_hi, rw_lo = router_split(0)
    xl, hl, at_l = _mix_out((att_l, gbl, cul), conv_p, w_out, x, mod(0, 2), g_ffn, mod(0, 3), mod(0, 4), None,
                            rw_hi, rw_lo, tile_l)
    xc, hc, at_c = _mix_out((att_c, gbc, cuc), conv_p, w_out, ctx, mod(0, 2), g_ffn, mod(0, 3), mod(0, 4),
                            ctx_row, rw_hi, rw_lo, tile_c)
    moe_l = _expert_choice(hl, at_l, 0, w_gate, w_up, w_down, 1)
    moe_c = _expert_choice(hc, at_c, 0, w_gate, w_up, w_down, b)

    g_mix = norm_mix_g[1].reshape(1, d)
    g_ffn = norm_ffn_g[1].reshape(1, d)
    w_in = bf(od_w_in[0])
    w_out = bf(od_w_out[0])
    xl, ggl, ul = _odd_in(xl, moe_l, mod(0, 5), g_mix, mod(1, 0), mod(1, 1), None, w_in, tile_l)
    _, _, uc = _odd_in(xc, moe_c, mod(0, 5), g_mix, mod(1, 0), mod(1, 1), ctx_row, w_in, tile_c)
    cw, cb = od_conv_w[0], od_conv_b[0].reshape(1, -1)
    wax = bf(0.5 * jnp.concatenate([od_wa[0], od_wx[0]], axis=-1))
    ba, bx = (0.5 * a[0].reshape(2, 1, -1) for a in (od_ba, od_bx))
    lam = od_lambda[0].reshape(2, 1, -1)
    zero_state = jnp.zeros((b, 1, LRU_WIDTH), F32)
    hf_c, h0_f = _lru(uc, cw, cb, wax, ba, bx, lam, zero_state, 0, tile_c)
    _, h0_b = _lru(uc, cw, cb, wax, ba, bx, lam, zero_state, 1, tile_c, hf=hf_c, gg=hf_c)
    hf_l, _ = _lru(ul, cw, cb, wax, ba, bx, lam, h0_f, 0, tile_l)
    yl, _ = _lru(ul, cw, cb, wax, ba, bx, lam, h0_b, 1, tile_l, hf=hf_l, gg=ggl)
    rw_hi, rw_lo = router_split(1)
    xl, hl, at_l = _mix_out(yl, None, w_out, xl, mod(1, 2), g_ffn, mod(1, 3), mod(1, 4), None,
                            rw_hi, rw_lo, tile_l)
    moe_l = _expert_choice(hl, at_l, 1, w_gate, w_up, w_down, 1)
    return _final(xl, moe_l, mod(1, 5), final_g.reshape(1, d), tile_l)
```

```python
import functools

import jax
import jax.numpy as jnp
from jax import lax
from jax.experimental import pallas as pl
from jax.experimental.pallas import tpu as pltpu

F32 = jnp.float32
MXU_DTYPE = jnp.bfloat16

D_MODEL = 1024
GRID_W = 64
EPS = 1e-6
NEG_INF = -1e30
HEAD_DIM = 64
N_Q_HEADS = 8
N_KV_HEADS = 2
Q_PER_KV = N_Q_HEADS // N_KV_HEADS
ATTN_WIDTH = N_Q_HEADS * HEAD_DIM
KV_WIDTH = N_KV_HEADS * HEAD_DIM
WINDOW = 128
BLOCK = 128
ROPE_BASE = 10000.0
CONV_B_WIDTH = D_MODEL // 2
EVEN_IN = ATTN_WIDTH + 2 * KV_WIDTH + 3 * CONV_B_WIDTH
LRU_WIDTH = D_MODEL
LRU_HEADS = 8
LRU_BLOCK = LRU_WIDTH // LRU_HEADS
LRU_C = 8.0
N_EXPERTS = 16
CAPACITY_FACTOR = 2
EXPERT_FF = 1408
MOD_ROWS = 16
LANES = 128
SUBLANES = 8
HALO = SUBLANES
CUMSUM_TILE = 256
MIX_OUT_SPLIT = 2
ATTN_QBLOCKS = 2
V7X_VMEM_LIMIT = 56 * 1024 * 1024


def _params(*sem):
    return pltpu.CompilerParams(dimension_semantics=sem, vmem_limit_bytes=V7X_VMEM_LIMIT)


def _split_hi_lo(a):
    hi = a.astype(MXU_DTYPE)
    lo = (a - hi.astype(F32)).astype(MXU_DTYPE)
    return hi, lo


def _sigmoid(z):
    return 0.5 * (1.0 + jnp.tanh(0.5 * z))


def _store_token_tiles(ref, val, row0=0):
    tile = val.shape[0]
    for j in range(SUBLANES):
        ref[0, pl.ds(row0 * SUBLANES + j, tile, stride=SUBLANES), :] = val[:, j * LANES:(j + 1) * LANES]


def _load_token_tiles(ref):
    tile = ref.shape[1] // SUBLANES
    return jnp.concatenate([ref[0, pl.ds(j, tile, stride=SUBLANES), :] for j in range(SUBLANES)], axis=1)


def _slab_spec(rows, index_map, width=LRU_WIDTH):
    return pl.BlockSpec((1, width // LANES, rows, LANES), index_map)


def _store_slabs(ref, val):
    for c in range(val.shape[1] // LANES):
        ref[0, c] = val[:, c * LANES:(c + 1) * LANES]


def _load_slabs(ref):
    return jnp.concatenate([ref[0, c] for c in range(ref.shape[1])], axis=1)


def _norm_mod(x, g, shift, scale):
    y = x * lax.rsqrt(jnp.mean(x * x, axis=-1, keepdims=True) + EPS) * g
    return y * (1.0 + scale) + shift


def _ada_kernel(c_ref, w_ref, b_ref, o_ref):
    c = c_ref[...]
    s_hi, s_lo = _split_hi_lo(c * _sigmoid(c))
    w_hi, w_lo = _split_hi_lo(w_ref[0])
    acc = jnp.dot(s_hi, w_hi, preferred_element_type=F32)
    acc += jnp.dot(s_hi, w_lo, preferred_element_type=F32)
    acc += jnp.dot(s_lo, w_hi, preferred_element_type=F32)
    o_ref[0] = acc + b_ref[0]


def _ada(cvec, ada_w, ada_b):
    depth, d, n6 = ada_w.shape
    tn = 1536
    return pl.pallas_call(
        _ada_kernel,
        grid=(depth, n6 // tn),
        in_specs=[
            pl.BlockSpec((MOD_ROWS, d), lambda l, j: (0, 0)),
            pl.BlockSpec((1, d, tn), lambda l, j: (l, 0, j)),
            pl.BlockSpec((1, 1, tn), lambda l, j: (l, 0, j)),
        ],
        out_specs=pl.BlockSpec((1, MOD_ROWS, tn), lambda l, j: (l, 0, j)),
        out_shape=jax.ShapeDtypeStruct((depth, MOD_ROWS, n6), F32),
        compiler_params=_params("arbitrary", "arbitrary"),
        name="ada",
    )(cvec, ada_w, ada_b.reshape(depth, 1, n6))


def _even_in_kernel(*refs, rope):
    if rope:
        (x_ref, g_ref, sh_ref, sc_ref, w_ref, cos_ref, sa_ref, sb_ref, q_ref, kv_ref, gb_ref, cu_ref) = refs
    else:
        (x_ref, g_ref, sh_ref, sc_ref, w_ref, q_ref, kv_ref, gb_ref, cu_ref) = refs
    h = _norm_mod(x_ref[0], g_ref[...], sh_ref[0], sc_ref[0]).astype(MXU_DTYPE)
    y = jnp.dot(h, w_ref[...], preferred_element_type=F32)
    q = y[:, :ATTN_WIDTH]
    k = y[:, ATTN_WIDTH:ATTN_WIDTH + KV_WIDTH]
    v = y[:, ATTN_WIDTH + KV_WIDTH:ATTN_WIDTH + 2 * KV_WIDTH]
    c0 = ATTN_WIDTH + 2 * KV_WIDTH
    if rope:
        cos, sa, sb = cos_ref[...], sa_ref[...], sb_ref[...]

        def rot(z):
            return z * cos + pltpu.roll(z, 16, 1) * sa + pltpu.roll(z, 112, 1) * sb

        q = jnp.concatenate([rot(q[:, j * 128:(j + 1) * 128]) for j in range(ATTN_WIDTH // 128)], axis=1)
        k = rot(k)
    q_ref[0] = (q * (HEAD_DIM ** -0.5)).astype(q_ref.dtype)
    kv_ref[0] = jnp.concatenate([k, v], axis=1).astype(kv_ref.dtype)
    gb_ref[0] = y[:, c0:c0 + CONV_B_WIDTH]
    cu_ref[0] = y[:, c0 + CONV_B_WIDTH:c0 + 2 * CONV_B_WIDTH] * y[:, c0 + 2 * CONV_B_WIDTH:]


def _even_in(x, g, shift, scale, mod_row, w, tables, tile):
    b, n, d = x.shape
    nt = n // tile
    rope = tables is not None
    row = (lambda bi: bi) if mod_row is None else (lambda bi: mod_row)
    in_specs = [
        pl.BlockSpec((1, tile, d), lambda bi, t: (bi, t, 0)),
        pl.BlockSpec((1, d), lambda bi, t: (0, 0)),
        pl.BlockSpec((1, 1, d), lambda bi, t: (row(bi), 0, 0)),
        pl.BlockSpec((1, 1, d), lambda bi, t: (row(bi), 0, 0)),
        pl.BlockSpec(w.shape, lambda bi, t: (0, 0)),
    ]
    args = [x, g, shift, scale, w]
    if rope:
        in_specs += [pl.BlockSpec((tile, 128), lambda bi, t: (t, 0))] * 3
        args += list(tables)
    out_shape = (
        jax.ShapeDtypeStruct((b, n, ATTN_WIDTH), MXU_DTYPE),
        jax.ShapeDtypeStruct((b, n, 2 * KV_WIDTH), MXU_DTYPE),
        jax.ShapeDtypeStruct((b, n, CONV_B_WIDTH), F32),
        jax.ShapeDtypeStruct((b, n, CONV_B_WIDTH), F32),
    )
    out_specs = tuple(pl.BlockSpec((1, tile, s.shape[-1]), lambda bi, t: (bi, t, 0)) for s in out_shape)
    return pl.pallas_call(
        functools.partial(_even_in_kernel, rope=rope),
        grid=(b, nt),
        in_specs=in_specs,
        out_specs=out_specs,
        out_shape=out_shape,
        compiler_params=_params("arbitrary", "arbitrary"),
        name="even_in_rope" if rope else "even_in",
    )(*args)


def _rope_tables(n):
    nf = HEAD_DIM // 4
    pos = jnp.arange(n)
    rows = (pos // GRID_W).astype(F32)
    cols = (pos % GRID_W).astype(F32)
    lane = jnp.arange(128)
    inv = ROPE_BASE ** (-(lane % nf).astype(F32) / nf)
    use_col = (lane % HEAD_DIM) >= HEAD_DIM // 2
    ang = jnp.where(use_col[None, :], cols[:, None], rows[:, None]) * inv[None, :]
    cos, sin = jnp.cos(ang), jnp.sin(ang)
    second = ((lane % (2 * nf)) >= nf)[None, :]
    return cos, jnp.where(second, sin, 0.0), jnp.where(second, 0.0, -sin)


def _attn_kernel(*refs, n, has_local):
    if has_local:
        sink_ref, q_ref, kv_ref, kvc_ref, bias_ref, o_ref = refs
    else:
        sink_ref, q_ref, kvc_ref, o_ref = refs
    n_loc = 3 * BLOCK
    nb = n // BLOCK
    grp = lax.broadcasted_iota(jnp.int32, (Q_PER_KV * BLOCK, 1), 0) // BLOCK
    for sb in range(q_ref.shape[1] // BLOCK):
        i = pl.program_id(1) * (q_ref.shape[1] // BLOCK) + sb
        q = q_ref[0, sb * BLOCK:(sb + 1) * BLOCK, :]
        kvall = kvc_ref[0]
        if has_local:
            start = pl.multiple_of(_local_start(i, n), BLOCK)
            kvall = jnp.concatenate([kv_ref[0, pl.ds(start, n_loc), :], kvall], axis=0)
            case = jnp.where(i == 0, 0, jnp.where(i == nb - 1, 2, 1))
            bias = jnp.concatenate([bias_ref[case]] * Q_PER_KV, axis=0)
        outs = []
        for hk in range(N_KV_HEADS):
            kh = kvall[:, hk * HEAD_DIM:(hk + 1) * HEAD_DIM]
            vh = kvall[:, KV_WIDTH + hk * HEAD_DIM:KV_WIDTH + (hk + 1) * HEAD_DIM]
            qg = jnp.concatenate(
                [q[:, (hk * Q_PER_KV + g) * HEAD_DIM:(hk * Q_PER_KV + g + 1) * HEAD_DIM] for g in range(Q_PER_KV)],
                axis=0)
            s = lax.dot_general(qg, kh, (((1,), (1,)), ((), ())), preferred_element_type=F32)
            if has_local:
                s = jnp.concatenate([s[:, :n_loc] + bias, s[:, n_loc:]], axis=1)
            snk = jnp.zeros((Q_PER_KV * BLOCK, 1), F32)
            for g in range(Q_PER_KV):
                snk = jnp.where(grp == g, sink_ref[hk * Q_PER_KV + g], snk)
            m = jnp.maximum(jnp.max(s, axis=1, keepdims=True), snk)
            p = jnp.exp(s - m)
            denom = jnp.sum(p, axis=1, keepdims=True) + jnp.exp(snk - m)
            o = jnp.dot(p.astype(MXU_DTYPE), vh, preferred_element_type=F32) / denom
            outs += [o[g * BLOCK:(g + 1) * BLOCK] for g in range(Q_PER_KV)]
        o_ref[0, sb * BLOCK:(sb + 1) * BLOCK, :] = jnp.concatenate(outs, axis=1).astype(o_ref.dtype)


def _local_start(i, n):
    return jnp.clip(i * BLOCK - BLOCK, 0, n - 3 * BLOCK)


def _window_bias(n):
    nb = n // BLOCK
    r = jnp.arange(BLOCK)[:, None]
    c = jnp.arange(3 * BLOCK)[None, :]
    cases = []
    for i in (0, 1, nb - 1):
        diff = (_local_start(i, n) + c) - (i * BLOCK + r)
        cases.append(jnp.where(jnp.abs(diff) <= WINDOW, 0.0, NEG_INF).astype(F32))
    return jnp.stack(cases)


def _attention(sink, q, kv, kvc):
    b, n, _ = q.shape
    lc = kvc.shape[1]
    has_local = kv is not None
    qrows = ATTN_QBLOCKS * BLOCK
    assert n % qrows == 0
    in_specs = [pl.BlockSpec(memory_space=pltpu.SMEM), pl.BlockSpec((1, qrows, ATTN_WIDTH), lambda bi, i: (bi, i, 0))]
    args = [sink, q]
    if has_local:
        in_specs.append(pl.BlockSpec((1, n, 2 * KV_WIDTH), lambda bi, i: (bi, 0, 0)))
        args.append(kv)
    in_specs.append(pl.BlockSpec((1, lc, 2 * KV_WIDTH), lambda bi, i: (bi, 0, 0)))
    args.append(kvc)
    if has_local:
        nb = n // BLOCK
        assert nb >= 4
        in_specs.append(pl.BlockSpec((3, BLOCK, 3 * BLOCK), lambda bi, i: (0, 0, 0)))
        args.append(_window_bias(n))
    return pl.pallas_call(
        functools.partial(_attn_kernel, n=n, has_local=has_local),
        grid=(b, n // qrows),
        in_specs=in_specs,
        out_specs=pl.BlockSpec((1, qrows, ATTN_WIDTH), lambda bi, i: (bi, i, 0)),
        out_shape=jax.ShapeDtypeStruct((b, n, ATTN_WIDTH), MXU_DTYPE),
        compiler_params=_params("arbitrary", "arbitrary"),
        name="attn_local" if has_local else "attn_ctx",
    )(*args)


def _mix_out_kernel(*refs, conv, first, last):
    if conv:
        (att_ref, gb_ref, cu_ref, cup_ref, cun_ref, cw_ref, cb_ref,
         w_ref, x_ref, gate_ref, g2_ref, sh2_ref, sc2_ref, rwh_ref, rwl_ref, xo_ref, h_ref, afft_ref) = refs
        t = pl.program_id(1)
        prev = jnp.where(t == first, 0.0, cup_ref[0])
        nxt = jnp.where(t == last, 0.0, cun_ref[0])
        ext = jnp.concatenate([prev, cu_ref[0], nxt], axis=0)
        cw = cw_ref[...]
    else:
        (y_ref, w_ref, x_ref, gate_ref, g2_ref, sh2_ref, sc2_ref, rwh_ref, rwl_ref, xo_ref, h_ref, afft_ref) = refs
    tile = x_ref.shape[1]
    rows = tile // MIX_OUT_SPLIT
    for s in range(MIX_OUT_SPLIT):
        r0 = s * rows
        rs = slice(r0, r0 + rows)
        if conv:
            cv = (cw[0:1] * ext[HALO - 1 + r0:HALO - 1 + r0 + rows] + cw[1:2] * ext[HALO + r0:HALO + r0 + rows]
                  + cw[2:3] * ext[HALO + 1 + r0:HALO + 1 + r0 + rows] + cb_ref[...])
            cat = jnp.concatenate([att_ref[0, rs, :], (gb_ref[0, rs, :] * cv).astype(MXU_DTYPE)], axis=1)
        else:
            cat = jnp.concatenate([y_ref[0, c, rs, :] for c in range(y_ref.shape[1])], axis=1).astype(MXU_DTYPE)
        y = jnp.dot(cat, w_ref[...], preferred_element_type=F32)
        x = x_ref[0, rs, :] + gate_ref[0] * y
        xo_ref[0, rs, :] = x
        h = _norm_mod(x, g2_ref[...], sh2_ref[0], sc2_ref[0])
        _store_token_tiles(h_ref, h, r0)
        h_hi, h_lo = _split_hi_lo(h)
        logits = jnp.dot(h_hi, rwh_ref[...], preferred_element_type=F32)
        logits += jnp.dot(h_hi, rwl_ref[...], preferred_element_type=F32)
        logits += jnp.dot(h_lo, rwh_ref[...], preferred_element_type=F32)
        lane = lax.broadcasted_iota(jnp.int32, logits.shape, 1)
        logits = jnp.where(lane < N_EXPERTS, logits, NEG_INF)
        e = jnp.exp(logits - jnp.max(logits, axis=1, keepdims=True))
        aff = e / jnp.sum(e, axis=1, keepdims=True)
        afft_ref[0, :, rs] = aff.T[:N_EXPERTS]


def _mix_out(mix_in, conv_params, w, x, gate, g2, sh2, sc2, mod_row, rw_hi, rw_lo, tile):
    b, n, d = x.shape
    nt = n // tile
    conv = conv_params is not None
    row = (lambda bi: bi) if mod_row is None else (lambda bi: mod_row)
    tok = lambda width: pl.BlockSpec((1, tile, width), lambda bi, t: (bi, t, 0))
    modspec = pl.BlockSpec((1, 1, d), lambda bi, t: (row(bi), 0, 0))
    full = lambda a: pl.BlockSpec(a.shape, lambda bi, t: (0,) * a.ndim)
    if conv:
        att, gb, cu = mix_in
        cw, cb = conv_params
        per = tile // HALO
        in_specs = [tok(ATTN_WIDTH), tok(CONV_B_WIDTH), tok(CONV_B_WIDTH),
                    pl.BlockSpec((1, HALO, CONV_B_WIDTH), lambda bi, t: (bi, jnp.maximum(t * per - 1, 0), 0)),
                    pl.BlockSpec((1, HALO, CONV_B_WIDTH), lambda bi, t: (bi, jnp.minimum((t + 1) * per, n // HALO - 1), 0)),
                    full(cw), full(cb)]
        args = [att, gb, cu, cu, cu, cw, cb]
    else:
        in_specs = [_slab_spec(tile, lambda bi, t: (bi, 0, t, 0), d)]
        args = [mix_in]
    in_specs += [full(w), tok(d), modspec, full(g2), modspec, modspec, full(rw_hi), full(rw_lo)]
    args += [w, x, gate, g2, sh2, sc2, rw_hi, rw_lo]
    out_shape = (jax.ShapeDtypeStruct((b, n, d), F32), jax.ShapeDtypeStruct((b, n * SUBLANES, LANES), F32),
                 jax.ShapeDtypeStruct((b, N_EXPERTS, n), F32))
    out_specs = (tok(d), pl.BlockSpec((1, tile * SUBLANES, LANES), lambda bi, t: (bi, t, 0)),
                 pl.BlockSpec((1, N_EXPERTS, tile), lambda bi, t: (bi, 0, t)))
    return pl.pallas_call(
        functools.partial(_mix_out_kernel, conv=conv, first=0, last=nt - 1),
        grid=(b, nt),
        in_specs=in_specs,
        out_specs=out_specs,
        out_shape=out_shape,
        compiler_params=_params("arbitrary", "arbitrary"),
        name="even_out" if conv else "odd_out",
    )(*args)


def _cumsum_lanes(x):
    n = x.shape[1]
    r = lax.broadcasted_iota(jnp.int32, (CUMSUM_TILE, CUMSUM_TILE), 0)
    c = lax.broadcasted_iota(jnp.int32, (CUMSUM_TILE, CUMSUM_TILE), 1)
    tri = jnp.where(r <= c, 1.0, 0.0).astype(MXU_DTYPE)
    carry = jnp.zeros((x.shape[0], 1), F32)
    outs = []
    for k in range(n // CUMSUM_TILE):
        blk = x[:, k * CUMSUM_TILE:(k + 1) * CUMSUM_TILE].astype(MXU_DTYPE)
        loc = jnp.dot(blk, tri, preferred_element_type=F32) + carry
        outs.append(loc)
        carry = loc[:, CUMSUM_TILE - 1:CUMSUM_TILE]
    return jnp.concatenate(outs, axis=1)


SLOT_EMPTY = 1 << 20


def _select_kernel(at_ref, idx_ref, g_ref, *, cap):
    at = at_ref[0]
    n_exp, n = at.shape
    capf = float(cap)

    def count_ge(thr):
        return jnp.sum(jnp.where(at >= thr, 1.0, 0.0), axis=1, keepdims=True)

    def bit_body(_, c):
        lo_i, hi_i = c
        mid = lo_i + ((hi_i - lo_i) >> 1)
        ge = count_ge(lax.bitcast_convert_type(mid, F32)) >= capf
        return jnp.where(ge, mid, lo_i), jnp.where(ge, hi_i, mid)

    lo_i, hi_i = lax.fori_loop(
        0, 31, bit_body, (jnp.zeros((n_exp, 1), jnp.int32), jnp.full((n_exp, 1), 0x3F800001, jnp.int32)))

    def val_body(_, c):
        lo, hi = c
        mid = 0.5 * (lo + hi)
        ge = count_ge(mid) >= capf
        return jnp.where(ge, mid, lo), jnp.where(ge, hi, mid)

    lo, hi = lax.fori_loop(
        0, 24, val_body, (lax.bitcast_convert_type(lo_i, F32), lax.bitcast_convert_type(hi_i, F32)))
    need = capf - count_ge(hi)

    above = jnp.where(at >= hi, 1.0, 0.0)
    band = jnp.where(at >= lo, 1.0, 0.0) - above
    sel = above + band * jnp.where(_cumsum_lanes(band) <= need, 1.0, 0.0)
    rank = _cumsum_lanes(sel)

    lane = lax.broadcasted_iota(jnp.int32, (n_exp, n), 1)
    disp = jnp.where(sel > 0.5, lane + 1 - rank.astype(jnp.int32), SLOT_EMPTY)
    g = at
    for k in range(n.bit_length() - 1):
        step = 1 << k
        moving = ((disp >> k) & 1) == 1
        disp_in = pltpu.roll(disp, n - step, 1)
        arriving = ((disp_in >> k) & 1) == 1
        g = jnp.where(arriving, pltpu.roll(g, n - step, 1), g)
        disp = jnp.where(arriving, disp_in, jnp.where(moving, SLOT_EMPTY, disp))

    capp = pl.cdiv(cap, LANES) * LANES
    slot = lax.broadcasted_iota(jnp.int32, (n_exp, cap), 1)
    idx_ref[0] = jnp.clip(slot + disp[:, :cap], 0, n - 1)
    g_pad = jnp.concatenate([g[:, :capp], jnp.zeros((LANES - n_exp, capp), F32)], axis=0)
    g_ref[0] = g_pad.T[:cap]


def _select(aff_t, cap):
    b, n_exp, n = aff_t.shape
    assert n & (n - 1) == 0 and n % CUMSUM_TILE == 0 and n < SLOT_EMPTY
    return pl.pallas_call(
        functools.partial(_select_kernel, cap=cap),
        grid=(b,),
        in_specs=[pl.BlockSpec((1, n_exp, n), lambda bi: (bi, 0, 0))],
        out_specs=(pl.BlockSpec((1, n_exp, cap), lambda bi: (bi, 0, 0)), pl.BlockSpec((1, cap, LANES), lambda bi: (bi, 0, 0))),
        out_shape=(jax.ShapeDtypeStruct((b, n_exp, cap), jnp.int32), jax.ShapeDtypeStruct((b, cap, LANES), F32)),
        compiler_params=_params("arbitrary"),
        name="moe_select",
    )(aff_t)


def _gather_kernel(idx_ref, src_ref, xe_ref, xcm_ref, *, cap, n_exp):
    bblk = src_ref.shape[0]
    out_rows = xe_ref.shape[2]
    stride = out_rows + SUBLANES
    for s in range(bblk):
        base = ((pl.program_id(0) * bblk + s) * n_exp + pl.program_id(1)) * cap
        for r in range(cap):
            t = idx_ref[base + r]
            xcm_ref[pl.ds(s * cap + r, SUBLANES, stride=stride), :] = (
                src_ref[s, pl.ds(pl.multiple_of(t * SUBLANES, SUBLANES), SUBLANES), :])
    used = bblk * cap
    if used < out_rows:
        for j in range(SUBLANES):
            xcm_ref[pl.ds(j * stride + used, out_rows - used), :] = jnp.zeros((out_rows - used, LANES), F32)
    xe_ref[0, 0] = jnp.concatenate(
        [xcm_ref[pl.ds(j * stride, out_rows), :] for j in range(SUBLANES)], axis=1).astype(xe_ref.dtype)


def _gather(idx_flat, src, cap, bblk, out_rows):
    b, rows, _ = src.shape
    d = SUBLANES * LANES
    assert cap % SUBLANES == 0 and bblk * cap <= out_rows and b % bblk == 0
    return pl.pallas_call(
        functools.partial(_gather_kernel, cap=cap, n_exp=N_EXPERTS),
        grid=(b // bblk, N_EXPERTS),
        in_specs=[pl.BlockSpec(memory_space=pltpu.SMEM),
                  pl.BlockSpec((bblk, rows, LANES), lambda bi, ei: (bi, 0, 0))],
        out_specs=pl.BlockSpec((1, 1, out_rows, d), lambda bi, ei: (bi, ei, 0, 0)),
        out_shape=jax.ShapeDtypeStruct((b // bblk, N_EXPERTS, out_rows, d), MXU_DTYPE),
        scratch_shapes=[pltpu.VMEM((SUBLANES * (out_rows + SUBLANES), LANES), F32)],
        compiler_params=_params("arbitrary", "arbitrary"),
        name="moe_gather",
    )(idx_flat, src)


WEIGHT_CHUNKS = 2


def _moe_ffn_kernel(*refs, layer, n_steps, has_ctx):
    if has_ctx:
        (x_ref, g_ref, xc_ref, gc_ref, wg_hbm, wu_hbm, wd_hbm, o_ref, oc_ref,
         wg_s, wu_s, wd_s, stage_in, stage_out, sem) = refs
    else:
        (x_ref, g_ref, wg_hbm, wu_hbm, wd_hbm, o_ref, wg_s, wu_s, wd_s, stage_in, stage_out, sem) = refs
    rows = x_ref.shape[2]
    n_exp = pl.num_programs(0)
    e = pl.program_id(0)
    s = pl.program_id(1)
    slot = e % 2
    chunks = []
    for hbm, dst, stage in ((wg_hbm, wg_s, stage_in), (wu_hbm, wu_s, stage_in), (wd_hbm, wd_s, stage_out)):
        n_rows = hbm.shape[2] // WEIGHT_CHUNKS
        chunks += [(hbm, dst, stage, c * n_rows, n_rows) for c in range(WEIGHT_CHUNKS)]
    per_step = pl.cdiv(len(chunks), n_steps)

    def copy(c, expert):
        hbm, _, stage, r0, n_rows = chunks[c]
        return pltpu.make_async_copy(hbm.at[layer, expert, pl.ds(r0, n_rows), :], stage, sem.at[0])

    def cast(c, to_slot):
        _, dst, stage, r0, n_rows = chunks[c]
        dst[to_slot, pl.ds(r0, n_rows), :] = stage[...].astype(dst.dtype)

    @pl.when((e == 0) & (s == 0))
    def _():
        for c in range(len(chunks)):
            copy(c, 0).start()
            copy(c, 0).wait()
            cast(c, 0)

    has_next = e + 1 < n_exp
    for c in range(len(chunks)):
        if c % per_step == 0:
            @pl.when(has_next & (s == c // per_step))
            def _():
                copy(c, e + 1).start()

    x, aff = x_ref[0, 0], g_ref[0]
    if has_ctx:
        is_ctx = s == n_steps - 1
        x = jnp.where(is_ctx, xc_ref[0, 0], x)
        aff = jnp.where(is_ctx, gc_ref[0], aff)
    a = jnp.dot(x, wg_s[slot], preferred_element_type=F32)
    u = jnp.dot(x, wu_s[slot], preferred_element_type=F32)
    mid = (a * _sigmoid(a) * u).astype(MXU_DTYPE)
    y = jnp.dot(mid, wd_s[slot], preferred_element_type=F32)
    lane = lax.broadcasted_iota(jnp.int32, aff.shape, 1)
    y = y * jnp.sum(jnp.where(lane == e, aff, 0.0), axis=1, keepdims=True)
    stride = rows + SUBLANES

    def store(ref):
        for j in range(SUBLANES):
            ref[0, 0, pl.ds(j * stride, rows), :] = y[:, j * LANES:(j + 1) * LANES]
            ref[0, 0, pl.ds(j * stride + rows, SUBLANES), :] = jnp.zeros((SUBLANES, LANES), F32)

    if has_ctx:
        pl.when(is_ctx)(lambda: store(oc_ref))
        pl.when(jnp.logical_not(is_ctx))(lambda: store(o_ref))
    else:
        store(o_ref)

    for c in range(len(chunks)):
        @pl.when(has_next & (s == c // per_step))
        def _():
            copy(c, e + 1).wait()
            cast(c, 1 - slot)
            if (c + 1) % per_step != 0 and c + 1 < len(chunks):
                copy(c + 1, e + 1).start()


def _moe_ffn(xe, ge, ctx_block, layer, wg, wu, wd):
    nb, e, rows, d = xe.shape
    ff = wg.shape[-1]
    assert d % WEIGHT_CHUNKS == 0 and ff % (WEIGHT_CHUNKS * SUBLANES) == 0
    has_ctx = ctx_block is not None
    n_steps = nb + int(has_ctx)
    hbm = pl.BlockSpec(memory_space=pl.ANY)
    out_rows = SUBLANES * (rows + SUBLANES)
    blk = lambda bi: jnp.minimum(bi, nb - 1)
    in_specs = [pl.BlockSpec((1, 1, rows, d), lambda ei, bi: (blk(bi), ei, 0, 0)),
                pl.BlockSpec((1, rows, LANES), lambda ei, bi: (blk(bi), 0, 0))]
    args = [xe, ge]
    out_specs = [pl.BlockSpec((1, 1, out_rows, LANES), lambda ei, bi: (blk(bi), ei, 0, 0))]
    out_shape = [jax.ShapeDtypeStruct((nb, e, out_rows, LANES), F32)]
    if has_ctx:
        assert ctx_block[0].shape == (1, e, rows, d) and ctx_block[1].shape == (1, rows, LANES)
        in_specs += [pl.BlockSpec((1, 1, rows, d), lambda ei, bi: (0, ei, 0, 0)),
                     pl.BlockSpec((1, rows, LANES), lambda ei, bi: (0, 0, 0))]
        args += list(ctx_block)
        out_specs.append(pl.BlockSpec((1, 1, out_rows, LANES), lambda ei, bi: (0, ei, 0, 0)))
        out_shape.append(jax.ShapeDtypeStruct((1, e, out_rows, LANES), F32))
    return pl.pallas_call(
        functools.partial(_moe_ffn_kernel, layer=layer, n_steps=n_steps, has_ctx=has_ctx),
        grid=(e, n_steps),
        in_specs=in_specs + [hbm, hbm, hbm],
        out_specs=tuple(out_specs),
        out_shape=tuple(out_shape),
        scratch_shapes=[pltpu.VMEM((2, d, ff), MXU_DTYPE), pltpu.VMEM((2, d, ff), MXU_DTYPE),
                        pltpu.VMEM((2, ff, d), MXU_DTYPE),
                        pltpu.VMEM((d // WEIGHT_CHUNKS, ff), F32), pltpu.VMEM((ff // WEIGHT_CHUNKS, d), F32),
                        pltpu.SemaphoreType.DMA((1,))],
        compiler_params=_params("arbitrary", "arbitrary"),
        name="moe_ffn",
    )(*args, wg, wu, wd)


COMBINE_UNROLL = 8


def _combine_kernel(idx_ref, y_ref, acc_ref, *, cap, n_exp):
    bblk = acc_ref.shape[0]
    ei = pl.program_id(1)
    stride = y_ref.shape[2] // SUBLANES

    @pl.when(ei == 0)
    def _():
        acc_ref[...] = jnp.zeros_like(acc_ref)

    for s in range(bblk):
        base = ((pl.program_id(0) * bblk + s) * n_exp + ei) * cap
        for r0 in range(0, cap, COMBINE_UNROLL):
            toks = [pl.multiple_of(idx_ref[base + r0 + i] * SUBLANES, SUBLANES) for i in range(COMBINE_UNROLL)]
            new = [acc_ref[s, pl.ds(toks[i], SUBLANES), :]
                   + y_ref[0, 0, pl.ds(s * cap + r0 + i, SUBLANES, stride=stride), :] for i in range(COMBINE_UNROLL)]
            for i in range(COMBINE_UNROLL):
                acc_ref[s, pl.ds(toks[i], SUBLANES), :] = new[i]


def _combine(idx_flat, y, b, n, cap, bblk):
    nb, n_exp, rows, _ = y.shape
    assert cap % COMBINE_UNROLL == 0 and nb * bblk == b and bblk * cap <= rows // SUBLANES - SUBLANES
    return pl.pallas_call(
        functools.partial(_combine_kernel, cap=cap, n_exp=n_exp),
        grid=(nb, n_exp),
        in_specs=[pl.BlockSpec(memory_space=pltpu.SMEM),
                  pl.BlockSpec((1, 1, rows, LANES), lambda bi, ei: (bi, ei, 0, 0))],
        out_specs=pl.BlockSpec((bblk, n * SUBLANES, LANES), lambda bi, ei: (bi, 0, 0)),
        out_shape=jax.ShapeDtypeStruct((b, n * SUBLANES, LANES), F32),
        compiler_params=_params("arbitrary", "arbitrary"),
        name="moe_combine",
    )(idx_flat, y)


def _expert_choice(h_lat, aff_lat, h_ctx, aff_ctx, layer, wg, wu, wd):
    b, _, n = aff_lat.shape
    cap = max(1, CAPACITY_FACTOR * n // N_EXPERTS)
    idx, g = _select(aff_lat, cap)
    idx = idx.reshape(-1)
    xe = _gather(idx, h_lat, cap, 1, cap)
    if h_ctx is None:
        (y,) = _moe_ffn(xe, g, None, layer, wg, wu, wd)
        return _combine(idx, y, b, n, cap, 1), None
    lc = aff_ctx.shape[2]
    cap_c = max(1, CAPACITY_FACTOR * lc // N_EXPERTS)
    idx_c, g_c = _select(aff_ctx, cap_c)
    idx_c = idx_c.reshape(-1)
    xe_c = _gather(idx_c, h_ctx, cap_c, b, cap)
    g_c = jnp.pad(g_c.reshape(1, b * cap_c, LANES), ((0, 0), (0, cap - b * cap_c), (0, 0)))
    y, y_c = _moe_ffn(xe, g, (xe_c, g_c), layer, wg, wu, wd)
    return _combine(idx, y, b, n, cap, 1), _combine(idx_c, y_c, b, lc, cap_c, b)


def _odd_in_kernel(x_ref, moe_ref, gate_ref, g_ref, sh_ref, sc_ref, w_ref, xo_ref, gg_ref, u_ref):
    x = x_ref[0] + gate_ref[0] * _load_token_tiles(moe_ref)
    xo_ref[0] = x
    h = _norm_mod(x, g_ref[...], sh_ref[0], sc_ref[0]).astype(MXU_DTYPE)
    y = jnp.dot(h, w_ref[...], preferred_element_type=F32)
    gl = y[:, :LRU_WIDTH]
    _store_slabs(gg_ref, 0.5 * gl * (1.0 + jnp.tanh(0.7978845608028654 * (gl + 0.044715 * gl * gl * gl))))
    _store_slabs(u_ref, y[:, LRU_WIDTH:])


def _odd_in(x, moe, gate, g, shift, scale, mod_row, w, tile):
    b, n, d = x.shape
    row = (lambda bi: bi) if mod_row is None else (lambda bi: mod_row)
    tok = pl.BlockSpec((1, tile, d), lambda bi, t: (bi, t, 0))
    tiles = pl.BlockSpec((1, tile * SUBLANES, LANES), lambda bi, t: (bi, t, 0))
    modspec = pl.BlockSpec((1, 1, d), lambda bi, t: (row(bi), 0, 0))
    return pl.pallas_call(
        _odd_in_kernel,
        grid=(b, n // tile),
        in_specs=[tok, tiles, modspec, pl.BlockSpec((1, d), lambda bi, t: (0, 0)), modspec, modspec,
                  pl.BlockSpec(w.shape, lambda bi, t: (0, 0))],
        out_specs=(tok, _slab_spec(tile, lambda bi, t: (bi, 0, t, 0)), _slab_spec(tile, lambda bi, t: (bi, 0, t, 0))),
        out_shape=(jax.ShapeDtypeStruct((b, n, d), F32),) + (jax.ShapeDtypeStruct((b, d // LANES, n, LANES), F32),) * 2,
        compiler_params=_params("arbitrary", "arbitrary"),
        name="odd_in",
    )(x, moe, gate, g, shift, scale, w)


def _lru_kernel(*refs, reverse, nt):
    if reverse:
        (u_ref, up_ref, un_ref, cw_ref, cb_ref, wax_ref, ba_ref, bx_ref, lam_ref, h0_ref, hf_ref, gg_ref,
         out_ref, hlast_ref, a_scr, b_scr, uc_scr, carry_scr) = refs
    else:
        (u_ref, up_ref, un_ref, cw_ref, cb_ref, wax_ref, ba_ref, bx_ref, lam_ref, h0_ref,
         out_ref, hlast_ref, a_scr, b_scr, uc_scr, carry_scr) = refs
    n_slab, tile = u_ref.shape[1], u_ref.shape[2]
    w = n_slab * LANES
    per = tile // SUBLANES
    phase = lambda ref, j: jnp.concatenate(
        [ref[0, c, pl.ds(j, per, stride=SUBLANES), :] for c in range(n_slab)], axis=1)
    rows = lambda j: slice(j * per, (j + 1) * per)
    t = pl.program_id(1)
    pos = (nt - 1 - t) if reverse else t
    prev = jnp.where(pos == 0, 0.0, _load_slabs(up_ref))
    nxt = jnp.where(pos == nt - 1, 0.0, _load_slabs(un_ref))

    rowid = lax.broadcasted_iota(jnp.int32, (per, w), 0)
    shift_down = lambda x, first: jnp.where(rowid == 0, first, pltpu.roll(x, 1, 0))
    shift_up = lambda x, last: jnp.where(rowid == per - 1, last, pltpu.roll(x, per - 1, 0))
    u = [phase(u_ref, j) for j in range(SUBLANES)]
    um1 = [shift_down(u[7], prev[7:8])] + u[:7]
    um2 = [shift_down(u[6], prev[6:7]), um1[0]] + u[:6]
    up1 = u[1:] + [shift_up(u[0], nxt[0:1])]
    cw = cw_ref[...]
    for j in range(SUBLANES):
        uc_scr[rows(j), :] = (cb_ref[...] + cw[0:1] * um2[j] + cw[1:2] * um1[j] + cw[2:3] * u[j] + cw[3:4] * up1[j])

    lam = lam_ref[0]
    half_decay = (0.5 * LRU_C) * (jnp.maximum(-lam, 0.0) + jnp.log1p(jnp.exp(-jnp.abs(lam))))
    for hd in range(LRU_HEADS):
        sl = slice(hd * LRU_BLOCK, (hd + 1) * LRU_BLOCK)
        uc = uc_scr[:, sl]
        z = jnp.dot(uc.astype(MXU_DTYPE), wax_ref[0, hd], preferred_element_type=F32)
        hd_row = half_decay[:, sl]
        neg_log_a = hd_row * jnp.tanh(z[:, :LRU_BLOCK] + ba_ref[0][:, sl]) + hd_row
        gate2 = 1.0 + jnp.tanh(z[:, LRU_BLOCK:] + bx_ref[0][:, sl])
        a = jnp.exp(-neg_log_a)
        m2 = jnp.tanh(neg_log_a) * (a * a + 1.0)
        mult = jnp.where(m2 > 0.0, m2 * lax.rsqrt(m2), 0.0)
        a_scr[:, sl] = a
        b_scr[:, sl] = (0.5 * mult) * (gate2 * uc)

    @pl.when(t == 0)
    def _():
        carry_scr[...] = jnp.broadcast_to(h0_ref[0], carry_scr.shape)

    order = list(range(SUBLANES))[::-1] if reverse else list(range(SUBLANES))
    hrun = b_scr[rows(order[0]), :]
    prun = a_scr[rows(order[0]), :]
    for j in order[1:]:
        aj = a_scr[rows(j), :]
        hrun = aj * hrun + b_scr[rows(j), :]
        prun = aj * prun
        b_scr[rows(j), :] = hrun
        a_scr[rows(j), :] = prun

    lane_row = lax.broadcasted_iota(jnp.int32, (SUBLANES, w), 0)
    carry = carry_scr[...]
    groups = list(range(per // SUBLANES))
    entering = [None] * len(groups)
    for m in (groups[::-1] if reverse else groups):
        a = prun[m * SUBLANES:(m + 1) * SUBLANES]
        bcoef = hrun[m * SUBLANES:(m + 1) * SUBLANES]
        for dist in (1, 2, 4):
            shift = (SUBLANES - dist) if reverse else dist
            msk = (lane_row < SUBLANES - dist) if reverse else (lane_row >= dist)
            a_s = pltpu.roll(a, shift, 0)
            b_s = pltpu.roll(bcoef, shift, 0)
            bcoef = jnp.where(msk, a * b_s + bcoef, bcoef)
            a = jnp.where(msk, a * a_s, a)
        after = a * carry + bcoef
        if reverse:
            entering[m] = jnp.where(lane_row == SUBLANES - 1, carry, pltpu.roll(after, SUBLANES - 1, 0))
            carry = jnp.broadcast_to(after[0:1], carry.shape)
        else:
            entering[m] = jnp.where(lane_row == 0, carry, pltpu.roll(after, 1, 0))
            carry = jnp.broadcast_to(after[SUBLANES - 1:SUBLANES], carry.shape)
    carry_scr[...] = carry
    hlast_ref[0] = carry[0:1]
    h_in = jnp.concatenate(entering, axis=0)

    for j in range(SUBLANES):
        hcur = b_scr[rows(j), :] + a_scr[rows(j), :] * h_in
        if reverse:
            hcur = phase(gg_ref, j) * (phase(hf_ref, j) + hcur)
        for c in range(n_slab):
            out_ref[0, c, pl.ds(j, per, stride=SUBLANES), :] = hcur[:, c * LANES:(c + 1) * LANES]


def _lru(u, cw, cb, wax, ba, bx, lam, h0, direction, tile, hf=None, gg=None):
    b, n_slab, n, _ = u.shape
    w = n_slab * LANES
    nt = n // tile
    per = tile // HALO
    reverse = direction == 1
    pos = (lambda t: nt - 1 - t) if reverse else (lambda t: t)
    tok = _slab_spec(tile, lambda bi, t: (bi, 0, pos(t), 0), w)
    rowspec = pl.BlockSpec((1, 1, w), lambda bi, t: (direction, 0, 0))
    in_specs = [
        tok,
        _slab_spec(HALO, lambda bi, t: (bi, 0, jnp.maximum(pos(t) * per - 1, 0), 0), w),
        _slab_spec(HALO, lambda bi, t: (bi, 0, jnp.minimum((pos(t) + 1) * per, n // HALO - 1), 0), w),
        pl.BlockSpec(cw.shape, lambda bi, t: (0, 0)),
        pl.BlockSpec(cb.shape, lambda bi, t: (0, 0)),
        pl.BlockSpec((1,) + wax.shape[1:], lambda bi, t: (direction, 0, 0, 0)),
        rowspec, rowspec, rowspec,
        pl.BlockSpec((1, 1, w), lambda bi, t: (bi, 0, 0)),
    ]
    args = [u, u, u, cw, cb, wax, ba, bx, lam, h0]
    if reverse:
        in_specs += [tok, tok]
        args += [hf, gg]
    assert tile % (SUBLANES * SUBLANES) == 0
    return pl.pallas_call(
        functools.partial(_lru_kernel, reverse=reverse, nt=nt),
        grid=(b, nt),
        in_specs=in_specs,
        out_specs=(tok, pl.BlockSpec((1, 1, w), lambda bi, t: (bi, 0, 0))),
        out_shape=(jax.ShapeDtypeStruct(u.shape, F32), jax.ShapeDtypeStruct((b, 1, w), F32)),
        scratch_shapes=[pltpu.VMEM((tile, w), F32), pltpu.VMEM((tile, w), F32), pltpu.VMEM((tile, w), F32),
                        pltpu.VMEM((SUBLANES, w), F32)],
        compiler_params=_params("arbitrary", "arbitrary"),
        name="lru_bwd" if reverse else "lru_fwd",
    )(*args)


def _final_kernel(x_ref, moe_ref, gate_ref, g_ref, o_ref):
    x = x_ref[0] + gate_ref[0] * _load_token_tiles(moe_ref)
    o_ref[0] = x * lax.rsqrt(jnp.mean(x * x, axis=-1, keepdims=True) + EPS) * g_ref[...]


def _final(x, moe, gate, g, tile):
    b, n, d = x.shape
    tok = pl.BlockSpec((1, tile, d), lambda bi, t: (bi, t, 0))
    tiles = pl.BlockSpec((1, tile * SUBLANES, LANES), lambda bi, t: (bi, t, 0))
    return pl.pallas_call(
        _final_kernel,
        grid=(b, n // tile),
        in_specs=[tok, tiles, pl.BlockSpec((1, 1, d), lambda bi, t: (bi, 0, 0)), pl.BlockSpec((1, d), lambda bi, t: (0, 0))],
        out_specs=tok,
        out_shape=jax.ShapeDtypeStruct((b, n, d), F32),
        compiler_params=_params("arbitrary", "arbitrary"),
        name="final_norm",
    )(x, moe, gate, g)


def kernel(x, c, ctx, c_ctx, ada_w, ada_b, norm_mix_g, norm_ffn_g, ev_w_in, ev_w_out, ev_sink, ev_conv_w, ev_conv_b, od_w_in, od_w_out, od_conv_w, od_conv_b, od_wa, od_ba, od_wx, od_bx, od_lambda, router_w, w_gate, w_up, w_down, final_g):
    b, n, d = x.shape
    lc = ctx.shape[1]
    depth = ada_w.shape[0]
    assert depth == 2 and d == D_MODEL and b < MOD_ROWS
    tile_l = min(512, n)
    tile_c = lc
    ctx_row = b

    cvec = jnp.zeros((MOD_ROWS, d), F32).at[:b].set(c).at[b].set(c_ctx)
    mods = _ada(cvec, ada_w, ada_b).reshape(depth, MOD_ROWS, 6, 1, d)
    mod = lambda l, j: mods[l, :, j]

    def router_split(l):
        return _split_hi_lo(jnp.pad(router_w[l], ((0, 0), (0, LANES - N_EXPERTS))))

    bf = lambda a: a.astype(MXU_DTYPE)

    g_mix = norm_mix_g[0].reshape(1, d)
    g_ffn = norm_ffn_g[0].reshape(1, d)
    w_in = bf(ev_w_in[0])
    w_out = bf(ev_w_out[0])
    conv_p = (ev_conv_w[0], ev_conv_b[0].reshape(1, -1))
    tables = _rope_tables(n)
    ql, kvl, gbl, cul = _even_in(x, g_mix, mod(0, 0), mod(0, 1), None, w_in, tables, tile_l)
    qc, kvc, gbc, cuc = _even_in(ctx, g_mix, mod(0, 0), mod(0, 1), ctx_row, w_in, None, tile_c)
    att_l = _attention(ev_sink[0], ql, kvl, kvc)
    att_c = _attention(ev_sink[0], qc, None, kvc)
    rw_hi, rw_lo = router_split(0)
    xl, hl, at_l = _mix_out((att_l, gbl, cul), conv_p, w_out, x, mod(0, 2), g_ffn, mod(0, 3), mod(0, 4), None,
                            rw_hi, rw_lo, tile_l)
    xc, hc, at_c = _mix_out((att_c, gbc, cuc), conv_p, w_out, ctx, mod(0, 2), g_ffn, mod(0, 3), mod(0, 4),
                            ctx_row, rw_hi, rw_lo, tile_c)
    moe_l, moe_c = _expert_choice(hl, at_l, hc, at_c, 0, w_gate, w_up, w_down)

    g_mix = norm_mix_g[1].reshape(1, d)
    g_ffn = norm_ffn_g[1].reshape(1, d)
    w_in = bf(od_w_in[0])
    w_out = bf(od_w_out[0])
    xl, ggl, ul = _odd_in(xl, moe_l, mod(0, 5), g_mix, mod(1, 0), mod(1, 1), None, w_in, tile_l)
    _, _, uc = _odd_in(xc, moe_c, mod(0, 5), g_mix, mod(1, 0), mod(1, 1), ctx_row, w_in, tile_c)
    cw, cb = od_conv_w[0], od_conv_b[0].reshape(1, -1)
    wax = bf(0.5 * jnp.concatenate([od_wa[0], od_wx[0]], axis=-1))
    ba, bx = (0.5 * a[0].reshape(2, 1, -1) for a in (od_ba, od_bx))
    lam = od_lambda[0].reshape(2, 1, -1)
    zero_state = jnp.zeros((b, 1, LRU_WIDTH), F32)
    hf_c, h0_f = _lru(uc, cw, cb, wax, ba, bx, lam, zero_state, 0, tile_c)
    _, h0_b = _lru(uc, cw, cb, wax, ba, bx, lam, zero_state, 1, tile_c, hf=hf_c, gg=hf_c)
    hf_l, _ = _lru(ul, cw, cb, wax, ba, bx, lam, h0_f, 0, tile_l)
    yl, _ = _lru(ul, cw, cb, wax, ba, bx, lam, h0_b, 1, tile_l, hf=hf_l, gg=ggl)
    rw_hi, rw_lo = router_split(1)
    xl, hl, at_l = _mix_out(yl, None, w_out, xl, mod(1, 2), g_ffn, mod(1, 3), mod(1, 4), None,
                            rw_hi, rw_lo, tile_l)
    moe_l, _ = _expert_choice(hl, at_l, None, None, 1, w_gate, w_up, w_down)
    return _final(xl, moe_l, mod(1, 5), final_g.reshape(1, d), tile_l)
```

```python
import functools

import jax
import jax.numpy as jnp
from jax import lax
from jax.experimental import pallas as pl
from jax.experimental.pallas import tpu as pltpu

F32 = jnp.float32
MXU_DTYPE = jnp.bfloat16

D_MODEL = 1024
GRID_W = 64
EPS = 1e-6
NEG_INF = -1e30
HEAD_DIM = 64
N_Q_HEADS = 8
N_KV_HEADS = 2
Q_PER_KV = N_Q_HEADS // N_KV_HEADS
ATTN_WIDTH = N_Q_HEADS * HEAD_DIM
KV_WIDTH = N_KV_HEADS * HEAD_DIM
WINDOW = 128
BLOCK = 128
ROPE_BASE = 10000.0
CONV_B_WIDTH = D_MODEL // 2
EVEN_IN = ATTN_WIDTH + 2 * KV_WIDTH + 3 * CONV_B_WIDTH
LRU_WIDTH = D_MODEL
LRU_HEADS = 8
LRU_BLOCK = LRU_WIDTH // LRU_HEADS
LRU_C = 8.0
N_EXPERTS = 16
CAPACITY_FACTOR = 2
EXPERT_FF = 1408
MOD_ROWS = 16
LANES = 128
SUBLANES = 8
HALO = SUBLANES
CUMSUM_TILE = 256
MIX_OUT_SPLIT = 2
ATTN_QBLOCKS = 2
DISPATCH_EXPERTS = 2
V7X_VMEM_LIMIT = 56 * 1024 * 1024


def _params(*sem):
    return pltpu.CompilerParams(dimension_semantics=sem, vmem_limit_bytes=V7X_VMEM_LIMIT)


def _split_hi_lo(a):
    hi = a.astype(MXU_DTYPE)
    lo = (a - hi.astype(F32)).astype(MXU_DTYPE)
    return hi, lo


def _sigmoid(z):
    return 0.5 * (1.0 + jnp.tanh(0.5 * z))


def _store_token_tiles(ref, val, row0=0):
    tile = val.shape[0]
    for j in range(SUBLANES):
        ref[0, pl.ds(row0 * SUBLANES + j, tile, stride=SUBLANES), :] = val[:, j * LANES:(j + 1) * LANES]


def _load_token_tiles(ref):
    tile = ref.shape[1] // SUBLANES
    return jnp.concatenate([ref[0, pl.ds(j, tile, stride=SUBLANES), :] for j in range(SUBLANES)], axis=1)


def _slab_spec(rows, index_map, width=LRU_WIDTH):
    return pl.BlockSpec((1, width // LANES, rows, LANES), index_map)


def _store_slabs(ref, val):
    for c in range(val.shape[1] // LANES):
        ref[0, c] = val[:, c * LANES:(c + 1) * LANES]


def _load_slabs(ref):
    return jnp.concatenate([ref[0, c] for c in range(ref.shape[1])], axis=1)


def _norm_mod(x, g, shift, scale):
    y = x * lax.rsqrt(jnp.mean(x * x, axis=-1, keepdims=True) + EPS) * g
    return y * (1.0 + scale) + shift


def _ada_kernel(c_ref, w_ref, b_ref, o_ref):
    c = c_ref[...]
    s_hi, s_lo = _split_hi_lo(c * _sigmoid(c))
    w_hi, w_lo = _split_hi_lo(w_ref[0])
    acc = jnp.dot(s_hi, w_hi, preferred_element_type=F32)
    acc += jnp.dot(s_hi, w_lo, preferred_element_type=F32)
    acc += jnp.dot(s_lo, w_hi, preferred_element_type=F32)
    o_ref[0] = acc + b_ref[0]


def _ada(cvec, ada_w, ada_b):
    depth, d, n6 = ada_w.shape
    tn = 1536
    return pl.pallas_call(
        _ada_kernel,
        grid=(depth, n6 // tn),
        in_specs=[
            pl.BlockSpec((MOD_ROWS, d), lambda l, j: (0, 0)),
            pl.BlockSpec((1, d, tn), lambda l, j: (l, 0, j)),
            pl.BlockSpec((1, 1, tn), lambda l, j: (l, 0, j)),
        ],
        out_specs=pl.BlockSpec((1, MOD_ROWS, tn), lambda l, j: (l, 0, j)),
        out_shape=jax.ShapeDtypeStruct((depth, MOD_ROWS, n6), F32),
        compiler_params=_params("arbitrary", "arbitrary"),
        name="ada",
    )(cvec, ada_w, ada_b.reshape(depth, 1, n6))


def _even_in_kernel(*refs, rope):
    if rope:
        (x_ref, g_ref, sh_ref, sc_ref, w_ref, cos_ref, sa_ref, sb_ref, q_ref, kv_ref, gb_ref, cu_ref) = refs
    else:
        (x_ref, g_ref, sh_ref, sc_ref, w_ref, q_ref, kv_ref, gb_ref, cu_ref) = refs
    h = _norm_mod(x_ref[0], g_ref[...], sh_ref[0], sc_ref[0]).astype(MXU_DTYPE)
    y = jnp.dot(h, w_ref[...], preferred_element_type=F32)
    q = y[:, :ATTN_WIDTH]
    k = y[:, ATTN_WIDTH:ATTN_WIDTH + KV_WIDTH]
    v = y[:, ATTN_WIDTH + KV_WIDTH:ATTN_WIDTH + 2 * KV_WIDTH]
    c0 = ATTN_WIDTH + 2 * KV_WIDTH
    if rope:
        cos, sa, sb = cos_ref[...], sa_ref[...], sb_ref[...]

        def rot(z):
            return z * cos + pltpu.roll(z, 16, 1) * sa + pltpu.roll(z, 112, 1) * sb

        q = jnp.concatenate([rot(q[:, j * 128:(j + 1) * 128]) for j in range(ATTN_WIDTH // 128)], axis=1)
        k = rot(k)
    q_ref[0] = (q * (HEAD_DIM ** -0.5)).astype(q_ref.dtype)
    kv_ref[0] = jnp.concatenate([k, v], axis=1).astype(kv_ref.dtype)
    gb_ref[0] = y[:, c0:c0 + CONV_B_WIDTH]
    cu_ref[0] = y[:, c0 + CONV_B_WIDTH:c0 + 2 * CONV_B_WIDTH] * y[:, c0 + 2 * CONV_B_WIDTH:]


def _even_in(x, g, shift, scale, mod_row, w, tables, tile):
    b, n, d = x.shape
    nt = n // tile
    rope = tables is not None
    row = (lambda bi: bi) if mod_row is None else (lambda bi: mod_row)
    in_specs = [
        pl.BlockSpec((1, tile, d), lambda bi, t: (bi, t, 0)),
        pl.BlockSpec((1, d), lambda bi, t: (0, 0)),
        pl.BlockSpec((1, 1, d), lambda bi, t: (row(bi), 0, 0)),
        pl.BlockSpec((1, 1, d), lambda bi, t: (row(bi), 0, 0)),
        pl.BlockSpec(w.shape, lambda bi, t: (0, 0)),
    ]
    args = [x, g, shift, scale, w]
    if rope:
        in_specs += [pl.BlockSpec((tile, 128), lambda bi, t: (t, 0))] * 3
        args += list(tables)
    out_shape = (
        jax.ShapeDtypeStruct((b, n, ATTN_WIDTH), MXU_DTYPE),
        jax.ShapeDtypeStruct((b, n, 2 * KV_WIDTH), MXU_DTYPE),
        jax.ShapeDtypeStruct((b, n, CONV_B_WIDTH), F32),
        jax.ShapeDtypeStruct((b, n, CONV_B_WIDTH), F32),
    )
    out_specs = tuple(pl.BlockSpec((1, tile, s.shape[-1]), lambda bi, t: (bi, t, 0)) for s in out_shape)
    return pl.pallas_call(
        functools.partial(_even_in_kernel, rope=rope),
        grid=(b, nt),
        in_specs=in_specs,
        out_specs=out_specs,
        out_shape=out_shape,
        compiler_params=_params("arbitrary", "arbitrary"),
        name="even_in_rope" if rope else "even_in",
    )(*args)


def _rope_tables(n):
    nf = HEAD_DIM // 4
    pos = jnp.arange(n)
    rows = (pos // GRID_W).astype(F32)
    cols = (pos % GRID_W).astype(F32)
    lane = jnp.arange(128)
    inv = ROPE_BASE ** (-(lane % nf).astype(F32) / nf)
    use_col = (lane % HEAD_DIM) >= HEAD_DIM // 2
    ang = jnp.where(use_col[None, :], cols[:, None], rows[:, None]) * inv[None, :]
    cos, sin = jnp.cos(ang), jnp.sin(ang)
    second = ((lane % (2 * nf)) >= nf)[None, :]
    return cos, jnp.where(second, sin, 0.0), jnp.where(second, 0.0, -sin)


def _attn_kernel(*refs, n, has_local):
    if has_local:
        sink_ref, q_ref, kv_ref, kvc_ref, bias_ref, o_ref = refs
    else:
        sink_ref, q_ref, kvc_ref, o_ref = refs
    n_loc = 3 * BLOCK
    nb = n // BLOCK
    grp = lax.broadcasted_iota(jnp.int32, (Q_PER_KV * BLOCK, 1), 0) // BLOCK
    for sb in range(q_ref.shape[1] // BLOCK):
        i = pl.program_id(1) * (q_ref.shape[1] // BLOCK) + sb
        q = q_ref[0, sb * BLOCK:(sb + 1) * BLOCK, :]
        kvall = kvc_ref[0]
        if has_local:
            start = pl.multiple_of(_local_start(i, n), BLOCK)
            kvall = jnp.concatenate([kv_ref[0, pl.ds(start, n_loc), :], kvall], axis=0)
            case = jnp.where(i == 0, 0, jnp.where(i == nb - 1, 2, 1))
            bias = jnp.concatenate([bias_ref[case]] * Q_PER_KV, axis=0)
        outs = []
        for hk in range(N_KV_HEADS):
            kh = kvall[:, hk * HEAD_DIM:(hk + 1) * HEAD_DIM]
            vh = kvall[:, KV_WIDTH + hk * HEAD_DIM:KV_WIDTH + (hk + 1) * HEAD_DIM]
            qg = jnp.concatenate(
                [q[:, (hk * Q_PER_KV + g) * HEAD_DIM:(hk * Q_PER_KV + g + 1) * HEAD_DIM] for g in range(Q_PER_KV)],
                axis=0)
            s = lax.dot_general(qg, kh, (((1,), (1,)), ((), ())), preferred_element_type=F32)
            if has_local:
                s = jnp.concatenate([s[:, :n_loc] + bias, s[:, n_loc:]], axis=1)
            snk = jnp.zeros((Q_PER_KV * BLOCK, 1), F32)
            for g in range(Q_PER_KV):
                snk = jnp.where(grp == g, sink_ref[hk * Q_PER_KV + g], snk)
            m = jnp.maximum(jnp.max(s, axis=1, keepdims=True), snk)
            p = jnp.exp(s - m)
            denom = jnp.sum(p, axis=1, keepdims=True) + jnp.exp(snk - m)
            o = jnp.dot(p.astype(MXU_DTYPE), vh, preferred_element_type=F32) / denom
            outs += [o[g * BLOCK:(g + 1) * BLOCK] for g in range(Q_PER_KV)]
        o_ref[0, sb * BLOCK:(sb + 1) * BLOCK, :] = jnp.concatenate(outs, axis=1).astype(o_ref.dtype)


def _local_start(i, n):
    return jnp.clip(i * BLOCK - BLOCK, 0, n - 3 * BLOCK)


def _window_bias(n):
    nb = n // BLOCK
    r = jnp.arange(BLOCK)[:, None]
    c = jnp.arange(3 * BLOCK)[None, :]
    cases = []
    for i in (0, 1, nb - 1):
        diff = (_local_start(i, n) + c) - (i * BLOCK + r)
        cases.append(jnp.where(jnp.abs(diff) <= WINDOW, 0.0, NEG_INF).astype(F32))
    return jnp.stack(cases)


def _attention(sink, q, kv, kvc):
    b, n, _ = q.shape
    lc = kvc.shape[1]
    has_local = kv is not None
    qrows = ATTN_QBLOCKS * BLOCK
    assert n % qrows == 0
    in_specs = [pl.BlockSpec(memory_space=pltpu.SMEM), pl.BlockSpec((1, qrows, ATTN_WIDTH), lambda bi, i: (bi, i, 0))]
    args = [sink, q]
    if has_local:
        in_specs.append(pl.BlockSpec((1, n, 2 * KV_WIDTH), lambda bi, i: (bi, 0, 0)))
        args.append(kv)
    in_specs.append(pl.BlockSpec((1, lc, 2 * KV_WIDTH), lambda bi, i: (bi, 0, 0)))
    args.append(kvc)
    if has_local:
        nb = n // BLOCK
        assert nb >= 4
        in_specs.append(pl.BlockSpec((3, BLOCK, 3 * BLOCK), lambda bi, i: (0, 0, 0)))
        args.append(_window_bias(n))
    return pl.pallas_call(
        functools.partial(_attn_kernel, n=n, has_local=has_local),
        grid=(b, n // qrows),
        in_specs=in_specs,
        out_specs=pl.BlockSpec((1, qrows, ATTN_WIDTH), lambda bi, i: (bi, i, 0)),
        out_shape=jax.ShapeDtypeStruct((b, n, ATTN_WIDTH), MXU_DTYPE),
        compiler_params=_params("arbitrary", "arbitrary"),
        name="attn_local" if has_local else "attn_ctx",
    )(*args)


def _mix_out_kernel(*refs, conv, first, last):
    if conv:
        (att_ref, gb_ref, cu_ref, cup_ref, cun_ref, cw_ref, cb_ref,
         w_ref, x_ref, gate_ref, g2_ref, sh2_ref, sc2_ref, rwh_ref, rwl_ref, xo_ref, h_ref, afft_ref) = refs
        t = pl.program_id(1)
        prev = jnp.where(t == first, 0.0, cup_ref[0])
        nxt = jnp.where(t == last, 0.0, cun_ref[0])
        ext = jnp.concatenate([prev, cu_ref[0], nxt], axis=0)
        cw = cw_ref[...]
    else:
        (y_ref, w_ref, x_ref, gate_ref, g2_ref, sh2_ref, sc2_ref, rwh_ref, rwl_ref, xo_ref, h_ref, afft_ref) = refs
    tile = x_ref.shape[1]
    rows = tile // MIX_OUT_SPLIT
    for s in range(MIX_OUT_SPLIT):
        r0 = s * rows
        rs = slice(r0, r0 + rows)
        if conv:
            cv = (cw[0:1] * ext[HALO - 1 + r0:HALO - 1 + r0 + rows] + cw[1:2] * ext[HALO + r0:HALO + r0 + rows]
                  + cw[2:3] * ext[HALO + 1 + r0:HALO + 1 + r0 + rows] + cb_ref[...])
            cat = jnp.concatenate([att_ref[0, rs, :], (gb_ref[0, rs, :] * cv).astype(MXU_DTYPE)], axis=1)
        else:
            cat = jnp.concatenate([y_ref[0, c, rs, :] for c in range(y_ref.shape[1])], axis=1).astype(MXU_DTYPE)
        y = jnp.dot(cat, w_ref[...], preferred_element_type=F32)
        x = x_ref[0, rs, :] + gate_ref[0] * y
        xo_ref[0, rs, :] = x
        h = _norm_mod(x, g2_ref[...], sh2_ref[0], sc2_ref[0])
        _store_token_tiles(h_ref, h, r0)
        h_hi, h_lo = _split_hi_lo(h)
        logits = jnp.dot(h_hi, rwh_ref[...], preferred_element_type=F32)
        logits += jnp.dot(h_hi, rwl_ref[...], preferred_element_type=F32)
        logits += jnp.dot(h_lo, rwh_ref[...], preferred_element_type=F32)
        lane = lax.broadcasted_iota(jnp.int32, logits.shape, 1)
        logits = jnp.where(lane < N_EXPERTS, logits, NEG_INF)
        e = jnp.exp(logits - jnp.max(logits, axis=1, keepdims=True))
        aff = e / jnp.sum(e, axis=1, keepdims=True)
        afft_ref[0, :, rs] = aff.T[:N_EXPERTS]


def _mix_out(mix_in, conv_params, w, x, gate, g2, sh2, sc2, mod_row, rw_hi, rw_lo, tile):
    b, n, d = x.shape
    nt = n // tile
    conv = conv_params is not None
    row = (lambda bi: bi) if mod_row is None else (lambda bi: mod_row)
    tok = lambda width: pl.BlockSpec((1, tile, width), lambda bi, t: (bi, t, 0))
    modspec = pl.BlockSpec((1, 1, d), lambda bi, t: (row(bi), 0, 0))
    full = lambda a: pl.BlockSpec(a.shape, lambda bi, t: (0,) * a.ndim)
    if conv:
        att, gb, cu = mix_in
        cw, cb = conv_params
        per = tile // HALO
        in_specs = [tok(ATTN_WIDTH), tok(CONV_B_WIDTH), tok(CONV_B_WIDTH),
                    pl.BlockSpec((1, HALO, CONV_B_WIDTH), lambda bi, t: (bi, jnp.maximum(t * per - 1, 0), 0)),
                    pl.BlockSpec((1, HALO, CONV_B_WIDTH), lambda bi, t: (bi, jnp.minimum((t + 1) * per, n // HALO - 1), 0)),
                    full(cw), full(cb)]
        args = [att, gb, cu, cu, cu, cw, cb]
    else:
        in_specs = [_slab_spec(tile, lambda bi, t: (bi, 0, t, 0), d)]
        args = [mix_in]
    in_specs += [full(w), tok(d), modspec, full(g2), modspec, modspec, full(rw_hi), full(rw_lo)]
    args += [w, x, gate, g2, sh2, sc2, rw_hi, rw_lo]
    out_shape = (jax.ShapeDtypeStruct((b, n, d), F32), jax.ShapeDtypeStruct((b, n * SUBLANES, LANES), F32),
                 jax.ShapeDtypeStruct((b, N_EXPERTS, n), F32))
    out_specs = (tok(d), pl.BlockSpec((1, tile * SUBLANES, LANES), lambda bi, t: (bi, t, 0)),
                 pl.BlockSpec((1, N_EXPERTS, tile), lambda bi, t: (bi, 0, t)))
    return pl.pallas_call(
        functools.partial(_mix_out_kernel, conv=conv, first=0, last=nt - 1),
        grid=(b, nt),
        in_specs=in_specs,
        out_specs=out_specs,
        out_shape=out_shape,
        compiler_params=_params("arbitrary", "arbitrary"),
        name="even_out" if conv else "odd_out",
    )(*args)


def _cumsum_lanes(x):
    n = x.shape[1]
    r = lax.broadcasted_iota(jnp.int32, (CUMSUM_TILE, CUMSUM_TILE), 0)
    c = lax.broadcasted_iota(jnp.int32, (CUMSUM_TILE, CUMSUM_TILE), 1)
    tri = jnp.where(r <= c, 1.0, 0.0).astype(MXU_DTYPE)
    carry = jnp.zeros((x.shape[0], 1), F32)
    outs = []
    for k in range(n // CUMSUM_TILE):
        blk = x[:, k * CUMSUM_TILE:(k + 1) * CUMSUM_TILE].astype(MXU_DTYPE)
        loc = jnp.dot(blk, tri, preferred_element_type=F32) + carry
        outs.append(loc)
        carry = loc[:, CUMSUM_TILE - 1:CUMSUM_TILE]
    return jnp.concatenate(outs, axis=1)


SLOT_EMPTY = 1 << 20


def _select_kernel(at_ref, idx_ref, g_ref, *, cap):
    bblk, n_e, n = at_ref.shape
    at = at_ref[...].reshape(bblk * n_e, n)
    n_exp = bblk * n_e
    capf = float(cap)

    def count_ge(thr):
        return jnp.sum(jnp.where(at >= thr, 1.0, 0.0), axis=1, keepdims=True)

    def bit_body(_, c):
        lo_i, hi_i = c
        mid = lo_i + ((hi_i - lo_i) >> 1)
        ge = count_ge(lax.bitcast_convert_type(mid, F32)) >= capf
        return jnp.where(ge, mid, lo_i), jnp.where(ge, hi_i, mid)

    lo_i, hi_i = lax.fori_loop(
        0, 31, bit_body, (jnp.zeros((n_exp, 1), jnp.int32), jnp.full((n_exp, 1), 0x3F800001, jnp.int32)))

    def val_body(_, c):
        lo, hi = c
        mid = 0.5 * (lo + hi)
        ge = count_ge(mid) >= capf
        return jnp.where(ge, mid, lo), jnp.where(ge, hi, mid)

    lo, hi = lax.fori_loop(
        0, 24, val_body, (lax.bitcast_convert_type(lo_i, F32), lax.bitcast_convert_type(hi_i, F32)))
    need = capf - count_ge(hi)

    above = jnp.where(at >= hi, 1.0, 0.0)
    band = jnp.where(at >= lo, 1.0, 0.0) - above
    sel = above + band * jnp.where(_cumsum_lanes(band) <= need, 1.0, 0.0)
    rank = _cumsum_lanes(sel)

    lane = lax.broadcasted_iota(jnp.int32, (n_exp, n), 1)
    disp = jnp.where(sel > 0.5, lane + 1 - rank.astype(jnp.int32), SLOT_EMPTY)
    g = at
    for k in range(n.bit_length() - 1):
        step = 1 << k
        moving = ((disp >> k) & 1) == 1
        disp_in = pltpu.roll(disp, n - step, 1)
        arriving = ((disp_in >> k) & 1) == 1
        g = jnp.where(arriving, pltpu.roll(g, n - step, 1), g)
        disp = jnp.where(arriving, disp_in, jnp.where(moving, SLOT_EMPTY, disp))

    capp = pl.cdiv(cap, LANES) * LANES
    slot = lax.broadcasted_iota(jnp.int32, (n_exp, cap), 1)
    idx_ref[...] = jnp.clip(slot + disp[:, :cap], 0, n - 1).reshape(bblk, n_e, cap)
    for s in range(bblk):
        g_pad = jnp.concatenate([g[s * n_e:(s + 1) * n_e, :capp], jnp.zeros((LANES - n_e, capp), F32)], axis=0)
        g_ref[s] = g_pad.T[:cap]


def _select(aff_t, cap):
    b, n_exp, n = aff_t.shape
    assert n & (n - 1) == 0 and n % CUMSUM_TILE == 0 and n < SLOT_EMPTY
    bblk = b
    return pl.pallas_call(
        functools.partial(_select_kernel, cap=cap),
        grid=(b // bblk,),
        in_specs=[pl.BlockSpec((bblk, n_exp, n), lambda bi: (bi, 0, 0))],
        out_specs=(pl.BlockSpec((bblk, n_exp, cap), lambda bi: (bi, 0, 0)),
                   pl.BlockSpec((bblk, cap, LANES), lambda bi: (bi, 0, 0))),
        out_shape=(jax.ShapeDtypeStruct((b, n_exp, cap), jnp.int32), jax.ShapeDtypeStruct((b, cap, LANES), F32)),
        compiler_params=_params("arbitrary"),
        name="moe_select",
    )(aff_t)


def _gather_kernel(idx_ref, src_ref, xe_ref, xcm_ref, *, cap, n_exp):
    bblk = src_ref.shape[0]
    eblk, out_rows = xe_ref.shape[1], xe_ref.shape[2]
    stride = out_rows + SUBLANES
    for k in range(eblk):
        for s in range(bblk):
            base = ((pl.program_id(0) * bblk + s) * n_exp + pl.program_id(1) * eblk + k) * cap
            for r in range(cap):
                t = idx_ref[base + r]
                xcm_ref[k, pl.ds(s * cap + r, SUBLANES, stride=stride), :] = (
                    src_ref[s, pl.ds(pl.multiple_of(t * SUBLANES, SUBLANES), SUBLANES), :])
        used = bblk * cap
        if used < out_rows:
            for j in range(SUBLANES):
                xcm_ref[k, pl.ds(j * stride + used, out_rows - used), :] = jnp.zeros((out_rows - used, LANES), F32)
        xe_ref[0, k] = jnp.concatenate(
            [xcm_ref[k, pl.ds(j * stride, out_rows), :] for j in range(SUBLANES)], axis=1).astype(xe_ref.dtype)


def _gather(idx_flat, src, cap, bblk, out_rows):
    b, rows, _ = src.shape
    d = SUBLANES * LANES
    assert cap % SUBLANES == 0 and bblk * cap <= out_rows and b % bblk == 0 and N_EXPERTS % DISPATCH_EXPERTS == 0
    return pl.pallas_call(
        functools.partial(_gather_kernel, cap=cap, n_exp=N_EXPERTS),
        grid=(b // bblk, N_EXPERTS // DISPATCH_EXPERTS),
        in_specs=[pl.BlockSpec(memory_space=pltpu.SMEM),
                  pl.BlockSpec((bblk, rows, LANES), lambda bi, ei: (bi, 0, 0))],
        out_specs=pl.BlockSpec((1, DISPATCH_EXPERTS, out_rows, d), lambda bi, ei: (bi, ei, 0, 0)),
        out_shape=jax.ShapeDtypeStruct((b // bblk, N_EXPERTS, out_rows, d), MXU_DTYPE),
        scratch_shapes=[pltpu.VMEM((DISPATCH_EXPERTS, SUBLANES * (out_rows + SUBLANES), LANES), F32)],
        compiler_params=_params("arbitrary", "arbitrary"),
        name="moe_gather",
    )(idx_flat, src)


WEIGHT_CHUNKS = 2
WEIGHT_DMA_PRIORITY = 1


def _moe_ffn_kernel(*refs, layer, n_steps, has_ctx):
    if has_ctx:
        (x_ref, g_ref, xc_ref, gc_ref, wg_hbm, wu_hbm, wd_hbm, o_ref, oc_ref,
         wg_s, wu_s, wd_s, stage_in, stage_out, sem) = refs
    else:
        (x_ref, g_ref, wg_hbm, wu_hbm, wd_hbm, o_ref, wg_s, wu_s, wd_s, stage_in, stage_out, sem) = refs
    n_exp = pl.num_programs(0)
    e = pl.program_id(0)
    s = pl.program_id(1)
    slot = e % 2
    chunks = []
    for hbm, dst, stage in ((wg_hbm, wg_s, stage_in), (wu_hbm, wu_s, stage_in), (wd_hbm, wd_s, stage_out)):
        n_rows = hbm.shape[2] // WEIGHT_CHUNKS
        chunks += [(hbm, dst, stage, c * n_rows, n_rows) for c in range(WEIGHT_CHUNKS)]
    per_step = pl.cdiv(len(chunks), n_steps)

    def copy(c, expert):
        hbm, _, stage, r0, n_rows = chunks[c]
        return pltpu.make_async_copy(hbm.at[layer, expert, pl.ds(r0, n_rows), :], stage, sem.at[0])

    def cast(c, to_slot):
        _, dst, stage, r0, n_rows = chunks[c]
        dst[to_slot, pl.ds(r0, n_rows), :] = stage[...].astype(dst.dtype)

    @pl.when((e == 0) & (s == 0))
    def _():
        for c in range(len(chunks)):
            copy(c, 0).start()
            copy(c, 0).wait()
            cast(c, 0)

    has_next = e + 1 < n_exp
    for c in range(len(chunks)):
        if c % per_step == 0:
            @pl.when(has_next & (s == c // per_step))
            def _():
                copy(c, e + 1).start(priority=WEIGHT_DMA_PRIORITY)

    def ffn(xr, gr, outr):
        x, aff = xr[0, 0], gr[0]
        rows = x.shape[0]
        a = jnp.dot(x, wg_s[slot], preferred_element_type=F32)
        u = jnp.dot(x, wu_s[slot], preferred_element_type=F32)
        mid = (a * _sigmoid(a) * u).astype(MXU_DTYPE)
        y = jnp.dot(mid, wd_s[slot], preferred_element_type=F32)
        lane = lax.broadcasted_iota(jnp.int32, aff.shape, 1)
        y = y * jnp.sum(jnp.where(lane == e, aff, 0.0), axis=1, keepdims=True)
        stride = rows + SUBLANES
        for j in range(SUBLANES):
            outr[0, 0, pl.ds(j * stride, rows), :] = y[:, j * LANES:(j + 1) * LANES]
            outr[0, 0, pl.ds(j * stride + rows, SUBLANES), :] = jnp.zeros((SUBLANES, LANES), F32)

    if has_ctx:
        is_ctx = s == n_steps - 1
        pl.when(is_ctx)(lambda: ffn(xc_ref, gc_ref, oc_ref))
        pl.when(jnp.logical_not(is_ctx))(lambda: ffn(x_ref, g_ref, o_ref))
    else:
        ffn(x_ref, g_ref, o_ref)

    for c in range(len(chunks)):
        @pl.when(has_next & (s == c // per_step))
        def _():
            copy(c, e + 1).wait()
            cast(c, 1 - slot)
            if (c + 1) % per_step != 0 and c + 1 < len(chunks):
                copy(c + 1, e + 1).start(priority=WEIGHT_DMA_PRIORITY)


def _moe_ffn(xe, ge, ctx_block, layer, wg, wu, wd):
    nb, e, rows, d = xe.shape
    ff = wg.shape[-1]
    assert d % WEIGHT_CHUNKS == 0 and ff % (WEIGHT_CHUNKS * SUBLANES) == 0
    has_ctx = ctx_block is not None
    n_steps = nb + int(has_ctx)
    hbm = pl.BlockSpec(memory_space=pl.ANY)
    out_rows = SUBLANES * (rows + SUBLANES)
    blk = lambda bi: jnp.minimum(bi, nb - 1)
    in_specs = [pl.BlockSpec((1, 1, rows, d), lambda ei, bi: (blk(bi), ei, 0, 0)),
                pl.BlockSpec((1, rows, LANES), lambda ei, bi: (blk(bi), 0, 0))]
    args = [xe, ge]
    out_specs = [pl.BlockSpec((1, 1, out_rows, LANES), lambda ei, bi: (blk(bi), ei, 0, 0))]
    out_shape = [jax.ShapeDtypeStruct((nb, e, out_rows, LANES), F32)]
    if has_ctx:
        rows_c = ctx_block[0].shape[2]
        assert ctx_block[0].shape == (1, e, rows_c, d) and ctx_block[1].shape == (1, rows_c, LANES)
        out_rows_c = SUBLANES * (rows_c + SUBLANES)
        in_specs += [pl.BlockSpec((1, 1, rows_c, d), lambda ei, bi: (0, ei, 0, 0)),
                     pl.BlockSpec((1, rows_c, LANES), lambda ei, bi: (0, 0, 0))]
        args += list(ctx_block)
        out_specs.append(pl.BlockSpec((1, 1, out_rows_c, LANES), lambda ei, bi: (0, ei, 0, 0)))
        out_shape.append(jax.ShapeDtypeStruct((1, e, out_rows_c, LANES), F32))
    return pl.pallas_call(
        functools.partial(_moe_ffn_kernel, layer=layer, n_steps=n_steps, has_ctx=has_ctx),
        grid=(e, n_steps),
        in_specs=in_specs + [hbm, hbm, hbm],
        out_specs=tuple(out_specs),
        out_shape=tuple(out_shape),
        scratch_shapes=[pltpu.VMEM((2, d, ff), MXU_DTYPE), pltpu.VMEM((2, d, ff), MXU_DTYPE),
                        pltpu.VMEM((2, ff, d), MXU_DTYPE),
                        pltpu.VMEM((d // WEIGHT_CHUNKS, ff), F32), pltpu.VMEM((ff // WEIGHT_CHUNKS, d), F32),
                        pltpu.SemaphoreType.DMA((1,))],
        compiler_params=_params("arbitrary", "arbitrary"),
        name="moe_ffn",
    )(*args, wg, wu, wd)


COMBINE_UNROLL = 8


def _combine_kernel(idx_ref, y_ref, acc_ref, *, cap, n_exp):
    bblk = acc_ref.shape[0]
    eblk = y_ref.shape[1]
    ei = pl.program_id(1)
    stride = y_ref.shape[2] // SUBLANES

    @pl.when(ei == 0)
    def _():
        acc_ref[...] = jnp.zeros_like(acc_ref)

    for k in range(eblk):
        for s in range(bblk):
            base = ((pl.program_id(0) * bblk + s) * n_exp + ei * eblk + k) * cap
            for r0 in range(0, cap, COMBINE_UNROLL):
                toks = [pl.multiple_of(idx_ref[base + r0 + i] * SUBLANES, SUBLANES) for i in range(COMBINE_UNROLL)]
                new = [acc_ref[s, pl.ds(toks[i], SUBLANES), :]
                       + y_ref[0, k, pl.ds(s * cap + r0 + i, SUBLANES, stride=stride), :]
                       for i in range(COMBINE_UNROLL)]
                for i in range(COMBINE_UNROLL):
                    acc_ref[s, pl.ds(toks[i], SUBLANES), :] = new[i]


def _combine(idx_flat, y, b, n, cap, bblk):
    nb, n_exp, rows, _ = y.shape
    assert cap % COMBINE_UNROLL == 0 and nb * bblk == b and bblk * cap <= rows // SUBLANES - SUBLANES
    return pl.pallas_call(
        functools.partial(_combine_kernel, cap=cap, n_exp=n_exp),
        grid=(nb, n_exp // DISPATCH_EXPERTS),
        in_specs=[pl.BlockSpec(memory_space=pltpu.SMEM),
                  pl.BlockSpec((1, DISPATCH_EXPERTS, rows, LANES), lambda bi, ei: (bi, ei, 0, 0))],
        out_specs=pl.BlockSpec((bblk, n * SUBLANES, LANES), lambda bi, ei: (bi, 0, 0)),
        out_shape=jax.ShapeDtypeStruct((b, n * SUBLANES, LANES), F32),
        compiler_params=_params("arbitrary", "arbitrary"),
        name="moe_combine",
    )(idx_flat, y)


def _expert_choice(h_lat, aff_lat, h_ctx, aff_ctx, layer, wg, wu, wd):
    b, _, n = aff_lat.shape
    cap = max(1, CAPACITY_FACTOR * n // N_EXPERTS)
    idx, g = _select(aff_lat, cap)
    idx = idx.reshape(-1)
    xe = _gather(idx, h_lat, cap, 1, cap)
    if h_ctx is None:
        (y,) = _moe_ffn(xe, g, None, layer, wg, wu, wd)
        return _combine(idx, y, b, n, cap, 1), None
    lc = aff_ctx.shape[2]
    cap_c = max(1, CAPACITY_FACTOR * lc // N_EXPERTS)
    idx_c, g_c = _select(aff_ctx, cap_c)
    idx_c = idx_c.reshape(-1)
    xe_c = _gather(idx_c, h_ctx, cap_c, b, b * cap_c)
    g_c = g_c.reshape(1, b * cap_c, LANES)
    y, y_c = _moe_ffn(xe, g, (xe_c, g_c), layer, wg, wu, wd)
    return _combine(idx, y, b, n, cap, 1), _combine(idx_c, y_c, b, lc, cap_c, b)


def _odd_in_kernel(x_ref, moe_ref, gate_ref, g_ref, sh_ref, sc_ref, w_ref, xo_ref, gg_ref, u_ref):
    x = x_ref[0] + gate_ref[0] * _load_token_tiles(moe_ref)
    xo_ref[0] = x
    h = _norm_mod(x, g_ref[...], sh_ref[0], sc_ref[0]).astype(MXU_DTYPE)
    y = jnp.dot(h, w_ref[...], preferred_element_type=F32)
    gl = y[:, :LRU_WIDTH]
    _store_slabs(gg_ref, 0.5 * gl * (1.0 + jnp.tanh(0.7978845608028654 * (gl + 0.044715 * gl * gl * gl))))
    _store_slabs(u_ref, y[:, LRU_WIDTH:])


def _odd_in(x, moe, gate, g, shift, scale, mod_row, w, tile):
    b, n, d = x.shape
    row = (lambda bi: bi) if mod_row is None else (lambda bi: mod_row)
    tok = pl.BlockSpec((1, tile, d), lambda bi, t: (bi, t, 0))
    tiles = pl.BlockSpec((1, tile * SUBLANES, LANES), lambda bi, t: (bi, t, 0))
    modspec = pl.BlockSpec((1, 1, d), lambda bi, t: (row(bi), 0, 0))
    return pl.pallas_call(
        _odd_in_kernel,
        grid=(b, n // tile),
        in_specs=[tok, tiles, modspec, pl.BlockSpec((1, d), lambda bi, t: (0, 0)), modspec, modspec,
                  pl.BlockSpec(w.shape, lambda bi, t: (0, 0))],
        out_specs=(tok, _slab_spec(tile, lambda bi, t: (bi, 0, t, 0)), _slab_spec(tile, lambda bi, t: (bi, 0, t, 0))),
        out_shape=(jax.ShapeDtypeStruct((b, n, d), F32),) + (jax.ShapeDtypeStruct((b, d // LANES, n, LANES), F32),) * 2,
        compiler_params=_params("arbitrary", "arbitrary"),
        name="odd_in",
    )(x, moe, gate, g, shift, scale, w)


def _lru_kernel(*refs, reverse, nt):
    if reverse:
        (u_ref, up_ref, un_ref, cw_ref, cb_ref, wax_ref, ba_ref, bx_ref, lam_ref, h0_ref, hf_ref, gg_ref,
         out_ref, hlast_ref, a_scr, b_scr, uc_scr, carry_scr) = refs
    else:
        (u_ref, up_ref, un_ref, cw_ref, cb_ref, wax_ref, ba_ref, bx_ref, lam_ref, h0_ref,
         out_ref, hlast_ref, a_scr, b_scr, uc_scr, carry_scr) = refs
    n_slab, tile = u_ref.shape[1], u_ref.shape[2]
    w = n_slab * LANES
    per = tile // SUBLANES
    phase = lambda ref, j: jnp.concatenate(
        [ref[0, c, pl.ds(j, per, stride=SUBLANES), :] for c in range(n_slab)], axis=1)
    rows = lambda j: slice(j * per, (j + 1) * per)
    t = pl.program_id(1)
    pos = (nt - 1 - t) if reverse else t
    prev = jnp.where(pos == 0, 0.0, _load_slabs(up_ref))
    nxt = jnp.where(pos == nt - 1, 0.0, _load_slabs(un_ref))

    rowid = lax.broadcasted_iota(jnp.int32, (per, w), 0)
    shift_down = lambda x, first: jnp.where(rowid == 0, first, pltpu.roll(x, 1, 0))
    shift_up = lambda x, last: jnp.where(rowid == per - 1, last, pltpu.roll(x, per - 1, 0))
    u = [phase(u_ref, j) for j in range(SUBLANES)]
    um1 = [shift_down(u[7], prev[7:8])] + u[:7]
    um2 = [shift_down(u[6], prev[6:7]), um1[0]] + u[:6]
    up1 = u[1:] + [shift_up(u[0], nxt[0:1])]
    cw = cw_ref[...]
    for j in range(SUBLANES):
        uc_scr[rows(j), :] = (cb_ref[...] + cw[0:1] * um2[j] + cw[1:2] * um1[j] + cw[2:3] * u[j] + cw[3:4] * up1[j])

    lam = lam_ref[0]
    half_decay = (0.5 * LRU_C) * (jnp.maximum(-lam, 0.0) + jnp.log1p(jnp.exp(-jnp.abs(lam))))
    for hd in range(LRU_HEADS):
        sl = slice(hd * LRU_BLOCK, (hd + 1) * LRU_BLOCK)
        uc = uc_scr[:, sl]
        z = jnp.dot(uc.astype(MXU_DTYPE), wax_ref[0, hd], preferred_element_type=F32)
        hd_row = half_decay[:, sl]
        neg_log_a = hd_row * jnp.tanh(z[:, :LRU_BLOCK] + ba_ref[0][:, sl]) + hd_row
        gate2 = 1.0 + jnp.tanh(z[:, LRU_BLOCK:] + bx_ref[0][:, sl])
        a = jnp.exp(-neg_log_a)
        m2 = jnp.tanh(neg_log_a) * (a * a + 1.0)
        mult = jnp.where(m2 > 0.0, m2 * lax.rsqrt(m2), 0.0)
        a_scr[:, sl] = a
        b_scr[:, sl] = (0.5 * mult) * (gate2 * uc)

    @pl.when(t == 0)
    def _():
        carry_scr[...] = jnp.broadcast_to(h0_ref[0], carry_scr.shape)

    order = list(range(SUBLANES))[::-1] if reverse else list(range(SUBLANES))
    hrun = b_scr[rows(order[0]), :]
    prun = a_scr[rows(order[0]), :]
    for j in order[1:]:
        aj = a_scr[rows(j), :]
        hrun = aj * hrun + b_scr[rows(j), :]
        prun = aj * prun
        b_scr[rows(j), :] = hrun
        a_scr[rows(j), :] = prun

    lane_row = lax.broadcasted_iota(jnp.int32, (SUBLANES, w), 0)
    carry = carry_scr[...]
    groups = list(range(per // SUBLANES))
    entering = [None] * len(groups)
    for m in (groups[::-1] if reverse else groups):
        a = prun[m * SUBLANES:(m + 1) * SUBLANES]
        bcoef = hrun[m * SUBLANES:(m + 1) * SUBLANES]
        for dist in (1, 2, 4):
            shift = (SUBLANES - dist) if reverse else dist
            msk = (lane_row < SUBLANES - dist) if reverse else (lane_row >= dist)
            a_s = pltpu.roll(a, shift, 0)
            b_s = pltpu.roll(bcoef, shift, 0)
            bcoef = jnp.where(msk, a * b_s + bcoef, bcoef)
            a = jnp.where(msk, a * a_s, a)
        after = a * carry + bcoef
        if reverse:
            entering[m] = jnp.where(lane_row == SUBLANES - 1, carry, pltpu.roll(after, SUBLANES - 1, 0))
            carry = jnp.broadcast_to(after[0:1], carry.shape)
        else:
            entering[m] = jnp.where(lane_row == 0, carry, pltpu.roll(after, 1, 0))
            carry = jnp.broadcast_to(after[SUBLANES - 1:SUBLANES], carry.shape)
    carry_scr[...] = carry
    hlast_ref[0] = carry[0:1]
    h_in = jnp.concatenate(entering, axis=0)

    for j in range(SUBLANES):
        hcur = b_scr[rows(j), :] + a_scr[rows(j), :] * h_in
        if reverse:
            hcur = phase(gg_ref, j) * (phase(hf_ref, j) + hcur)
        for c in range(n_slab):
            out_ref[0, c, pl.ds(j, per, stride=SUBLANES), :] = hcur[:, c * LANES:(c + 1) * LANES]


def _lru(u, cw, cb, wax, ba, bx, lam, h0, direction, tile, hf=None, gg=None):
    b, n_slab, n, _ = u.shape
    w = n_slab * LANES
    nt = n // tile
    per = tile // HALO
    reverse = direction == 1
    pos = (lambda t: nt - 1 - t) if reverse else (lambda t: t)
    tok = _slab_spec(tile, lambda bi, t: (bi, 0, pos(t), 0), w)
    rowspec = pl.BlockSpec((1, 1, w), lambda bi, t: (direction, 0, 0))
    in_specs = [
        tok,
        _slab_spec(HALO, lambda bi, t: (bi, 0, jnp.maximum(pos(t) * per - 1, 0), 0), w),
        _slab_spec(HALO, lambda bi, t: (bi, 0, jnp.minimum((pos(t) + 1) * per, n // HALO - 1), 0), w),
        pl.BlockSpec(cw.shape, lambda bi, t: (0, 0)),
        pl.BlockSpec(cb.shape, lambda bi, t: (0, 0)),
        pl.BlockSpec((1,) + wax.shape[1:], lambda bi, t: (direction, 0, 0, 0)),
        rowspec, rowspec, rowspec,
        pl.BlockSpec((1, 1, w), lambda bi, t: (bi, 0, 0)),
    ]
    args = [u, u, u, cw, cb, wax, ba, bx, lam, h0]
    if reverse:
        in_specs += [tok, tok]
        args += [hf, gg]
    assert tile % (SUBLANES * SUBLANES) == 0
    return pl.pallas_call(
        functools.partial(_lru_kernel, reverse=reverse, nt=nt),
        grid=(b, nt),
        in_specs=in_specs,
        out_specs=(tok, pl.BlockSpec((1, 1, w), lambda bi, t: (bi, 0, 0))),
        out_shape=(jax.ShapeDtypeStruct(u.shape, F32), jax.ShapeDtypeStruct((b, 1, w), F32)),
        scratch_shapes=[pltpu.VMEM((tile, w), F32), pltpu.VMEM((tile, w), F32), pltpu.VMEM((tile, w), F32),
                        pltpu.VMEM((SUBLANES, w), F32)],
        compiler_params=_params("arbitrary", "arbitrary"),
        name="lru_bwd" if reverse else "lru_fwd",
    )(*args)


def _final_kernel(x_ref, moe_ref, gate_ref, g_ref, o_ref):
    x = x_ref[0] + gate_ref[0] * _load_token_tiles(moe_ref)
    o_ref[0] = x * lax.rsqrt(jnp.mean(x * x, axis=-1, keepdims=True) + EPS) * g_ref[...]


def _final(x, moe, gate, g, tile):
    b, n, d = x.shape
    tok = pl.BlockSpec((1, tile, d), lambda bi, t: (bi, t, 0))
    tiles = pl.BlockSpec((1, tile * SUBLANES, LANES), lambda bi, t: (bi, t, 0))
    return pl.pallas_call(
        _final_kernel,
        grid=(b, n // tile),
        in_specs=[tok, tiles, pl.BlockSpec((1, 1, d), lambda bi, t: (bi, 0, 0)), pl.BlockSpec((1, d), lambda bi, t: (0, 0))],
        out_specs=tok,
        out_shape=jax.ShapeDtypeStruct((b, n, d), F32),
        compiler_params=_params("arbitrary", "arbitrary"),
        name="final_norm",
    )(x, moe, gate, g)


def kernel(x, c, ctx, c_ctx, ada_w, ada_b, norm_mix_g, norm_ffn_g, ev_w_in, ev_w_out, ev_sink, ev_conv_w, ev_conv_b, od_w_in, od_w_out, od_conv_w, od_conv_b, od_wa, od_ba, od_wx, od_bx, od_lambda, router_w, w_gate, w_up, w_down, final_g):
    b, n, d = x.shape
    lc = ctx.shape[1]
    depth = ada_w.shape[0]
    assert depth == 2 and d == D_MODEL and b < MOD_ROWS
    tile_l = min(512, n)
    tile_c = lc
    ctx_row = b

    cvec = jnp.zeros((MOD_ROWS, d), F32).at[:b].set(c).at[b].set(c_ctx)
    mods = _ada(cvec, ada_w, ada_b).reshape(depth, MOD_ROWS, 6, 1, d)
    mod = lambda l, j: mods[l, :, j]

    def router_split(l):
        return _split_hi_lo(jnp.pad(router_w[l], ((0, 0), (0, LANES - N_EXPERTS))))

    bf = lambda a: a.astype(MXU_DTYPE)

    g_mix = norm_mix_g[0].reshape(1, d)
    g_ffn = norm_ffn_g[0].reshape(1, d)
    w_in = bf(ev_w_in[0])
    w_out = bf(ev_w_out[0])
    conv_p = (ev_conv_w[0], ev_conv_b[0].reshape(1, -1))
    tables = _rope_tables(n)
    ql, kvl, gbl, cul = _even_in(x, g_mix, mod(0, 0), mod(0, 1), None, w_in, tables, tile_l)
    qc, kvc, gbc, cuc = _even_in(ctx, g_mix, mod(0, 0), mod(0, 1), ctx_row, w_in, None, tile_c)
    att_l = _attention(ev_sink[0], ql, kvl, kvc)
    att_c = _attention(ev_sink[0], qc, None, kvc)
    rw_hi, rw_lo = router_split(0)
    xl, hl, at_l = _mix_out((att_l, gbl, cul), conv_p, w_out, x, mod(0, 2), g_ffn, mod(0, 3), mod(0, 4), None,
                            rw_hi, rw_lo, tile_l)
    xc, hc, at_c = _mix_out((att_c, gbc, cuc), conv_p, w_out, ctx, mod(0, 2), g_ffn, mod(0, 3), mod(0, 4),
                            ctx_row, rw_hi, rw_lo, tile_c)
    moe_l, moe_c = _expert_choice(hl, at_l, hc, at_c, 0, w_gate, w_up, w_down)

    g_mix = norm_mix_g[1].reshape(1, d)
    g_ffn = norm_ffn_g[1].reshape(1, d)
    w_in = bf(od_w_in[0])
    w_out = bf(od_w_out[0])
    xl, ggl, ul = _odd_in(xl, moe_l, mod(0, 5), g_mix, mod(1, 0), mod(1, 1), None, w_in, tile_l)
    _, _, uc = _odd_in(xc, moe_c, mod(0, 5), g_mix, mod(1, 0), mod(1, 1), ctx_row, w_in, tile_c)
    cw, cb = od_conv_w[0], od_conv_b[0].reshape(1, -1)
    wax = bf(0.5 * jnp.concatenate([od_wa[0], od_wx[0]], axis=-1))
    ba, bx = (0.5 * a[0].reshape(2, 1, -1) for a in (od_ba, od_bx))
    lam = od_lambda[0].reshape(2, 1, -1)
    zero_state = jnp.zeros((b, 1, LRU_WIDTH), F32)
    hf_c, h0_f = _lru(uc, cw, cb, wax, ba, bx, lam, zero_state, 0, tile_c)
    _, h0_b = _lru(uc, cw, cb, wax, ba, bx, lam, zero_state, 1, tile_c, hf=hf_c, gg=hf_c)
    hf_l, _ = _lru(ul, cw, cb, wax, ba, bx, lam, h0_f, 0, tile_l)
    yl, _ = _lru(ul, cw, cb, wax, ba, bx, lam, h0_b, 1, tile_l, hf=hf_l, gg=ggl)
    rw_hi, rw_lo = router_split(1)
    xl, hl, at_l = _mix_out(yl, None, w_out, xl, mod(1, 2), g_ffn, mod(1, 3), mod(1, 4), None,
                            rw_hi, rw_lo, tile_l)
    moe_l, _ = _expert_choice(hl, at_l, None, None, 1, w_gate, w_up, w_down)
    return _final(xl, moe_l, mod(1, 5), final_g.reshape(1, d), tile_l)
```

```python
import functools

import jax
import jax.numpy as jnp
from jax import lax
from jax.experimental import pallas as pl
from jax.experimental.pallas import tpu as pltpu

F32 = jnp.float32
MXU_DTYPE = jnp.bfloat16

D_MODEL = 1024
GRID_W = 64
EPS = 1e-6
NEG_INF = -1e30
HEAD_DIM = 64
N_Q_HEADS = 8
N_KV_HEADS = 2
Q_PER_KV = N_Q_HEADS // N_KV_HEADS
ATTN_WIDTH = N_Q_HEADS * HEAD_DIM
KV_WIDTH = N_KV_HEADS * HEAD_DIM
WINDOW = 128
BLOCK = 128
ROPE_BASE = 10000.0
CONV_B_WIDTH = D_MODEL // 2
EVEN_IN = ATTN_WIDTH + 2 * KV_WIDTH + 3 * CONV_B_WIDTH
LRU_WIDTH = D_MODEL
LRU_HEADS = 8
LRU_BLOCK = LRU_WIDTH // LRU_HEADS
LRU_C = 8.0
N_EXPERTS = 16
CAPACITY_FACTOR = 2
EXPERT_FF = 1408
MOD_ROWS = 16
LANES = 128
SUBLANES = 8
HALO = SUBLANES
CUMSUM_TILE = 256
MIX_OUT_SPLIT = 2
ATTN_QBLOCKS = 2
DISPATCH_EXPERTS = 4
V7X_VMEM_LIMIT = 56 * 1024 * 1024


def _params(*sem):
    return pltpu.CompilerParams(dimension_semantics=sem, vmem_limit_bytes=V7X_VMEM_LIMIT)


def _split_hi_lo(a):
    hi = a.astype(MXU_DTYPE)
    lo = (a - hi.astype(F32)).astype(MXU_DTYPE)
    return hi, lo


def _sigmoid(z):
    return 0.5 * (1.0 + jnp.tanh(0.5 * z))


def _store_token_tiles(ref, val, row0=0):
    tile = val.shape[0]
    for j in range(SUBLANES):
        ref[0, pl.ds(row0 * SUBLANES + j, tile, stride=SUBLANES), :] = val[:, j * LANES:(j + 1) * LANES]


def _load_token_tiles(ref):
    tile = ref.shape[1] // SUBLANES
    return jnp.concatenate([ref[0, pl.ds(j, tile, stride=SUBLANES), :] for j in range(SUBLANES)], axis=1)


def _slab_spec(rows, index_map, width=LRU_WIDTH):
    return pl.BlockSpec((1, width // LANES, rows, LANES), index_map)


def _store_slabs(ref, val):
    for c in range(val.shape[1] // LANES):
        ref[0, c] = val[:, c * LANES:(c + 1) * LANES]


def _load_slabs(ref):
    return jnp.concatenate([ref[0, c] for c in range(ref.shape[1])], axis=1)


def _norm_mod(x, g, shift, scale):
    y = x * lax.rsqrt(jnp.mean(x * x, axis=-1, keepdims=True) + EPS) * g
    return y * (1.0 + scale) + shift


def _ada_kernel(c_ref, w_ref, b_ref, o_ref):
    c = c_ref[...]
    s_hi, s_lo = _split_hi_lo(c * _sigmoid(c))
    w_hi, w_lo = _split_hi_lo(w_ref[0])
    acc = jnp.dot(s_hi, w_hi, preferred_element_type=F32)
    acc += jnp.dot(s_hi, w_lo, preferred_element_type=F32)
    acc += jnp.dot(s_lo, w_hi, preferred_element_type=F32)
    o_ref[0] = acc + b_ref[0]


def _ada(cvec, ada_w, ada_b):
    depth, d, n6 = ada_w.shape
    tn = 1536
    return pl.pallas_call(
        _ada_kernel,
        grid=(depth, n6 // tn),
        in_specs=[
            pl.BlockSpec((MOD_ROWS, d), lambda l, j: (0, 0)),
            pl.BlockSpec((1, d, tn), lambda l, j: (l, 0, j)),
            pl.BlockSpec((1, 1, tn), lambda l, j: (l, 0, j)),
        ],
        out_specs=pl.BlockSpec((1, MOD_ROWS, tn), lambda l, j: (l, 0, j)),
        out_shape=jax.ShapeDtypeStruct((depth, MOD_ROWS, n6), F32),
        compiler_params=_params("arbitrary", "arbitrary"),
        name="ada",
    )(cvec, ada_w, ada_b.reshape(depth, 1, n6))


def _even_in_kernel(*refs, rope):
    if rope:
        (x_ref, g_ref, sh_ref, sc_ref, w_ref, cos_ref, sa_ref, sb_ref, q_ref, kv_ref, gb_ref, cu_ref) = refs
    else:
        (x_ref, g_ref, sh_ref, sc_ref, w_ref, q_ref, kv_ref, gb_ref, cu_ref) = refs
    h = _norm_mod(x_ref[0], g_ref[...], sh_ref[0], sc_ref[0]).astype(MXU_DTYPE)
    y = jnp.dot(h, w_ref[...], preferred_element_type=F32)
    q = y[:, :ATTN_WIDTH]
    k = y[:, ATTN_WIDTH:ATTN_WIDTH + KV_WIDTH]
    v = y[:, ATTN_WIDTH + KV_WIDTH:ATTN_WIDTH + 2 * KV_WIDTH]
    c0 = ATTN_WIDTH + 2 * KV_WIDTH
    if rope:
        cos, sa, sb = cos_ref[...], sa_ref[...], sb_ref[...]

        def rot(z):
            return z * cos + pltpu.roll(z, 16, 1) * sa + pltpu.roll(z, 112, 1) * sb

        q = jnp.concatenate([rot(q[:, j * 128:(j + 1) * 128]) for j in range(ATTN_WIDTH // 128)], axis=1)
        k = rot(k)
    q_ref[0] = (q * (HEAD_DIM ** -0.5)).astype(q_ref.dtype)
    kv_ref[0] = jnp.concatenate([k, v], axis=1).astype(kv_ref.dtype)
    gb_ref[0] = y[:, c0:c0 + CONV_B_WIDTH]
    cu_ref[0] = y[:, c0 + CONV_B_WIDTH:c0 + 2 * CONV_B_WIDTH] * y[:, c0 + 2 * CONV_B_WIDTH:]


def _even_in(x, g, shift, scale, mod_row, w, tables, tile):
    b, n, d = x.shape
    nt = n // tile
    rope = tables is not None
    row = (lambda bi: bi) if mod_row is None else (lambda bi: mod_row)
    in_specs = [
        pl.BlockSpec((1, tile, d), lambda bi, t: (bi, t, 0)),
        pl.BlockSpec((1, d), lambda bi, t: (0, 0)),
        pl.BlockSpec((1, 1, d), lambda bi, t: (row(bi), 0, 0)),
        pl.BlockSpec((1, 1, d), lambda bi, t: (row(bi), 0, 0)),
        pl.BlockSpec(w.shape, lambda bi, t: (0, 0)),
    ]
    args = [x, g, shift, scale, w]
    if rope:
        in_specs += [pl.BlockSpec((tile, 128), lambda bi, t: (t, 0))] * 3
        args += list(tables)
    out_shape = (
        jax.ShapeDtypeStruct((b, n, ATTN_WIDTH), MXU_DTYPE),
        jax.ShapeDtypeStruct((b, n, 2 * KV_WIDTH), MXU_DTYPE),
        jax.ShapeDtypeStruct((b, n, CONV_B_WIDTH), F32),
        jax.ShapeDtypeStruct((b, n, CONV_B_WIDTH), F32),
    )
    out_specs = tuple(pl.BlockSpec((1, tile, s.shape[-1]), lambda bi, t: (bi, t, 0)) for s in out_shape)
    return pl.pallas_call(
        functools.partial(_even_in_kernel, rope=rope),
        grid=(b, nt),
        in_specs=in_specs,
        out_specs=out_specs,
        out_shape=out_shape,
        compiler_params=_params("arbitrary", "arbitrary"),
        name="even_in_rope" if rope else "even_in",
    )(*args)


def _rope_tables(n):
    nf = HEAD_DIM // 4
    pos = jnp.arange(n)
    rows = (pos // GRID_W).astype(F32)
    cols = (pos % GRID_W).astype(F32)
    lane = jnp.arange(128)
    inv = ROPE_BASE ** (-(lane % nf).astype(F32) / nf)
    use_col = (lane % HEAD_DIM) >= HEAD_DIM // 2
    ang = jnp.where(use_col[None, :], cols[:, None], rows[:, None]) * inv[None, :]
    cos, sin = jnp.cos(ang), jnp.sin(ang)
    second = ((lane % (2 * nf)) >= nf)[None, :]
    return cos, jnp.where(second, sin, 0.0), jnp.where(second, 0.0, -sin)


def _attn_kernel(*refs, n, has_local):
    if has_local:
        sink_ref, q_ref, kv_ref, kvc_ref, bias_ref, o_ref = refs
    else:
        sink_ref, q_ref, kvc_ref, o_ref = refs
    n_loc = 3 * BLOCK
    nb = n // BLOCK
    grp = lax.broadcasted_iota(jnp.int32, (Q_PER_KV * BLOCK, 1), 0) // BLOCK
    for sb in range(q_ref.shape[1] // BLOCK):
        i = pl.program_id(1) * (q_ref.shape[1] // BLOCK) + sb
        q = q_ref[0, sb * BLOCK:(sb + 1) * BLOCK, :]
        kvall = kvc_ref[0]
        if has_local:
            start = pl.multiple_of(_local_start(i, n), BLOCK)
            kvall = jnp.concatenate([kv_ref[0, pl.ds(start, n_loc), :], kvall], axis=0)
            case = jnp.where(i == 0, 0, jnp.where(i == nb - 1, 2, 1))
            bias = jnp.concatenate([bias_ref[case]] * Q_PER_KV, axis=0)
        outs = []
        for hk in range(N_KV_HEADS):
            kh = kvall[:, hk * HEAD_DIM:(hk + 1) * HEAD_DIM]
            vh = kvall[:, KV_WIDTH + hk * HEAD_DIM:KV_WIDTH + (hk + 1) * HEAD_DIM]
            qg = jnp.concatenate(
                [q[:, (hk * Q_PER_KV + g) * HEAD_DIM:(hk * Q_PER_KV + g + 1) * HEAD_DIM] for g in range(Q_PER_KV)],
                axis=0)
            s = lax.dot_general(qg, kh, (((1,), (1,)), ((), ())), preferred_element_type=F32)
            if has_local:
                s = jnp.concatenate([s[:, :n_loc] + bias, s[:, n_loc:]], axis=1)
            snk = jnp.zeros((Q_PER_KV * BLOCK, 1), F32)
            for g in range(Q_PER_KV):
                snk = jnp.where(grp == g, sink_ref[hk * Q_PER_KV + g], snk)
            m = jnp.maximum(jnp.max(s, axis=1, keepdims=True), snk)
            p = jnp.exp(s - m).astype(MXU_DTYPE)
            v_ones = jnp.concatenate([vh, jnp.ones_like(vh)], axis=1)
            ov = jnp.dot(p, v_ones, preferred_element_type=F32)
            o = ov[:, :HEAD_DIM] / (ov[:, HEAD_DIM:HEAD_DIM + 1] + jnp.exp(snk - m))
            outs += [o[g * BLOCK:(g + 1) * BLOCK] for g in range(Q_PER_KV)]
        o_ref[0, sb * BLOCK:(sb + 1) * BLOCK, :] = jnp.concatenate(outs, axis=1).astype(o_ref.dtype)


def _local_start(i, n):
    return jnp.clip(i * BLOCK - BLOCK, 0, n - 3 * BLOCK)


def _window_bias(n):
    nb = n // BLOCK
    r = jnp.arange(BLOCK)[:, None]
    c = jnp.arange(3 * BLOCK)[None, :]
    cases = []
    for i in (0, 1, nb - 1):
        diff = (_local_start(i, n) + c) - (i * BLOCK + r)
        cases.append(jnp.where(jnp.abs(diff) <= WINDOW, 0.0, NEG_INF).astype(F32))
    return jnp.stack(cases)


def _attention(sink, q, kv, kvc):
    b, n, _ = q.shape
    lc = kvc.shape[1]
    has_local = kv is not None
    qrows = ATTN_QBLOCKS * BLOCK
    assert n % qrows == 0
    in_specs = [pl.BlockSpec(memory_space=pltpu.SMEM), pl.BlockSpec((1, qrows, ATTN_WIDTH), lambda bi, i: (bi, i, 0))]
    args = [sink, q]
    if has_local:
        in_specs.append(pl.BlockSpec((1, n, 2 * KV_WIDTH), lambda bi, i: (bi, 0, 0)))
        args.append(kv)
    in_specs.append(pl.BlockSpec((1, lc, 2 * KV_WIDTH), lambda bi, i: (bi, 0, 0)))
    args.append(kvc)
    if has_local:
        nb = n // BLOCK
        assert nb >= 4
        in_specs.append(pl.BlockSpec((3, BLOCK, 3 * BLOCK), lambda bi, i: (0, 0, 0)))
        args.append(_window_bias(n))
    return pl.pallas_call(
        functools.partial(_attn_kernel, n=n, has_local=has_local),
        grid=(b, n // qrows),
        in_specs=in_specs,
        out_specs=pl.BlockSpec((1, qrows, ATTN_WIDTH), lambda bi, i: (bi, i, 0)),
        out_shape=jax.ShapeDtypeStruct((b, n, ATTN_WIDTH), MXU_DTYPE),
        compiler_params=_params("arbitrary", "arbitrary"),
        name="attn_local" if has_local else "attn_ctx",
    )(*args)


def _mix_out_kernel(*refs, conv, first, last):
    if conv:
        (att_ref, gb_ref, cu_ref, cup_ref, cun_ref, cw_ref, cb_ref,
         w_ref, x_ref, gate_ref, g2_ref, sh2_ref, sc2_ref, rwh_ref, rwl_ref, xo_ref, h_ref, afft_ref) = refs
        t = pl.program_id(1)
        prev = jnp.where(t == first, 0.0, cup_ref[0])
        nxt = jnp.where(t == last, 0.0, cun_ref[0])
        ext = jnp.concatenate([prev, cu_ref[0], nxt], axis=0)
        cw = cw_ref[...]
    else:
        (y_ref, w_ref, x_ref, gate_ref, g2_ref, sh2_ref, sc2_ref, rwh_ref, rwl_ref, xo_ref, h_ref, afft_ref) = refs
    tile = x_ref.shape[1]
    rows = tile // MIX_OUT_SPLIT
    for s in range(MIX_OUT_SPLIT):
        r0 = s * rows
        rs = slice(r0, r0 + rows)
        if conv:
            cv = (cw[0:1] * ext[HALO - 1 + r0:HALO - 1 + r0 + rows] + cw[1:2] * ext[HALO + r0:HALO + r0 + rows]
                  + cw[2:3] * ext[HALO + 1 + r0:HALO + 1 + r0 + rows] + cb_ref[...])
            cat = jnp.concatenate([att_ref[0, rs, :], (gb_ref[0, rs, :] * cv).astype(MXU_DTYPE)], axis=1)
        else:
            cat = jnp.concatenate([y_ref[0, c, rs, :] for c in range(y_ref.shape[1])], axis=1).astype(MXU_DTYPE)
        y = jnp.dot(cat, w_ref[...], preferred_element_type=F32)
        x = x_ref[0, rs, :] + gate_ref[0] * y
        xo_ref[0, rs, :] = x
        h = _norm_mod(x, g2_ref[...], sh2_ref[0], sc2_ref[0])
        _store_token_tiles(h_ref, h, r0)
        h_hi, h_lo = _split_hi_lo(h)
        logits = jnp.dot(h_hi, rwh_ref[...], preferred_element_type=F32)
        logits += jnp.dot(h_hi, rwl_ref[...], preferred_element_type=F32)
        logits += jnp.dot(h_lo, rwh_ref[...], preferred_element_type=F32)
        lane = lax.broadcasted_iota(jnp.int32, logits.shape, 1)
        logits = jnp.where(lane < N_EXPERTS, logits, NEG_INF)
        e = jnp.exp(logits - jnp.max(logits, axis=1, keepdims=True))
        aff = e / jnp.sum(e, axis=1, keepdims=True)
        afft_ref[0, :, rs] = aff.T[:N_EXPERTS]


def _mix_out(mix_in, conv_params, w, x, gate, g2, sh2, sc2, mod_row, rw_hi, rw_lo, tile):
    b, n, d = x.shape
    nt = n // tile
    conv = conv_params is not None
    row = (lambda bi: bi) if mod_row is None else (lambda bi: mod_row)
    tok = lambda width: pl.BlockSpec((1, tile, width), lambda bi, t: (bi, t, 0))
    modspec = pl.BlockSpec((1, 1, d), lambda bi, t: (row(bi), 0, 0))
    full = lambda a: pl.BlockSpec(a.shape, lambda bi, t: (0,) * a.ndim)
    if conv:
        att, gb, cu = mix_in
        cw, cb = conv_params
        per = tile // HALO
        in_specs = [tok(ATTN_WIDTH), tok(CONV_B_WIDTH), tok(CONV_B_WIDTH),
                    pl.BlockSpec((1, HALO, CONV_B_WIDTH), lambda bi, t: (bi, jnp.maximum(t * per - 1, 0), 0)),
                    pl.BlockSpec((1, HALO, CONV_B_WIDTH), lambda bi, t: (bi, jnp.minimum((t + 1) * per, n // HALO - 1), 0)),
                    full(cw), full(cb)]
        args = [att, gb, cu, cu, cu, cw, cb]
    else:
        in_specs = [_slab_spec(tile, lambda bi, t: (bi, 0, t, 0), d)]
        args = [mix_in]
    in_specs += [full(w), tok(d), modspec, full(g2), modspec, modspec, full(rw_hi), full(rw_lo)]
    args += [w, x, gate, g2, sh2, sc2, rw_hi, rw_lo]
    out_shape = (jax.ShapeDtypeStruct((b, n, d), F32), jax.ShapeDtypeStruct((b, n * SUBLANES, LANES), F32),
                 jax.ShapeDtypeStruct((b, N_EXPERTS, n), F32))
    out_specs = (tok(d), pl.BlockSpec((1, tile * SUBLANES, LANES), lambda bi, t: (bi, t, 0)),
                 pl.BlockSpec((1, N_EXPERTS, tile), lambda bi, t: (bi, 0, t)))
    return pl.pallas_call(
        functools.partial(_mix_out_kernel, conv=conv, first=0, last=nt - 1),
        grid=(b, nt),
        in_specs=in_specs,
        out_specs=out_specs,
        out_shape=out_shape,
        compiler_params=_params("arbitrary", "arbitrary"),
        name="even_out" if conv else "odd_out",
    )(*args)


def _cumsum_lanes(x):
    n = x.shape[1]
    r = lax.broadcasted_iota(jnp.int32, (CUMSUM_TILE, CUMSUM_TILE), 0)
    c = lax.broadcasted_iota(jnp.int32, (CUMSUM_TILE, CUMSUM_TILE), 1)
    tri = jnp.where(r <= c, 1.0, 0.0).astype(MXU_DTYPE)
    carry = jnp.zeros((x.shape[0], 1), F32)
    outs = []
    for k in range(n // CUMSUM_TILE):
        blk = x[:, k * CUMSUM_TILE:(k + 1) * CUMSUM_TILE].astype(MXU_DTYPE)
        loc = jnp.dot(blk, tri, preferred_element_type=F32) + carry
        outs.append(loc)
        carry = loc[:, CUMSUM_TILE - 1:CUMSUM_TILE]
    return jnp.concatenate(outs, axis=1)


SLOT_EMPTY = 1 << 20


def _select_kernel(at_ref, idx_ref, g_ref, *, cap):
    bblk, n_e, n = at_ref.shape
    at = at_ref[...].reshape(bblk * n_e, n)
    n_exp = bblk * n_e
    capf = float(cap)

    def count_ge(thr):
        return jnp.sum(jnp.where(at >= thr, 1.0, 0.0), axis=1, keepdims=True)

    def bit_body(_, c):
        lo_i, hi_i = c
        mid = lo_i + ((hi_i - lo_i) >> 1)
        ge = count_ge(lax.bitcast_convert_type(mid, F32)) >= capf
        return jnp.where(ge, mid, lo_i), jnp.where(ge, hi_i, mid)

    lo_i, hi_i = lax.fori_loop(
        0, 31, bit_body, (jnp.zeros((n_exp, 1), jnp.int32), jnp.full((n_exp, 1), 0x3F800001, jnp.int32)))

    def val_body(_, c):
        lo, hi = c
        mid = 0.5 * (lo + hi)
        ge = count_ge(mid) >= capf
        return jnp.where(ge, mid, lo), jnp.where(ge, hi, mid)

    lo, hi = lax.fori_loop(
        0, 24, val_body, (lax.bitcast_convert_type(lo_i, F32), lax.bitcast_convert_type(hi_i, F32)))
    need = capf - count_ge(hi)

    above = jnp.where(at >= hi, 1.0, 0.0)
    band = jnp.where(at >= lo, 1.0, 0.0) - above
    sel = above + band * jnp.where(_cumsum_lanes(band) <= need, 1.0, 0.0)
    rank = _cumsum_lanes(sel)

    lane = lax.broadcasted_iota(jnp.int32, (n_exp, n), 1)
    disp = jnp.where(sel > 0.5, lane + 1 - rank.astype(jnp.int32), SLOT_EMPTY)
    g = at
    for k in range(n.bit_length() - 1):
        step = 1 << k
        moving = ((disp >> k) & 1) == 1
        disp_in = pltpu.roll(disp, n - step, 1)
        arriving = ((disp_in >> k) & 1) == 1
        g = jnp.where(arriving, pltpu.roll(g, n - step, 1), g)
        disp = jnp.where(arriving, disp_in, jnp.where(moving, SLOT_EMPTY, disp))

    capp = pl.cdiv(cap, LANES) * LANES
    slot = lax.broadcasted_iota(jnp.int32, (n_exp, cap), 1)
    idx_ref[...] = jnp.clip(slot + disp[:, :cap], 0, n - 1).reshape(bblk, n_e, cap)
    for s in range(bblk):
        g_pad = jnp.concatenate([g[s * n_e:(s + 1) * n_e, :capp], jnp.zeros((LANES - n_e, capp), F32)], axis=0)
        g_ref[s] = g_pad.T[:cap]


def _select(aff_t, cap):
    b, n_exp, n = aff_t.shape
    assert n & (n - 1) == 0 and n % CUMSUM_TILE == 0 and n < SLOT_EMPTY
    bblk = b
    return pl.pallas_call(
        functools.partial(_select_kernel, cap=cap),
        grid=(b // bblk,),
        in_specs=[pl.BlockSpec((bblk, n_exp, n), lambda bi: (bi, 0, 0))],
        out_specs=(pl.BlockSpec((bblk, n_exp, cap), lambda bi: (bi, 0, 0)),
                   pl.BlockSpec((bblk, cap, LANES), lambda bi: (bi, 0, 0))),
        out_shape=(jax.ShapeDtypeStruct((b, n_exp, cap), jnp.int32), jax.ShapeDtypeStruct((b, cap, LANES), F32)),
        compiler_params=_params("arbitrary"),
        name="moe_select",
    )(aff_t)


def _gather_kernel(idx_ref, src_ref, xe_ref, xcm_ref, *, cap, n_exp):
    bblk = src_ref.shape[0]
    eblk, out_rows = xe_ref.shape[1], xe_ref.shape[2]
    stride = out_rows + SUBLANES
    for k in range(eblk):
        for s in range(bblk):
            base = ((pl.program_id(0) * bblk + s) * n_exp + pl.program_id(1) * eblk + k) * cap
            for r in range(cap):
                t = idx_ref[base + r]
                xcm_ref[k, pl.ds(s * cap + r, SUBLANES, stride=stride), :] = (
                    src_ref[s, pl.ds(pl.multiple_of(t * SUBLANES, SUBLANES), SUBLANES), :])
        used = bblk * cap
        if used < out_rows:
            for j in range(SUBLANES):
                xcm_ref[k, pl.ds(j * stride + used, out_rows - used), :] = jnp.zeros((out_rows - used, LANES), F32)
        xe_ref[0, k] = jnp.concatenate(
            [xcm_ref[k, pl.ds(j * stride, out_rows), :] for j in range(SUBLANES)], axis=1).astype(xe_ref.dtype)


def _gather(idx_flat, src, cap, bblk, out_rows):
    b, rows, _ = src.shape
    d = SUBLANES * LANES
    assert cap % SUBLANES == 0 and bblk * cap <= out_rows and b % bblk == 0 and N_EXPERTS % DISPATCH_EXPERTS == 0
    return pl.pallas_call(
        functools.partial(_gather_kernel, cap=cap, n_exp=N_EXPERTS),
        grid=(b // bblk, N_EXPERTS // DISPATCH_EXPERTS),
        in_specs=[pl.BlockSpec(memory_space=pltpu.SMEM),
                  pl.BlockSpec((bblk, rows, LANES), lambda bi, ei: (bi, 0, 0))],
        out_specs=pl.BlockSpec((1, DISPATCH_EXPERTS, out_rows, d), lambda bi, ei: (bi, ei, 0, 0)),
        out_shape=jax.ShapeDtypeStruct((b // bblk, N_EXPERTS, out_rows, d), MXU_DTYPE),
        scratch_shapes=[pltpu.VMEM((DISPATCH_EXPERTS, SUBLANES * (out_rows + SUBLANES), LANES), F32)],
        compiler_params=_params("arbitrary", "arbitrary"),
        name="moe_gather",
    )(idx_flat, src)


WEIGHT_CHUNKS = 2
WEIGHT_DMA_PRIORITY = 1


def _moe_ffn_kernel(*refs, layer, n_steps, has_ctx):
    if has_ctx:
        (x_ref, g_ref, xc_ref, gc_ref, wg_hbm, wu_hbm, wd_hbm, o_ref, oc_ref,
         wgu_s, wd_s, stage_in, stage_out, sem) = refs
    else:
        (x_ref, g_ref, wg_hbm, wu_hbm, wd_hbm, o_ref, wgu_s, wd_s, stage_in, stage_out, sem) = refs
    n_exp = pl.num_programs(0)
    e = pl.program_id(0)
    s = pl.program_id(1)
    slot = e % 2
    ff = wg_hbm.shape[3]
    chunks = []
    for hbm, dst, col0, stage in ((wg_hbm, wgu_s, 0, stage_in), (wu_hbm, wgu_s, ff, stage_in),
                                  (wd_hbm, wd_s, 0, stage_out)):
        n_rows = hbm.shape[2] // WEIGHT_CHUNKS
        chunks += [(hbm, dst, col0, stage, c * n_rows, n_rows) for c in range(WEIGHT_CHUNKS)]
    per_step = pl.cdiv(len(chunks), n_steps)

    def copy(c, expert):
        hbm, _, _, stage, r0, n_rows = chunks[c]
        return pltpu.make_async_copy(hbm.at[layer, expert, pl.ds(r0, n_rows), :], stage, sem.at[0])

    def cast(c, to_slot):
        hbm, dst, col0, stage, r0, n_rows = chunks[c]
        dst[to_slot, pl.ds(r0, n_rows), pl.ds(col0, hbm.shape[3])] = stage[...].astype(dst.dtype)

    @pl.when((e == 0) & (s == 0))
    def _():
        for c in range(len(chunks)):
            copy(c, 0).start()
            copy(c, 0).wait()
            cast(c, 0)

    has_next = e + 1 < n_exp
    for c in range(len(chunks)):
        if c % per_step == 0:
            @pl.when(has_next & (s == c // per_step))
            def _():
                copy(c, e + 1).start(priority=WEIGHT_DMA_PRIORITY)

    def ffn(xr, gr, outr):
        x, aff = xr[0, 0], gr[0]
        rows = x.shape[0]
        au = jnp.dot(x, wgu_s[slot], preferred_element_type=F32)
        a, u = au[:, :ff], au[:, ff:]
        mid = (a * _sigmoid(a) * u).astype(MXU_DTYPE)
        y = jnp.dot(mid, wd_s[slot], preferred_element_type=F32)
        lane = lax.broadcasted_iota(jnp.int32, aff.shape, 1)
        y = y * jnp.sum(jnp.where(lane == e, aff, 0.0), axis=1, keepdims=True)
        stride = rows + SUBLANES
        for j in range(SUBLANES):
            outr[0, 0, pl.ds(j * stride, rows), :] = y[:, j * LANES:(j + 1) * LANES]
            outr[0, 0, pl.ds(j * stride + rows, SUBLANES), :] = jnp.zeros((SUBLANES, LANES), F32)

    if has_ctx:
        is_ctx = s == n_steps - 1
        pl.when(is_ctx)(lambda: ffn(xc_ref, gc_ref, oc_ref))
        pl.when(jnp.logical_not(is_ctx))(lambda: ffn(x_ref, g_ref, o_ref))
    else:
        pl.when(s < n_steps)(lambda: ffn(x_ref, g_ref, o_ref))

    for c in range(len(chunks)):
        @pl.when(has_next & (s == c // per_step))
        def _():
            copy(c, e + 1).wait()
            cast(c, 1 - slot)
            if (c + 1) % per_step != 0 and c + 1 < len(chunks):
                copy(c + 1, e + 1).start(priority=WEIGHT_DMA_PRIORITY)


def _moe_ffn(xe, ge, ctx_block, layer, wg, wu, wd):
    nb, e, rows, d = xe.shape
    ff = wg.shape[-1]
    assert d % WEIGHT_CHUNKS == 0 and ff % (WEIGHT_CHUNKS * SUBLANES) == 0 and ff % LANES == 0
    has_ctx = ctx_block is not None
    n_steps = nb + int(has_ctx)
    hbm = pl.BlockSpec(memory_space=pl.ANY)
    out_rows = SUBLANES * (rows + SUBLANES)
    blk = lambda bi: jnp.minimum(bi, nb - 1)
    in_specs = [pl.BlockSpec((1, 1, rows, d), lambda ei, bi: (blk(bi), ei, 0, 0)),
                pl.BlockSpec((1, rows, LANES), lambda ei, bi: (blk(bi), 0, 0))]
    args = [xe, ge]
    out_specs = [pl.BlockSpec((1, 1, out_rows, LANES), lambda ei, bi: (blk(bi), ei, 0, 0))]
    out_shape = [jax.ShapeDtypeStruct((nb, e, out_rows, LANES), F32)]
    if has_ctx:
        rows_c = ctx_block[0].shape[2]
        assert ctx_block[0].shape == (1, e, rows_c, d) and ctx_block[1].shape == (1, rows_c, LANES)
        out_rows_c = SUBLANES * (rows_c + SUBLANES)
        in_specs += [pl.BlockSpec((1, 1, rows_c, d), lambda ei, bi: (0, ei, 0, 0)),
                     pl.BlockSpec((1, rows_c, LANES), lambda ei, bi: (0, 0, 0))]
        args += list(ctx_block)
        out_specs.append(pl.BlockSpec((1, 1, out_rows_c, LANES), lambda ei, bi: (0, ei, 0, 0)))
        out_shape.append(jax.ShapeDtypeStruct((1, e, out_rows_c, LANES), F32))
    return pl.pallas_call(
        functools.partial(_moe_ffn_kernel, layer=layer, n_steps=n_steps, has_ctx=has_ctx),
        grid=(e, n_steps),
        in_specs=in_specs + [hbm, hbm, hbm],
        out_specs=tuple(out_specs),
        out_shape=tuple(out_shape),
        scratch_shapes=[pltpu.VMEM((2, d, 2 * ff), MXU_DTYPE), pltpu.VMEM((2, ff, d), MXU_DTYPE),
                        pltpu.VMEM((d // WEIGHT_CHUNKS, ff), F32), pltpu.VMEM((ff // WEIGHT_CHUNKS, d), F32),
                        pltpu.SemaphoreType.DMA((1,))],
        compiler_params=_params("arbitrary", "arbitrary"),
        name="moe_ffn",
    )(*args, wg, wu, wd)


COMBINE_UNROLL = 8


def _combine_kernel(idx_ref, y_ref, acc_ref, *, cap, n_exp):
    bblk = acc_ref.shape[0]
    eblk = y_ref.shape[1]
    ei = pl.program_id(1)
    stride = y_ref.shape[2] // SUBLANES

    @pl.when(ei == 0)
    def _():
        acc_ref[...] = jnp.zeros_like(acc_ref)

    for k in range(eblk):
        for s in range(bblk):
            base = ((pl.program_id(0) * bblk + s) * n_exp + ei * eblk + k) * cap
            for r0 in range(0, cap, COMBINE_UNROLL):
                toks = [pl.multiple_of(idx_ref[base + r0 + i] * SUBLANES, SUBLANES) for i in range(COMBINE_UNROLL)]
                new = [acc_ref[s, pl.ds(toks[i], SUBLANES), :]
                       + y_ref[0, k, pl.ds(s * cap + r0 + i, SUBLANES, stride=stride), :]
                       for i in range(COMBINE_UNROLL)]
                for i in range(COMBINE_UNROLL):
                    acc_ref[s, pl.ds(toks[i], SUBLANES), :] = new[i]


def _combine(idx_flat, y, b, n, cap, bblk):
    nb, n_exp, rows, _ = y.shape
    assert cap % COMBINE_UNROLL == 0 and nb * bblk == b and bblk * cap <= rows // SUBLANES - SUBLANES
    return pl.pallas_call(
        functools.partial(_combine_kernel, cap=cap, n_exp=n_exp),
        grid=(nb, n_exp // DISPATCH_EXPERTS),
        in_specs=[pl.BlockSpec(memory_space=pltpu.SMEM),
                  pl.BlockSpec((1, DISPATCH_EXPERTS, rows, LANES), lambda bi, ei: (bi, ei, 0, 0))],
        out_specs=pl.BlockSpec((bblk, n * SUBLANES, LANES), lambda bi, ei: (bi, 0, 0)),
        out_shape=jax.ShapeDtypeStruct((b, n * SUBLANES, LANES), F32),
        compiler_params=_params("arbitrary", "arbitrary"),
        name="moe_combine",
    )(idx_flat, y)


def _expert_choice(h_lat, aff_lat, h_ctx, aff_ctx, layer, wg, wu, wd):
    b, _, n = aff_lat.shape
    cap = max(1, CAPACITY_FACTOR * n // N_EXPERTS)
    idx, g = _select(aff_lat, cap)
    idx = idx.reshape(-1)
    xe = _gather(idx, h_lat, cap, 1, cap)
    if h_ctx is None:
        (y,) = _moe_ffn(xe, g, None, layer, wg, wu, wd)
        return _combine(idx, y, b, n, cap, 1), None
    lc = aff_ctx.shape[2]
    cap_c = max(1, CAPACITY_FACTOR * lc // N_EXPERTS)
    idx_c, g_c = _select(aff_ctx, cap_c)
    idx_c = idx_c.reshape(-1)
    xe_c = _gather(idx_c, h_ctx, cap_c, b, b * cap_c)
    g_c = g_c.reshape(1, b * cap_c, LANES)
    y, y_c = _moe_ffn(xe, g, (xe_c, g_c), layer, wg, wu, wd)
    return _combine(idx, y, b, n, cap, 1), _combine(idx_c, y_c, b, lc, cap_c, b)


def _odd_in_kernel(x_ref, moe_ref, gate_ref, g_ref, sh_ref, sc_ref, w_ref, xo_ref, gg_ref, u_ref):
    x = x_ref[0] + gate_ref[0] * _load_token_tiles(moe_ref)
    xo_ref[0] = x
    h = _norm_mod(x, g_ref[...], sh_ref[0], sc_ref[0]).astype(MXU_DTYPE)
    y = jnp.dot(h, w_ref[...], preferred_element_type=F32)
    gl = y[:, :LRU_WIDTH]
    _store_slabs(gg_ref, 0.5 * gl * (1.0 + jnp.tanh(0.7978845608028654 * (gl + 0.044715 * gl * gl * gl))))
    _store_slabs(u_ref, y[:, LRU_WIDTH:])


def _odd_in(x, moe, gate, g, shift, scale, mod_row, w, tile):
    b, n, d = x.shape
    row = (lambda bi: bi) if mod_row is None else (lambda bi: mod_row)
    tok = pl.BlockSpec((1, tile, d), lambda bi, t: (bi, t, 0))
    tiles = pl.BlockSpec((1, tile * SUBLANES, LANES), lambda bi, t: (bi, t, 0))
    modspec = pl.BlockSpec((1, 1, d), lambda bi, t: (row(bi), 0, 0))
    return pl.pallas_call(
        _odd_in_kernel,
        grid=(b, n // tile),
        in_specs=[tok, tiles, modspec, pl.BlockSpec((1, d), lambda bi, t: (0, 0)), modspec, modspec,
                  pl.BlockSpec(w.shape, lambda bi, t: (0, 0))],
        out_specs=(tok, _slab_spec(tile, lambda bi, t: (bi, 0, t, 0)), _slab_spec(tile, lambda bi, t: (bi, 0, t, 0))),
        out_shape=(jax.ShapeDtypeStruct((b, n, d), F32),) + (jax.ShapeDtypeStruct((b, d // LANES, n, LANES), F32),) * 2,
        compiler_params=_params("arbitrary", "arbitrary"),
        name="odd_in",
    )(x, moe, gate, g, shift, scale, w)


def _lru_kernel(*refs, reverse, nt):
    if reverse:
        (u_ref, up_ref, un_ref, cw_ref, cb_ref, wax_ref, ba_ref, bx_ref, lam_ref, h0_ref, hf_ref, gg_ref,
         out_ref, hlast_ref, a_scr, b_scr, uc_scr, carry_scr) = refs
    else:
        (u_ref, up_ref, un_ref, cw_ref, cb_ref, wax_ref, ba_ref, bx_ref, lam_ref, h0_ref,
         out_ref, hlast_ref, a_scr, b_scr, uc_scr, carry_scr) = refs
    n_slab, tile = u_ref.shape[1], u_ref.shape[2]
    w = n_slab * LANES
    per = tile // SUBLANES
    phase = lambda ref, j: jnp.concatenate(
        [ref[0, c, pl.ds(j, per, stride=SUBLANES), :] for c in range(n_slab)], axis=1)
    rows = lambda j: slice(j * per, (j + 1) * per)
    t = pl.program_id(1)
    pos = (nt - 1 - t) if reverse else t
    prev = jnp.where(pos == 0, 0.0, _load_slabs(up_ref))
    nxt = jnp.where(pos == nt - 1, 0.0, _load_slabs(un_ref))

    rowid = lax.broadcasted_iota(jnp.int32, (per, w), 0)
    shift_down = lambda x, first: jnp.where(rowid == 0, first, pltpu.roll(x, 1, 0))
    shift_up = lambda x, last: jnp.where(rowid == per - 1, last, pltpu.roll(x, per - 1, 0))
    u = [phase(u_ref, j) for j in range(SUBLANES)]
    um1 = [shift_down(u[7], prev[7:8])] + u[:7]
    um2 = [shift_down(u[6], prev[6:7]), um1[0]] + u[:6]
    up1 = u[1:] + [shift_up(u[0], nxt[0:1])]
    cw = cw_ref[...]
    for j in range(SUBLANES):
        uc_scr[rows(j), :] = (cb_ref[...] + cw[0:1] * um2[j] + cw[1:2] * um1[j] + cw[2:3] * u[j] + cw[3:4] * up1[j])

    lam = lam_ref[0]
    half_decay = (0.5 * LRU_C) * (jnp.maximum(-lam, 0.0) + jnp.log1p(jnp.exp(-jnp.abs(lam))))
    for hd in range(LRU_HEADS):
        sl = slice(hd * LRU_BLOCK, (hd + 1) * LRU_BLOCK)
        uc = uc_scr[:, sl]
        z = jnp.dot(uc.astype(MXU_DTYPE), wax_ref[0, hd], preferred_element_type=F32)
        hd_row = half_decay[:, sl]
        neg_log_a = hd_row * jnp.tanh(z[:, :LRU_BLOCK] + ba_ref[0][:, sl]) + hd_row
        gate2 = 1.0 + jnp.tanh(z[:, LRU_BLOCK:] + bx_ref[0][:, sl])
        a = jnp.exp(-neg_log_a)
        m2 = jnp.tanh(neg_log_a) * (a * a + 1.0)
        mult = jnp.where(m2 > 0.0, m2 * lax.rsqrt(m2), 0.0)
        a_scr[:, sl] = a
        b_scr[:, sl] = (0.5 * mult) * (gate2 * uc)

    @pl.when(t == 0)
    def _():
        carry_scr[...] = jnp.broadcast_to(h0_ref[0], carry_scr.shape)

    order = list(range(SUBLANES))[::-1] if reverse else list(range(SUBLANES))
    hrun = b_scr[rows(order[0]), :]
    prun = a_scr[rows(order[0]), :]
    for j in order[1:]:
        aj = a_scr[rows(j), :]
        hrun = aj * hrun + b_scr[rows(j), :]
        prun = aj * prun
        b_scr[rows(j), :] = hrun
        a_scr[rows(j), :] = prun

    lane_row = lax.broadcasted_iota(jnp.int32, (SUBLANES, w), 0)
    carry = carry_scr[...]
    groups = list(range(per // SUBLANES))
    entering = [None] * len(groups)
    for m in (groups[::-1] if reverse else groups):
        a = prun[m * SUBLANES:(m + 1) * SUBLANES]
        bcoef = hrun[m * SUBLANES:(m + 1) * SUBLANES]
        for dist in (1, 2, 4):
            shift = (SUBLANES - dist) if reverse else dist
            msk = (lane_row < SUBLANES - dist) if reverse else (lane_row >= dist)
            a_s = pltpu.roll(a, shift, 0)
            b_s = pltpu.roll(bcoef, shift, 0)
            bcoef = jnp.where(msk, a * b_s + bcoef, bcoef)
            a = jnp.where(msk, a * a_s, a)
        after = a * carry + bcoef
        if reverse:
            entering[m] = jnp.where(lane_row == SUBLANES - 1, carry, pltpu.roll(after, SUBLANES - 1, 0))
            carry = jnp.broadcast_to(after[0:1], carry.shape)
        else:
            entering[m] = jnp.where(lane_row == 0, carry, pltpu.roll(after, 1, 0))
            carry = jnp.broadcast_to(after[SUBLANES - 1:SUBLANES], carry.shape)
    carry_scr[...] = carry
    hlast_ref[0] = carry[0:1]
    h_in = jnp.concatenate(entering, axis=0)

    for j in range(SUBLANES):
        hcur = b_scr[rows(j), :] + a_scr[rows(j), :] * h_in
        if reverse:
            hcur = phase(gg_ref, j) * (phase(hf_ref, j) + hcur)
        for c in range(n_slab):
            out_ref[0, c, pl.ds(j, per, stride=SUBLANES), :] = hcur[:, c * LANES:(c + 1) * LANES]


def _lru(u, cw, cb, wax, ba, bx, lam, h0, direction, tile, hf=None, gg=None):
    b, n_slab, n, _ = u.shape
    w = n_slab * LANES
    nt = n // tile
    per = tile // HALO
    reverse = direction == 1
    pos = (lambda t: nt - 1 - t) if reverse else (lambda t: t)
    tok = _slab_spec(tile, lambda bi, t: (bi, 0, pos(t), 0), w)
    rowspec = pl.BlockSpec((1, 1, w), lambda bi, t: (direction, 0, 0))
    in_specs = [
        tok,
        _slab_spec(HALO, lambda bi, t: (bi, 0, jnp.maximum(pos(t) * per - 1, 0), 0), w),
        _slab_spec(HALO, lambda bi, t: (bi, 0, jnp.minimum((pos(t) + 1) * per, n // HALO - 1), 0), w),
        pl.BlockSpec(cw.shape, lambda bi, t: (0, 0)),
        pl.BlockSpec(cb.shape, lambda bi, t: (0, 0)),
        pl.BlockSpec((1,) + wax.shape[1:], lambda bi, t: (direction, 0, 0, 0)),
        rowspec, rowspec, rowspec,
        pl.BlockSpec((1, 1, w), lambda bi, t: (bi, 0, 0)),
    ]
    args = [u, u, u, cw, cb, wax, ba, bx, lam, h0]
    if reverse:
        in_specs += [tok, tok]
        args += [hf, gg]
    assert tile % (SUBLANES * SUBLANES) == 0
    return pl.pallas_call(
        functools.partial(_lru_kernel, reverse=reverse, nt=nt),
        grid=(b, nt),
        in_specs=in_specs,
        out_specs=(tok, pl.BlockSpec((1, 1, w), lambda bi, t: (bi, 0, 0))),
        out_shape=(jax.ShapeDtypeStruct(u.shape, F32), jax.ShapeDtypeStruct((b, 1, w), F32)),
        scratch_shapes=[pltpu.VMEM((tile, w), F32), pltpu.VMEM((tile, w), F32), pltpu.VMEM((tile, w), F32),
                        pltpu.VMEM((SUBLANES, w), F32)],
        compiler_params=_params("arbitrary", "arbitrary"),
        name="lru_bwd" if reverse else "lru_fwd",
    )(*args)


def _final_kernel(x_ref, moe_ref, gate_ref, g_ref, o_ref):
    x = x_ref[0] + gate_ref[0] * _load_token_tiles(moe_ref)
    o_ref[0] = x * lax.rsqrt(jnp.mean(x * x, axis=-1, keepdims=True) + EPS) * g_ref[...]


def _final(x, moe, gate, g, tile):
    b, n, d = x.shape
    tok = pl.BlockSpec((1, tile, d), lambda bi, t: (bi, t, 0))
    tiles = pl.BlockSpec((1, tile * SUBLANES, LANES), lambda bi, t: (bi, t, 0))
    return pl.pallas_call(
        _final_kernel,
        grid=(b, n // tile),
        in_specs=[tok, tiles, pl.BlockSpec((1, 1, d), lambda bi, t: (bi, 0, 0)), pl.BlockSpec((1, d), lambda bi, t: (0, 0))],
        out_specs=tok,
        out_shape=jax.ShapeDtypeStruct((b, n, d), F32),
        compiler_params=_params("arbitrary", "arbitrary"),
        name="final_norm",
    )(x, moe, gate, g)


def kernel(x, c, ctx, c_ctx, ada_w, ada_b, norm_mix_g, norm_ffn_g, ev_w_in, ev_w_out, ev_sink, ev_conv_w, ev_conv_b, od_w_in, od_w_out, od_conv_w, od_conv_b, od_wa, od_ba, od_wx, od_bx, od_lambda, router_w, w_gate, w_up, w_down, final_g):
    b, n, d = x.shape
    lc = ctx.shape[1]
    depth = ada_w.shape[0]
    assert depth == 2 and d == D_MODEL and b < MOD_ROWS
    tile_l = min(512, n)
    tile_c = lc
    ctx_row = b

    cvec = jnp.zeros((MOD_ROWS, d), F32).at[:b].set(c).at[b].set(c_ctx)
    mods = _ada(cvec, ada_w, ada_b).reshape(depth, MOD_ROWS, 6, 1, d)
    mod = lambda l, j: mods[l, :, j]

    def router_split(l):
        return _split_hi_lo(jnp.pad(router_w[l], ((0, 0), (0, LANES - N_EXPERTS))))

    bf = lambda a: a.astype(MXU_DTYPE)

    g_mix = norm_mix_g[0].reshape(1, d)
    g_ffn = norm_ffn_g[0].reshape(1, d)
    w_in = bf(ev_w_in[0])
    w_out = bf(ev_w_out[0])
    conv_p = (ev_conv_w[0], ev_conv_b[0].reshape(1, -1))
    tables = _rope_tables(n)
    ql, kvl, gbl, cul = _even_in(x, g_mix, mod(0, 0), mod(0, 1), None, w_in, tables, tile_l)
    qc, kvc, gbc, cuc = _even_in(ctx, g_mix, mod(0, 0), mod(0, 1), ctx_row, w_in, None, tile_c)
    att_l = _attention(ev_sink[0], ql, kvl, kvc)
    att_c = _attention(ev_sink[0], qc, None, kvc)
    rw_hi, rw_lo = router_split(0)
    xl, hl, at_l = _mix_out((att_l, gbl, cul), conv_p, w_out, x, mod(0, 2), g_ffn, mod(0, 3), mod(0, 4), None,
                            rw_hi, rw_lo, tile_l)
    xc, hc, at_c = _mix_out((att_c, gbc, cuc), conv_p, w_out, ctx, mod(0, 2), g_ffn, mod(0, 3), mod(0, 4),
                            ctx_row, rw_hi, rw_lo, tile_c)
    moe_l, moe_c = _expert_choice(hl, at_l, hc, at_c, 0, w_gate, w_up, w_down)

    g_mix = norm_mix_g[1].reshape(1, d)
    g_ffn = norm_ffn_g[1].reshape(1, d)
    w_in = bf(od_w_in[0])
    w_out = bf(od_w_out[0])
    xl, ggl, ul = _odd_in(xl, moe_l, mod(0, 5), g_mix, mod(1, 0), mod(1, 1), None, w_in, tile_l)
    _, _, uc = _odd_in(xc, moe_c, mod(0, 5), g_mix, mod(1, 0), mod(1, 1), ctx_row, w_in, tile_c)
    cw, cb = od_conv_w[0], od_conv_b[0].reshape(1, -1)
    wax = bf(0.5 * jnp.concatenate([od_wa[0], od_wx[0]], axis=-1))
    ba, bx = (0.5 * a[0].reshape(2, 1, -1) for a in (od_ba, od_bx))
    lam = od_lambda[0].reshape(2, 1, -1)
    zero_state = jnp.zeros((b, 1, LRU_WIDTH), F32)
    hf_c, h0_f = _lru(uc, cw, cb, wax, ba, bx, lam, zero_state, 0, tile_c)
    _, h0_b = _lru(uc, cw, cb, wax, ba, bx, lam, zero_state, 1, tile_c, hf=hf_c, gg=hf_c)
    hf_l, _ = _lru(ul, cw, cb, wax, ba, bx, lam, h0_f, 0, tile_l)
    yl, _ = _lru(ul, cw, cb, wax, ba, bx, lam, h0_b, 1, tile_l, hf=hf_l, gg=ggl)
    rw_hi, rw_lo = router_split(1)
    xl, hl, at_l = _mix_out(yl, None, w_out, xl, mod(1, 2), g_ffn, mod(1, 3), mod(1, 4), None,
                            rw_hi, rw_lo, tile_l)
    moe_l, _ = _expert_choice(hl, at_l, None, None, 1, w_gate, w_up, w_down)
    return _final(xl, moe_l, mod(1, 5), final_g.reshape(1, d), tile_l)
```

```python
import functools

import jax
import jax.numpy as jnp
from jax import lax
from jax.experimental import pallas as pl
from jax.experimental.pallas import tpu as pltpu

F32 = jnp.float32
MXU_DTYPE = jnp.bfloat16

D_MODEL = 1024
GRID_W = 64
EPS = 1e-6
NEG_INF = -1e30
HEAD_DIM = 64
N_Q_HEADS = 8
N_KV_HEADS = 2
Q_PER_KV = N_Q_HEADS // N_KV_HEADS
ATTN_WIDTH = N_Q_HEADS * HEAD_DIM
KV_WIDTH = N_KV_HEADS * HEAD_DIM
WINDOW = 128
BLOCK = 128
ROPE_BASE = 10000.0
CONV_B_WIDTH = D_MODEL // 2
EVEN_IN = ATTN_WIDTH + 2 * KV_WIDTH + 3 * CONV_B_WIDTH
LRU_WIDTH = D_MODEL
LRU_HEADS = 8
LRU_BLOCK = LRU_WIDTH // LRU_HEADS
LRU_C = 8.0
N_EXPERTS = 16
CAPACITY_FACTOR = 2
EXPERT_FF = 1408
MOD_ROWS = 16
LANES = 128
SUBLANES = 8
HALO = SUBLANES
CUMSUM_TILE = 256
MIX_OUT_SPLIT = 2
ATTN_QBLOCKS = 4
DISPATCH_EXPERTS = 4
V7X_VMEM_LIMIT = 56 * 1024 * 1024


def _params(*sem):
    return pltpu.CompilerParams(dimension_semantics=sem, vmem_limit_bytes=V7X_VMEM_LIMIT)


def _split_hi_lo(a):
    hi = a.astype(MXU_DTYPE)
    lo = (a - hi.astype(F32)).astype(MXU_DTYPE)
    return hi, lo


def _sigmoid(z):
    return 0.5 * (1.0 + jnp.tanh(0.5 * z))


def _store_token_tiles(ref, val, row0=0):
    tile = val.shape[0]
    for j in range(SUBLANES):
        ref[0, pl.ds(row0 * SUBLANES + j, tile, stride=SUBLANES), :] = val[:, j * LANES:(j + 1) * LANES]


def _load_token_tiles(ref):
    tile = ref.shape[1] // SUBLANES
    return jnp.concatenate([ref[0, pl.ds(j, tile, stride=SUBLANES), :] for j in range(SUBLANES)], axis=1)


def _slab_spec(rows, index_map, width=LRU_WIDTH):
    return pl.BlockSpec((1, width // LANES, rows, LANES), index_map)


def _store_slabs(ref, val):
    for c in range(val.shape[1] // LANES):
        ref[0, c] = val[:, c * LANES:(c + 1) * LANES]


def _load_slabs(ref):
    return jnp.concatenate([ref[0, c] for c in range(ref.shape[1])], axis=1)


def _norm_mod(x, g, shift, scale):
    y = x * lax.rsqrt(jnp.mean(x * x, axis=-1, keepdims=True) + EPS) * g
    return y * (1.0 + scale) + shift


def _ada_kernel(c_ref, w_ref, b_ref, o_ref):
    c = c_ref[...]
    s_hi, s_lo = _split_hi_lo(c * _sigmoid(c))
    w_hi, w_lo = _split_hi_lo(w_ref[0])
    acc = jnp.dot(s_hi, w_hi, preferred_element_type=F32)
    acc += jnp.dot(s_hi, w_lo, preferred_element_type=F32)
    acc += jnp.dot(s_lo, w_hi, preferred_element_type=F32)
    o_ref[0] = acc + b_ref[0]


def _ada(cvec, ada_w, ada_b):
    depth, d, n6 = ada_w.shape
    tn = 1536
    return pl.pallas_call(
        _ada_kernel,
        grid=(depth, n6 // tn),
        in_specs=[
            pl.BlockSpec((MOD_ROWS, d), lambda l, j: (0, 0)),
            pl.BlockSpec((1, d, tn), lambda l, j: (l, 0, j)),
            pl.BlockSpec((1, 1, tn), lambda l, j: (l, 0, j)),
        ],
        out_specs=pl.BlockSpec((1, MOD_ROWS, tn), lambda l, j: (l, 0, j)),
        out_shape=jax.ShapeDtypeStruct((depth, MOD_ROWS, n6), F32),
        compiler_params=_params("arbitrary", "arbitrary"),
        name="ada",
    )(cvec, ada_w, ada_b.reshape(depth, 1, n6))


def _even_in_kernel(*refs, rope):
    if rope:
        (x_ref, g_ref, sh_ref, sc_ref, w_ref, cos_ref, sa_ref, sb_ref, q_ref, kv_ref, gb_ref, cu_ref) = refs
    else:
        (x_ref, g_ref, sh_ref, sc_ref, w_ref, q_ref, kv_ref, gb_ref, cu_ref) = refs
    h = _norm_mod(x_ref[0], g_ref[...], sh_ref[0], sc_ref[0]).astype(MXU_DTYPE)
    y = jnp.dot(h, w_ref[...], preferred_element_type=F32)
    q = y[:, :ATTN_WIDTH]
    k = y[:, ATTN_WIDTH:ATTN_WIDTH + KV_WIDTH]
    v = y[:, ATTN_WIDTH + KV_WIDTH:ATTN_WIDTH + 2 * KV_WIDTH]
    c0 = ATTN_WIDTH + 2 * KV_WIDTH
    if rope:
        cos, sa, sb = cos_ref[...], sa_ref[...], sb_ref[...]

        def rot(z):
            return z * cos + pltpu.roll(z, 16, 1) * sa + pltpu.roll(z, 112, 1) * sb

        q = jnp.concatenate([rot(q[:, j * 128:(j + 1) * 128]) for j in range(ATTN_WIDTH // 128)], axis=1)
        k = rot(k)
    q_ref[0] = (q * (HEAD_DIM ** -0.5)).astype(q_ref.dtype)
    kv_ref[0] = jnp.concatenate([k, v], axis=1).astype(kv_ref.dtype)
    gb_ref[0] = y[:, c0:c0 + CONV_B_WIDTH]
    cu_ref[0] = y[:, c0 + CONV_B_WIDTH:c0 + 2 * CONV_B_WIDTH] * y[:, c0 + 2 * CONV_B_WIDTH:]


def _even_in(x, g, shift, scale, mod_row, w, tables, tile):
    b, n, d = x.shape
    nt = n // tile
    rope = tables is not None
    row = (lambda bi: bi) if mod_row is None else (lambda bi: mod_row)
    in_specs = [
        pl.BlockSpec((1, tile, d), lambda bi, t: (bi, t, 0)),
        pl.BlockSpec((1, d), lambda bi, t: (0, 0)),
        pl.BlockSpec((1, 1, d), lambda bi, t: (row(bi), 0, 0)),
        pl.BlockSpec((1, 1, d), lambda bi, t: (row(bi), 0, 0)),
        pl.BlockSpec(w.shape, lambda bi, t: (0, 0)),
    ]
    args = [x, g, shift, scale, w]
    if rope:
        in_specs += [pl.BlockSpec((tile, 128), lambda bi, t: (t, 0))] * 3
        args += list(tables)
    out_shape = (
        jax.ShapeDtypeStruct((b, n, ATTN_WIDTH), MXU_DTYPE),
        jax.ShapeDtypeStruct((b, n, 2 * KV_WIDTH), MXU_DTYPE),
        jax.ShapeDtypeStruct((b, n, CONV_B_WIDTH), F32),
        jax.ShapeDtypeStruct((b, n, CONV_B_WIDTH), F32),
    )
    out_specs = tuple(pl.BlockSpec((1, tile, s.shape[-1]), lambda bi, t: (bi, t, 0)) for s in out_shape)
    return pl.pallas_call(
        functools.partial(_even_in_kernel, rope=rope),
        grid=(b, nt),
        in_specs=in_specs,
        out_specs=out_specs,
        out_shape=out_shape,
        compiler_params=_params("arbitrary", "arbitrary"),
        name="even_in_rope" if rope else "even_in",
    )(*args)


def _rope_tables(n):
    nf = HEAD_DIM // 4
    pos = jnp.arange(n)
    rows = (pos // GRID_W).astype(F32)
    cols = (pos % GRID_W).astype(F32)
    lane = jnp.arange(128)
    inv = ROPE_BASE ** (-(lane % nf).astype(F32) / nf)
    use_col = (lane % HEAD_DIM) >= HEAD_DIM // 2
    ang = jnp.where(use_col[None, :], cols[:, None], rows[:, None]) * inv[None, :]
    cos, sin = jnp.cos(ang), jnp.sin(ang)
    second = ((lane % (2 * nf)) >= nf)[None, :]
    return cos, jnp.where(second, sin, 0.0), jnp.where(second, 0.0, -sin)


def _attn_kernel(*refs, n, has_local):
    if has_local:
        sink_ref, q_ref, kv_ref, kvc_ref, bias_ref, o_ref = refs
    else:
        sink_ref, q_ref, kvc_ref, o_ref = refs
    n_loc = 3 * BLOCK
    nb = n // BLOCK
    grp = lax.broadcasted_iota(jnp.int32, (Q_PER_KV * BLOCK, 1), 0) // BLOCK
    for sb in range(q_ref.shape[1] // BLOCK):
        i = pl.program_id(1) * (q_ref.shape[1] // BLOCK) + sb
        q = q_ref[0, sb * BLOCK:(sb + 1) * BLOCK, :]
        kvall = kvc_ref[0]
        if has_local:
            start = pl.multiple_of(_local_start(i, n), BLOCK)
            kvall = jnp.concatenate([kv_ref[0, pl.ds(start, n_loc), :], kvall], axis=0)
            case = jnp.where(i == 0, 0, jnp.where(i == nb - 1, 2, 1))
            bias = jnp.concatenate([bias_ref[case]] * Q_PER_KV, axis=0)
        outs = []
        for hk in range(N_KV_HEADS):
            kh = kvall[:, hk * HEAD_DIM:(hk + 1) * HEAD_DIM]
            vh = kvall[:, KV_WIDTH + hk * HEAD_DIM:KV_WIDTH + (hk + 1) * HEAD_DIM]
            qg = jnp.concatenate(
                [q[:, (hk * Q_PER_KV + g) * HEAD_DIM:(hk * Q_PER_KV + g + 1) * HEAD_DIM] for g in range(Q_PER_KV)],
                axis=0)
            s = lax.dot_general(qg, kh, (((1,), (1,)), ((), ())), preferred_element_type=F32)
            if has_local:
                s = jnp.concatenate([s[:, :n_loc] + bias, s[:, n_loc:]], axis=1)
            snk = jnp.zeros((Q_PER_KV * BLOCK, 1), F32)
            for g in range(Q_PER_KV):
                snk = jnp.where(grp == g, sink_ref[hk * Q_PER_KV + g], snk)
            m = jnp.maximum(jnp.max(s, axis=1, keepdims=True), snk)
            p = jnp.exp(s - m).astype(MXU_DTYPE)
            v_ones = jnp.concatenate([vh, jnp.ones_like(vh)], axis=1)
            ov = jnp.dot(p, v_ones, preferred_element_type=F32)
            o = ov[:, :HEAD_DIM] / (ov[:, HEAD_DIM:HEAD_DIM + 1] + jnp.exp(snk - m))
            outs += [o[g * BLOCK:(g + 1) * BLOCK] for g in range(Q_PER_KV)]
        o_ref[0, sb * BLOCK:(sb + 1) * BLOCK, :] = jnp.concatenate(outs, axis=1).astype(o_ref.dtype)


def _local_start(i, n):
    return jnp.clip(i * BLOCK - BLOCK, 0, n - 3 * BLOCK)


def _window_bias(n):
    nb = n // BLOCK
    r = jnp.arange(BLOCK)[:, None]
    c = jnp.arange(3 * BLOCK)[None, :]
    cases = []
    for i in (0, 1, nb - 1):
        diff = (_local_start(i, n) + c) - (i * BLOCK + r)
        cases.append(jnp.where(jnp.abs(diff) <= WINDOW, 0.0, NEG_INF).astype(F32))
    return jnp.stack(cases)


def _attention(sink, q, kv, kvc):
    b, n, _ = q.shape
    lc = kvc.shape[1]
    has_local = kv is not None
    qrows = min(ATTN_QBLOCKS * BLOCK, n)
    assert n % qrows == 0
    in_specs = [pl.BlockSpec(memory_space=pltpu.SMEM), pl.BlockSpec((1, qrows, ATTN_WIDTH), lambda bi, i: (bi, i, 0))]
    args = [sink, q]
    if has_local:
        in_specs.append(pl.BlockSpec((1, n, 2 * KV_WIDTH), lambda bi, i: (bi, 0, 0)))
        args.append(kv)
    in_specs.append(pl.BlockSpec((1, lc, 2 * KV_WIDTH), lambda bi, i: (bi, 0, 0)))
    args.append(kvc)
    if has_local:
        nb = n // BLOCK
        assert nb >= 4
        in_specs.append(pl.BlockSpec((3, BLOCK, 3 * BLOCK), lambda bi, i: (0, 0, 0)))
        args.append(_window_bias(n))
    return pl.pallas_call(
        functools.partial(_attn_kernel, n=n, has_local=has_local),
        grid=(b, n // qrows),
        in_specs=in_specs,
        out_specs=pl.BlockSpec((1, qrows, ATTN_WIDTH), lambda bi, i: (bi, i, 0)),
        out_shape=jax.ShapeDtypeStruct((b, n, ATTN_WIDTH), MXU_DTYPE),
        compiler_params=_params("arbitrary", "arbitrary"),
        name="attn_local" if has_local else "attn_ctx",
    )(*args)


def _mix_out_kernel(*refs, conv, first, last):
    if conv:
        (att_ref, gb_ref, cu_ref, cup_ref, cun_ref, cw_ref, cb_ref,
         w_ref, x_ref, gate_ref, g2_ref, sh2_ref, sc2_ref, rwh_ref, rwl_ref, xo_ref, h_ref, afft_ref) = refs
        t = pl.program_id(1)
        prev = jnp.where(t == first, 0.0, cup_ref[0])
        nxt = jnp.where(t == last, 0.0, cun_ref[0])
        ext = jnp.concatenate([prev, cu_ref[0], nxt], axis=0)
        cw = cw_ref[...]
    else:
        (y_ref, w_ref, x_ref, gate_ref, g2_ref, sh2_ref, sc2_ref, rwh_ref, rwl_ref, xo_ref, h_ref, afft_ref) = refs
    tile = x_ref.shape[1]
    rows = tile // MIX_OUT_SPLIT
    for s in range(MIX_OUT_SPLIT):
        r0 = s * rows
        rs = slice(r0, r0 + rows)
        if conv:
            cv = (cw[0:1] * ext[HALO - 1 + r0:HALO - 1 + r0 + rows] + cw[1:2] * ext[HALO + r0:HALO + r0 + rows]
                  + cw[2:3] * ext[HALO + 1 + r0:HALO + 1 + r0 + rows] + cb_ref[...])
            cat = jnp.concatenate([att_ref[0, rs, :], (gb_ref[0, rs, :] * cv).astype(MXU_DTYPE)], axis=1)
        else:
            cat = jnp.concatenate([y_ref[0, c, rs, :] for c in range(y_ref.shape[1])], axis=1).astype(MXU_DTYPE)
        y = jnp.dot(cat, w_ref[...], preferred_element_type=F32)
        x = x_ref[0, rs, :] + gate_ref[0] * y
        xo_ref[0, rs, :] = x
        h = _norm_mod(x, g2_ref[...], sh2_ref[0], sc2_ref[0])
        _store_token_tiles(h_ref, h, r0)
        h_hi, h_lo = _split_hi_lo(h)
        logits = jnp.dot(h_hi, rwh_ref[...], preferred_element_type=F32)
        logits += jnp.dot(h_hi, rwl_ref[...], preferred_element_type=F32)
        logits += jnp.dot(h_lo, rwh_ref[...], preferred_element_type=F32)
        lane = lax.broadcasted_iota(jnp.int32, logits.shape, 1)
        logits = jnp.where(lane < N_EXPERTS, logits, NEG_INF)
        e = jnp.exp(logits - jnp.max(logits, axis=1, keepdims=True))
        aff = e / jnp.sum(e, axis=1, keepdims=True)
        afft_ref[0, :, rs] = aff.T[:N_EXPERTS]


def _mix_out(mix_in, conv_params, w, x, gate, g2, sh2, sc2, mod_row, rw_hi, rw_lo, tile):
    b, n, d = x.shape
    nt = n // tile
    conv = conv_params is not None
    row = (lambda bi: bi) if mod_row is None else (lambda bi: mod_row)
    tok = lambda width: pl.BlockSpec((1, tile, width), lambda bi, t: (bi, t, 0))
    modspec = pl.BlockSpec((1, 1, d), lambda bi, t: (row(bi), 0, 0))
    full = lambda a: pl.BlockSpec(a.shape, lambda bi, t: (0,) * a.ndim)
    if conv:
        att, gb, cu = mix_in
        cw, cb = conv_params
        per = tile // HALO
        in_specs = [tok(ATTN_WIDTH), tok(CONV_B_WIDTH), tok(CONV_B_WIDTH),
                    pl.BlockSpec((1, HALO, CONV_B_WIDTH), lambda bi, t: (bi, jnp.maximum(t * per - 1, 0), 0)),
                    pl.BlockSpec((1, HALO, CONV_B_WIDTH), lambda bi, t: (bi, jnp.minimum((t + 1) * per, n // HALO - 1), 0)),
                    full(cw), full(cb)]
        args = [att, gb, cu, cu, cu, cw, cb]
    else:
        in_specs = [_slab_spec(tile, lambda bi, t: (bi, 0, t, 0), d)]
        args = [mix_in]
    in_specs += [full(w), tok(d), modspec, full(g2), modspec, modspec, full(rw_hi), full(rw_lo)]
    args += [w, x, gate, g2, sh2, sc2, rw_hi, rw_lo]
    out_shape = (jax.ShapeDtypeStruct((b, n, d), F32), jax.ShapeDtypeStruct((b, n * SUBLANES, LANES), F32),
                 jax.ShapeDtypeStruct((b, N_EXPERTS, n), F32))
    out_specs = (tok(d), pl.BlockSpec((1, tile * SUBLANES, LANES), lambda bi, t: (bi, t, 0)),
                 pl.BlockSpec((1, N_EXPERTS, tile), lambda bi, t: (bi, 0, t)))
    return pl.pallas_call(
        functools.partial(_mix_out_kernel, conv=conv, first=0, last=nt - 1),
        grid=(b, nt),
        in_specs=in_specs,
        out_specs=out_specs,
        out_shape=out_shape,
        compiler_params=_params("arbitrary", "arbitrary"),
        name="even_out" if conv else "odd_out",
    )(*args)


def _cumsum_lanes(x):
    n = x.shape[1]
    r = lax.broadcasted_iota(jnp.int32, (CUMSUM_TILE, CUMSUM_TILE), 0)
    c = lax.broadcasted_iota(jnp.int32, (CUMSUM_TILE, CUMSUM_TILE), 1)
    tri = jnp.where(r <= c, 1.0, 0.0).astype(MXU_DTYPE)
    carry = jnp.zeros((x.shape[0], 1), F32)
    outs = []
    for k in range(n // CUMSUM_TILE):
        blk = x[:, k * CUMSUM_TILE:(k + 1) * CUMSUM_TILE].astype(MXU_DTYPE)
        loc = jnp.dot(blk, tri, preferred_element_type=F32) + carry
        outs.append(loc)
        carry = loc[:, CUMSUM_TILE - 1:CUMSUM_TILE]
    return jnp.concatenate(outs, axis=1)


SLOT_EMPTY = 1 << 20


def _select_kernel(at_ref, idx_ref, g_ref, *, cap):
    bblk, n_e, n = at_ref.shape
    at = at_ref[...].reshape(bblk * n_e, n)
    n_exp = bblk * n_e
    capf = float(cap)

    def count_ge(thr):
        return jnp.sum(jnp.where(at >= thr, 1.0, 0.0), axis=1, keepdims=True)

    def bit_body(_, c):
        lo_i, hi_i = c
        mid = lo_i + ((hi_i - lo_i) >> 1)
        ge = count_ge(lax.bitcast_convert_type(mid, F32)) >= capf
        return jnp.where(ge, mid, lo_i), jnp.where(ge, hi_i, mid)

    lo_i, hi_i = lax.fori_loop(
        0, 31, bit_body, (jnp.zeros((n_exp, 1), jnp.int32), jnp.full((n_exp, 1), 0x3F800001, jnp.int32)))

    def val_body(_, c):
        lo, hi = c
        mid = 0.5 * (lo + hi)
        ge = count_ge(mid) >= capf
        return jnp.where(ge, mid, lo), jnp.where(ge, hi, mid)

    lo, hi = lax.fori_loop(
        0, 24, val_body, (lax.bitcast_convert_type(lo_i, F32), lax.bitcast_convert_type(hi_i, F32)))
    need = capf - count_ge(hi)

    above = jnp.where(at >= hi, 1.0, 0.0)
    band = jnp.where(at >= lo, 1.0, 0.0) - above
    sel = above + band * jnp.where(_cumsum_lanes(band) <= need, 1.0, 0.0)
    rank = _cumsum_lanes(sel)

    lane = lax.broadcasted_iota(jnp.int32, (n_exp, n), 1)
    disp = jnp.where(sel > 0.5, lane + 1 - rank.astype(jnp.int32), SLOT_EMPTY)
    g = at
    for k in range(n.bit_length() - 1):
        step = 1 << k
        moving = ((disp >> k) & 1) == 1
        disp_in = pltpu.roll(disp, n - step, 1)
        arriving = ((disp_in >> k) & 1) == 1
        g = jnp.where(arriving, pltpu.roll(g, n - step, 1), g)
        disp = jnp.where(arriving, disp_in, jnp.where(moving, SLOT_EMPTY, disp))

    capp = pl.cdiv(cap, LANES) * LANES
    slot = lax.broadcasted_iota(jnp.int32, (n_exp, cap), 1)
    idx_ref[...] = jnp.clip(slot + disp[:, :cap], 0, n - 1).reshape(bblk, n_e, cap)
    for s in range(bblk):
        g_pad = jnp.concatenate([g[s * n_e:(s + 1) * n_e, :capp], jnp.zeros((LANES - n_e, capp), F32)], axis=0)
        g_ref[s] = g_pad.T[:cap]


def _select(aff_t, cap):
    b, n_exp, n = aff_t.shape
    assert n & (n - 1) == 0 and n % CUMSUM_TILE == 0 and n < SLOT_EMPTY
    bblk = b
    return pl.pallas_call(
        functools.partial(_select_kernel, cap=cap),
        grid=(b // bblk,),
        in_specs=[pl.BlockSpec((bblk, n_exp, n), lambda bi: (bi, 0, 0))],
        out_specs=(pl.BlockSpec((bblk, n_exp, cap), lambda bi: (bi, 0, 0)),
                   pl.BlockSpec((bblk, cap, LANES), lambda bi: (bi, 0, 0))),
        out_shape=(jax.ShapeDtypeStruct((b, n_exp, cap), jnp.int32), jax.ShapeDtypeStruct((b, cap, LANES), F32)),
        compiler_params=_params("arbitrary"),
        name="moe_select",
    )(aff_t)


def _gather_kernel(idx_ref, src_ref, xe_ref, xcm_ref, *, cap, n_exp):
    bblk = src_ref.shape[0]
    eblk, out_rows = xe_ref.shape[1], xe_ref.shape[2]
    stride = out_rows + SUBLANES
    for k in range(eblk):
        for s in range(bblk):
            base = ((pl.program_id(0) * bblk + s) * n_exp + pl.program_id(1) * eblk + k) * cap
            for r in range(cap):
                t = idx_ref[base + r]
                xcm_ref[k, pl.ds(s * cap + r, SUBLANES, stride=stride), :] = (
                    src_ref[s, pl.ds(pl.multiple_of(t * SUBLANES, SUBLANES), SUBLANES), :])
        used = bblk * cap
        if used < out_rows:
            for j in range(SUBLANES):
                xcm_ref[k, pl.ds(j * stride + used, out_rows - used), :] = jnp.zeros((out_rows - used, LANES), F32)
        xe_ref[0, k] = jnp.concatenate(
            [xcm_ref[k, pl.ds(j * stride, out_rows), :] for j in range(SUBLANES)], axis=1).astype(xe_ref.dtype)


def _gather(idx_flat, src, cap, bblk, out_rows):
    b, rows, _ = src.shape
    d = SUBLANES * LANES
    assert cap % SUBLANES == 0 and bblk * cap <= out_rows and b % bblk == 0 and N_EXPERTS % DISPATCH_EXPERTS == 0
    return pl.pallas_call(
        functools.partial(_gather_kernel, cap=cap, n_exp=N_EXPERTS),
        grid=(b // bblk, N_EXPERTS // DISPATCH_EXPERTS),
        in_specs=[pl.BlockSpec(memory_space=pltpu.SMEM),
                  pl.BlockSpec((bblk, rows, LANES), lambda bi, ei: (bi, 0, 0))],
        out_specs=pl.BlockSpec((1, DISPATCH_EXPERTS, out_rows, d), lambda bi, ei: (bi, ei, 0, 0)),
        out_shape=jax.ShapeDtypeStruct((b // bblk, N_EXPERTS, out_rows, d), MXU_DTYPE),
        scratch_shapes=[pltpu.VMEM((DISPATCH_EXPERTS, SUBLANES * (out_rows + SUBLANES), LANES), F32)],
        compiler_params=_params("arbitrary", "arbitrary"),
        name="moe_gather",
    )(idx_flat, src)


WEIGHT_CHUNKS = 2
WEIGHT_DMA_PRIORITY = 1


def _moe_ffn_kernel(*refs, layer, n_steps, has_ctx):
    if has_ctx:
        (x_ref, g_ref, xc_ref, gc_ref, wg_hbm, wu_hbm, wd_hbm, o_ref, oc_ref,
         wgu_s, wd_s, stage_in, stage_out, sem) = refs
    else:
        (x_ref, g_ref, wg_hbm, wu_hbm, wd_hbm, o_ref, wgu_s, wd_s, stage_in, stage_out, sem) = refs
    n_exp = pl.num_programs(0)
    e = pl.program_id(0)
    s = pl.program_id(1)
    slot = e % 2
    ff = wg_hbm.shape[3]
    chunks = []
    for hbm, dst, col0, stage in ((wg_hbm, wgu_s, 0, stage_in), (wu_hbm, wgu_s, ff, stage_in),
                                  (wd_hbm, wd_s, 0, stage_out)):
        n_rows = hbm.shape[2] // WEIGHT_CHUNKS
        chunks += [(hbm, dst, col0, stage, c * n_rows, n_rows) for c in range(WEIGHT_CHUNKS)]
    per_step = pl.cdiv(len(chunks), n_steps)

    def copy(c, expert):
        hbm, _, _, stage, r0, n_rows = chunks[c]
        return pltpu.make_async_copy(hbm.at[layer, expert, pl.ds(r0, n_rows), :], stage, sem.at[0])

    def cast(c, to_slot):
        hbm, dst, col0, stage, r0, n_rows = chunks[c]
        dst[to_slot, pl.ds(r0, n_rows), pl.ds(col0, hbm.shape[3])] = stage[...].astype(dst.dtype)

    @pl.when((e == 0) & (s == 0))
    def _():
        for c in range(len(chunks)):
            copy(c, 0).start()
            copy(c, 0).wait()
            cast(c, 0)

    has_next = e + 1 < n_exp
    for c in range(len(chunks)):
        if c % per_step == 0:
            @pl.when(has_next & (s == c // per_step))
            def _():
                copy(c, e + 1).start(priority=WEIGHT_DMA_PRIORITY)

    def ffn(xr, gr, outr):
        x, aff = xr[0, 0], gr[0]
        rows = x.shape[0]
        au = jnp.dot(x, wgu_s[slot], preferred_element_type=F32)
        a, u = au[:, :ff], au[:, ff:]
        mid = (a * _sigmoid(a) * u).astype(MXU_DTYPE)
        y = jnp.dot(mid, wd_s[slot], preferred_element_type=F32)
        lane = lax.broadcasted_iota(jnp.int32, aff.shape, 1)
        y = y * jnp.sum(jnp.where(lane == e, aff, 0.0), axis=1, keepdims=True)
        stride = rows + SUBLANES
        for j in range(SUBLANES):
            outr[0, 0, pl.ds(j * stride, rows), :] = y[:, j * LANES:(j + 1) * LANES]
            outr[0, 0, pl.ds(j * stride + rows, SUBLANES), :] = jnp.zeros((SUBLANES, LANES), F32)

    if has_ctx:
        is_ctx = s == n_steps - 1
        pl.when(is_ctx)(lambda: ffn(xc_ref, gc_ref, oc_ref))
        pl.when(jnp.logical_not(is_ctx))(lambda: ffn(x_ref, g_ref, o_ref))
    else:
        pl.when(s < n_steps)(lambda: ffn(x_ref, g_ref, o_ref))

    for c in range(len(chunks)):
        @pl.when(has_next & (s == c // per_step))
        def _():
            copy(c, e + 1).wait()
            cast(c, 1 - slot)
            if (c + 1) % per_step != 0 and c + 1 < len(chunks):
                copy(c + 1, e + 1).start(priority=WEIGHT_DMA_PRIORITY)


def _moe_ffn(xe, ge, ctx_block, layer, wg, wu, wd):
    nb, e, rows, d = xe.shape
    ff = wg.shape[-1]
    assert d % WEIGHT_CHUNKS == 0 and ff % (WEIGHT_CHUNKS * SUBLANES) == 0 and ff % LANES == 0
    has_ctx = ctx_block is not None
    n_steps = nb + int(has_ctx)
    hbm = pl.BlockSpec(memory_space=pl.ANY)
    out_rows = SUBLANES * (rows + SUBLANES)
    blk = lambda bi: jnp.minimum(bi, nb - 1)
    in_specs = [pl.BlockSpec((1, 1, rows, d), lambda ei, bi: (blk(bi), ei, 0, 0)),
                pl.BlockSpec((1, rows, LANES), lambda ei, bi: (blk(bi), 0, 0))]
    args = [xe, ge]
    out_specs = [pl.BlockSpec((1, 1, out_rows, LANES), lambda ei, bi: (blk(bi), ei, 0, 0))]
    out_shape = [jax.ShapeDtypeStruct((nb, e, out_rows, LANES), F32)]
    if has_ctx:
        rows_c = ctx_block[0].shape[2]
        assert ctx_block[0].shape == (1, e, rows_c, d) and ctx_block[1].shape == (1, rows_c, LANES)
        out_rows_c = SUBLANES * (rows_c + SUBLANES)
        in_specs += [pl.BlockSpec((1, 1, rows_c, d), lambda ei, bi: (0, ei, 0, 0)),
                     pl.BlockSpec((1, rows_c, LANES), lambda ei, bi: (0, 0, 0))]
        args += list(ctx_block)
        out_specs.append(pl.BlockSpec((1, 1, out_rows_c, LANES), lambda ei, bi: (0, ei, 0, 0)))
        out_shape.append(jax.ShapeDtypeStruct((1, e, out_rows_c, LANES), F32))
    return pl.pallas_call(
        functools.partial(_moe_ffn_kernel, layer=layer, n_steps=n_steps, has_ctx=has_ctx),
        grid=(e, n_steps),
        in_specs=in_specs + [hbm, hbm, hbm],
        out_specs=tuple(out_specs),
        out_shape=tuple(out_shape),
        scratch_shapes=[pltpu.VMEM((2, d, 2 * ff), MXU_DTYPE), pltpu.VMEM((2, ff, d), MXU_DTYPE),
                        pltpu.VMEM((d // WEIGHT_CHUNKS, ff), F32), pltpu.VMEM((ff // WEIGHT_CHUNKS, d), F32),
                        pltpu.SemaphoreType.DMA((1,))],
        compiler_params=_params("arbitrary", "arbitrary"),
        name="moe_ffn",
    )(*args, wg, wu, wd)


COMBINE_UNROLL = 8


def _combine_kernel(idx_ref, y_ref, acc_ref, *, cap, n_exp):
    bblk = acc_ref.shape[0]
    eblk = y_ref.shape[1]
    ei = pl.program_id(1)
    stride = y_ref.shape[2] // SUBLANES

    @pl.when(ei == 0)
    def _():
        acc_ref[...] = jnp.zeros_like(acc_ref)

    for k in range(eblk):
        for s in range(bblk):
            base = ((pl.program_id(0) * bblk + s) * n_exp + ei * eblk + k) * cap
            for r0 in range(0, cap, COMBINE_UNROLL):
                toks = [pl.multiple_of(idx_ref[base + r0 + i] * SUBLANES, SUBLANES) for i in range(COMBINE_UNROLL)]
                new = [acc_ref[s, pl.ds(toks[i], SUBLANES), :]
                       + y_ref[0, k, pl.ds(s * cap + r0 + i, SUBLANES, stride=stride), :]
                       for i in range(COMBINE_UNROLL)]
                for i in range(COMBINE_UNROLL):
                    acc_ref[s, pl.ds(toks[i], SUBLANES), :] = new[i]


def _combine(idx_flat, y, b, n, cap, bblk):
    nb, n_exp, rows, _ = y.shape
    assert cap % COMBINE_UNROLL == 0 and nb * bblk == b and bblk * cap <= rows // SUBLANES - SUBLANES
    return pl.pallas_call(
        functools.partial(_combine_kernel, cap=cap, n_exp=n_exp),
        grid=(nb, n_exp // DISPATCH_EXPERTS),
        in_specs=[pl.BlockSpec(memory_space=pltpu.SMEM),
                  pl.BlockSpec((1, DISPATCH_EXPERTS, rows, LANES), lambda bi, ei: (bi, ei, 0, 0))],
        out_specs=pl.BlockSpec((bblk, n * SUBLANES, LANES), lambda bi, ei: (bi, 0, 0)),
        out_shape=jax.ShapeDtypeStruct((b, n * SUBLANES, LANES), F32),
        compiler_params=_params("arbitrary", "arbitrary"),
        name="moe_combine",
    )(idx_flat, y)


def _expert_choice(h_lat, aff_lat, h_ctx, aff_ctx, layer, wg, wu, wd):
    b, _, n = aff_lat.shape
    cap = max(1, CAPACITY_FACTOR * n // N_EXPERTS)
    idx, g = _select(aff_lat, cap)
    idx = idx.reshape(-1)
    xe = _gather(idx, h_lat, cap, 1, cap)
    if h_ctx is None:
        (y,) = _moe_ffn(xe, g, None, layer, wg, wu, wd)
        return _combine(idx, y, b, n, cap, 1), None
    lc = aff_ctx.shape[2]
    cap_c = max(1, CAPACITY_FACTOR * lc // N_EXPERTS)
    idx_c, g_c = _select(aff_ctx, cap_c)
    idx_c = idx_c.reshape(-1)
    xe_c = _gather(idx_c, h_ctx, cap_c, b, b * cap_c)
    g_c = g_c.reshape(1, b * cap_c, LANES)
    y, y_c = _moe_ffn(xe, g, (xe_c, g_c), layer, wg, wu, wd)
    return _combine(idx, y, b, n, cap, 1), _combine(idx_c, y_c, b, lc, cap_c, b)


def _odd_in_kernel(x_ref, moe_ref, gate_ref, g_ref, sh_ref, sc_ref, w_ref, xo_ref, gg_ref, u_ref):
    x = x_ref[0] + gate_ref[0] * _load_token_tiles(moe_ref)
    xo_ref[0] = x
    h = _norm_mod(x, g_ref[...], sh_ref[0], sc_ref[0]).astype(MXU_DTYPE)
    y = jnp.dot(h, w_ref[...], preferred_element_type=F32)
    gl = y[:, :LRU_WIDTH]
    _store_slabs(gg_ref, 0.25 * gl * (1.0 + jnp.tanh(0.7978845608028654 * (gl + 0.044715 * gl * gl * gl))))
    _store_slabs(u_ref, y[:, LRU_WIDTH:])


def _odd_in(x, moe, gate, g, shift, scale, mod_row, w, tile):
    b, n, d = x.shape
    row = (lambda bi: bi) if mod_row is None else (lambda bi: mod_row)
    tok = pl.BlockSpec((1, tile, d), lambda bi, t: (bi, t, 0))
    tiles = pl.BlockSpec((1, tile * SUBLANES, LANES), lambda bi, t: (bi, t, 0))
    modspec = pl.BlockSpec((1, 1, d), lambda bi, t: (row(bi), 0, 0))
    return pl.pallas_call(
        _odd_in_kernel,
        grid=(b, n // tile),
        in_specs=[tok, tiles, modspec, pl.BlockSpec((1, d), lambda bi, t: (0, 0)), modspec, modspec,
                  pl.BlockSpec(w.shape, lambda bi, t: (0, 0))],
        out_specs=(tok, _slab_spec(tile, lambda bi, t: (bi, 0, t, 0)), _slab_spec(tile, lambda bi, t: (bi, 0, t, 0))),
        out_shape=(jax.ShapeDtypeStruct((b, n, d), F32),) + (jax.ShapeDtypeStruct((b, d // LANES, n, LANES), F32),) * 2,
        compiler_params=_params("arbitrary", "arbitrary"),
        name="odd_in",
    )(x, moe, gate, g, shift, scale, w)


def _lru_kernel(*refs, reverse, nt):
    if reverse:
        (uc_ref, wax_ref, lam_ref, h0_ref, hf_ref, gg_ref, out_ref, hlast_ref, a_scr, b_scr, uc_scr, carry_scr) = refs
        n_slab, tile = uc_ref.shape[1], uc_ref.shape[2]
    else:
        (u_ref, up_ref, un_ref, cw_ref, cb_ref, wax_ref, lam_ref, h0_ref,
         out_ref, uco_ref, hlast_ref, a_scr, b_scr, uc_scr, carry_scr) = refs
        n_slab, tile = u_ref.shape[1], u_ref.shape[2]
    w = n_slab * LANES
    per = tile // SUBLANES
    phase = lambda ref, j: jnp.concatenate(
        [ref[0, c, pl.ds(j, per, stride=SUBLANES), :] for c in range(n_slab)], axis=1)

    def store_phase(ref, j, val):
        for c in range(n_slab):
            ref[0, c, pl.ds(j, per, stride=SUBLANES), :] = val[:, c * LANES:(c + 1) * LANES]

    rows = lambda j: slice(j * per, (j + 1) * per)
    t = pl.program_id(1)
    if reverse:
        for j in range(SUBLANES):
            uc_scr[rows(j), :] = phase(uc_ref, j)
    else:
        prev = jnp.where(t == 0, 0.0, _load_slabs(up_ref))
        nxt = jnp.where(t == nt - 1, 0.0, _load_slabs(un_ref))
        rowid = lax.broadcasted_iota(jnp.int32, (per, w), 0)
        shift_down = lambda x, first: jnp.where(rowid == 0, first, pltpu.roll(x, 1, 0))
        shift_up = lambda x, last: jnp.where(rowid == per - 1, last, pltpu.roll(x, per - 1, 0))
        u = [phase(u_ref, j) for j in range(SUBLANES)]
        um1 = [shift_down(u[7], prev[7:8])] + u[:7]
        um2 = [shift_down(u[6], prev[6:7]), um1[0]] + u[:6]
        up1 = u[1:] + [shift_up(u[0], nxt[0:1])]
        cw = cw_ref[...]
        for j in range(SUBLANES):
            uc_j = cb_ref[...] + cw[0:1] * um2[j] + cw[1:2] * um1[j] + cw[2:3] * u[j] + cw[3:4] * up1[j]
            uc_scr[rows(j), :] = uc_j
            store_phase(uco_ref, j, uc_j)

    lam = lam_ref[0]
    half_decay = (0.5 * LRU_C) * (jnp.maximum(-lam, 0.0) + jnp.log1p(jnp.exp(-jnp.abs(lam))))
    ones = jnp.where(lax.broadcasted_iota(jnp.int32, (tile, LRU_BLOCK), 1) < BIAS_TERMS, 1.0, 0.0).astype(MXU_DTYPE)
    for hd in range(LRU_HEADS):
        sl = slice(hd * LRU_BLOCK, (hd + 1) * LRU_BLOCK)
        uc = uc_scr[:, sl]
        z = jnp.dot(jnp.concatenate([uc.astype(MXU_DTYPE), ones], axis=1), wax_ref[0, hd], preferred_element_type=F32)
        hd_row = half_decay[:, sl]
        neg_log_a = hd_row * jnp.tanh(z[:, :LRU_BLOCK]) + hd_row
        gate2 = 1.0 + jnp.tanh(z[:, LRU_BLOCK:])
        a = jnp.exp(-neg_log_a)
        m2 = jnp.tanh(neg_log_a) * (a * a + 1.0)
        mult = jnp.where(m2 > 0.0, m2 * lax.rsqrt(m2), 0.0)
        a_scr[:, sl] = a
        b_scr[:, sl] = mult * (gate2 * uc)

    @pl.when(t == 0)
    def _():
        carry_scr[...] = jnp.broadcast_to(h0_ref[0], carry_scr.shape)

    order = list(range(SUBLANES))[::-1] if reverse else list(range(SUBLANES))
    hrun = b_scr[rows(order[0]), :]
    prun = a_scr[rows(order[0]), :]
    for j in order[1:]:
        aj = a_scr[rows(j), :]
        hrun = aj * hrun + b_scr[rows(j), :]
        prun = aj * prun
        b_scr[rows(j), :] = hrun
        a_scr[rows(j), :] = prun

    lane_row = lax.broadcasted_iota(jnp.int32, (SUBLANES, w), 0)
    carry = carry_scr[...]
    groups = list(range(per // SUBLANES))
    entering = [None] * len(groups)
    for m in (groups[::-1] if reverse else groups):
        a = prun[m * SUBLANES:(m + 1) * SUBLANES]
        bcoef = hrun[m * SUBLANES:(m + 1) * SUBLANES]
        for dist in (1, 2, 4):
            shift = (SUBLANES - dist) if reverse else dist
            msk = (lane_row < SUBLANES - dist) if reverse else (lane_row >= dist)
            a_s = pltpu.roll(a, shift, 0)
            b_s = pltpu.roll(bcoef, shift, 0)
            bcoef = jnp.where(msk, a * b_s + bcoef, bcoef)
            a = jnp.where(msk, a * a_s, a)
        after = a * carry + bcoef
        if reverse:
            entering[m] = jnp.where(lane_row == SUBLANES - 1, carry, pltpu.roll(after, SUBLANES - 1, 0))
            carry = jnp.broadcast_to(after[0:1], carry.shape)
        else:
            entering[m] = jnp.where(lane_row == 0, carry, pltpu.roll(after, 1, 0))
            carry = jnp.broadcast_to(after[SUBLANES - 1:SUBLANES], carry.shape)
    carry_scr[...] = carry
    hlast_ref[0] = carry[0:1]
    h_in = jnp.concatenate(entering, axis=0)

    for j in range(SUBLANES):
        hcur = b_scr[rows(j), :] + a_scr[rows(j), :] * h_in
        if reverse:
            hcur = phase(gg_ref, j) * (phase(hf_ref, j) + hcur)
        store_phase(out_ref, j, hcur)


BIAS_TERMS = 3


def _gate_weights(wa, wx, ba, bx):
    n_dir, heads, blk, _ = wa.shape
    w = 0.5 * jnp.concatenate([wa, wx], axis=-1)
    bias = 0.5 * jnp.concatenate([ba.reshape(n_dir, heads, blk), bx.reshape(n_dir, heads, blk)], axis=-1)
    terms = []
    for _ in range(BIAS_TERMS):
        term = bias.astype(MXU_DTYPE)
        terms.append(term)
        bias = bias - term.astype(F32)
    rows = jnp.stack(terms, axis=2)
    pad = jnp.zeros((n_dir, heads, blk - BIAS_TERMS, 2 * blk), MXU_DTYPE)
    return jnp.concatenate([w.astype(MXU_DTYPE), rows, pad], axis=2)


def _lru_fwd(u, cw, cb, wax, lam, h0, tile):
    b, n_slab, n, _ = u.shape
    w = n_slab * LANES
    nt = n // tile
    per = tile // HALO
    assert tile % (SUBLANES * SUBLANES) == 0
    tok = _slab_spec(tile, lambda bi, t: (bi, 0, t, 0), w)
    state = pl.BlockSpec((1, 1, w), lambda bi, t: (bi, 0, 0))
    return pl.pallas_call(
        functools.partial(_lru_kernel, reverse=False, nt=nt),
        grid=(b, nt),
        in_specs=[tok,
                  _slab_spec(HALO, lambda bi, t: (bi, 0, jnp.maximum(t * per - 1, 0), 0), w),
                  _slab_spec(HALO, lambda bi, t: (bi, 0, jnp.minimum((t + 1) * per, n // HALO - 1), 0), w),
                  pl.BlockSpec(cw.shape, lambda bi, t: (0, 0)),
                  pl.BlockSpec(cb.shape, lambda bi, t: (0, 0)),
                  pl.BlockSpec((1,) + wax.shape[1:], lambda bi, t: (0, 0, 0, 0)),
                  pl.BlockSpec((1, 1, w), lambda bi, t: (0, 0, 0)),
                  state],
        out_specs=(tok, tok, state),
        out_shape=(jax.ShapeDtypeStruct(u.shape, F32), jax.ShapeDtypeStruct(u.shape, F32),
                   jax.ShapeDtypeStruct((b, 1, w), F32)),
        scratch_shapes=[pltpu.VMEM((tile, w), F32), pltpu.VMEM((tile, w), F32), pltpu.VMEM((tile, w), F32),
                        pltpu.VMEM((SUBLANES, w), F32)],
        compiler_params=_params("arbitrary", "arbitrary"),
        name="lru_fwd",
    )(u, u, u, cw, cb, wax, lam, h0)


def _lru_bwd(uc, wax, lam, h0, hf, gg, tile):
    b, n_slab, n, _ = uc.shape
    w = n_slab * LANES
    nt = n // tile
    assert tile % (SUBLANES * SUBLANES) == 0
    tok = _slab_spec(tile, lambda bi, t: (bi, 0, nt - 1 - t, 0), w)
    state = pl.BlockSpec((1, 1, w), lambda bi, t: (bi, 0, 0))
    return pl.pallas_call(
        functools.partial(_lru_kernel, reverse=True, nt=nt),
        grid=(b, nt),
        in_specs=[tok,
                  pl.BlockSpec((1,) + wax.shape[1:], lambda bi, t: (1, 0, 0, 0)),
                  pl.BlockSpec((1, 1, w), lambda bi, t: (1, 0, 0)),
                  state, tok, tok],
        out_specs=(tok, state),
        out_shape=(jax.ShapeDtypeStruct(uc.shape, F32), jax.ShapeDtypeStruct((b, 1, w), F32)),
        scratch_shapes=[pltpu.VMEM((tile, w), F32), pltpu.VMEM((tile, w), F32), pltpu.VMEM((tile, w), F32),
                        pltpu.VMEM((SUBLANES, w), F32)],
        compiler_params=_params("arbitrary", "arbitrary"),
        name="lru_bwd",
    )(uc, wax, lam, h0, hf, gg)


def _final_kernel(x_ref, moe_ref, gate_ref, g_ref, o_ref):
    x = x_ref[0] + gate_ref[0] * _load_token_tiles(moe_ref)
    o_ref[0] = x * lax.rsqrt(jnp.mean(x * x, axis=-1, keepdims=True) + EPS) * g_ref[...]


def _final(x, moe, gate, g, tile):
    b, n, d = x.shape
    tok = pl.BlockSpec((1, tile, d), lambda bi, t: (bi, t, 0))
    tiles = pl.BlockSpec((1, tile * SUBLANES, LANES), lambda bi, t: (bi, t, 0))
    return pl.pallas_call(
        _final_kernel,
        grid=(b, n // tile),
        in_specs=[tok, tiles, pl.BlockSpec((1, 1, d), lambda bi, t: (bi, 0, 0)), pl.BlockSpec((1, d), lambda bi, t: (0, 0))],
        out_specs=tok,
        out_shape=jax.ShapeDtypeStruct((b, n, d), F32),
        compiler_params=_params("arbitrary", "arbitrary"),
        name="final_norm",
    )(x, moe, gate, g)


def kernel(x, c, ctx, c_ctx, ada_w, ada_b, norm_mix_g, norm_ffn_g, ev_w_in, ev_w_out, ev_sink, ev_conv_w, ev_conv_b, od_w_in, od_w_out, od_conv_w, od_conv_b, od_wa, od_ba, od_wx, od_bx, od_lambda, router_w, w_gate, w_up, w_down, final_g):
    b, n, d = x.shape
    lc = ctx.shape[1]
    depth = ada_w.shape[0]
    assert depth == 2 and d == D_MODEL and b < MOD_ROWS
    tile_l = min(512, n)
    tile_c = lc
    ctx_row = b

    cvec = jnp.zeros((MOD_ROWS, d), F32).at[:b].set(c).at[b].set(c_ctx)
    mods = _ada(cvec, ada_w, ada_b).reshape(depth, MOD_ROWS, 6, 1, d)
    mod = lambda l, j: mods[l, :, j]

    def router_split(l):
        return _split_hi_lo(jnp.pad(router_w[l], ((0, 0), (0, LANES - N_EXPERTS))))

    bf = lambda a: a.astype(MXU_DTYPE)

    g_mix = norm_mix_g[0].reshape(1, d)
    g_ffn = norm_ffn_g[0].reshape(1, d)
    w_in = bf(ev_w_in[0])
    w_out = bf(ev_w_out[0])
    conv_p = (ev_conv_w[0], ev_conv_b[0].reshape(1, -1))
    tables = _rope_tables(n)
    ql, kvl, gbl, cul = _even_in(x, g_mix, mod(0, 0), mod(0, 1), None, w_in, tables, tile_l)
    qc, kvc, gbc, cuc = _even_in(ctx, g_mix, mod(0, 0), mod(0, 1), ctx_row, w_in, None, tile_c)
    att_l = _attention(ev_sink[0], ql, kvl, kvc)
    att_c = _attention(ev_sink[0], qc, None, kvc)
    rw_hi, rw_lo = router_split(0)
    xl, hl, at_l = _mix_out((att_l, gbl, cul), conv_p, w_out, x, mod(0, 2), g_ffn, mod(0, 3), mod(0, 4), None,
                            rw_hi, rw_lo, tile_l)
    xc, hc, at_c = _mix_out((att_c, gbc, cuc), conv_p, w_out, ctx, mod(0, 2), g_ffn, mod(0, 3), mod(0, 4),
                            ctx_row, rw_hi, rw_lo, tile_c)
    moe_l, moe_c = _expert_choice(hl, at_l, hc, at_c, 0, w_gate, w_up, w_down)

    g_mix = norm_mix_g[1].reshape(1, d)
    g_ffn = norm_ffn_g[1].reshape(1, d)
    w_in = bf(od_w_in[0])
    w_out = bf(od_w_out[0])
    xl, ggl, ul = _odd_in(xl, moe_l, mod(0, 5), g_mix, mod(1, 0), mod(1, 1), None, w_in, tile_l)
    _, _, u_ctx = _odd_in(xc, moe_c, mod(0, 5), g_mix, mod(1, 0), mod(1, 1), ctx_row, w_in, tile_c)
    cw, cb = od_conv_w[0], od_conv_b[0].reshape(1, -1)
    wax = _gate_weights(od_wa[0], od_wx[0], od_ba[0], od_bx[0])
    lam = od_lambda[0].reshape(2, 1, -1)
    zero_state = jnp.zeros((b, 1, LRU_WIDTH), F32)
    hf_c, uc_c, h0_f = _lru_fwd(u_ctx, cw, cb, wax, lam, zero_state, tile_c)
    _, h0_b = _lru_bwd(uc_c, wax, lam, zero_state, hf_c, hf_c, tile_c)
    hf_l, uc_l, _ = _lru_fwd(ul, cw, cb, wax, lam, h0_f, tile_l)
    yl, _ = _lru_bwd(uc_l, wax, lam, h0_b, hf_l, ggl, tile_l)
    rw_hi, rw_lo = router_split(1)
    xl, hl, at_l = _mix_out(yl, None, w_out, xl, mod(1, 2), g_ffn, mod(1, 3), mod(1, 4), None,
                            rw_hi, rw_lo, tile_l)
    moe_l, _ = _expert_choice(hl, at_l, None, None, 1, w_gate, w_up, w_down)
    return _final(xl, moe_l, mod(1, 5), final_g.reshape(1, d), tile_l)
```

```python
import functools

import jax
import jax.numpy as jnp
from jax import lax
from jax.experimental import pallas as pl
from jax.experimental.pallas import tpu as pltpu

F32 = jnp.float32
MXU_DTYPE = jnp.bfloat16

D_MODEL = 1024
GRID_W = 64
EPS = 1e-6
NEG_INF = -1e30
HEAD_DIM = 64
N_Q_HEADS = 8
N_KV_HEADS = 2
Q_PER_KV = N_Q_HEADS // N_KV_HEADS
ATTN_WIDTH = N_Q_HEADS * HEAD_DIM
KV_WIDTH = N_KV_HEADS * HEAD_DIM
WINDOW = 128
BLOCK = 128
ROPE_BASE = 10000.0
CONV_B_WIDTH = D_MODEL // 2
EVEN_IN = ATTN_WIDTH + 2 * KV_WIDTH + 3 * CONV_B_WIDTH
LRU_WIDTH = D_MODEL
LRU_HEADS = 8
LRU_BLOCK = LRU_WIDTH // LRU_HEADS
LRU_C = 8.0
N_EXPERTS = 16
CAPACITY_FACTOR = 2
EXPERT_FF = 1408
MOD_ROWS = 16
LANES = 128
SUBLANES = 8
HALO = SUBLANES
CUMSUM_TILE = 256
MIX_OUT_SPLIT = 2
ATTN_QBLOCKS = 4
DISPATCH_EXPERTS = 4
V7X_VMEM_LIMIT = 56 * 1024 * 1024


def _params(*sem):
    return pltpu.CompilerParams(dimension_semantics=sem, vmem_limit_bytes=V7X_VMEM_LIMIT)


def _split_hi_lo(a):
    hi = a.astype(MXU_DTYPE)
    lo = (a - hi.astype(F32)).astype(MXU_DTYPE)
    return hi, lo


def _sigmoid(z):
    return 0.5 * (1.0 + jnp.tanh(0.5 * z))


def _store_token_tiles(ref, val, row0=0):
    tile = val.shape[0]
    for j in range(SUBLANES):
        ref[0, pl.ds(row0 * SUBLANES + j, tile, stride=SUBLANES), :] = val[:, j * LANES:(j + 1) * LANES]


def _load_token_tiles(ref):
    tile = ref.shape[1] // SUBLANES
    return jnp.concatenate([ref[0, pl.ds(j, tile, stride=SUBLANES), :] for j in range(SUBLANES)], axis=1)


def _slab_spec(rows, index_map, width=LRU_WIDTH):
    return pl.BlockSpec((1, width // LANES, rows, LANES), index_map)


def _store_slabs(ref, val):
    for c in range(val.shape[1] // LANES):
        ref[0, c] = val[:, c * LANES:(c + 1) * LANES]


def _load_slabs(ref):
    return jnp.concatenate([ref[0, c] for c in range(ref.shape[1])], axis=1)


def _norm_mod(x, g, shift, scale):
    y = x * lax.rsqrt(jnp.mean(x * x, axis=-1, keepdims=True) + EPS) * g
    return y * (1.0 + scale) + shift


def _ada_kernel(c_ref, w_ref, b_ref, o_ref):
    c = c_ref[...]
    s_hi, s_lo = _split_hi_lo(c * _sigmoid(c))
    w_hi, w_lo = _split_hi_lo(w_ref[0])
    acc = jnp.dot(s_hi, w_hi, preferred_element_type=F32)
    acc += jnp.dot(s_hi, w_lo, preferred_element_type=F32)
    acc += jnp.dot(s_lo, w_hi, preferred_element_type=F32)
    o_ref[0] = acc + b_ref[0]


def _ada(cvec, ada_w, ada_b):
    depth, d, n6 = ada_w.shape
    tn = 1536
    return pl.pallas_call(
        _ada_kernel,
        grid=(depth, n6 // tn),
        in_specs=[
            pl.BlockSpec((MOD_ROWS, d), lambda l, j: (0, 0)),
            pl.BlockSpec((1, d, tn), lambda l, j: (l, 0, j)),
            pl.BlockSpec((1, 1, tn), lambda l, j: (l, 0, j)),
        ],
        out_specs=pl.BlockSpec((1, MOD_ROWS, tn), lambda l, j: (l, 0, j)),
        out_shape=jax.ShapeDtypeStruct((depth, MOD_ROWS, n6), F32),
        compiler_params=_params("arbitrary", "arbitrary"),
        name="ada",
    )(cvec, ada_w, ada_b.reshape(depth, 1, n6))


def _even_in_kernel(*refs, rope):
    if rope:
        (x_ref, g_ref, sh_ref, sc_ref, w_ref, cos_ref, sa_ref, sb_ref, q_ref, kv_ref, gb_ref, cu_ref) = refs
    else:
        (x_ref, g_ref, sh_ref, sc_ref, w_ref, q_ref, kv_ref, gb_ref, cu_ref) = refs
    h = _norm_mod(x_ref[0], g_ref[...], sh_ref[0], sc_ref[0]).astype(MXU_DTYPE)
    y = jnp.dot(h, w_ref[...], preferred_element_type=F32)
    q = y[:, :ATTN_WIDTH]
    k = y[:, ATTN_WIDTH:ATTN_WIDTH + KV_WIDTH]
    v = y[:, ATTN_WIDTH + KV_WIDTH:ATTN_WIDTH + 2 * KV_WIDTH]
    c0 = ATTN_WIDTH + 2 * KV_WIDTH
    if rope:
        cos, sa, sb = cos_ref[...], sa_ref[...], sb_ref[...]

        def rot(z):
            return z * cos + pltpu.roll(z, 16, 1) * sa + pltpu.roll(z, 112, 1) * sb

        q = jnp.concatenate([rot(q[:, j * 128:(j + 1) * 128]) for j in range(ATTN_WIDTH // 128)], axis=1)
        k = rot(k)
    q_ref[0] = (q * (HEAD_DIM ** -0.5)).astype(q_ref.dtype)
    kv_ref[0] = jnp.concatenate([k, v], axis=1).astype(kv_ref.dtype)
    gb_ref[0] = y[:, c0:c0 + CONV_B_WIDTH]
    cu_ref[0] = y[:, c0 + CONV_B_WIDTH:c0 + 2 * CONV_B_WIDTH] * y[:, c0 + 2 * CONV_B_WIDTH:]


def _even_in(x, g, shift, scale, mod_row, w, tables, tile):
    b, n, d = x.shape
    nt = n // tile
    rope = tables is not None
    row = (lambda bi: bi) if mod_row is None else (lambda bi: mod_row)
    in_specs = [
        pl.BlockSpec((1, tile, d), lambda bi, t: (bi, t, 0)),
        pl.BlockSpec((1, d), lambda bi, t: (0, 0)),
        pl.BlockSpec((1, 1, d), lambda bi, t: (row(bi), 0, 0)),
        pl.BlockSpec((1, 1, d), lambda bi, t: (row(bi), 0, 0)),
        pl.BlockSpec(w.shape, lambda bi, t: (0, 0)),
    ]
    args = [x, g, shift, scale, w]
    if rope:
        in_specs += [pl.BlockSpec((tile, 128), lambda bi, t: (t, 0))] * 3
        args += list(tables)
    out_shape = (
        jax.ShapeDtypeStruct((b, n, ATTN_WIDTH), MXU_DTYPE),
        jax.ShapeDtypeStruct((b, n, 2 * KV_WIDTH), MXU_DTYPE),
        jax.ShapeDtypeStruct((b, n, CONV_B_WIDTH), F32),
        jax.ShapeDtypeStruct((b, n, CONV_B_WIDTH), F32),
    )
    out_specs = tuple(pl.BlockSpec((1, tile, s.shape[-1]), lambda bi, t: (bi, t, 0)) for s in out_shape)
    return pl.pallas_call(
        functools.partial(_even_in_kernel, rope=rope),
        grid=(b, nt),
        in_specs=in_specs,
        out_specs=out_specs,
        out_shape=out_shape,
        compiler_params=_params("arbitrary", "arbitrary"),
        name="even_in_rope" if rope else "even_in",
    )(*args)


def _rope_tables(n):
    nf = HEAD_DIM // 4
    pos = jnp.arange(n)
    rows = (pos // GRID_W).astype(F32)
    cols = (pos % GRID_W).astype(F32)
    lane = jnp.arange(128)
    inv = ROPE_BASE ** (-(lane % nf).astype(F32) / nf)
    use_col = (lane % HEAD_DIM) >= HEAD_DIM // 2
    ang = jnp.where(use_col[None, :], cols[:, None], rows[:, None]) * inv[None, :]
    cos, sin = jnp.cos(ang), jnp.sin(ang)
    second = ((lane % (2 * nf)) >= nf)[None, :]
    return cos, jnp.where(second, sin, 0.0), jnp.where(second, 0.0, -sin)


def _attn_kernel(*refs, n, has_local):
    if has_local:
        sink_ref, q_ref, kv_ref, kvc_ref, bias_ref, o_ref = refs
    else:
        sink_ref, q_ref, kvc_ref, o_ref = refs
    n_loc = 3 * BLOCK
    nb = n // BLOCK
    grp = lax.broadcasted_iota(jnp.int32, (Q_PER_KV * BLOCK, 1), 0) // BLOCK
    for sb in range(q_ref.shape[1] // BLOCK):
        i = pl.program_id(1) * (q_ref.shape[1] // BLOCK) + sb
        q = q_ref[0, sb * BLOCK:(sb + 1) * BLOCK, :]
        kvall = kvc_ref[0]
        if has_local:
            start = pl.multiple_of(_local_start(i, n), BLOCK)
            kvall = jnp.concatenate([kv_ref[0, pl.ds(start, n_loc), :], kvall], axis=0)
            case = jnp.where(i == 0, 0, jnp.where(i == nb - 1, 2, 1))
            bias = jnp.concatenate([bias_ref[case]] * Q_PER_KV, axis=0)
        outs = []
        for hk in range(N_KV_HEADS):
            kh = kvall[:, hk * HEAD_DIM:(hk + 1) * HEAD_DIM]
            vh = kvall[:, KV_WIDTH + hk * HEAD_DIM:KV_WIDTH + (hk + 1) * HEAD_DIM]
            qg = jnp.concatenate(
                [q[:, (hk * Q_PER_KV + g) * HEAD_DIM:(hk * Q_PER_KV + g + 1) * HEAD_DIM] for g in range(Q_PER_KV)],
                axis=0)
            s = lax.dot_general(qg, kh, (((1,), (1,)), ((), ())), preferred_element_type=F32)
            if has_local:
                s = jnp.concatenate([s[:, :n_loc] + bias, s[:, n_loc:]], axis=1)
            snk = jnp.zeros((Q_PER_KV * BLOCK, 1), F32)
            for g in range(Q_PER_KV):
                snk = jnp.where(grp == g, sink_ref[hk * Q_PER_KV + g], snk)
            m = jnp.maximum(jnp.max(s, axis=1, keepdims=True), snk)
            p = jnp.exp(s - m).astype(MXU_DTYPE)
            v_ones = jnp.concatenate([vh, jnp.ones_like(vh)], axis=1)
            ov = jnp.dot(p, v_ones, preferred_element_type=F32)
            o = ov[:, :HEAD_DIM] / (ov[:, HEAD_DIM:HEAD_DIM + 1] + jnp.exp(snk - m))
            outs += [o[g * BLOCK:(g + 1) * BLOCK] for g in range(Q_PER_KV)]
        o_ref[0, sb * BLOCK:(sb + 1) * BLOCK, :] = jnp.concatenate(outs, axis=1).astype(o_ref.dtype)


def _local_start(i, n):
    return jnp.clip(i * BLOCK - BLOCK, 0, n - 3 * BLOCK)


def _window_bias(n):
    nb = n // BLOCK
    r = jnp.arange(BLOCK)[:, None]
    c = jnp.arange(3 * BLOCK)[None, :]
    cases = []
    for i in (0, 1, nb - 1):
        diff = (_local_start(i, n) + c) - (i * BLOCK + r)
        cases.append(jnp.where(jnp.abs(diff) <= WINDOW, 0.0, NEG_INF).astype(F32))
    return jnp.stack(cases)


def _attention(sink, q, kv, kvc):
    b, n, _ = q.shape
    lc = kvc.shape[1]
    has_local = kv is not None
    qrows = min(ATTN_QBLOCKS * BLOCK, n)
    assert n % qrows == 0
    in_specs = [pl.BlockSpec(memory_space=pltpu.SMEM), pl.BlockSpec((1, qrows, ATTN_WIDTH), lambda bi, i: (bi, i, 0))]
    args = [sink, q]
    if has_local:
        in_specs.append(pl.BlockSpec((1, n, 2 * KV_WIDTH), lambda bi, i: (bi, 0, 0)))
        args.append(kv)
    in_specs.append(pl.BlockSpec((1, lc, 2 * KV_WIDTH), lambda bi, i: (bi, 0, 0)))
    args.append(kvc)
    if has_local:
        nb = n // BLOCK
        assert nb >= 4
        in_specs.append(pl.BlockSpec((3, BLOCK, 3 * BLOCK), lambda bi, i: (0, 0, 0)))
        args.append(_window_bias(n))
    return pl.pallas_call(
        functools.partial(_attn_kernel, n=n, has_local=has_local),
        grid=(b, n // qrows),
        in_specs=in_specs,
        out_specs=pl.BlockSpec((1, qrows, ATTN_WIDTH), lambda bi, i: (bi, i, 0)),
        out_shape=jax.ShapeDtypeStruct((b, n, ATTN_WIDTH), MXU_DTYPE),
        compiler_params=_params("arbitrary", "arbitrary"),
        name="attn_local" if has_local else "attn_ctx",
    )(*args)


def _mix_out_kernel(*refs, conv, first, last):
    if conv:
        (att_ref, gb_ref, cu_ref, cup_ref, cun_ref, cw_ref, cb_ref,
         w_ref, x_ref, gate_ref, g2_ref, sh2_ref, sc2_ref, rwh_ref, rwl_ref, xo_ref, h_ref, afft_ref) = refs
        t = pl.program_id(1)
        prev = jnp.where(t == first, 0.0, cup_ref[0])
        nxt = jnp.where(t == last, 0.0, cun_ref[0])
        ext = jnp.concatenate([prev, cu_ref[0], nxt], axis=0)
        cw = cw_ref[...]
    else:
        (y_ref, w_ref, x_ref, gate_ref, g2_ref, sh2_ref, sc2_ref, rwh_ref, rwl_ref, xo_ref, h_ref, afft_ref) = refs
    tile = x_ref.shape[1]
    rows = tile // MIX_OUT_SPLIT
    for s in range(MIX_OUT_SPLIT):
        r0 = s * rows
        rs = slice(r0, r0 + rows)
        if conv:
            cv = (cw[0:1] * ext[HALO - 1 + r0:HALO - 1 + r0 + rows] + cw[1:2] * ext[HALO + r0:HALO + r0 + rows]
                  + cw[2:3] * ext[HALO + 1 + r0:HALO + 1 + r0 + rows] + cb_ref[...])
            cat = jnp.concatenate([att_ref[0, rs, :], (gb_ref[0, rs, :] * cv).astype(MXU_DTYPE)], axis=1)
        else:
            cat = jnp.concatenate([y_ref[0, c, rs, :] for c in range(y_ref.shape[1])], axis=1).astype(MXU_DTYPE)
        y = jnp.dot(cat, w_ref[...], preferred_element_type=F32)
        x = x_ref[0, rs, :] + gate_ref[0] * y
        xo_ref[0, rs, :] = x
        h = _norm_mod(x, g2_ref[...], sh2_ref[0], sc2_ref[0])
        _store_token_tiles(h_ref, h, r0)
        h_hi, h_lo = _split_hi_lo(h)
        logits = jnp.dot(h_hi, rwh_ref[...], preferred_element_type=F32)
        logits += jnp.dot(h_hi, rwl_ref[...], preferred_element_type=F32)
        logits += jnp.dot(h_lo, rwh_ref[...], preferred_element_type=F32)
        lane = lax.broadcasted_iota(jnp.int32, logits.shape, 1)
        logits = jnp.where(lane < N_EXPERTS, logits, NEG_INF)
        e = jnp.exp(logits - jnp.max(logits, axis=1, keepdims=True))
        aff = e / jnp.sum(e, axis=1, keepdims=True)
        afft_ref[0, :, rs] = aff.T[:N_EXPERTS]


def _mix_out(mix_in, conv_params, w, x, gate, g2, sh2, sc2, mod_row, rw_hi, rw_lo, tile):
    b, n, d = x.shape
    nt = n // tile
    conv = conv_params is not None
    row = (lambda bi: bi) if mod_row is None else (lambda bi: mod_row)
    tok = lambda width: pl.BlockSpec((1, tile, width), lambda bi, t: (bi, t, 0))
    modspec = pl.BlockSpec((1, 1, d), lambda bi, t: (row(bi), 0, 0))
    full = lambda a: pl.BlockSpec(a.shape, lambda bi, t: (0,) * a.ndim)
    if conv:
        att, gb, cu = mix_in
        cw, cb = conv_params
        per = tile // HALO
        in_specs = [tok(ATTN_WIDTH), tok(CONV_B_WIDTH), tok(CONV_B_WIDTH),
                    pl.BlockSpec((1, HALO, CONV_B_WIDTH), lambda bi, t: (bi, jnp.maximum(t * per - 1, 0), 0)),
                    pl.BlockSpec((1, HALO, CONV_B_WIDTH), lambda bi, t: (bi, jnp.minimum((t + 1) * per, n // HALO - 1), 0)),
                    full(cw), full(cb)]
        args = [att, gb, cu, cu, cu, cw, cb]
    else:
        in_specs = [_slab_spec(tile, lambda bi, t: (bi, 0, t, 0), d)]
        args = [mix_in]
    in_specs += [full(w), tok(d), modspec, full(g2), modspec, modspec, full(rw_hi), full(rw_lo)]
    args += [w, x, gate, g2, sh2, sc2, rw_hi, rw_lo]
    out_shape = (jax.ShapeDtypeStruct((b, n, d), F32), jax.ShapeDtypeStruct((b, n * SUBLANES, LANES), F32),
                 jax.ShapeDtypeStruct((b, N_EXPERTS, n), F32))
    out_specs = (tok(d), pl.BlockSpec((1, tile * SUBLANES, LANES), lambda bi, t: (bi, t, 0)),
                 pl.BlockSpec((1, N_EXPERTS, tile), lambda bi, t: (bi, 0, t)))
    return pl.pallas_call(
        functools.partial(_mix_out_kernel, conv=conv, first=0, last=nt - 1),
        grid=(b, nt),
        in_specs=in_specs,
        out_specs=out_specs,
        out_shape=out_shape,
        compiler_params=_params("arbitrary", "arbitrary"),
        name="even_out" if conv else "odd_out",
    )(*args)


def _cumsum_lanes(x):
    n = x.shape[1]
    r = lax.broadcasted_iota(jnp.int32, (CUMSUM_TILE, CUMSUM_TILE), 0)
    c = lax.broadcasted_iota(jnp.int32, (CUMSUM_TILE, CUMSUM_TILE), 1)
    tri = jnp.where(r <= c, 1.0, 0.0).astype(MXU_DTYPE)
    carry = jnp.zeros((x.shape[0], 1), F32)
    outs = []
    for k in range(n // CUMSUM_TILE):
        blk = x[:, k * CUMSUM_TILE:(k + 1) * CUMSUM_TILE].astype(MXU_DTYPE)
        loc = jnp.dot(blk, tri, preferred_element_type=F32) + carry
        outs.append(loc)
        carry = loc[:, CUMSUM_TILE - 1:CUMSUM_TILE]
    return jnp.concatenate(outs, axis=1)


SLOT_EMPTY = 1 << 20


def _select_kernel(at_ref, idx_ref, g_ref, *, cap):
    bblk, n_e, n = at_ref.shape
    at = at_ref[...].reshape(bblk * n_e, n)
    n_exp = bblk * n_e
    capf = float(cap)

    def count_ge(thr):
        return jnp.sum(jnp.where(at >= thr, 1.0, 0.0), axis=1, keepdims=True)

    def bit_body(_, c):
        lo_i, hi_i = c
        mid = lo_i + ((hi_i - lo_i) >> 1)
        ge = count_ge(lax.bitcast_convert_type(mid, F32)) >= capf
        return jnp.where(ge, mid, lo_i), jnp.where(ge, hi_i, mid)

    lo_i, hi_i = lax.fori_loop(
        0, 31, bit_body, (jnp.zeros((n_exp, 1), jnp.int32), jnp.full((n_exp, 1), 0x3F800001, jnp.int32)))

    def val_body(_, c):
        lo, hi = c
        mid = 0.5 * (lo + hi)
        ge = count_ge(mid) >= capf
        return jnp.where(ge, mid, lo), jnp.where(ge, hi, mid)

    lo, hi = lax.fori_loop(
        0, 24, val_body, (lax.bitcast_convert_type(lo_i, F32), lax.bitcast_convert_type(hi_i, F32)))
    need = capf - count_ge(hi)

    above = jnp.where(at >= hi, 1.0, 0.0)
    band = jnp.where(at >= lo, 1.0, 0.0) - above
    sel = above + band * jnp.where(_cumsum_lanes(band) <= need, 1.0, 0.0)
    rank = _cumsum_lanes(sel)

    lane = lax.broadcasted_iota(jnp.int32, (n_exp, n), 1)
    disp = jnp.where(sel > 0.5, lane + 1 - rank.astype(jnp.int32), SLOT_EMPTY)
    g = at
    for k in range(n.bit_length() - 1):
        step = 1 << k
        moving = ((disp >> k) & 1) == 1
        disp_in = pltpu.roll(disp, n - step, 1)
        arriving = ((disp_in >> k) & 1) == 1
        g = jnp.where(arriving, pltpu.roll(g, n - step, 1), g)
        disp = jnp.where(arriving, disp_in, jnp.where(moving, SLOT_EMPTY, disp))

    capp = pl.cdiv(cap, LANES) * LANES
    slot = lax.broadcasted_iota(jnp.int32, (n_exp, cap), 1)
    idx_ref[...] = jnp.clip(slot + disp[:, :cap], 0, n - 1).reshape(bblk, n_e, cap)
    for s in range(bblk):
        g_pad = jnp.concatenate([g[s * n_e:(s + 1) * n_e, :capp], jnp.zeros((LANES - n_e, capp), F32)], axis=0)
        g_ref[s] = g_pad.T[:cap]


def _select(aff_t, cap):
    b, n_exp, n = aff_t.shape
    assert n & (n - 1) == 0 and n % CUMSUM_TILE == 0 and n < SLOT_EMPTY
    bblk = b
    return pl.pallas_call(
        functools.partial(_select_kernel, cap=cap),
        grid=(b // bblk,),
        in_specs=[pl.BlockSpec((bblk, n_exp, n), lambda bi: (bi, 0, 0))],
        out_specs=(pl.BlockSpec((bblk, n_exp, cap), lambda bi: (bi, 0, 0)),
                   pl.BlockSpec((bblk, cap, LANES), lambda bi: (bi, 0, 0))),
        out_shape=(jax.ShapeDtypeStruct((b, n_exp, cap), jnp.int32), jax.ShapeDtypeStruct((b, cap, LANES), F32)),
        compiler_params=_params("arbitrary"),
        name="moe_select",
    )(aff_t)


def _gather_kernel(idx_ref, src_ref, xe_ref, xcm_ref, *, cap, n_exp):
    bblk = src_ref.shape[0]
    eblk, out_rows = xe_ref.shape[1], xe_ref.shape[2]
    stride = out_rows + SUBLANES
    for k in range(eblk):
        for s in range(bblk):
            base = ((pl.program_id(0) * bblk + s) * n_exp + pl.program_id(1) * eblk + k) * cap
            for r in range(cap):
                t = idx_ref[base + r]
                xcm_ref[k, pl.ds(s * cap + r, SUBLANES, stride=stride), :] = (
                    src_ref[s, pl.ds(pl.multiple_of(t * SUBLANES, SUBLANES), SUBLANES), :])
        used = bblk * cap
        if used < out_rows:
            for j in range(SUBLANES):
                xcm_ref[k, pl.ds(j * stride + used, out_rows - used), :] = jnp.zeros((out_rows - used, LANES), F32)
        xe_ref[0, k] = jnp.concatenate(
            [xcm_ref[k, pl.ds(j * stride, out_rows), :] for j in range(SUBLANES)], axis=1).astype(xe_ref.dtype)


def _gather(idx_flat, src, cap, bblk, out_rows):
    b, rows, _ = src.shape
    d = SUBLANES * LANES
    assert cap % SUBLANES == 0 and bblk * cap <= out_rows and b % bblk == 0 and N_EXPERTS % DISPATCH_EXPERTS == 0
    return pl.pallas_call(
        functools.partial(_gather_kernel, cap=cap, n_exp=N_EXPERTS),
        grid=(b // bblk, N_EXPERTS // DISPATCH_EXPERTS),
        in_specs=[pl.BlockSpec(memory_space=pltpu.SMEM),
                  pl.BlockSpec((bblk, rows, LANES), lambda bi, ei: (bi, 0, 0))],
        out_specs=pl.BlockSpec((1, DISPATCH_EXPERTS, out_rows, d), lambda bi, ei: (bi, ei, 0, 0)),
        out_shape=jax.ShapeDtypeStruct((b // bblk, N_EXPERTS, out_rows, d), MXU_DTYPE),
        scratch_shapes=[pltpu.VMEM((DISPATCH_EXPERTS, SUBLANES * (out_rows + SUBLANES), LANES), F32)],
        compiler_params=_params("arbitrary", "arbitrary"),
        name="moe_gather",
    )(idx_flat, src)


WEIGHT_CHUNKS = 2
WEIGHT_DMA_PRIORITY = 1


def _moe_ffn_kernel(*refs, layer, n_steps, has_ctx):
    if has_ctx:
        (x_ref, g_ref, xc_ref, gc_ref, wg_hbm, wu_hbm, wd_hbm, o_ref, oc_ref,
         wgu_s, wd_s, stage_in, stage_out, sem) = refs
    else:
        (x_ref, g_ref, wg_hbm, wu_hbm, wd_hbm, o_ref, wgu_s, wd_s, stage_in, stage_out, sem) = refs
    n_exp = pl.num_programs(0)
    e = pl.program_id(0)
    s = pl.program_id(1)
    slot = e % 2
    ff = wg_hbm.shape[3]
    chunks = []
    for hbm, dst, col0, stage in ((wg_hbm, wgu_s, 0, stage_in), (wu_hbm, wgu_s, ff, stage_in),
                                  (wd_hbm, wd_s, 0, stage_out)):
        n_rows = hbm.shape[2] // WEIGHT_CHUNKS
        chunks += [(hbm, dst, col0, stage, c * n_rows, n_rows) for c in range(WEIGHT_CHUNKS)]
    per_step = pl.cdiv(len(chunks), n_steps)

    def copy(c, expert):
        hbm, _, _, stage, r0, n_rows = chunks[c]
        return pltpu.make_async_copy(hbm.at[layer, expert, pl.ds(r0, n_rows), :], stage, sem.at[0])

    def cast(c, to_slot):
        hbm, dst, col0, stage, r0, n_rows = chunks[c]
        dst[to_slot, pl.ds(r0, n_rows), pl.ds(col0, hbm.shape[3])] = stage[...].astype(dst.dtype)

    @pl.when((e == 0) & (s == 0))
    def _():
        for c in range(len(chunks)):
            copy(c, 0).start()
            copy(c, 0).wait()
            cast(c, 0)

    has_next = e + 1 < n_exp
    for c in range(len(chunks)):
        if c % per_step == 0:
            @pl.when(has_next & (s == c // per_step))
            def _():
                copy(c, e + 1).start(priority=WEIGHT_DMA_PRIORITY)

    def ffn(xr, gr, outr):
        x, aff = xr[0, 0], gr[0]
        rows = x.shape[0]
        au = jnp.dot(x, wgu_s[slot], preferred_element_type=F32)
        a, u = au[:, :ff], au[:, ff:]
        mid = (a * _sigmoid(a) * u).astype(MXU_DTYPE)
        y = jnp.dot(mid, wd_s[slot], preferred_element_type=F32)
        lane = lax.broadcasted_iota(jnp.int32, aff.shape, 1)
        y = y * jnp.sum(jnp.where(lane == e, aff, 0.0), axis=1, keepdims=True)
        stride = rows + SUBLANES
        for j in range(SUBLANES):
            outr[0, 0, pl.ds(j * stride, rows), :] = y[:, j * LANES:(j + 1) * LANES]
            outr[0, 0, pl.ds(j * stride + rows, SUBLANES), :] = jnp.zeros((SUBLANES, LANES), F32)

    if has_ctx:
        is_ctx = s == n_steps - 1
        pl.when(is_ctx)(lambda: ffn(xc_ref, gc_ref, oc_ref))
        pl.when(jnp.logical_not(is_ctx))(lambda: ffn(x_ref, g_ref, o_ref))
    else:
        pl.when(s < n_steps)(lambda: ffn(x_ref, g_ref, o_ref))

    for c in range(len(chunks)):
        @pl.when(has_next & (s == c // per_step))
        def _():
            copy(c, e + 1).wait()
            cast(c, 1 - slot)
            if (c + 1) % per_step != 0 and c + 1 < len(chunks):
                copy(c + 1, e + 1).start(priority=WEIGHT_DMA_PRIORITY)


def _moe_ffn(xe, ge, ctx_block, layer, wg, wu, wd):
    nb, e, rows, d = xe.shape
    ff = wg.shape[-1]
    assert d % WEIGHT_CHUNKS == 0 and ff % (WEIGHT_CHUNKS * SUBLANES) == 0 and ff % LANES == 0
    has_ctx = ctx_block is not None
    n_steps = nb + int(has_ctx)
    hbm = pl.BlockSpec(memory_space=pl.ANY)
    out_rows = SUBLANES * (rows + SUBLANES)
    blk = lambda bi: jnp.minimum(bi, nb - 1)
    in_specs = [pl.BlockSpec((1, 1, rows, d), lambda ei, bi: (blk(bi), ei, 0, 0)),
                pl.BlockSpec((1, rows, LANES), lambda ei, bi: (blk(bi), 0, 0))]
    args = [xe, ge]
    out_specs = [pl.BlockSpec((1, 1, out_rows, LANES), lambda ei, bi: (blk(bi), ei, 0, 0))]
    out_shape = [jax.ShapeDtypeStruct((nb, e, out_rows, LANES), F32)]
    if has_ctx:
        rows_c = ctx_block[0].shape[2]
        assert ctx_block[0].shape == (1, e, rows_c, d) and ctx_block[1].shape == (1, rows_c, LANES)
        out_rows_c = SUBLANES * (rows_c + SUBLANES)
        in_specs += [pl.BlockSpec((1, 1, rows_c, d), lambda ei, bi: (0, ei, 0, 0)),
                     pl.BlockSpec((1, rows_c, LANES), lambda ei, bi: (0, 0, 0))]
        args += list(ctx_block)
        out_specs.append(pl.BlockSpec((1, 1, out_rows_c, LANES), lambda ei, bi: (0, ei, 0, 0)))
        out_shape.append(jax.ShapeDtypeStruct((1, e, out_rows_c, LANES), F32))
    return pl.pallas_call(
        functools.partial(_moe_ffn_kernel, layer=layer, n_steps=n_steps, has_ctx=has_ctx),
        grid=(e, n_steps),
        in_specs=in_specs + [hbm, hbm, hbm],
        out_specs=tuple(out_specs),
        out_shape=tuple(out_shape),
        scratch_shapes=[pltpu.VMEM((2, d, 2 * ff), MXU_DTYPE), pltpu.VMEM((2, ff, d), MXU_DTYPE),
                        pltpu.VMEM((d // WEIGHT_CHUNKS, ff), F32), pltpu.VMEM((ff // WEIGHT_CHUNKS, d), F32),
                        pltpu.SemaphoreType.DMA((1,))],
        compiler_params=_params("arbitrary", "arbitrary"),
        name="moe_ffn",
    )(*args, wg, wu, wd)


COMBINE_UNROLL = 8


def _combine_kernel(idx_ref, y_ref, acc_ref, *, cap, n_exp):
    bblk = acc_ref.shape[0]
    eblk = y_ref.shape[1]
    ei = pl.program_id(1)
    stride = y_ref.shape[2] // SUBLANES

    @pl.when(ei == 0)
    def _():
        acc_ref[...] = jnp.zeros_like(acc_ref)

    for k in range(eblk):
        for s in range(bblk):
            base = ((pl.program_id(0) * bblk + s) * n_exp + ei * eblk + k) * cap
            for r0 in range(0, cap, COMBINE_UNROLL):
                toks = [pl.multiple_of(idx_ref[base + r0 + i] * SUBLANES, SUBLANES) for i in range(COMBINE_UNROLL)]
                new = [acc_ref[s, pl.ds(toks[i], SUBLANES), :]
                       + y_ref[0, k, pl.ds(s * cap + r0 + i, SUBLANES, stride=stride), :]
                       for i in range(COMBINE_UNROLL)]
                for i in range(COMBINE_UNROLL):
                    acc_ref[s, pl.ds(toks[i], SUBLANES), :] = new[i]


def _combine(idx_flat, y, b, n, cap, bblk):
    nb, n_exp, rows, _ = y.shape
    assert cap % COMBINE_UNROLL == 0 and nb * bblk == b and bblk * cap <= rows // SUBLANES - SUBLANES
    return pl.pallas_call(
        functools.partial(_combine_kernel, cap=cap, n_exp=n_exp),
        grid=(nb, n_exp // DISPATCH_EXPERTS),
        in_specs=[pl.BlockSpec(memory_space=pltpu.SMEM),
                  pl.BlockSpec((1, DISPATCH_EXPERTS, rows, LANES), lambda bi, ei: (bi, ei, 0, 0))],
        out_specs=pl.BlockSpec((bblk, n * SUBLANES, LANES), lambda bi, ei: (bi, 0, 0)),
        out_shape=jax.ShapeDtypeStruct((b, n * SUBLANES, LANES), F32),
        compiler_params=_params("arbitrary", "arbitrary"),
        name="moe_combine",
    )(idx_flat, y)


def _expert_choice(h_lat, aff_lat, h_ctx, aff_ctx, layer, wg, wu, wd):
    b, _, n = aff_lat.shape
    cap = max(1, CAPACITY_FACTOR * n // N_EXPERTS)
    idx, g = _select(aff_lat, cap)
    idx = idx.reshape(-1)
    xe = _gather(idx, h_lat, cap, 1, cap)
    if h_ctx is None:
        (y,) = _moe_ffn(xe, g, None, layer, wg, wu, wd)
        return _combine(idx, y, b, n, cap, 1), None
    lc = aff_ctx.shape[2]
    cap_c = max(1, CAPACITY_FACTOR * lc // N_EXPERTS)
    idx_c, g_c = _select(aff_ctx, cap_c)
    idx_c = idx_c.reshape(-1)
    xe_c = _gather(idx_c, h_ctx, cap_c, b, b * cap_c)
    g_c = g_c.reshape(1, b * cap_c, LANES)
    y, y_c = _moe_ffn(xe, g, (xe_c, g_c), layer, wg, wu, wd)
    return _combine(idx, y, b, n, cap, 1), _combine(idx_c, y_c, b, lc, cap_c, b)


def _odd_in_kernel(x_ref, moe_ref, gate_ref, g_ref, sh_ref, sc_ref, w_ref, xo_ref, gg_ref, u_ref):
    x = x_ref[0] + gate_ref[0] * _load_token_tiles(moe_ref)
    xo_ref[0] = x
    h = _norm_mod(x, g_ref[...], sh_ref[0], sc_ref[0]).astype(MXU_DTYPE)
    y = jnp.dot(h, w_ref[...], preferred_element_type=F32)
    gl = y[:, :LRU_WIDTH]
    _store_slabs(gg_ref, 0.25 * gl * (1.0 + jnp.tanh(0.7978845608028654 * (gl + 0.044715 * gl * gl * gl))))
    _store_slabs(u_ref, y[:, LRU_WIDTH:])


def _odd_in(x, moe, gate, g, shift, scale, mod_row, w, tile):
    b, n, d = x.shape
    row = (lambda bi: bi) if mod_row is None else (lambda bi: mod_row)
    tok = pl.BlockSpec((1, tile, d), lambda bi, t: (bi, t, 0))
    tiles = pl.BlockSpec((1, tile * SUBLANES, LANES), lambda bi, t: (bi, t, 0))
    modspec = pl.BlockSpec((1, 1, d), lambda bi, t: (row(bi), 0, 0))
    return pl.pallas_call(
        _odd_in_kernel,
        grid=(b, n // tile),
        in_specs=[tok, tiles, modspec, pl.BlockSpec((1, d), lambda bi, t: (0, 0)), modspec, modspec,
                  pl.BlockSpec(w.shape, lambda bi, t: (0, 0))],
        out_specs=(tok, _slab_spec(tile, lambda bi, t: (bi, 0, t, 0)), _slab_spec(tile, lambda bi, t: (bi, 0, t, 0))),
        out_shape=(jax.ShapeDtypeStruct((b, n, d), F32),) + (jax.ShapeDtypeStruct((b, d // LANES, n, LANES), F32),) * 2,
        compiler_params=_params("arbitrary", "arbitrary"),
        name="odd_in",
    )(x, moe, gate, g, shift, scale, w)


def _lru_kernel(*refs, reverse, nt):
    if reverse:
        (uc_ref, wax_ref, lam_ref, h0_ref, hf_ref, gg_ref, out_ref, hlast_ref, a_scr, b_scr, uc_scr, carry_scr) = refs
        n_slab, tile = uc_ref.shape[1], uc_ref.shape[2]
    else:
        (u_ref, up_ref, un_ref, cw_ref, cb_ref, wax_ref, lam_ref, h0_ref,
         out_ref, uco_ref, hlast_ref, a_scr, b_scr, uc_scr, carry_scr) = refs
        n_slab, tile = u_ref.shape[1], u_ref.shape[2]
    w = n_slab * LANES
    per = tile // SUBLANES
    rows = lambda j: slice(j * per, (j + 1) * per)
    phase = lambda ref, j: jnp.concatenate(
        [ref[0, c, pl.ds(j, per, stride=SUBLANES), :] for c in range(n_slab)], axis=1)
    phase_major = lambda ref, j: jnp.concatenate([ref[0, c, rows(j), :] for c in range(n_slab)], axis=1)

    def store_phase(ref, j, val, time_order):
        for c in range(n_slab):
            dst = pl.ds(j, per, stride=SUBLANES) if time_order else rows(j)
            ref[0, c, dst, :] = val[:, c * LANES:(c + 1) * LANES]

    t = pl.program_id(1)
    if reverse:
        for j in range(SUBLANES):
            uc_scr[rows(j), :] = phase_major(uc_ref, j)
    else:
        prev = jnp.where(t == 0, 0.0, _load_slabs(up_ref))
        nxt = jnp.where(t == nt - 1, 0.0, _load_slabs(un_ref))
        rowid = lax.broadcasted_iota(jnp.int32, (per, w), 0)
        shift_down = lambda x, first: jnp.where(rowid == 0, first, pltpu.roll(x, 1, 0))
        shift_up = lambda x, last: jnp.where(rowid == per - 1, last, pltpu.roll(x, per - 1, 0))
        u = [phase(u_ref, j) for j in range(SUBLANES)]
        um1 = [shift_down(u[7], prev[7:8])] + u[:7]
        um2 = [shift_down(u[6], prev[6:7]), um1[0]] + u[:6]
        up1 = u[1:] + [shift_up(u[0], nxt[0:1])]
        cw = cw_ref[...]
        for j in range(SUBLANES):
            uc_j = cb_ref[...] + cw[0:1] * um2[j] + cw[1:2] * um1[j] + cw[2:3] * u[j] + cw[3:4] * up1[j]
            uc_scr[rows(j), :] = uc_j
            store_phase(uco_ref, j, uc_j, False)

    lam = lam_ref[0]
    half_decay = (0.5 * LRU_C) * (jnp.maximum(-lam, 0.0) + jnp.log1p(jnp.exp(-jnp.abs(lam))))
    ones = jnp.where(lax.broadcasted_iota(jnp.int32, (tile, LRU_BLOCK), 1) < BIAS_TERMS, 1.0, 0.0).astype(MXU_DTYPE)
    for hd in range(LRU_HEADS):
        sl = slice(hd * LRU_BLOCK, (hd + 1) * LRU_BLOCK)
        uc = uc_scr[:, sl]
        z = jnp.dot(jnp.concatenate([uc.astype(MXU_DTYPE), ones], axis=1), wax_ref[0, hd], preferred_element_type=F32)
        hd_row = half_decay[:, sl]
        neg_log_a = hd_row * jnp.tanh(z[:, :LRU_BLOCK]) + hd_row
        gate2 = 1.0 + jnp.tanh(z[:, LRU_BLOCK:])
        a = jnp.exp(-neg_log_a)
        m2 = jnp.tanh(neg_log_a) * (a * a + 1.0)
        mult = jnp.where(m2 > 0.0, m2 * lax.rsqrt(m2), 0.0)
        a_scr[:, sl] = a
        b_scr[:, sl] = mult * (gate2 * uc)

    @pl.when(t == 0)
    def _():
        carry_scr[...] = jnp.broadcast_to(h0_ref[0], carry_scr.shape)

    order = list(range(SUBLANES))[::-1] if reverse else list(range(SUBLANES))
    hrun = b_scr[rows(order[0]), :]
    prun = a_scr[rows(order[0]), :]
    for j in order[1:]:
        aj = a_scr[rows(j), :]
        hrun = aj * hrun + b_scr[rows(j), :]
        prun = aj * prun
        b_scr[rows(j), :] = hrun
        a_scr[rows(j), :] = prun

    lane_row = lax.broadcasted_iota(jnp.int32, (SUBLANES, w), 0)
    carry = carry_scr[...]
    groups = list(range(per // SUBLANES))
    entering = [None] * len(groups)
    for m in (groups[::-1] if reverse else groups):
        a = prun[m * SUBLANES:(m + 1) * SUBLANES]
        bcoef = hrun[m * SUBLANES:(m + 1) * SUBLANES]
        for dist in (1, 2, 4):
            shift = (SUBLANES - dist) if reverse else dist
            msk = (lane_row < SUBLANES - dist) if reverse else (lane_row >= dist)
            a_s = pltpu.roll(a, shift, 0)
            b_s = pltpu.roll(bcoef, shift, 0)
            bcoef = jnp.where(msk, a * b_s + bcoef, bcoef)
            a = jnp.where(msk, a * a_s, a)
        after = a * carry + bcoef
        if reverse:
            entering[m] = jnp.where(lane_row == SUBLANES - 1, carry, pltpu.roll(after, SUBLANES - 1, 0))
            carry = jnp.broadcast_to(after[0:1], carry.shape)
        else:
            entering[m] = jnp.where(lane_row == 0, carry, pltpu.roll(after, 1, 0))
            carry = jnp.broadcast_to(after[SUBLANES - 1:SUBLANES], carry.shape)
    carry_scr[...] = carry
    hlast_ref[0] = carry[0:1]
    h_in = jnp.concatenate(entering, axis=0)

    for j in range(SUBLANES):
        hcur = b_scr[rows(j), :] + a_scr[rows(j), :] * h_in
        if reverse:
            hcur = phase(gg_ref, j) * (phase_major(hf_ref, j) + hcur)
        store_phase(out_ref, j, hcur, reverse)


BIAS_TERMS = 3


def _gate_weights(wa, wx, ba, bx):
    n_dir, heads, blk, _ = wa.shape
    w = 0.5 * jnp.concatenate([wa, wx], axis=-1)
    bias = 0.5 * jnp.concatenate([ba.reshape(n_dir, heads, blk), bx.reshape(n_dir, heads, blk)], axis=-1)
    terms = []
    for _ in range(BIAS_TERMS):
        term = bias.astype(MXU_DTYPE)
        terms.append(term)
        bias = bias - term.astype(F32)
    rows = jnp.stack(terms, axis=2)
    pad = jnp.zeros((n_dir, heads, blk - BIAS_TERMS, 2 * blk), MXU_DTYPE)
    return jnp.concatenate([w.astype(MXU_DTYPE), rows, pad], axis=2)


def _lru_fwd(u, cw, cb, wax, lam, h0, tile):
    b, n_slab, n, _ = u.shape
    w = n_slab * LANES
    nt = n // tile
    per = tile // HALO
    assert tile % (SUBLANES * SUBLANES) == 0
    tok = _slab_spec(tile, lambda bi, t: (bi, 0, t, 0), w)
    state = pl.BlockSpec((1, 1, w), lambda bi, t: (bi, 0, 0))
    return pl.pallas_call(
        functools.partial(_lru_kernel, reverse=False, nt=nt),
        grid=(b, nt),
        in_specs=[tok,
                  _slab_spec(HALO, lambda bi, t: (bi, 0, jnp.maximum(t * per - 1, 0), 0), w),
                  _slab_spec(HALO, lambda bi, t: (bi, 0, jnp.minimum((t + 1) * per, n // HALO - 1), 0), w),
                  pl.BlockSpec(cw.shape, lambda bi, t: (0, 0)),
                  pl.BlockSpec(cb.shape, lambda bi, t: (0, 0)),
                  pl.BlockSpec((1,) + wax.shape[1:], lambda bi, t: (0, 0, 0, 0)),
                  pl.BlockSpec((1, 1, w), lambda bi, t: (0, 0, 0)),
                  state],
        out_specs=(tok, tok, state),
        out_shape=(jax.ShapeDtypeStruct(u.shape, F32), jax.ShapeDtypeStruct(u.shape, F32),
                   jax.ShapeDtypeStruct((b, 1, w), F32)),
        scratch_shapes=[pltpu.VMEM((tile, w), F32), pltpu.VMEM((tile, w), F32), pltpu.VMEM((tile, w), F32),
                        pltpu.VMEM((SUBLANES, w), F32)],
        compiler_params=_params("arbitrary", "arbitrary"),
        name="lru_fwd",
    )(u, u, u, cw, cb, wax, lam, h0)


def _lru_bwd(uc, wax, lam, h0, hf, gg, tile):
    b, n_slab, n, _ = uc.shape
    w = n_slab * LANES
    nt = n // tile
    assert tile % (SUBLANES * SUBLANES) == 0
    tok = _slab_spec(tile, lambda bi, t: (bi, 0, nt - 1 - t, 0), w)
    state = pl.BlockSpec((1, 1, w), lambda bi, t: (bi, 0, 0))
    return pl.pallas_call(
        functools.partial(_lru_kernel, reverse=True, nt=nt),
        grid=(b, nt),
        in_specs=[tok,
                  pl.BlockSpec((1,) + wax.shape[1:], lambda bi, t: (1, 0, 0, 0)),
                  pl.BlockSpec((1, 1, w), lambda bi, t: (1, 0, 0)),
                  state, tok, tok],
        out_specs=(tok, state),
        out_shape=(jax.ShapeDtypeStruct(uc.shape, F32), jax.ShapeDtypeStruct((b, 1, w), F32)),
        scratch_shapes=[pltpu.VMEM((tile, w), F32), pltpu.VMEM((tile, w), F32), pltpu.VMEM((tile, w), F32),
                        pltpu.VMEM((SUBLANES, w), F32)],
        compiler_params=_params("arbitrary", "arbitrary"),
        name="lru_bwd",
    )(uc, wax, lam, h0, hf, gg)


def _final_kernel(x_ref, moe_ref, gate_ref, g_ref, o_ref):
    x = x_ref[0] + gate_ref[0] * _load_token_tiles(moe_ref)
    o_ref[0] = x * lax.rsqrt(jnp.mean(x * x, axis=-1, keepdims=True) + EPS) * g_ref[...]


def _final(x, moe, gate, g, tile):
    b, n, d = x.shape
    tok = pl.BlockSpec((1, tile, d), lambda bi, t: (bi, t, 0))
    tiles = pl.BlockSpec((1, tile * SUBLANES, LANES), lambda bi, t: (bi, t, 0))
    return pl.pallas_call(
        _final_kernel,
        grid=(b, n // tile),
        in_specs=[tok, tiles, pl.BlockSpec((1, 1, d), lambda bi, t: (bi, 0, 0)), pl.BlockSpec((1, d), lambda bi, t: (0, 0))],
        out_specs=tok,
        out_shape=jax.ShapeDtypeStruct((b, n, d), F32),
        compiler_params=_params("arbitrary", "arbitrary"),
        name="final_norm",
    )(x, moe, gate, g)


def kernel(x, c, ctx, c_ctx, ada_w, ada_b, norm_mix_g, norm_ffn_g, ev_w_in, ev_w_out, ev_sink, ev_conv_w, ev_conv_b, od_w_in, od_w_out, od_conv_w, od_conv_b, od_wa, od_ba, od_wx, od_bx, od_lambda, router_w, w_gate, w_up, w_down, final_g):
    b, n, d = x.shape
    lc = ctx.shape[1]
    depth = ada_w.shape[0]
    assert depth == 2 and d == D_MODEL and b < MOD_ROWS
    tile_l = min(512, n)
    tile_w = min(1024, n)
    tile_c = lc
    ctx_row = b

    cvec = jnp.zeros((MOD_ROWS, d), F32).at[:b].set(c).at[b].set(c_ctx)
    mods = _ada(cvec, ada_w, ada_b).reshape(depth, MOD_ROWS, 6, 1, d)
    mod = lambda l, j: mods[l, :, j]

    def router_split(l):
        return _split_hi_lo(jnp.pad(router_w[l], ((0, 0), (0, LANES - N_EXPERTS))))

    bf = lambda a: a.astype(MXU_DTYPE)

    g_mix = norm_mix_g[0].reshape(1, d)
    g_ffn = norm_ffn_g[0].reshape(1, d)
    w_in = bf(ev_w_in[0])
    w_out = bf(ev_w_out[0])
    conv_p = (ev_conv_w[0], ev_conv_b[0].reshape(1, -1))
    tables = _rope_tables(n)
    ql, kvl, gbl, cul = _even_in(x, g_mix, mod(0, 0), mod(0, 1), None, w_in, tables, tile_l)
    qc, kvc, gbc, cuc = _even_in(ctx, g_mix, mod(0, 0), mod(0, 1), ctx_row, w_in, None, tile_c)
    att_l = _attention(ev_sink[0], ql, kvl, kvc)
    att_c = _attention(ev_sink[0], qc, None, kvc)
    rw_hi, rw_lo = router_split(0)
    xl, hl, at_l = _mix_out((att_l, gbl, cul), conv_p, w_out, x, mod(0, 2), g_ffn, mod(0, 3), mod(0, 4), None,
                            rw_hi, rw_lo, tile_w)
    xc, hc, at_c = _mix_out((att_c, gbc, cuc), conv_p, w_out, ctx, mod(0, 2), g_ffn, mod(0, 3), mod(0, 4),
                            ctx_row, rw_hi, rw_lo, tile_c)
    moe_l, moe_c = _expert_choice(hl, at_l, hc, at_c, 0, w_gate, w_up, w_down)

    g_mix = norm_mix_g[1].reshape(1, d)
    g_ffn = norm_ffn_g[1].reshape(1, d)
    w_in = bf(od_w_in[0])
    w_out = bf(od_w_out[0])
    xl, ggl, ul = _odd_in(xl, moe_l, mod(0, 5), g_mix, mod(1, 0), mod(1, 1), None, w_in, tile_l)
    _, _, u_ctx = _odd_in(xc, moe_c, mod(0, 5), g_mix, mod(1, 0), mod(1, 1), ctx_row, w_in, tile_c)
    cw, cb = od_conv_w[0], od_conv_b[0].reshape(1, -1)
    wax = _gate_weights(od_wa[0], od_wx[0], od_ba[0], od_bx[0])
    lam = od_lambda[0].reshape(2, 1, -1)
    zero_state = jnp.zeros((b, 1, LRU_WIDTH), F32)
    hf_c, uc_c, h0_f = _lru_fwd(u_ctx, cw, cb, wax, lam, zero_state, tile_c)
    _, h0_b = _lru_bwd(uc_c, wax, lam, zero_state, hf_c, hf_c, tile_c)
    hf_l, uc_l, _ = _lru_fwd(ul, cw, cb, wax, lam, h0_f, tile_w)
    yl, _ = _lru_bwd(uc_l, wax, lam, h0_b, hf_l, ggl, tile_w)
    rw_hi, rw_lo = router_split(1)
    xl, hl, at_l = _mix_out(yl, None, w_out, xl, mod(1, 2), g_ffn, mod(1, 3), mod(1, 4), None,
                            rw_hi, rw_lo, tile_w)
    moe_l, _ = _expert_choice(hl, at_l, None, None, 1, w_gate, w_up, w_down)
    return _final(xl, moe_l, mod(1, 5), final_g.reshape(1, d), tile_w)
```

```python
import functools

import jax
import jax.numpy as jnp
from jax import lax
from jax.experimental import pallas as pl
from jax.experimental.pallas import tpu as pltpu

F32 = jnp.float32
MXU_DTYPE = jnp.bfloat16

D_MODEL = 1024
GRID_W = 64
EPS = 1e-6
NEG_INF = -1e30
HEAD_DIM = 64
N_Q_HEADS = 8
N_KV_HEADS = 2
Q_PER_KV = N_Q_HEADS // N_KV_HEADS
ATTN_WIDTH = N_Q_HEADS * HEAD_DIM
KV_WIDTH = N_KV_HEADS * HEAD_DIM
WINDOW = 128
BLOCK = 128
ROPE_BASE = 10000.0
CONV_B_WIDTH = D_MODEL // 2
EVEN_IN = ATTN_WIDTH + 2 * KV_WIDTH + 3 * CONV_B_WIDTH
LRU_WIDTH = D_MODEL
LRU_HEADS = 8
LRU_BLOCK = LRU_WIDTH // LRU_HEADS
LRU_C = 8.0
N_EXPERTS = 16
CAPACITY_FACTOR = 2
EXPERT_FF = 1408
MOD_ROWS = 16
LANES = 128
SUBLANES = 8
HALO = SUBLANES
CUMSUM_TILE = 256
MIX_OUT_ROWS = 256
ATTN_QBLOCKS = 4
DISPATCH_EXPERTS = 4
V7X_VMEM_LIMIT = 56 * 1024 * 1024


def _params(*sem):
    return pltpu.CompilerParams(dimension_semantics=sem, vmem_limit_bytes=V7X_VMEM_LIMIT)


def _split_hi_lo(a):
    hi = a.astype(MXU_DTYPE)
    lo = (a - hi.astype(F32)).astype(MXU_DTYPE)
    return hi, lo


def _sigmoid(z):
    return 0.5 * (1.0 + jnp.tanh(0.5 * z))


def _store_token_tiles(ref, val, row0=0):
    tile = val.shape[0]
    for j in range(SUBLANES):
        ref[0, pl.ds(row0 * SUBLANES + j, tile, stride=SUBLANES), :] = val[:, j * LANES:(j + 1) * LANES]


def _load_token_tiles(ref):
    tile = ref.shape[1] // SUBLANES
    return jnp.concatenate([ref[0, pl.ds(j, tile, stride=SUBLANES), :] for j in range(SUBLANES)], axis=1)


def _slab_spec(rows, index_map, width=LRU_WIDTH):
    return pl.BlockSpec((1, width // LANES, rows, LANES), index_map)


def _store_slabs(ref, val):
    for c in range(val.shape[1] // LANES):
        ref[0, c] = val[:, c * LANES:(c + 1) * LANES]


def _load_slabs(ref):
    return jnp.concatenate([ref[0, c] for c in range(ref.shape[1])], axis=1)


def _norm_mod(x, g, shift, scale):
    y = x * lax.rsqrt(jnp.mean(x * x, axis=-1, keepdims=True) + EPS) * g
    return y * (1.0 + scale) + shift


def _ada_kernel(c_ref, w_ref, b_ref, o_ref):
    c = c_ref[...]
    s_hi, s_lo = _split_hi_lo(c * _sigmoid(c))
    w_hi, w_lo = _split_hi_lo(w_ref[0])
    acc = jnp.dot(s_hi, w_hi, preferred_element_type=F32)
    acc += jnp.dot(s_hi, w_lo, preferred_element_type=F32)
    acc += jnp.dot(s_lo, w_hi, preferred_element_type=F32)
    o_ref[0] = acc + b_ref[0]


def _ada(cvec, ada_w, ada_b):
    depth, d, n6 = ada_w.shape
    tn = 1536
    return pl.pallas_call(
        _ada_kernel,
        grid=(depth, n6 // tn),
        in_specs=[
            pl.BlockSpec((MOD_ROWS, d), lambda l, j: (0, 0)),
            pl.BlockSpec((1, d, tn), lambda l, j: (l, 0, j)),
            pl.BlockSpec((1, 1, tn), lambda l, j: (l, 0, j)),
        ],
        out_specs=pl.BlockSpec((1, MOD_ROWS, tn), lambda l, j: (l, 0, j)),
        out_shape=jax.ShapeDtypeStruct((depth, MOD_ROWS, n6), F32),
        compiler_params=_params("arbitrary", "arbitrary"),
        name="ada",
    )(cvec, ada_w, ada_b.reshape(depth, 1, n6))


def _even_in_kernel(*refs, rope):
    if rope:
        (x_ref, g_ref, sh_ref, sc_ref, w_ref, cos_ref, sa_ref, sb_ref, q_ref, kv_ref, gb_ref, cu_ref) = refs
    else:
        (x_ref, g_ref, sh_ref, sc_ref, w_ref, q_ref, kv_ref, gb_ref, cu_ref) = refs
    h = _norm_mod(x_ref[0], g_ref[...], sh_ref[0], sc_ref[0]).astype(MXU_DTYPE)
    y = jnp.dot(h, w_ref[...], preferred_element_type=F32)
    q = y[:, :ATTN_WIDTH]
    k = y[:, ATTN_WIDTH:ATTN_WIDTH + KV_WIDTH]
    v = y[:, ATTN_WIDTH + KV_WIDTH:ATTN_WIDTH + 2 * KV_WIDTH]
    c0 = ATTN_WIDTH + 2 * KV_WIDTH
    if rope:
        cos, sa, sb = cos_ref[...], sa_ref[...], sb_ref[...]

        def rot(z):
            return z * cos + pltpu.roll(z, 16, 1) * sa + pltpu.roll(z, 112, 1) * sb

        q = jnp.concatenate([rot(q[:, j * 128:(j + 1) * 128]) for j in range(ATTN_WIDTH // 128)], axis=1)
        k = rot(k)
    q_ref[0] = (q * (HEAD_DIM ** -0.5)).astype(q_ref.dtype)
    kv_ref[0] = jnp.concatenate([k, v], axis=1).astype(kv_ref.dtype)
    gb_ref[0] = y[:, c0:c0 + CONV_B_WIDTH]
    cu_ref[0] = y[:, c0 + CONV_B_WIDTH:c0 + 2 * CONV_B_WIDTH] * y[:, c0 + 2 * CONV_B_WIDTH:]


def _even_in(x, g, shift, scale, mod_row, w, tables, tile):
    b, n, d = x.shape
    nt = n // tile
    rope = tables is not None
    row = (lambda bi: bi) if mod_row is None else (lambda bi: mod_row)
    in_specs = [
        pl.BlockSpec((1, tile, d), lambda bi, t: (bi, t, 0)),
        pl.BlockSpec((1, d), lambda bi, t: (0, 0)),
        pl.BlockSpec((1, 1, d), lambda bi, t: (row(bi), 0, 0)),
        pl.BlockSpec((1, 1, d), lambda bi, t: (row(bi), 0, 0)),
        pl.BlockSpec(w.shape, lambda bi, t: (0, 0)),
    ]
    args = [x, g, shift, scale, w]
    if rope:
        in_specs += [pl.BlockSpec((tile, 128), lambda bi, t: (t, 0))] * 3
        args += list(tables)
    out_shape = (
        jax.ShapeDtypeStruct((b, n, ATTN_WIDTH), MXU_DTYPE),
        jax.ShapeDtypeStruct((b, n, 2 * KV_WIDTH), MXU_DTYPE),
        jax.ShapeDtypeStruct((b, n, CONV_B_WIDTH), F32),
        jax.ShapeDtypeStruct((b, n, CONV_B_WIDTH), F32),
    )
    out_specs = tuple(pl.BlockSpec((1, tile, s.shape[-1]), lambda bi, t: (bi, t, 0)) for s in out_shape)
    return pl.pallas_call(
        functools.partial(_even_in_kernel, rope=rope),
        grid=(b, nt),
        in_specs=in_specs,
        out_specs=out_specs,
        out_shape=out_shape,
        compiler_params=_params("arbitrary", "arbitrary"),
        name="even_in_rope" if rope else "even_in",
    )(*args)


def _rope_tables(n):
    nf = HEAD_DIM // 4
    pos = jnp.arange(n)
    rows = (pos // GRID_W).astype(F32)
    cols = (pos % GRID_W).astype(F32)
    lane = jnp.arange(128)
    inv = ROPE_BASE ** (-(lane % nf).astype(F32) / nf)
    use_col = (lane % HEAD_DIM) >= HEAD_DIM // 2
    ang = jnp.where(use_col[None, :], cols[:, None], rows[:, None]) * inv[None, :]
    cos, sin = jnp.cos(ang), jnp.sin(ang)
    second = ((lane % (2 * nf)) >= nf)[None, :]
    return cos, jnp.where(second, sin, 0.0), jnp.where(second, 0.0, -sin)


def _attn_kernel(*refs, n, has_local):
    if has_local:
        sink_ref, q_ref, kv_ref, kvc_ref, bias_ref, o_ref = refs
    else:
        sink_ref, q_ref, kvc_ref, o_ref = refs
    n_loc = 3 * BLOCK
    nb = n // BLOCK
    grp = lax.broadcasted_iota(jnp.int32, (Q_PER_KV * BLOCK, 1), 0) // BLOCK
    for sb in range(q_ref.shape[1] // BLOCK):
        i = pl.program_id(1) * (q_ref.shape[1] // BLOCK) + sb
        q = q_ref[0, sb * BLOCK:(sb + 1) * BLOCK, :]
        kvall = kvc_ref[0]
        if has_local:
            start = pl.multiple_of(_local_start(i, n), BLOCK)
            kvall = jnp.concatenate([kv_ref[0, pl.ds(start, n_loc), :], kvall], axis=0)
            case = jnp.where(i == 0, 0, jnp.where(i == nb - 1, 2, 1))
            bias = jnp.concatenate([bias_ref[case]] * Q_PER_KV, axis=0)
        outs = []
        for hk in range(N_KV_HEADS):
            kh = kvall[:, hk * HEAD_DIM:(hk + 1) * HEAD_DIM]
            vh = kvall[:, KV_WIDTH + hk * HEAD_DIM:KV_WIDTH + (hk + 1) * HEAD_DIM]
            qg = jnp.concatenate(
                [q[:, (hk * Q_PER_KV + g) * HEAD_DIM:(hk * Q_PER_KV + g + 1) * HEAD_DIM] for g in range(Q_PER_KV)],
                axis=0)
            s = lax.dot_general(qg, kh, (((1,), (1,)), ((), ())), preferred_element_type=F32)
            if has_local:
                s = jnp.concatenate([s[:, :n_loc] + bias, s[:, n_loc:]], axis=1)
            snk = jnp.zeros((Q_PER_KV * BLOCK, 1), F32)
            for g in range(Q_PER_KV):
                snk = jnp.where(grp == g, sink_ref[hk * Q_PER_KV + g], snk)
            m = jnp.maximum(jnp.max(s, axis=1, keepdims=True), snk)
            p = jnp.exp(s - m).astype(MXU_DTYPE)
            v_ones = jnp.concatenate([vh, jnp.ones_like(vh)], axis=1)
            ov = jnp.dot(p, v_ones, preferred_element_type=F32)
            o = ov[:, :HEAD_DIM] / (ov[:, HEAD_DIM:HEAD_DIM + 1] + jnp.exp(snk - m))
            outs += [o[g * BLOCK:(g + 1) * BLOCK] for g in range(Q_PER_KV)]
        o_ref[0, sb * BLOCK:(sb + 1) * BLOCK, :] = jnp.concatenate(outs, axis=1).astype(o_ref.dtype)


def _local_start(i, n):
    return jnp.clip(i * BLOCK - BLOCK, 0, n - 3 * BLOCK)


def _window_bias(n):
    nb = n // BLOCK
    r = jnp.arange(BLOCK)[:, None]
    c = jnp.arange(3 * BLOCK)[None, :]
    cases = []
    for i in (0, 1, nb - 1):
        diff = (_local_start(i, n) + c) - (i * BLOCK + r)
        cases.append(jnp.where(jnp.abs(diff) <= WINDOW, 0.0, NEG_INF).astype(F32))
    return jnp.stack(cases)


def _attention(sink, q, kv, kvc):
    b, n, _ = q.shape
    lc = kvc.shape[1]
    has_local = kv is not None
    qrows = min(ATTN_QBLOCKS * BLOCK, n)
    assert n % qrows == 0
    in_specs = [pl.BlockSpec(memory_space=pltpu.SMEM), pl.BlockSpec((1, qrows, ATTN_WIDTH), lambda bi, i: (bi, i, 0))]
    args = [sink, q]
    if has_local:
        in_specs.append(pl.BlockSpec((1, n, 2 * KV_WIDTH), lambda bi, i: (bi, 0, 0)))
        args.append(kv)
    in_specs.append(pl.BlockSpec((1, lc, 2 * KV_WIDTH), lambda bi, i: (bi, 0, 0)))
    args.append(kvc)
    if has_local:
        nb = n // BLOCK
        assert nb >= 4
        in_specs.append(pl.BlockSpec((3, BLOCK, 3 * BLOCK), lambda bi, i: (0, 0, 0)))
        args.append(_window_bias(n))
    return pl.pallas_call(
        functools.partial(_attn_kernel, n=n, has_local=has_local),
        grid=(b, n // qrows),
        in_specs=in_specs,
        out_specs=pl.BlockSpec((1, qrows, ATTN_WIDTH), lambda bi, i: (bi, i, 0)),
        out_shape=jax.ShapeDtypeStruct((b, n, ATTN_WIDTH), MXU_DTYPE),
        compiler_params=_params("arbitrary", "arbitrary"),
        name="attn_local" if has_local else "attn_ctx",
    )(*args)


def _mix_out_kernel(*refs, conv, first, last):
    if conv:
        (att_ref, gb_ref, cu_ref, cup_ref, cun_ref, cw_ref, cb_ref,
         w_ref, x_ref, gate_ref, g2_ref, sh2_ref, sc2_ref, rwh_ref, rwl_ref, xo_ref, h_ref, afft_ref) = refs
        t = pl.program_id(1)
        prev = jnp.where(t == first, 0.0, cup_ref[0])
        nxt = jnp.where(t == last, 0.0, cun_ref[0])
        ext = jnp.concatenate([prev, cu_ref[0], nxt], axis=0)
        cw = cw_ref[...]
    else:
        (y_ref, w_ref, x_ref, gate_ref, g2_ref, sh2_ref, sc2_ref, rwh_ref, rwl_ref, xo_ref, h_ref, afft_ref) = refs
    tile = x_ref.shape[1]
    rows = min(MIX_OUT_ROWS, tile)
    for s in range(tile // rows):
        r0 = s * rows
        rs = slice(r0, r0 + rows)
        if conv:
            cv = (cw[0:1] * ext[HALO - 1 + r0:HALO - 1 + r0 + rows] + cw[1:2] * ext[HALO + r0:HALO + r0 + rows]
                  + cw[2:3] * ext[HALO + 1 + r0:HALO + 1 + r0 + rows] + cb_ref[...])
            cat = jnp.concatenate([att_ref[0, rs, :], (gb_ref[0, rs, :] * cv).astype(MXU_DTYPE)], axis=1)
        else:
            cat = jnp.concatenate([y_ref[0, c, rs, :] for c in range(y_ref.shape[1])], axis=1).astype(MXU_DTYPE)
        y = jnp.dot(cat, w_ref[...], preferred_element_type=F32)
        x = x_ref[0, rs, :] + gate_ref[0] * y
        xo_ref[0, rs, :] = x
        h = _norm_mod(x, g2_ref[...], sh2_ref[0], sc2_ref[0])
        _store_token_tiles(h_ref, h, r0)
        h_hi, h_lo = _split_hi_lo(h)
        both = jnp.dot(h_hi, jnp.concatenate([rwh_ref[...], rwl_ref[...]], axis=1), preferred_element_type=F32)
        logits = both[:, :LANES] + both[:, LANES:] + jnp.dot(h_lo, rwh_ref[...], preferred_element_type=F32)
        lane = lax.broadcasted_iota(jnp.int32, logits.shape, 1)
        logits = jnp.where(lane < N_EXPERTS, logits, NEG_INF)
        e = jnp.exp(logits - jnp.max(logits, axis=1, keepdims=True))
        aff = e / jnp.sum(e, axis=1, keepdims=True)
        afft_ref[0, :, rs] = aff.T[:N_EXPERTS]


def _mix_out(mix_in, conv_params, w, x, gate, g2, sh2, sc2, mod_row, rw_hi, rw_lo, tile):
    b, n, d = x.shape
    nt = n // tile
    conv = conv_params is not None
    row = (lambda bi: bi) if mod_row is None else (lambda bi: mod_row)
    tok = lambda width: pl.BlockSpec((1, tile, width), lambda bi, t: (bi, t, 0))
    modspec = pl.BlockSpec((1, 1, d), lambda bi, t: (row(bi), 0, 0))
    full = lambda a: pl.BlockSpec(a.shape, lambda bi, t: (0,) * a.ndim)
    if conv:
        att, gb, cu = mix_in
        cw, cb = conv_params
        per = tile // HALO
        in_specs = [tok(ATTN_WIDTH), tok(CONV_B_WIDTH), tok(CONV_B_WIDTH),
                    pl.BlockSpec((1, HALO, CONV_B_WIDTH), lambda bi, t: (bi, jnp.maximum(t * per - 1, 0), 0)),
                    pl.BlockSpec((1, HALO, CONV_B_WIDTH), lambda bi, t: (bi, jnp.minimum((t + 1) * per, n // HALO - 1), 0)),
                    full(cw), full(cb)]
        args = [att, gb, cu, cu, cu, cw, cb]
    else:
        in_specs = [_slab_spec(tile, lambda bi, t: (bi, 0, t, 0), d)]
        args = [mix_in]
    in_specs += [full(w), tok(d), modspec, full(g2), modspec, modspec, full(rw_hi), full(rw_lo)]
    args += [w, x, gate, g2, sh2, sc2, rw_hi, rw_lo]
    out_shape = (jax.ShapeDtypeStruct((b, n, d), F32), jax.ShapeDtypeStruct((b, n * SUBLANES, LANES), F32),
                 jax.ShapeDtypeStruct((b, N_EXPERTS, n), F32))
    out_specs = (tok(d), pl.BlockSpec((1, tile * SUBLANES, LANES), lambda bi, t: (bi, t, 0)),
                 pl.BlockSpec((1, N_EXPERTS, tile), lambda bi, t: (bi, 0, t)))
    return pl.pallas_call(
        functools.partial(_mix_out_kernel, conv=conv, first=0, last=nt - 1),
        grid=(b, nt),
        in_specs=in_specs,
        out_specs=out_specs,
        out_shape=out_shape,
        compiler_params=_params("arbitrary", "arbitrary"),
        name="even_out" if conv else "odd_out",
    )(*args)


def _cumsum_lanes(x):
    n = x.shape[1]
    r = lax.broadcasted_iota(jnp.int32, (CUMSUM_TILE, CUMSUM_TILE), 0)
    c = lax.broadcasted_iota(jnp.int32, (CUMSUM_TILE, CUMSUM_TILE), 1)
    tri = jnp.where(r <= c, 1.0, 0.0).astype(MXU_DTYPE)
    carry = jnp.zeros((x.shape[0], 1), F32)
    outs = []
    for k in range(n // CUMSUM_TILE):
        blk = x[:, k * CUMSUM_TILE:(k + 1) * CUMSUM_TILE].astype(MXU_DTYPE)
        loc = jnp.dot(blk, tri, preferred_element_type=F32) + carry
        outs.append(loc)
        carry = loc[:, CUMSUM_TILE - 1:CUMSUM_TILE]
    return jnp.concatenate(outs, axis=1)


SLOT_EMPTY = 1 << 20


def _select_kernel(at_ref, idx_ref, g_ref, *, cap):
    bblk, n_e, n = at_ref.shape
    at = at_ref[...].reshape(bblk * n_e, n)
    n_exp = bblk * n_e
    capf = float(cap)

    def count_ge(thr):
        return jnp.sum(jnp.where(at >= thr, 1.0, 0.0), axis=1, keepdims=True)

    def bit_body(_, c):
        lo_i, hi_i = c
        mid = lo_i + ((hi_i - lo_i) >> 1)
        ge = count_ge(lax.bitcast_convert_type(mid, F32)) >= capf
        return jnp.where(ge, mid, lo_i), jnp.where(ge, hi_i, mid)

    lo_i, hi_i = lax.fori_loop(
        0, 31, bit_body, (jnp.zeros((n_exp, 1), jnp.int32), jnp.full((n_exp, 1), 0x3F800001, jnp.int32)))

    def val_body(_, c):
        lo, hi = c
        mid = 0.5 * (lo + hi)
        ge = count_ge(mid) >= capf
        return jnp.where(ge, mid, lo), jnp.where(ge, hi, mid)

    lo, hi = lax.fori_loop(
        0, 24, val_body, (lax.bitcast_convert_type(lo_i, F32), lax.bitcast_convert_type(hi_i, F32)))
    need = capf - count_ge(hi)

    above = jnp.where(at >= hi, 1.0, 0.0)
    band = jnp.where(at >= lo, 1.0, 0.0) - above
    sel = above + band * jnp.where(_cumsum_lanes(band) <= need, 1.0, 0.0)
    rank = _cumsum_lanes(sel)

    lane = lax.broadcasted_iota(jnp.int32, (n_exp, n), 1)
    disp = jnp.where(sel > 0.5, lane + 1 - rank.astype(jnp.int32), SLOT_EMPTY)
    g = at
    for k in range(n.bit_length() - 1):
        step = 1 << k
        moving = ((disp >> k) & 1) == 1
        disp_in = pltpu.roll(disp, n - step, 1)
        arriving = ((disp_in >> k) & 1) == 1
        g = jnp.where(arriving, pltpu.roll(g, n - step, 1), g)
        disp = jnp.where(arriving, disp_in, jnp.where(moving, SLOT_EMPTY, disp))

    capp = pl.cdiv(cap, LANES) * LANES
    slot = lax.broadcasted_iota(jnp.int32, (n_exp, cap), 1)
    idx_ref[...] = jnp.clip(slot + disp[:, :cap], 0, n - 1).reshape(bblk, n_e, cap)
    for s in range(bblk):
        g_pad = jnp.concatenate([g[s * n_e:(s + 1) * n_e, :capp], jnp.zeros((LANES - n_e, capp), F32)], axis=0)
        g_ref[s] = g_pad.T[:cap]


def _select(aff_t, cap):
    b, n_exp, n = aff_t.shape
    assert n & (n - 1) == 0 and n % CUMSUM_TILE == 0 and n < SLOT_EMPTY
    bblk = b
    return pl.pallas_call(
        functools.partial(_select_kernel, cap=cap),
        grid=(b // bblk,),
        in_specs=[pl.BlockSpec((bblk, n_exp, n), lambda bi: (bi, 0, 0))],
        out_specs=(pl.BlockSpec((bblk, n_exp, cap), lambda bi: (bi, 0, 0)),
                   pl.BlockSpec((bblk, cap, LANES), lambda bi: (bi, 0, 0))),
        out_shape=(jax.ShapeDtypeStruct((b, n_exp, cap), jnp.int32), jax.ShapeDtypeStruct((b, cap, LANES), F32)),
        compiler_params=_params("arbitrary"),
        name="moe_select",
    )(aff_t)


def _gather_kernel(idx_ref, src_ref, xe_ref, xcm_ref, *, cap, n_exp):
    bblk = src_ref.shape[0]
    eblk, out_rows = xe_ref.shape[1], xe_ref.shape[2]
    stride = out_rows + SUBLANES
    for k in range(eblk):
        for s in range(bblk):
            base = ((pl.program_id(0) * bblk + s) * n_exp + pl.program_id(1) * eblk + k) * cap
            for r in range(cap):
                t = idx_ref[base + r]
                xcm_ref[k, pl.ds(s * cap + r, SUBLANES, stride=stride), :] = (
                    src_ref[s, pl.ds(pl.multiple_of(t * SUBLANES, SUBLANES), SUBLANES), :])
        used = bblk * cap
        if used < out_rows:
            for j in range(SUBLANES):
                xcm_ref[k, pl.ds(j * stride + used, out_rows - used), :] = jnp.zeros((out_rows - used, LANES), F32)
        xe_ref[0, k] = jnp.concatenate(
            [xcm_ref[k, pl.ds(j * stride, out_rows), :] for j in range(SUBLANES)], axis=1).astype(xe_ref.dtype)


def _gather(idx_flat, src, cap, bblk, out_rows):
    b, rows, _ = src.shape
    d = SUBLANES * LANES
    assert cap % SUBLANES == 0 and bblk * cap <= out_rows and b % bblk == 0 and N_EXPERTS % DISPATCH_EXPERTS == 0
    return pl.pallas_call(
        functools.partial(_gather_kernel, cap=cap, n_exp=N_EXPERTS),
        grid=(b // bblk, N_EXPERTS // DISPATCH_EXPERTS),
        in_specs=[pl.BlockSpec(memory_space=pltpu.SMEM),
                  pl.BlockSpec((bblk, rows, LANES), lambda bi, ei: (bi, 0, 0))],
        out_specs=pl.BlockSpec((1, DISPATCH_EXPERTS, out_rows, d), lambda bi, ei: (bi, ei, 0, 0)),
        out_shape=jax.ShapeDtypeStruct((b // bblk, N_EXPERTS, out_rows, d), MXU_DTYPE),
        scratch_shapes=[pltpu.VMEM((DISPATCH_EXPERTS, SUBLANES * (out_rows + SUBLANES), LANES), F32)],
        compiler_params=_params("arbitrary", "arbitrary"),
        name="moe_gather",
    )(idx_flat, src)


WEIGHT_CHUNKS = 2
WEIGHT_DMA_PRIORITY = 1


def _moe_ffn_kernel(*refs, layer, n_steps, has_ctx):
    if has_ctx:
        (x_ref, g_ref, xc_ref, gc_ref, wg_hbm, wu_hbm, wd_hbm, o_ref, oc_ref,
         wgu_s, wd_s, stage_in, stage_out, sem) = refs
    else:
        (x_ref, g_ref, wg_hbm, wu_hbm, wd_hbm, o_ref, wgu_s, wd_s, stage_in, stage_out, sem) = refs
    n_exp = pl.num_programs(0)
    e = pl.program_id(0)
    s = pl.program_id(1)
    slot = e % 2
    ff = wg_hbm.shape[3]
    chunks = []
    for hbm, dst, col0, stage in ((wg_hbm, wgu_s, 0, stage_in), (wu_hbm, wgu_s, ff, stage_in),
                                  (wd_hbm, wd_s, 0, stage_out)):
        n_rows = hbm.shape[2] // WEIGHT_CHUNKS
        chunks += [(hbm, dst, col0, stage, c * n_rows, n_rows) for c in range(WEIGHT_CHUNKS)]
    per_step = pl.cdiv(len(chunks), n_steps)

    def copy(c, expert):
        hbm, _, _, stage, r0, n_rows = chunks[c]
        return pltpu.make_async_copy(hbm.at[layer, expert, pl.ds(r0, n_rows), :], stage, sem.at[0])

    def cast(c, to_slot):
        hbm, dst, col0, stage, r0, n_rows = chunks[c]
        dst[to_slot, pl.ds(r0, n_rows), pl.ds(col0, hbm.shape[3])] = stage[...].astype(dst.dtype)

    @pl.when((e == 0) & (s == 0))
    def _():
        for c in range(len(chunks)):
            copy(c, 0).start()
            copy(c, 0).wait()
            cast(c, 0)

    has_next = e + 1 < n_exp
    for c in range(len(chunks)):
        if c % per_step == 0:
            @pl.when(has_next & (s == c // per_step))
            def _():
                copy(c, e + 1).start(priority=WEIGHT_DMA_PRIORITY)

    def ffn(xr, gr, outr):
        x, aff = xr[0, 0], gr[0]
        rows = x.shape[0]
        au = jnp.dot(x, wgu_s[slot], preferred_element_type=F32)
        a, u = au[:, :ff], au[:, ff:]
        mid = (a * _sigmoid(a) * u).astype(MXU_DTYPE)
        y = jnp.dot(mid, wd_s[slot], preferred_element_type=F32)
        lane = lax.broadcasted_iota(jnp.int32, aff.shape, 1)
        y = y * jnp.sum(jnp.where(lane == e, aff, 0.0), axis=1, keepdims=True)
        stride = rows + SUBLANES
        for j in range(SUBLANES):
            outr[0, 0, pl.ds(j * stride, rows), :] = y[:, j * LANES:(j + 1) * LANES]
            outr[0, 0, pl.ds(j * stride + rows, SUBLANES), :] = jnp.zeros((SUBLANES, LANES), F32)

    if has_ctx:
        is_ctx = s == n_steps - 1
        pl.when(is_ctx)(lambda: ffn(xc_ref, gc_ref, oc_ref))
        pl.when(jnp.logical_not(is_ctx))(lambda: ffn(x_ref, g_ref, o_ref))
    else:
        pl.when(s < n_steps)(lambda: ffn(x_ref, g_ref, o_ref))

    for c in range(len(chunks)):
        @pl.when(has_next & (s == c // per_step))
        def _():
            copy(c, e + 1).wait()
            cast(c, 1 - slot)
            if (c + 1) % per_step != 0 and c + 1 < len(chunks):
                copy(c + 1, e + 1).start(priority=WEIGHT_DMA_PRIORITY)


def _moe_ffn(xe, ge, ctx_block, layer, wg, wu, wd):
    nb, e, rows, d = xe.shape
    ff = wg.shape[-1]
    assert d % WEIGHT_CHUNKS == 0 and ff % (WEIGHT_CHUNKS * SUBLANES) == 0 and ff % LANES == 0
    has_ctx = ctx_block is not None
    n_steps = nb + int(has_ctx)
    hbm = pl.BlockSpec(memory_space=pl.ANY)
    out_rows = SUBLANES * (rows + SUBLANES)
    blk = lambda bi: jnp.minimum(bi, nb - 1)
    in_specs = [pl.BlockSpec((1, 1, rows, d), lambda ei, bi: (blk(bi), ei, 0, 0)),
                pl.BlockSpec((1, rows, LANES), lambda ei, bi: (blk(bi), 0, 0))]
    args = [xe, ge]
    out_specs = [pl.BlockSpec((1, 1, out_rows, LANES), lambda ei, bi: (blk(bi), ei, 0, 0))]
    out_shape = [jax.ShapeDtypeStruct((nb, e, out_rows, LANES), F32)]
    if has_ctx:
        rows_c = ctx_block[0].shape[2]
        assert ctx_block[0].shape == (1, e, rows_c, d) and ctx_block[1].shape == (1, rows_c, LANES)
        out_rows_c = SUBLANES * (rows_c + SUBLANES)
        in_specs += [pl.BlockSpec((1, 1, rows_c, d), lambda ei, bi: (0, ei, 0, 0)),
                     pl.BlockSpec((1, rows_c, LANES), lambda ei, bi: (0, 0, 0))]
        args += list(ctx_block)
        out_specs.append(pl.BlockSpec((1, 1, out_rows_c, LANES), lambda ei, bi: (0, ei, 0, 0)))
        out_shape.append(jax.ShapeDtypeStruct((1, e, out_rows_c, LANES), F32))
    return pl.pallas_call(
        functools.partial(_moe_ffn_kernel, layer=layer, n_steps=n_steps, has_ctx=has_ctx),
        grid=(e, n_steps),
        in_specs=in_specs + [hbm, hbm, hbm],
        out_specs=tuple(out_specs),
        out_shape=tuple(out_shape),
        scratch_shapes=[pltpu.VMEM((2, d, 2 * ff), MXU_DTYPE), pltpu.VMEM((2, ff, d), MXU_DTYPE),
                        pltpu.VMEM((d // WEIGHT_CHUNKS, ff), F32), pltpu.VMEM((ff // WEIGHT_CHUNKS, d), F32),
                        pltpu.SemaphoreType.DMA((1,))],
        compiler_params=_params("arbitrary", "arbitrary"),
        name="moe_ffn",
    )(*args, wg, wu, wd)


COMBINE_UNROLL = 8


def _combine_kernel(idx_ref, y_ref, acc_ref, *, cap, n_exp):
    bblk = acc_ref.shape[0]
    eblk = y_ref.shape[1]
    ei = pl.program_id(1)
    stride = y_ref.shape[2] // SUBLANES

    @pl.when(ei == 0)
    def _():
        acc_ref[...] = jnp.zeros_like(acc_ref)

    for k in range(eblk):
        for s in range(bblk):
            base = ((pl.program_id(0) * bblk + s) * n_exp + ei * eblk + k) * cap
            for r0 in range(0, cap, COMBINE_UNROLL):
                toks = [pl.multiple_of(idx_ref[base + r0 + i] * SUBLANES, SUBLANES) for i in range(COMBINE_UNROLL)]
                new = [acc_ref[s, pl.ds(toks[i], SUBLANES), :]
                       + y_ref[0, k, pl.ds(s * cap + r0 + i, SUBLANES, stride=stride), :]
                       for i in range(COMBINE_UNROLL)]
                for i in range(COMBINE_UNROLL):
                    acc_ref[s, pl.ds(toks[i], SUBLANES), :] = new[i]


def _combine(idx_flat, y, b, n, cap, bblk):
    nb, n_exp, rows, _ = y.shape
    assert cap % COMBINE_UNROLL == 0 and nb * bblk == b and bblk * cap <= rows // SUBLANES - SUBLANES
    return pl.pallas_call(
        functools.partial(_combine_kernel, cap=cap, n_exp=n_exp),
        grid=(nb, n_exp // DISPATCH_EXPERTS),
        in_specs=[pl.BlockSpec(memory_space=pltpu.SMEM),
                  pl.BlockSpec((1, DISPATCH_EXPERTS, rows, LANES), lambda bi, ei: (bi, ei, 0, 0))],
        out_specs=pl.BlockSpec((bblk, n * SUBLANES, LANES), lambda bi, ei: (bi, 0, 0)),
        out_shape=jax.ShapeDtypeStruct((b, n * SUBLANES, LANES), F32),
        compiler_params=_params("arbitrary", "arbitrary"),
        name="moe_combine",
    )(idx_flat, y)


def _expert_choice(h_lat, aff_lat, h_ctx, aff_ctx, layer, wg, wu, wd):
    b, _, n = aff_lat.shape
    cap = max(1, CAPACITY_FACTOR * n // N_EXPERTS)
    idx, g = _select(aff_lat, cap)
    idx = idx.reshape(-1)
    xe = _gather(idx, h_lat, cap, 1, cap)
    if h_ctx is None:
        (y,) = _moe_ffn(xe, g, None, layer, wg, wu, wd)
        return _combine(idx, y, b, n, cap, 1), None
    lc = aff_ctx.shape[2]
    cap_c = max(1, CAPACITY_FACTOR * lc // N_EXPERTS)
    idx_c, g_c = _select(aff_ctx, cap_c)
    idx_c = idx_c.reshape(-1)
    xe_c = _gather(idx_c, h_ctx, cap_c, b, b * cap_c)
    g_c = g_c.reshape(1, b * cap_c, LANES)
    y, y_c = _moe_ffn(xe, g, (xe_c, g_c), layer, wg, wu, wd)
    return _combine(idx, y, b, n, cap, 1), _combine(idx_c, y_c, b, lc, cap_c, b)


def _odd_in_kernel(x_ref, moe_ref, gate_ref, g_ref, sh_ref, sc_ref, w_ref, xo_ref, gg_ref, u_ref):
    x = x_ref[0] + gate_ref[0] * _load_token_tiles(moe_ref)
    xo_ref[0] = x
    h = _norm_mod(x, g_ref[...], sh_ref[0], sc_ref[0]).astype(MXU_DTYPE)
    y = jnp.dot(h, w_ref[...], preferred_element_type=F32)
    gl = y[:, :LRU_WIDTH]
    _store_slabs(gg_ref, 0.25 * gl * (1.0 + jnp.tanh(0.7978845608028654 * (gl + 0.044715 * gl * gl * gl))))
    _store_slabs(u_ref, y[:, LRU_WIDTH:])


def _odd_in(x, moe, gate, g, shift, scale, mod_row, w, tile):
    b, n, d = x.shape
    row = (lambda bi: bi) if mod_row is None else (lambda bi: mod_row)
    tok = pl.BlockSpec((1, tile, d), lambda bi, t: (bi, t, 0))
    tiles = pl.BlockSpec((1, tile * SUBLANES, LANES), lambda bi, t: (bi, t, 0))
    modspec = pl.BlockSpec((1, 1, d), lambda bi, t: (row(bi), 0, 0))
    return pl.pallas_call(
        _odd_in_kernel,
        grid=(b, n // tile),
        in_specs=[tok, tiles, modspec, pl.BlockSpec((1, d), lambda bi, t: (0, 0)), modspec, modspec,
                  pl.BlockSpec(w.shape, lambda bi, t: (0, 0))],
        out_specs=(tok, _slab_spec(tile, lambda bi, t: (bi, 0, t, 0)), _slab_spec(tile, lambda bi, t: (bi, 0, t, 0))),
        out_shape=(jax.ShapeDtypeStruct((b, n, d), F32),) + (jax.ShapeDtypeStruct((b, d // LANES, n, LANES), F32),) * 2,
        compiler_params=_params("arbitrary", "arbitrary"),
        name="odd_in",
    )(x, moe, gate, g, shift, scale, w)


def _lru_kernel(*refs, reverse, nt):
    if reverse:
        (uc_ref, wax_ref, lam_ref, h0_ref, hf_ref, gg_ref, out_ref, hlast_ref, a_scr, b_scr, uc_scr, carry_scr) = refs
        n_slab, tile = uc_ref.shape[1], uc_ref.shape[2]
    else:
        (u_ref, up_ref, un_ref, cw_ref, cb_ref, wax_ref, lam_ref, h0_ref,
         out_ref, uco_ref, hlast_ref, a_scr, b_scr, uc_scr, carry_scr) = refs
        n_slab, tile = u_ref.shape[1], u_ref.shape[2]
    w = n_slab * LANES
    per = tile // SUBLANES
    rows = lambda j: slice(j * per, (j + 1) * per)
    phase = lambda ref, j: jnp.concatenate(
        [ref[0, c, pl.ds(j, per, stride=SUBLANES), :] for c in range(n_slab)], axis=1)
    phase_major = lambda ref, j: jnp.concatenate([ref[0, c, rows(j), :] for c in range(n_slab)], axis=1)

    def store_phase(ref, j, val, time_order):
        for c in range(n_slab):
            dst = pl.ds(j, per, stride=SUBLANES) if time_order else rows(j)
            ref[0, c, dst, :] = val[:, c * LANES:(c + 1) * LANES]

    t = pl.program_id(1)
    if reverse:
        for j in range(SUBLANES):
            uc_scr[rows(j), :] = phase_major(uc_ref, j)
    else:
        prev = jnp.where(t == 0, 0.0, _load_slabs(up_ref))
        nxt = jnp.where(t == nt - 1, 0.0, _load_slabs(un_ref))
        rowid = lax.broadcasted_iota(jnp.int32, (per, w), 0)
        shift_down = lambda x, first: jnp.where(rowid == 0, first, pltpu.roll(x, 1, 0))
        shift_up = lambda x, last: jnp.where(rowid == per - 1, last, pltpu.roll(x, per - 1, 0))
        u = [phase(u_ref, j) for j in range(SUBLANES)]
        um1 = [shift_down(u[7], prev[7:8])] + u[:7]
        um2 = [shift_down(u[6], prev[6:7]), um1[0]] + u[:6]
        up1 = u[1:] + [shift_up(u[0], nxt[0:1])]
        cw = cw_ref[...]
        for j in range(SUBLANES):
            uc_j = cb_ref[...] + cw[0:1] * um2[j] + cw[1:2] * um1[j] + cw[2:3] * u[j] + cw[3:4] * up1[j]
            uc_scr[rows(j), :] = uc_j
            store_phase(uco_ref, j, uc_j, False)

    lam = lam_ref[0]
    half_decay = (0.5 * LRU_C) * (jnp.maximum(-lam, 0.0) + jnp.log1p(jnp.exp(-jnp.abs(lam))))
    ones = jnp.where(lax.broadcasted_iota(jnp.int32, (tile, LRU_BLOCK), 1) < BIAS_TERMS, 1.0, 0.0).astype(MXU_DTYPE)
    for hd in range(LRU_HEADS):
        sl = slice(hd * LRU_BLOCK, (hd + 1) * LRU_BLOCK)
        uc = uc_scr[:, sl]
        z = jnp.dot(jnp.concatenate([uc.astype(MXU_DTYPE), ones], axis=1), wax_ref[0, hd], preferred_element_type=F32)
        hd_row = half_decay[:, sl]
        neg_log_a = hd_row * jnp.tanh(z[:, :LRU_BLOCK]) + hd_row
        gate2 = 1.0 + jnp.tanh(z[:, LRU_BLOCK:])
        a = jnp.exp(-neg_log_a)
        m2 = jnp.tanh(neg_log_a) * (a * a + 1.0)
        mult = jnp.where(m2 > 0.0, m2 * lax.rsqrt(m2), 0.0)
        a_scr[:, sl] = a
        b_scr[:, sl] = mult * (gate2 * uc)

    @pl.when(t == 0)
    def _():
        carry_scr[...] = jnp.broadcast_to(h0_ref[0], carry_scr.shape)

    order = list(range(SUBLANES))[::-1] if reverse else list(range(SUBLANES))
    hrun = b_scr[rows(order[0]), :]
    prun = a_scr[rows(order[0]), :]
    for j in order[1:]:
        aj = a_scr[rows(j), :]
        hrun = aj * hrun + b_scr[rows(j), :]
        prun = aj * prun
        b_scr[rows(j), :] = hrun
        a_scr[rows(j), :] = prun

    lane_row = lax.broadcasted_iota(jnp.int32, (SUBLANES, w), 0)
    carry = carry_scr[...]
    groups = list(range(per // SUBLANES))
    entering = [None] * len(groups)
    for m in (groups[::-1] if reverse else groups):
        a = prun[m * SUBLANES:(m + 1) * SUBLANES]
        bcoef = hrun[m * SUBLANES:(m + 1) * SUBLANES]
        for dist in (1, 2, 4):
            shift = (SUBLANES - dist) if reverse else dist
            msk = (lane_row < SUBLANES - dist) if reverse else (lane_row >= dist)
            a_s = pltpu.roll(a, shift, 0)
            b_s = pltpu.roll(bcoef, shift, 0)
            bcoef = jnp.where(msk, a * b_s + bcoef, bcoef)
            a = jnp.where(msk, a * a_s, a)
        after = a * carry + bcoef
        if reverse:
            entering[m] = jnp.where(lane_row == SUBLANES - 1, carry, pltpu.roll(after, SUBLANES - 1, 0))
            carry = jnp.broadcast_to(after[0:1], carry.shape)
        else:
            entering[m] = jnp.where(lane_row == 0, carry, pltpu.roll(after, 1, 0))
            carry = jnp.broadcast_to(after[SUBLANES - 1:SUBLANES], carry.shape)
    carry_scr[...] = carry
    hlast_ref[0] = carry[0:1]
    h_in = jnp.concatenate(entering, axis=0)

    for j in range(SUBLANES):
        hcur = b_scr[rows(j), :] + a_scr[rows(j), :] * h_in
        if reverse:
            hcur = phase(gg_ref, j) * (phase_major(hf_ref, j) + hcur)
        store_phase(out_ref, j, hcur, reverse)


BIAS_TERMS = 3


def _gate_weights(wa, wx, ba, bx):
    n_dir, heads, blk, _ = wa.shape
    w = 0.5 * jnp.concatenate([wa, wx], axis=-1)
    bias = 0.5 * jnp.concatenate([ba.reshape(n_dir, heads, blk), bx.reshape(n_dir, heads, blk)], axis=-1)
    terms = []
    for _ in range(BIAS_TERMS):
        term = bias.astype(MXU_DTYPE)
        terms.append(term)
        bias = bias - term.astype(F32)
    rows = jnp.stack(terms, axis=2)
    pad = jnp.zeros((n_dir, heads, blk - BIAS_TERMS, 2 * blk), MXU_DTYPE)
    return jnp.concatenate([w.astype(MXU_DTYPE), rows, pad], axis=2)


def _lru_fwd(u, cw, cb, wax, lam, h0, tile):
    b, n_slab, n, _ = u.shape
    w = n_slab * LANES
    nt = n // tile
    per = tile // HALO
    assert tile % (SUBLANES * SUBLANES) == 0
    tok = _slab_spec(tile, lambda bi, t: (bi, 0, t, 0), w)
    state = pl.BlockSpec((1, 1, w), lambda bi, t: (bi, 0, 0))
    return pl.pallas_call(
        functools.partial(_lru_kernel, reverse=False, nt=nt),
        grid=(b, nt),
        in_specs=[tok,
                  _slab_spec(HALO, lambda bi, t: (bi, 0, jnp.maximum(t * per - 1, 0), 0), w),
                  _slab_spec(HALO, lambda bi, t: (bi, 0, jnp.minimum((t + 1) * per, n // HALO - 1), 0), w),
                  pl.BlockSpec(cw.shape, lambda bi, t: (0, 0)),
                  pl.BlockSpec(cb.shape, lambda bi, t: (0, 0)),
                  pl.BlockSpec((1,) + wax.shape[1:], lambda bi, t: (0, 0, 0, 0)),
                  pl.BlockSpec((1, 1, w), lambda bi, t: (0, 0, 0)),
                  state],
        out_specs=(tok, tok, state),
        out_shape=(jax.ShapeDtypeStruct(u.shape, F32), jax.ShapeDtypeStruct(u.shape, F32),
                   jax.ShapeDtypeStruct((b, 1, w), F32)),
        scratch_shapes=[pltpu.VMEM((tile, w), F32), pltpu.VMEM((tile, w), F32), pltpu.VMEM((tile, w), F32),
                        pltpu.VMEM((SUBLANES, w), F32)],
        compiler_params=_params("arbitrary", "arbitrary"),
        name="lru_fwd",
    )(u, u, u, cw, cb, wax, lam, h0)


def _lru_bwd(uc, wax, lam, h0, hf, gg, tile):
    b, n_slab, n, _ = uc.shape
    w = n_slab * LANES
    nt = n // tile
    assert tile % (SUBLANES * SUBLANES) == 0
    tok = _slab_spec(tile, lambda bi, t: (bi, 0, nt - 1 - t, 0), w)
    state = pl.BlockSpec((1, 1, w), lambda bi, t: (bi, 0, 0))
    return pl.pallas_call(
        functools.partial(_lru_kernel, reverse=True, nt=nt),
        grid=(b, nt),
        in_specs=[tok,
                  pl.BlockSpec((1,) + wax.shape[1:], lambda bi, t: (1, 0, 0, 0)),
                  pl.BlockSpec((1, 1, w), lambda bi, t: (1, 0, 0)),
                  state, tok, tok],
        out_specs=(tok, state),
        out_shape=(jax.ShapeDtypeStruct(uc.shape, F32), jax.ShapeDtypeStruct((b, 1, w), F32)),
        scratch_shapes=[pltpu.VMEM((tile, w), F32), pltpu.VMEM((tile, w), F32), pltpu.VMEM((tile, w), F32),
                        pltpu.VMEM((SUBLANES, w), F32)],
        compiler_params=_params("arbitrary", "arbitrary"),
        name="lru_bwd",
    )(uc, wax, lam, h0, hf, gg)


def _final_kernel(x_ref, moe_ref, gate_ref, g_ref, o_ref):
    x = x_ref[0] + gate_ref[0] * _load_token_tiles(moe_ref)
    o_ref[0] = x * lax.rsqrt(jnp.mean(x * x, axis=-1, keepdims=True) + EPS) * g_ref[...]


def _final(x, moe, gate, g, tile):
    b, n, d = x.shape
    tok = pl.BlockSpec((1, tile, d), lambda bi, t: (bi, t, 0))
    tiles = pl.BlockSpec((1, tile * SUBLANES, LANES), lambda bi, t: (bi, t, 0))
    return pl.pallas_call(
        _final_kernel,
        grid=(b, n // tile),
        in_specs=[tok, tiles, pl.BlockSpec((1, 1, d), lambda bi, t: (bi, 0, 0)), pl.BlockSpec((1, d), lambda bi, t: (0, 0))],
        out_specs=tok,
        out_shape=jax.ShapeDtypeStruct((b, n, d), F32),
        compiler_params=_params("arbitrary", "arbitrary"),
        name="final_norm",
    )(x, moe, gate, g)


def kernel(x, c, ctx, c_ctx, ada_w, ada_b, norm_mix_g, norm_ffn_g, ev_w_in, ev_w_out, ev_sink, ev_conv_w, ev_conv_b, od_w_in, od_w_out, od_conv_w, od_conv_b, od_wa, od_ba, od_wx, od_bx, od_lambda, router_w, w_gate, w_up, w_down, final_g):
    b, n, d = x.shape
    lc = ctx.shape[1]
    depth = ada_w.shape[0]
    assert depth == 2 and d == D_MODEL and b < MOD_ROWS
    tile_l = min(512, n)
    tile_w = min(1024, n)
    tile_c = lc
    ctx_row = b

    cvec = jnp.zeros((MOD_ROWS, d), F32).at[:b].set(c).at[b].set(c_ctx)
    mods = _ada(cvec, ada_w, ada_b).reshape(depth, MOD_ROWS, 6, 1, d)
    mod = lambda l, j: mods[l, :, j]

    def router_split(l):
        return _split_hi_lo(jnp.pad(router_w[l], ((0, 0), (0, LANES - N_EXPERTS))))

    bf = lambda a: a.astype(MXU_DTYPE)

    g_mix = norm_mix_g[0].reshape(1, d)
    g_ffn = norm_ffn_g[0].reshape(1, d)
    w_in = bf(ev_w_in[0])
    w_out = bf(ev_w_out[0])
    conv_p = (ev_conv_w[0], ev_conv_b[0].reshape(1, -1))
    tables = _rope_tables(n)
    ql, kvl, gbl, cul = _even_in(x, g_mix, mod(0, 0), mod(0, 1), None, w_in, tables, tile_l)
    qc, kvc, gbc, cuc = _even_in(ctx, g_mix, mod(0, 0), mod(0, 1), ctx_row, w_in, None, tile_c)
    att_l = _attention(ev_sink[0], ql, kvl, kvc)
    att_c = _attention(ev_sink[0], qc, None, kvc)
    rw_hi, rw_lo = router_split(0)
    xl, hl, at_l = _mix_out((att_l, gbl, cul), conv_p, w_out, x, mod(0, 2), g_ffn, mod(0, 3), mod(0, 4), None,
                            rw_hi, rw_lo, tile_w)
    xc, hc, at_c = _mix_out((att_c, gbc, cuc), conv_p, w_out, ctx, mod(0, 2), g_ffn, mod(0, 3), mod(0, 4),
                            ctx_row, rw_hi, rw_lo, tile_c)
    moe_l, moe_c = _expert_choice(hl, at_l, hc, at_c, 0, w_gate, w_up, w_down)

    g_mix = norm_mix_g[1].reshape(1, d)
    g_ffn = norm_ffn_g[1].reshape(1, d)
    w_in = bf(od_w_in[0])
    w_out = bf(od_w_out[0])
    xl, ggl, ul = _odd_in(xl, moe_l, mod(0, 5), g_mix, mod(1, 0), mod(1, 1), None, w_in, tile_l)
    _, _, u_ctx = _odd_in(xc, moe_c, mod(0, 5), g_mix, mod(1, 0), mod(1, 1), ctx_row, w_in, tile_c)
    cw, cb = od_conv_w[0], od_conv_b[0].reshape(1, -1)
    wax = _gate_weights(od_wa[0], od_wx[0], od_ba[0], od_bx[0])
    lam = od_lambda[0].reshape(2, 1, -1)
    zero_state = jnp.zeros((b, 1, LRU_WIDTH), F32)
    hf_c, uc_c, h0_f = _lru_fwd(u_ctx, cw, cb, wax, lam, zero_state, tile_c)
    _, h0_b = _lru_bwd(uc_c, wax, lam, zero_state, hf_c, hf_c, tile_c)
    hf_l, uc_l, _ = _lru_fwd(ul, cw, cb, wax, lam, h0_f, tile_w)
    yl, _ = _lru_bwd(uc_l, wax, lam, h0_b, hf_l, ggl, tile_w)
    rw_hi, rw_lo = router_split(1)
    xl, hl, at_l = _mix_out(yl, None, w_out, xl, mod(1, 2), g_ffn, mod(1, 3), mod(1, 4), None,
                            rw_hi, rw_lo, tile_w)
    moe_l, _ = _expert_choice(hl, at_l, None, None, 1, w_gate, w_up, w_down)
    return _final(xl, moe_l, mod(1, 5), final_g.reshape(1, d), tile_w)
```

```python
import functools

import jax
import jax.numpy as jnp
from jax import lax
from jax.experimental import pallas as pl
from jax.experimental.pallas import tpu as pltpu

F32 = jnp.float32
MXU_DTYPE = jnp.bfloat16

D_MODEL = 1024
GRID_W = 64
EPS = 1e-6
NEG_INF = -1e30
HEAD_DIM = 64
N_Q_HEADS = 8
N_KV_HEADS = 2
Q_PER_KV = N_Q_HEADS // N_KV_HEADS
ATTN_WIDTH = N_Q_HEADS * HEAD_DIM
KV_WIDTH = N_KV_HEADS * HEAD_DIM
WINDOW = 128
BLOCK = 128
ROPE_BASE = 10000.0
CONV_B_WIDTH = D_MODEL // 2
EVEN_IN = ATTN_WIDTH + 2 * KV_WIDTH + 3 * CONV_B_WIDTH
LRU_WIDTH = D_MODEL
LRU_HEADS = 8
LRU_BLOCK = LRU_WIDTH // LRU_HEADS
LRU_C = 8.0
N_EXPERTS = 16
CAPACITY_FACTOR = 2
EXPERT_FF = 1408
MOD_ROWS = 16
LANES = 128
SUBLANES = 8
HALO = SUBLANES
CUMSUM_TILE = 256
MIX_OUT_ROWS = 256
ATTN_QBLOCKS = 4
ATTN_HEADS_PER_DOT = 4
DISPATCH_EXPERTS = 4
V7X_VMEM_LIMIT = 56 * 1024 * 1024


def _params(*sem):
    return pltpu.CompilerParams(dimension_semantics=sem, vmem_limit_bytes=V7X_VMEM_LIMIT)


def _split_hi_lo(a):
    hi = a.astype(MXU_DTYPE)
    lo = (a - hi.astype(F32)).astype(MXU_DTYPE)
    return hi, lo


def _sigmoid(z):
    return 0.5 * (1.0 + jnp.tanh(0.5 * z))


def _store_token_tiles(ref, val, row0=0):
    tile = val.shape[0]
    for j in range(SUBLANES):
        ref[0, pl.ds(row0 * SUBLANES + j, tile, stride=SUBLANES), :] = val[:, j * LANES:(j + 1) * LANES]


def _load_token_tiles(ref):
    tile = ref.shape[1] // SUBLANES
    return jnp.concatenate([ref[0, pl.ds(j, tile, stride=SUBLANES), :] for j in range(SUBLANES)], axis=1)


def _slab_spec(rows, index_map, width=LRU_WIDTH):
    return pl.BlockSpec((1, width // LANES, rows, LANES), index_map)


def _store_slabs(ref, val):
    for c in range(val.shape[1] // LANES):
        ref[0, c] = val[:, c * LANES:(c + 1) * LANES]


def _load_slabs(ref):
    return jnp.concatenate([ref[0, c] for c in range(ref.shape[1])], axis=1)


def _norm_mod(x, g, shift, scale):
    y = x * lax.rsqrt(jnp.mean(x * x, axis=-1, keepdims=True) + EPS) * g
    return y * (1.0 + scale) + shift


def _ada_kernel(c_ref, w_ref, b_ref, o_ref):
    c = c_ref[...]
    s_hi, s_lo = _split_hi_lo(c * _sigmoid(c))
    w_hi, w_lo = _split_hi_lo(w_ref[0])
    acc = jnp.dot(s_hi, w_hi, preferred_element_type=F32)
    acc += jnp.dot(s_hi, w_lo, preferred_element_type=F32)
    acc += jnp.dot(s_lo, w_hi, preferred_element_type=F32)
    o_ref[0] = acc + b_ref[0]


def _ada(cvec, ada_w, ada_b):
    depth, d, n6 = ada_w.shape
    tn = 1536
    return pl.pallas_call(
        _ada_kernel,
        grid=(depth, n6 // tn),
        in_specs=[
            pl.BlockSpec((MOD_ROWS, d), lambda l, j: (0, 0)),
            pl.BlockSpec((1, d, tn), lambda l, j: (l, 0, j)),
            pl.BlockSpec((1, 1, tn), lambda l, j: (l, 0, j)),
        ],
        out_specs=pl.BlockSpec((1, MOD_ROWS, tn), lambda l, j: (l, 0, j)),
        out_shape=jax.ShapeDtypeStruct((depth, MOD_ROWS, n6), F32),
        compiler_params=_params("arbitrary", "arbitrary"),
        name="ada",
    )(cvec, ada_w, ada_b.reshape(depth, 1, n6))


def _even_in_kernel(*refs, rope):
    if rope:
        (x_ref, g_ref, sh_ref, sc_ref, w_ref, cos_ref, sa_ref, sb_ref, q_ref, kv_ref, gb_ref, cu_ref) = refs
    else:
        (x_ref, g_ref, sh_ref, sc_ref, w_ref, q_ref, kv_ref, gb_ref, cu_ref) = refs
    h = _norm_mod(x_ref[0], g_ref[...], sh_ref[0], sc_ref[0]).astype(MXU_DTYPE)
    y = jnp.dot(h, w_ref[...], preferred_element_type=F32)
    q = y[:, :ATTN_WIDTH]
    k = y[:, ATTN_WIDTH:ATTN_WIDTH + KV_WIDTH]
    v = y[:, ATTN_WIDTH + KV_WIDTH:ATTN_WIDTH + 2 * KV_WIDTH]
    c0 = ATTN_WIDTH + 2 * KV_WIDTH
    if rope:
        cos, sa, sb = cos_ref[...], sa_ref[...], sb_ref[...]

        def rot(z):
            return z * cos + pltpu.roll(z, 16, 1) * sa + pltpu.roll(z, 112, 1) * sb

        q = jnp.concatenate([rot(q[:, j * 128:(j + 1) * 128]) for j in range(ATTN_WIDTH // 128)], axis=1)
        k = rot(k)
    q_ref[0] = (q * (HEAD_DIM ** -0.5)).astype(q_ref.dtype)
    kv_ref[0] = jnp.concatenate([k, v], axis=1).astype(kv_ref.dtype)
    gb_ref[0] = y[:, c0:c0 + CONV_B_WIDTH]
    cu_ref[0] = y[:, c0 + CONV_B_WIDTH:c0 + 2 * CONV_B_WIDTH] * y[:, c0 + 2 * CONV_B_WIDTH:]


def _even_in(x, g, shift, scale, mod_row, w, tables, tile):
    b, n, d = x.shape
    nt = n // tile
    rope = tables is not None
    row = (lambda bi: bi) if mod_row is None else (lambda bi: mod_row)
    in_specs = [
        pl.BlockSpec((1, tile, d), lambda bi, t: (bi, t, 0)),
        pl.BlockSpec((1, d), lambda bi, t: (0, 0)),
        pl.BlockSpec((1, 1, d), lambda bi, t: (row(bi), 0, 0)),
        pl.BlockSpec((1, 1, d), lambda bi, t: (row(bi), 0, 0)),
        pl.BlockSpec(w.shape, lambda bi, t: (0, 0)),
    ]
    args = [x, g, shift, scale, w]
    if rope:
        in_specs += [pl.BlockSpec((tile, 128), lambda bi, t: (t, 0))] * 3
        args += list(tables)
    out_shape = (
        jax.ShapeDtypeStruct((b, n, ATTN_WIDTH), MXU_DTYPE),
        jax.ShapeDtypeStruct((b, n, 2 * KV_WIDTH), MXU_DTYPE),
        jax.ShapeDtypeStruct((b, n, CONV_B_WIDTH), F32),
        jax.ShapeDtypeStruct((b, n, CONV_B_WIDTH), F32),
    )
    out_specs = tuple(pl.BlockSpec((1, tile, s.shape[-1]), lambda bi, t: (bi, t, 0)) for s in out_shape)
    return pl.pallas_call(
        functools.partial(_even_in_kernel, rope=rope),
        grid=(b, nt),
        in_specs=in_specs,
        out_specs=out_specs,
        out_shape=out_shape,
        compiler_params=_params("arbitrary", "arbitrary"),
        name="even_in_rope" if rope else "even_in",
    )(*args)


def _rope_tables(n):
    nf = HEAD_DIM // 4
    pos = jnp.arange(n)
    rows = (pos // GRID_W).astype(F32)
    cols = (pos % GRID_W).astype(F32)
    lane = jnp.arange(128)
    inv = ROPE_BASE ** (-(lane % nf).astype(F32) / nf)
    use_col = (lane % HEAD_DIM) >= HEAD_DIM // 2
    ang = jnp.where(use_col[None, :], cols[:, None], rows[:, None]) * inv[None, :]
    cos, sin = jnp.cos(ang), jnp.sin(ang)
    second = ((lane % (2 * nf)) >= nf)[None, :]
    return cos, jnp.where(second, sin, 0.0), jnp.where(second, 0.0, -sin)


def _attn_kernel(*refs, n, has_local):
    if has_local:
        sink_ref, q_ref, kv_ref, kvc_ref, bias_ref, o_ref = refs
    else:
        sink_ref, q_ref, kvc_ref, o_ref = refs
    n_loc = 3 * BLOCK
    nb = n // BLOCK
    hpd = ATTN_HEADS_PER_DOT
    grp = lax.broadcasted_iota(jnp.int32, (hpd * BLOCK, 1), 0) // BLOCK
    for sb in range(q_ref.shape[1] // BLOCK):
        i = pl.program_id(1) * (q_ref.shape[1] // BLOCK) + sb
        q = q_ref[0, sb * BLOCK:(sb + 1) * BLOCK, :]
        kvall = kvc_ref[0]
        if has_local:
            start = pl.multiple_of(_local_start(i, n), BLOCK)
            kvall = jnp.concatenate([kv_ref[0, pl.ds(start, n_loc), :], kvall], axis=0)
            case = jnp.where(i == 0, 0, jnp.where(i == nb - 1, 2, 1))
            bias = jnp.concatenate([bias_ref[case]] * hpd, axis=0)
        outs = []
        for h0 in range(0, N_Q_HEADS, hpd):
            hk = h0 // Q_PER_KV
            kh = kvall[:, hk * HEAD_DIM:(hk + 1) * HEAD_DIM]
            vh = kvall[:, KV_WIDTH + hk * HEAD_DIM:KV_WIDTH + (hk + 1) * HEAD_DIM]
            qg = jnp.concatenate([q[:, (h0 + g) * HEAD_DIM:(h0 + g + 1) * HEAD_DIM] for g in range(hpd)], axis=0)
            s = lax.dot_general(qg, kh, (((1,), (1,)), ((), ())), preferred_element_type=F32)
            if has_local:
                s = jnp.concatenate([s[:, :n_loc] + bias, s[:, n_loc:]], axis=1)
            snk = jnp.zeros((hpd * BLOCK, 1), F32)
            for g in range(hpd):
                snk = jnp.where(grp == g, sink_ref[h0 + g], snk)
            m = jnp.maximum(jnp.max(s, axis=1, keepdims=True), snk)
            p = jnp.exp(s - m).astype(MXU_DTYPE)
            v_ones = jnp.concatenate([vh, jnp.ones_like(vh)], axis=1)
            ov = jnp.dot(p, v_ones, preferred_element_type=F32)
            o = ov[:, :HEAD_DIM] / (ov[:, HEAD_DIM:HEAD_DIM + 1] + jnp.exp(snk - m))
            outs += [o[g * BLOCK:(g + 1) * BLOCK] for g in range(hpd)]
        o_ref[0, sb * BLOCK:(sb + 1) * BLOCK, :] = jnp.concatenate(outs, axis=1).astype(o_ref.dtype)


def _local_start(i, n):
    return jnp.clip(i * BLOCK - BLOCK, 0, n - 3 * BLOCK)


def _window_bias(n):
    nb = n // BLOCK
    r = jnp.arange(BLOCK)[:, None]
    c = jnp.arange(3 * BLOCK)[None, :]
    cases = []
    for i in (0, 1, nb - 1):
        diff = (_local_start(i, n) + c) - (i * BLOCK + r)
        cases.append(jnp.where(jnp.abs(diff) <= WINDOW, 0.0, NEG_INF).astype(F32))
    return jnp.stack(cases)


def _attention(sink, q, kv, kvc):
    b, n, _ = q.shape
    lc = kvc.shape[1]
    has_local = kv is not None
    qrows = min(ATTN_QBLOCKS * BLOCK, n)
    assert n % qrows == 0
    in_specs = [pl.BlockSpec(memory_space=pltpu.SMEM), pl.BlockSpec((1, qrows, ATTN_WIDTH), lambda bi, i: (bi, i, 0))]
    args = [sink, q]
    if has_local:
        in_specs.append(pl.BlockSpec((1, n, 2 * KV_WIDTH), lambda bi, i: (bi, 0, 0)))
        args.append(kv)
    in_specs.append(pl.BlockSpec((1, lc, 2 * KV_WIDTH), lambda bi, i: (bi, 0, 0)))
    args.append(kvc)
    if has_local:
        nb = n // BLOCK
        assert nb >= 4
        in_specs.append(pl.BlockSpec((3, BLOCK, 3 * BLOCK), lambda bi, i: (0, 0, 0)))
        args.append(_window_bias(n))
    return pl.pallas_call(
        functools.partial(_attn_kernel, n=n, has_local=has_local),
        grid=(b, n // qrows),
        in_specs=in_specs,
        out_specs=pl.BlockSpec((1, qrows, ATTN_WIDTH), lambda bi, i: (bi, i, 0)),
        out_shape=jax.ShapeDtypeStruct((b, n, ATTN_WIDTH), MXU_DTYPE),
        compiler_params=_params("arbitrary", "arbitrary"),
        name="attn_local" if has_local else "attn_ctx",
    )(*args)


def _mix_out_kernel(*refs, conv, first, last):
    if conv:
        (att_ref, gb_ref, cu_ref, cup_ref, cun_ref, cw_ref, cb_ref,
         w_ref, x_ref, gate_ref, g2_ref, sh2_ref, sc2_ref, rwh_ref, rwl_ref, xo_ref, h_ref, afft_ref) = refs
        t = pl.program_id(1)
        prev = jnp.where(t == first, 0.0, cup_ref[0])
        nxt = jnp.where(t == last, 0.0, cun_ref[0])
        ext = jnp.concatenate([prev, cu_ref[0], nxt], axis=0)
        cw = cw_ref[...]
    else:
        (y_ref, w_ref, x_ref, gate_ref, g2_ref, sh2_ref, sc2_ref, rwh_ref, rwl_ref, xo_ref, h_ref, afft_ref) = refs
    tile = x_ref.shape[1]
    rows = min(MIX_OUT_ROWS, tile)
    for s in range(tile // rows):
        r0 = s * rows
        rs = slice(r0, r0 + rows)
        if conv:
            cv = (cw[0:1] * ext[HALO - 1 + r0:HALO - 1 + r0 + rows] + cw[1:2] * ext[HALO + r0:HALO + r0 + rows]
                  + cw[2:3] * ext[HALO + 1 + r0:HALO + 1 + r0 + rows] + cb_ref[...])
            cat = jnp.concatenate([att_ref[0, rs, :], (gb_ref[0, rs, :] * cv).astype(MXU_DTYPE)], axis=1)
        else:
            cat = jnp.concatenate([y_ref[0, c, rs, :] for c in range(y_ref.shape[1])], axis=1).astype(MXU_DTYPE)
        y = jnp.dot(cat, w_ref[...], preferred_element_type=F32)
        x = x_ref[0, rs, :] + gate_ref[0] * y
        xo_ref[0, rs, :] = x
        h = _norm_mod(x, g2_ref[...], sh2_ref[0], sc2_ref[0])
        _store_token_tiles(h_ref, h, r0)
        h_hi, h_lo = _split_hi_lo(h)
        both = jnp.dot(h_hi, jnp.concatenate([rwh_ref[...], rwl_ref[...]], axis=1), preferred_element_type=F32)
        logits = both[:, :LANES] + both[:, LANES:] + jnp.dot(h_lo, rwh_ref[...], preferred_element_type=F32)
        lane = lax.broadcasted_iota(jnp.int32, logits.shape, 1)
        logits = jnp.where(lane < N_EXPERTS, logits, NEG_INF)
        e = jnp.exp(logits - jnp.max(logits, axis=1, keepdims=True))
        aff = e / jnp.sum(e, axis=1, keepdims=True)
        afft_ref[0, :, rs] = aff.T[:N_EXPERTS]


def _mix_out(mix_in, conv_params, w, x, gate, g2, sh2, sc2, mod_row, rw_hi, rw_lo, tile):
    b, n, d = x.shape
    nt = n // tile
    conv = conv_params is not None
    row = (lambda bi: bi) if mod_row is None else (lambda bi: mod_row)
    tok = lambda width: pl.BlockSpec((1, tile, width), lambda bi, t: (bi, t, 0))
    modspec = pl.BlockSpec((1, 1, d), lambda bi, t: (row(bi), 0, 0))
    full = lambda a: pl.BlockSpec(a.shape, lambda bi, t: (0,) * a.ndim)
    if conv:
        att, gb, cu = mix_in
        cw, cb = conv_params
        per = tile // HALO
        in_specs = [tok(ATTN_WIDTH), tok(CONV_B_WIDTH), tok(CONV_B_WIDTH),
                    pl.BlockSpec((1, HALO, CONV_B_WIDTH), lambda bi, t: (bi, jnp.maximum(t * per - 1, 0), 0)),
                    pl.BlockSpec((1, HALO, CONV_B_WIDTH), lambda bi, t: (bi, jnp.minimum((t + 1) * per, n // HALO - 1), 0)),
                    full(cw), full(cb)]
        args = [att, gb, cu, cu, cu, cw, cb]
    else:
        in_specs = [_slab_spec(tile, lambda bi, t: (bi, 0, t, 0), d)]
        args = [mix_in]
    in_specs += [full(w), tok(d), modspec, full(g2), modspec, modspec, full(rw_hi), full(rw_lo)]
    args += [w, x, gate, g2, sh2, sc2, rw_hi, rw_lo]
    out_shape = (jax.ShapeDtypeStruct((b, n, d), F32), jax.ShapeDtypeStruct((b, n * SUBLANES, LANES), F32),
                 jax.ShapeDtypeStruct((b, N_EXPERTS, n), F32))
    out_specs = (tok(d), pl.BlockSpec((1, tile * SUBLANES, LANES), lambda bi, t: (bi, t, 0)),
                 pl.BlockSpec((1, N_EXPERTS, tile), lambda bi, t: (bi, 0, t)))
    return pl.pallas_call(
        functools.partial(_mix_out_kernel, conv=conv, first=0, last=nt - 1),
        grid=(b, nt),
        in_specs=in_specs,
        out_specs=out_specs,
        out_shape=out_shape,
        compiler_params=_params("arbitrary", "arbitrary"),
        name="even_out" if conv else "odd_out",
    )(*args)


def _cumsum_lanes(x):
    n = x.shape[1]
    r = lax.broadcasted_iota(jnp.int32, (CUMSUM_TILE, CUMSUM_TILE), 0)
    c = lax.broadcasted_iota(jnp.int32, (CUMSUM_TILE, CUMSUM_TILE), 1)
    tri = jnp.where(r <= c, 1.0, 0.0).astype(MXU_DTYPE)
    carry = jnp.zeros((x.shape[0], 1), F32)
    outs = []
    for k in range(n // CUMSUM_TILE):
        blk = x[:, k * CUMSUM_TILE:(k + 1) * CUMSUM_TILE].astype(MXU_DTYPE)
        loc = jnp.dot(blk, tri, preferred_element_type=F32) + carry
        outs.append(loc)
        carry = loc[:, CUMSUM_TILE - 1:CUMSUM_TILE]
    return jnp.concatenate(outs, axis=1)


SLOT_EMPTY = 1 << 20


def _select_kernel(at_ref, idx_ref, g_ref, *, cap):
    bblk, n_e, n = at_ref.shape
    at = at_ref[...].reshape(bblk * n_e, n)
    n_exp = bblk * n_e
    capf = float(cap)

    def count_ge(thr):
        return jnp.sum(jnp.where(at >= thr, 1.0, 0.0), axis=1, keepdims=True)

    def bit_body(_, c):
        lo_i, hi_i = c
        mid = lo_i + ((hi_i - lo_i) >> 1)
        ge = count_ge(lax.bitcast_convert_type(mid, F32)) >= capf
        return jnp.where(ge, mid, lo_i), jnp.where(ge, hi_i, mid)

    lo_i, hi_i = lax.fori_loop(
        0, 31, bit_body, (jnp.zeros((n_exp, 1), jnp.int32), jnp.full((n_exp, 1), 0x3F800001, jnp.int32)))

    def val_body(_, c):
        lo, hi = c
        mid = 0.5 * (lo + hi)
        ge = count_ge(mid) >= capf
        return jnp.where(ge, mid, lo), jnp.where(ge, hi, mid)

    lo, hi = lax.fori_loop(
        0, 24, val_body, (lax.bitcast_convert_type(lo_i, F32), lax.bitcast_convert_type(hi_i, F32)))
    need = capf - count_ge(hi)

    above = jnp.where(at >= hi, 1.0, 0.0)
    band = jnp.where(at >= lo, 1.0, 0.0) - above
    sel = above + band * jnp.where(_cumsum_lanes(band) <= need, 1.0, 0.0)
    rank = _cumsum_lanes(sel)

    lane = lax.broadcasted_iota(jnp.int32, (n_exp, n), 1)
    disp = jnp.where(sel > 0.5, lane + 1 - rank.astype(jnp.int32), SLOT_EMPTY)
    g = at
    for k in range(n.bit_length() - 1):
        step = 1 << k
        moving = ((disp >> k) & 1) == 1
        disp_in = pltpu.roll(disp, n - step, 1)
        arriving = ((disp_in >> k) & 1) == 1
        g = jnp.where(arriving, pltpu.roll(g, n - step, 1), g)
        disp = jnp.where(arriving, disp_in, jnp.where(moving, SLOT_EMPTY, disp))

    capp = pl.cdiv(cap, LANES) * LANES
    slot = lax.broadcasted_iota(jnp.int32, (n_exp, cap), 1)
    idx_ref[...] = jnp.clip(slot + disp[:, :cap], 0, n - 1).reshape(bblk, n_e, cap)
    for s in range(bblk):
        g_pad = jnp.concatenate([g[s * n_e:(s + 1) * n_e, :capp], jnp.zeros((LANES - n_e, capp), F32)], axis=0)
        g_ref[s] = g_pad.T[:cap]


def _select(aff_t, cap):
    b, n_exp, n = aff_t.shape
    assert n & (n - 1) == 0 and n % CUMSUM_TILE == 0 and n < SLOT_EMPTY
    bblk = b
    return pl.pallas_call(
        functools.partial(_select_kernel, cap=cap),
        grid=(b // bblk,),
        in_specs=[pl.BlockSpec((bblk, n_exp, n), lambda bi: (bi, 0, 0))],
        out_specs=(pl.BlockSpec((bblk, n_exp, cap), lambda bi: (bi, 0, 0)),
                   pl.BlockSpec((bblk, cap, LANES), lambda bi: (bi, 0, 0))),
        out_shape=(jax.ShapeDtypeStruct((b, n_exp, cap), jnp.int32), jax.ShapeDtypeStruct((b, cap, LANES), F32)),
        compiler_params=_params("arbitrary"),
        name="moe_select",
    )(aff_t)


def _gather_kernel(idx_ref, src_ref, xe_ref, xcm_ref, *, cap, n_exp):
    bblk = src_ref.shape[0]
    eblk, out_rows = xe_ref.shape[1], xe_ref.shape[2]
    stride = out_rows + SUBLANES
    for k in range(eblk):
        for s in range(bblk):
            base = ((pl.program_id(0) * bblk + s) * n_exp + pl.program_id(1) * eblk + k) * cap
            for r in range(cap):
                t = idx_ref[base + r]
                xcm_ref[k, pl.ds(s * cap + r, SUBLANES, stride=stride), :] = (
                    src_ref[s, pl.ds(pl.multiple_of(t * SUBLANES, SUBLANES), SUBLANES), :])
        used = bblk * cap
        if used < out_rows:
            for j in range(SUBLANES):
                xcm_ref[k, pl.ds(j * stride + used, out_rows - used), :] = jnp.zeros((out_rows - used, LANES), F32)
        xe_ref[0, k] = jnp.concatenate(
            [xcm_ref[k, pl.ds(j * stride, out_rows), :] for j in range(SUBLANES)], axis=1).astype(xe_ref.dtype)


def _gather(idx_flat, src, cap, bblk, out_rows):
    b, rows, _ = src.shape
    d = SUBLANES * LANES
    assert cap % SUBLANES == 0 and bblk * cap <= out_rows and b % bblk == 0 and N_EXPERTS % DISPATCH_EXPERTS == 0
    return pl.pallas_call(
        functools.partial(_gather_kernel, cap=cap, n_exp=N_EXPERTS),
        grid=(b // bblk, N_EXPERTS // DISPATCH_EXPERTS),
        in_specs=[pl.BlockSpec(memory_space=pltpu.SMEM),
                  pl.BlockSpec((bblk, rows, LANES), lambda bi, ei: (bi, 0, 0))],
        out_specs=pl.BlockSpec((1, DISPATCH_EXPERTS, out_rows, d), lambda bi, ei: (bi, ei, 0, 0)),
        out_shape=jax.ShapeDtypeStruct((b // bblk, N_EXPERTS, out_rows, d), MXU_DTYPE),
        scratch_shapes=[pltpu.VMEM((DISPATCH_EXPERTS, SUBLANES * (out_rows + SUBLANES), LANES), F32)],
        compiler_params=_params("arbitrary", "arbitrary"),
        name="moe_gather",
    )(idx_flat, src)


WEIGHT_CHUNKS = 2
WEIGHT_DMA_PRIORITY = 1
FFN_BLOCKS = 2


def _moe_ffn_kernel(*refs, layer, n_steps, has_ctx):
    if has_ctx:
        (x_ref, g_ref, xc_ref, gc_ref, wg_hbm, wu_hbm, wd_hbm, o_ref, oc_ref,
         wgu_s, wd_s, stage_in, stage_out, sem) = refs
    else:
        (x_ref, g_ref, wg_hbm, wu_hbm, wd_hbm, o_ref, wgu_s, wd_s, stage_in, stage_out, sem) = refs
    n_exp = pl.num_programs(0)
    e = pl.program_id(0)
    s = pl.program_id(1)
    slot = e % 2
    ff = wg_hbm.shape[3]
    chunks = []
    for hbm, dst, col0, stage in ((wg_hbm, wgu_s, 0, stage_in), (wu_hbm, wgu_s, ff, stage_in),
                                  (wd_hbm, wd_s, 0, stage_out)):
        n_rows = hbm.shape[2] // WEIGHT_CHUNKS
        chunks += [(hbm, dst, col0, stage, c * n_rows, n_rows) for c in range(WEIGHT_CHUNKS)]
    per_step = pl.cdiv(len(chunks), n_steps)

    def copy(c, expert):
        hbm, _, _, stage, r0, n_rows = chunks[c]
        return pltpu.make_async_copy(hbm.at[layer, expert, pl.ds(r0, n_rows), :], stage, sem.at[0])

    def cast(c, to_slot):
        hbm, dst, col0, stage, r0, n_rows = chunks[c]
        dst[to_slot, pl.ds(r0, n_rows), pl.ds(col0, hbm.shape[3])] = stage[...].astype(dst.dtype)

    @pl.when((e == 0) & (s == 0))
    def _():
        for c in range(len(chunks)):
            copy(c, 0).start()
            copy(c, 0).wait()
            cast(c, 0)

    has_next = e + 1 < n_exp
    for c in range(len(chunks)):
        if c % per_step == 0:
            @pl.when(has_next & (s == c // per_step))
            def _():
                copy(c, e + 1).start(priority=WEIGHT_DMA_PRIORITY)

    def ffn(xr, gr, outr, k):
        x, aff = xr[k, 0], gr[k]
        rows = x.shape[0]
        au = jnp.dot(x, wgu_s[slot], preferred_element_type=F32)
        a, u = au[:, :ff], au[:, ff:]
        mid = (a * _sigmoid(a) * u).astype(MXU_DTYPE)
        y = jnp.dot(mid, wd_s[slot], preferred_element_type=F32)
        lane = lax.broadcasted_iota(jnp.int32, aff.shape, 1)
        y = y * jnp.sum(jnp.where(lane == e, aff, 0.0), axis=1, keepdims=True)
        stride = rows + SUBLANES
        for j in range(SUBLANES):
            outr[k, 0, pl.ds(j * stride, rows), :] = y[:, j * LANES:(j + 1) * LANES]
            outr[k, 0, pl.ds(j * stride + rows, SUBLANES), :] = jnp.zeros((SUBLANES, LANES), F32)

    n_blocks = x_ref.shape[0]
    is_ctx = (s == n_steps - 1) if has_ctx else None
    for k in range(n_blocks):
        if has_ctx:
            if k == 0:
                pl.when(is_ctx)(lambda: ffn(xc_ref, gc_ref, oc_ref, 0))
            pl.when(jnp.logical_not(is_ctx))(lambda k=k: ffn(x_ref, g_ref, o_ref, k))
        else:
            pl.when(s < n_steps)(lambda k=k: ffn(x_ref, g_ref, o_ref, k))
        for c in range(len(chunks)):
            if min(c % per_step, n_blocks - 1) == k:
                @pl.when(has_next & (s == c // per_step))
                def _():
                    copy(c, e + 1).wait()
                    cast(c, 1 - slot)
                    if (c + 1) % per_step != 0 and c + 1 < len(chunks):
                        copy(c + 1, e + 1).start(priority=WEIGHT_DMA_PRIORITY)


def _moe_ffn(xe, ge, ctx_block, layer, wg, wu, wd):
    nb, e, rows, d = xe.shape
    ff = wg.shape[-1]
    assert d % WEIGHT_CHUNKS == 0 and ff % (WEIGHT_CHUNKS * SUBLANES) == 0 and ff % LANES == 0
    has_ctx = ctx_block is not None
    fb = FFN_BLOCKS if nb % FFN_BLOCKS == 0 else 1
    n_steps = nb // fb + int(has_ctx)
    hbm = pl.BlockSpec(memory_space=pl.ANY)
    out_rows = SUBLANES * (rows + SUBLANES)
    blk = lambda bi: jnp.minimum(bi, nb // fb - 1)
    in_specs = [pl.BlockSpec((fb, 1, rows, d), lambda ei, bi: (blk(bi), ei, 0, 0)),
                pl.BlockSpec((fb, rows, LANES), lambda ei, bi: (blk(bi), 0, 0))]
    args = [xe, ge]
    out_specs = [pl.BlockSpec((fb, 1, out_rows, LANES), lambda ei, bi: (blk(bi), ei, 0, 0))]
    out_shape = [jax.ShapeDtypeStruct((nb, e, out_rows, LANES), F32)]
    if has_ctx:
        rows_c = ctx_block[0].shape[2]
        assert ctx_block[0].shape == (1, e, rows_c, d) and ctx_block[1].shape == (1, rows_c, LANES)
        out_rows_c = SUBLANES * (rows_c + SUBLANES)
        in_specs += [pl.BlockSpec((1, 1, rows_c, d), lambda ei, bi: (0, ei, 0, 0)),
                     pl.BlockSpec((1, rows_c, LANES), lambda ei, bi: (0, 0, 0))]
        args += list(ctx_block)
        out_specs.append(pl.BlockSpec((1, 1, out_rows_c, LANES), lambda ei, bi: (0, ei, 0, 0)))
        out_shape.append(jax.ShapeDtypeStruct((1, e, out_rows_c, LANES), F32))
    return pl.pallas_call(
        functools.partial(_moe_ffn_kernel, layer=layer, n_steps=n_steps, has_ctx=has_ctx),
        grid=(e, n_steps),
        in_specs=in_specs + [hbm, hbm, hbm],
        out_specs=tuple(out_specs),
        out_shape=tuple(out_shape),
        scratch_shapes=[pltpu.VMEM((2, d, 2 * ff), MXU_DTYPE), pltpu.VMEM((2, ff, d), MXU_DTYPE),
                        pltpu.VMEM((d // WEIGHT_CHUNKS, ff), F32), pltpu.VMEM((ff // WEIGHT_CHUNKS, d), F32),
                        pltpu.SemaphoreType.DMA((1,))],
        compiler_params=_params("arbitrary", "arbitrary"),
        name="moe_ffn",
    )(*args, wg, wu, wd)


COMBINE_UNROLL = 8


def _combine_kernel(idx_ref, y_ref, acc_ref, *, cap, n_exp):
    bblk = acc_ref.shape[0]
    eblk = y_ref.shape[1]
    ei = pl.program_id(1)
    stride = y_ref.shape[2] // SUBLANES

    @pl.when(ei == 0)
    def _():
        acc_ref[...] = jnp.zeros_like(acc_ref)

    for k in range(eblk):
        for s in range(bblk):
            base = ((pl.program_id(0) * bblk + s) * n_exp + ei * eblk + k) * cap
            for r0 in range(0, cap, COMBINE_UNROLL):
                toks = [pl.multiple_of(idx_ref[base + r0 + i] * SUBLANES, SUBLANES) for i in range(COMBINE_UNROLL)]
                new = [acc_ref[s, pl.ds(toks[i], SUBLANES), :]
                       + y_ref[0, k, pl.ds(s * cap + r0 + i, SUBLANES, stride=stride), :]
                       for i in range(COMBINE_UNROLL)]
                for i in range(COMBINE_UNROLL):
                    acc_ref[s, pl.ds(toks[i], SUBLANES), :] = new[i]


def _combine(idx_flat, y, b, n, cap, bblk):
    nb, n_exp, rows, _ = y.shape
    assert cap % COMBINE_UNROLL == 0 and nb * bblk == b and bblk * cap <= rows // SUBLANES - SUBLANES
    return pl.pallas_call(
        functools.partial(_combine_kernel, cap=cap, n_exp=n_exp),
        grid=(nb, n_exp // DISPATCH_EXPERTS),
        in_specs=[pl.BlockSpec(memory_space=pltpu.SMEM),
                  pl.BlockSpec((1, DISPATCH_EXPERTS, rows, LANES), lambda bi, ei: (bi, ei, 0, 0))],
        out_specs=pl.BlockSpec((bblk, n * SUBLANES, LANES), lambda bi, ei: (bi, 0, 0)),
        out_shape=jax.ShapeDtypeStruct((b, n * SUBLANES, LANES), F32),
        compiler_params=_params("arbitrary", "arbitrary"),
        name="moe_combine",
    )(idx_flat, y)


def _expert_choice(h_lat, aff_lat, h_ctx, aff_ctx, layer, wg, wu, wd):
    b, _, n = aff_lat.shape
    cap = max(1, CAPACITY_FACTOR * n // N_EXPERTS)
    idx, g = _select(aff_lat, cap)
    idx = idx.reshape(-1)
    xe = _gather(idx, h_lat, cap, 1, cap)
    if h_ctx is None:
        (y,) = _moe_ffn(xe, g, None, layer, wg, wu, wd)
        return _combine(idx, y, b, n, cap, 1), None
    lc = aff_ctx.shape[2]
    cap_c = max(1, CAPACITY_FACTOR * lc // N_EXPERTS)
    idx_c, g_c = _select(aff_ctx, cap_c)
    idx_c = idx_c.reshape(-1)
    xe_c = _gather(idx_c, h_ctx, cap_c, b, b * cap_c)
    g_c = g_c.reshape(1, b * cap_c, LANES)
    y, y_c = _moe_ffn(xe, g, (xe_c, g_c), layer, wg, wu, wd)
    return _combine(idx, y, b, n, cap, 1), _combine(idx_c, y_c, b, lc, cap_c, b)


def _odd_in_kernel(x_ref, moe_ref, gate_ref, g_ref, sh_ref, sc_ref, w_ref, xo_ref, gg_ref, u_ref):
    x = x_ref[0] + gate_ref[0] * _load_token_tiles(moe_ref)
    xo_ref[0] = x
    h = _norm_mod(x, g_ref[...], sh_ref[0], sc_ref[0]).astype(MXU_DTYPE)
    y = jnp.dot(h, w_ref[...], preferred_element_type=F32)
    gl = y[:, :LRU_WIDTH]
    _store_slabs(gg_ref, 0.25 * gl * (1.0 + jnp.tanh(0.7978845608028654 * (gl + 0.044715 * gl * gl * gl))))
    _store_slabs(u_ref, y[:, LRU_WIDTH:])


def _odd_in(x, moe, gate, g, shift, scale, mod_row, w, tile):
    b, n, d = x.shape
    row = (lambda bi: bi) if mod_row is None else (lambda bi: mod_row)
    tok = pl.BlockSpec((1, tile, d), lambda bi, t: (bi, t, 0))
    tiles = pl.BlockSpec((1, tile * SUBLANES, LANES), lambda bi, t: (bi, t, 0))
    modspec = pl.BlockSpec((1, 1, d), lambda bi, t: (row(bi), 0, 0))
    return pl.pallas_call(
        _odd_in_kernel,
        grid=(b, n // tile),
        in_specs=[tok, tiles, modspec, pl.BlockSpec((1, d), lambda bi, t: (0, 0)), modspec, modspec,
                  pl.BlockSpec(w.shape, lambda bi, t: (0, 0))],
        out_specs=(tok, _slab_spec(tile, lambda bi, t: (bi, 0, t, 0)), _slab_spec(tile, lambda bi, t: (bi, 0, t, 0))),
        out_shape=(jax.ShapeDtypeStruct((b, n, d), F32),) + (jax.ShapeDtypeStruct((b, d // LANES, n, LANES), F32),) * 2,
        compiler_params=_params("arbitrary", "arbitrary"),
        name="odd_in",
    )(x, moe, gate, g, shift, scale, w)


def _lru_kernel(*refs, reverse, nt):
    if reverse:
        (uc_ref, wax_ref, lam_ref, h0_ref, hf_ref, gg_ref, out_ref, hlast_ref, a_scr, b_scr, uc_scr, carry_scr) = refs
        n_slab, tile = uc_ref.shape[1], uc_ref.shape[2]
    else:
        (u_ref, up_ref, un_ref, cw_ref, cb_ref, wax_ref, lam_ref, h0_ref,
         out_ref, uco_ref, hlast_ref, a_scr, b_scr, uc_scr, carry_scr) = refs
        n_slab, tile = u_ref.shape[1], u_ref.shape[2]
    w = n_slab * LANES
    per = tile // SUBLANES
    rows = lambda j: slice(j * per, (j + 1) * per)
    phase = lambda ref, j: jnp.concatenate(
        [ref[0, c, pl.ds(j, per, stride=SUBLANES), :] for c in range(n_slab)], axis=1)
    phase_major = lambda ref, j: jnp.concatenate([ref[0, c, rows(j), :] for c in range(n_slab)], axis=1)

    def store_phase(ref, j, val, time_order):
        for c in range(n_slab):
            dst = pl.ds(j, per, stride=SUBLANES) if time_order else rows(j)
            ref[0, c, dst, :] = val[:, c * LANES:(c + 1) * LANES]

    t = pl.program_id(1)
    if reverse:
        for j in range(SUBLANES):
            uc_scr[rows(j), :] = phase_major(uc_ref, j)
    else:
        prev = jnp.where(t == 0, 0.0, _load_slabs(up_ref))
        nxt = jnp.where(t == nt - 1, 0.0, _load_slabs(un_ref))
        rowid = lax.broadcasted_iota(jnp.int32, (per, w), 0)
        shift_down = lambda x, first: jnp.where(rowid == 0, first, pltpu.roll(x, 1, 0))
        shift_up = lambda x, last: jnp.where(rowid == per - 1, last, pltpu.roll(x, per - 1, 0))
        u = [phase(u_ref, j) for j in range(SUBLANES)]
        um1 = [shift_down(u[7], prev[7:8])] + u[:7]
        um2 = [shift_down(u[6], prev[6:7]), um1[0]] + u[:6]
        up1 = u[1:] + [shift_up(u[0], nxt[0:1])]
        cw = cw_ref[...]
        for j in range(SUBLANES):
            uc_j = cb_ref[...] + cw[0:1] * um2[j] + cw[1:2] * um1[j] + cw[2:3] * u[j] + cw[3:4] * up1[j]
            uc_scr[rows(j), :] = uc_j
            store_phase(uco_ref, j, uc_j, False)

    lam = lam_ref[0]
    half_decay = (0.5 * LRU_C) * (jnp.maximum(-lam, 0.0) + jnp.log1p(jnp.exp(-jnp.abs(lam))))
    ones = jnp.where(lax.broadcasted_iota(jnp.int32, (tile, LRU_BLOCK), 1) < BIAS_TERMS, 1.0, 0.0).astype(MXU_DTYPE)
    for hd in range(LRU_HEADS):
        sl = slice(hd * LRU_BLOCK, (hd + 1) * LRU_BLOCK)
        uc = uc_scr[:, sl]
        z = jnp.dot(jnp.concatenate([uc.astype(MXU_DTYPE), ones], axis=1), wax_ref[0, hd], preferred_element_type=F32)
        hd_row = half_decay[:, sl]
        neg_log_a = hd_row * jnp.tanh(z[:, :LRU_BLOCK]) + hd_row
        gate2 = 1.0 + jnp.tanh(z[:, LRU_BLOCK:])
        a = jnp.exp(-neg_log_a)
        m2 = jnp.tanh(neg_log_a) * (a * a + 1.0)
        mult = jnp.where(m2 > 0.0, m2 * lax.rsqrt(m2), 0.0)
        a_scr[:, sl] = a
        b_scr[:, sl] = mult * (gate2 * uc)

    @pl.when(t == 0)
    def _():
        carry_scr[...] = jnp.broadcast_to(h0_ref[0], carry_scr.shape)

    order = list(range(SUBLANES))[::-1] if reverse else list(range(SUBLANES))
    hrun = b_scr[rows(order[0]), :]
    prun = a_scr[rows(order[0]), :]
    for j in order[1:]:
        aj = a_scr[rows(j), :]
        hrun = aj * hrun + b_scr[rows(j), :]
        prun = aj * prun
        b_scr[rows(j), :] = hrun
        a_scr[rows(j), :] = prun

    lane_row = lax.broadcasted_iota(jnp.int32, (SUBLANES, w), 0)
    carry = carry_scr[...]
    groups = list(range(per // SUBLANES))
    entering = [None] * len(groups)
    for m in (groups[::-1] if reverse else groups):
        a = prun[m * SUBLANES:(m + 1) * SUBLANES]
        bcoef = hrun[m * SUBLANES:(m + 1) * SUBLANES]
        for dist in (1, 2, 4):
            shift = (SUBLANES - dist) if reverse else dist
            msk = (lane_row < SUBLANES - dist) if reverse else (lane_row >= dist)
            a_s = pltpu.roll(a, shift, 0)
            b_s = pltpu.roll(bcoef, shift, 0)
            bcoef = jnp.where(msk, a * b_s + bcoef, bcoef)
            a = jnp.where(msk, a * a_s, a)
        after = a * carry + bcoef
        if reverse:
            entering[m] = jnp.where(lane_row == SUBLANES - 1, carry, pltpu.roll(after, SUBLANES - 1, 0))
            carry = jnp.broadcast_to(after[0:1], carry.shape)
        else:
            entering[m] = jnp.where(lane_row == 0, carry, pltpu.roll(after, 1, 0))
            carry = jnp.broadcast_to(after[SUBLANES - 1:SUBLANES], carry.shape)
    carry_scr[...] = carry
    hlast_ref[0] = carry[0:1]
    h_in = jnp.concatenate(entering, axis=0)

    for j in range(SUBLANES):
        hcur = b_scr[rows(j), :] + a_scr[rows(j), :] * h_in
        if reverse:
            hcur = phase(gg_ref, j) * (phase_major(hf_ref, j) + hcur)
        store_phase(out_ref, j, hcur, reverse)


BIAS_TERMS = 3


def _gate_weights(wa, wx, ba, bx):
    n_dir, heads, blk, _ = wa.shape
    w = 0.5 * jnp.concatenate([wa, wx], axis=-1)
    bias = 0.5 * jnp.concatenate([ba.reshape(n_dir, heads, blk), bx.reshape(n_dir, heads, blk)], axis=-1)
    terms = []
    for _ in range(BIAS_TERMS):
        term = bias.astype(MXU_DTYPE)
        terms.append(term)
        bias = bias - term.astype(F32)
    rows = jnp.stack(terms, axis=2)
    pad = jnp.zeros((n_dir, heads, blk - BIAS_TERMS, 2 * blk), MXU_DTYPE)
    return jnp.concatenate([w.astype(MXU_DTYPE), rows, pad], axis=2)


def _lru_fwd(u, cw, cb, wax, lam, h0, tile):
    b, n_slab, n, _ = u.shape
    w = n_slab * LANES
    nt = n // tile
    per = tile // HALO
    assert tile % (SUBLANES * SUBLANES) == 0
    tok = _slab_spec(tile, lambda bi, t: (bi, 0, t, 0), w)
    state = pl.BlockSpec((1, 1, w), lambda bi, t: (bi, 0, 0))
    return pl.pallas_call(
        functools.partial(_lru_kernel, reverse=False, nt=nt),
        grid=(b, nt),
        in_specs=[tok,
                  _slab_spec(HALO, lambda bi, t: (bi, 0, jnp.maximum(t * per - 1, 0), 0), w),
                  _slab_spec(HALO, lambda bi, t: (bi, 0, jnp.minimum((t + 1) * per, n // HALO - 1), 0), w),
                  pl.BlockSpec(cw.shape, lambda bi, t: (0, 0)),
                  pl.BlockSpec(cb.shape, lambda bi, t: (0, 0)),
                  pl.BlockSpec((1,) + wax.shape[1:], lambda bi, t: (0, 0, 0, 0)),
                  pl.BlockSpec((1, 1, w), lambda bi, t: (0, 0, 0)),
                  state],
        out_specs=(tok, tok, state),
        out_shape=(jax.ShapeDtypeStruct(u.shape, F32), jax.ShapeDtypeStruct(u.shape, F32),
                   jax.ShapeDtypeStruct((b, 1, w), F32)),
        scratch_shapes=[pltpu.VMEM((tile, w), F32), pltpu.VMEM((tile, w), F32), pltpu.VMEM((tile, w), F32),
                        pltpu.VMEM((SUBLANES, w), F32)],
        compiler_params=_params("arbitrary", "arbitrary"),
        name="lru_fwd",
    )(u, u, u, cw, cb, wax, lam, h0)


def _lru_bwd(uc, wax, lam, h0, hf, gg, tile):
    b, n_slab, n, _ = uc.shape
    w = n_slab * LANES
    nt = n // tile
    assert tile % (SUBLANES * SUBLANES) == 0
    tok = _slab_spec(tile, lambda bi, t: (bi, 0, nt - 1 - t, 0), w)
    state = pl.BlockSpec((1, 1, w), lambda bi, t: (bi, 0, 0))
    return pl.pallas_call(
        functools.partial(_lru_kernel, reverse=True, nt=nt),
        grid=(b, nt),
        in_specs=[tok,
                  pl.BlockSpec((1,) + wax.shape[1:], lambda bi, t: (1, 0, 0, 0)),
                  pl.BlockSpec((1, 1, w), lambda bi, t: (1, 0, 0)),
                  state, tok, tok],
        out_specs=(tok, state),
        out_shape=(jax.ShapeDtypeStruct(uc.shape, F32), jax.ShapeDtypeStruct((b, 1, w), F32)),
        scratch_shapes=[pltpu.VMEM((tile, w), F32), pltpu.VMEM((tile, w), F32), pltpu.VMEM((tile, w), F32),
                        pltpu.VMEM((SUBLANES, w), F32)],
        compiler_params=_params("arbitrary", "arbitrary"),
        name="lru_bwd",
    )(uc, wax, lam, h0, hf, gg)


def _final_kernel(x_ref, moe_ref, gate_ref, g_ref, o_ref):
    x = x_ref[0] + gate_ref[0] * _load_token_tiles(moe_ref)
    o_ref[0] = x * lax.rsqrt(jnp.mean(x * x, axis=-1, keepdims=True) + EPS) * g_ref[...]


def _final(x, moe, gate, g, tile):
    b, n, d = x.shape
    tok = pl.BlockSpec((1, tile, d), lambda bi, t: (bi, t, 0))
    tiles = pl.BlockSpec((1, tile * SUBLANES, LANES), lambda bi, t: (bi, t, 0))
    return pl.pallas_call(
        _final_kernel,
        grid=(b, n // tile),
        in_specs=[tok, tiles, pl.BlockSpec((1, 1, d), lambda bi, t: (bi, 0, 0)), pl.BlockSpec((1, d), lambda bi, t: (0, 0))],
        out_specs=tok,
        out_shape=jax.ShapeDtypeStruct((b, n, d), F32),
        compiler_params=_params("arbitrary", "arbitrary"),
        name="final_norm",
    )(x, moe, gate, g)


def kernel(x, c, ctx, c_ctx, ada_w, ada_b, norm_mix_g, norm_ffn_g, ev_w_in, ev_w_out, ev_sink, ev_conv_w, ev_conv_b, od_w_in, od_w_out, od_conv_w, od_conv_b, od_wa, od_ba, od_wx, od_bx, od_lambda, router_w, w_gate, w_up, w_down, final_g):
    b, n, d = x.shape
    lc = ctx.shape[1]
    depth = ada_w.shape[0]
    assert depth == 2 and d == D_MODEL and b < MOD_ROWS
    tile_l = tile_w = min(1024, n)
    tile_c = lc
    ctx_row = b

    cvec = jnp.zeros((MOD_ROWS, d), F32).at[:b].set(c).at[b].set(c_ctx)
    mods = _ada(cvec, ada_w, ada_b).reshape(depth, MOD_ROWS, 6, 1, d)
    mod = lambda l, j: mods[l, :, j]

    def router_split(l):
        return _split_hi_lo(jnp.pad(router_w[l], ((0, 0), (0, LANES - N_EXPERTS))))

    bf = lambda a: a.astype(MXU_DTYPE)

    g_mix = norm_mix_g[0].reshape(1, d)
    g_ffn = norm_ffn_g[0].reshape(1, d)
    w_in = bf(ev_w_in[0])
    w_out = bf(ev_w_out[0])
    conv_p = (ev_conv_w[0], ev_conv_b[0].reshape(1, -1))
    tables = _rope_tables(n)
    ql, kvl, gbl, cul = _even_in(x, g_mix, mod(0, 0), mod(0, 1), None, w_in, tables, tile_l)
    qc, kvc, gbc, cuc = _even_in(ctx, g_mix, mod(0, 0), mod(0, 1), ctx_row, w_in, None, tile_c)
    att_l = _attention(ev_sink[0], ql, kvl, kvc)
    att_c = _attention(ev_sink[0], qc, None, kvc)
    rw_hi, rw_lo = router_split(0)
    xl, hl, at_l = _mix_out((att_l, gbl, cul), conv_p, w_out, x, mod(0, 2), g_ffn, mod(0, 3), mod(0, 4), None,
                            rw_hi, rw_lo, tile_w)
    xc, hc, at_c = _mix_out((att_c, gbc, cuc), conv_p, w_out, ctx, mod(0, 2), g_ffn, mod(0, 3), mod(0, 4),
                            ctx_row, rw_hi, rw_lo, tile_c)
    moe_l, moe_c = _expert_choice(hl, at_l, hc, at_c, 0, w_gate, w_up, w_down)

    g_mix = norm_mix_g[1].reshape(1, d)
    g_ffn = norm_ffn_g[1].reshape(1, d)
    w_in = bf(od_w_in[0])
    w_out = bf(od_w_out[0])
    xl, ggl, ul = _odd_in(xl, moe_l, mod(0, 5), g_mix, mod(1, 0), mod(1, 1), None, w_in, tile_l)
    _, _, u_ctx = _odd_in(xc, moe_c, mod(0, 5), g_mix, mod(1, 0), mod(1, 1), ctx_row, w_in, tile_c)
    cw, cb = od_conv_w[0], od_conv_b[0].reshape(1, -1)
    wax = _gate_weights(od_wa[0], od_wx[0], od_ba[0], od_bx[0])
    lam = od_lambda[0].reshape(2, 1, -1)
    zero_state = jnp.zeros((b, 1, LRU_WIDTH), F32)
    hf_c, uc_c, h0_f = _lru_fwd(u_ctx, cw, cb, wax, lam, zero_state, tile_c)
    _, h0_b = _lru_bwd(uc_c, wax, lam, zero_state, hf_c, hf_c, tile_c)
    hf_l, uc_l, _ = _lru_fwd(ul, cw, cb, wax, lam, h0_f, tile_w)
    yl, _ = _lru_bwd(uc_l, wax, lam, h0_b, hf_l, ggl, tile_w)
    rw_hi, rw_lo = router_split(1)
    xl, hl, at_l = _mix_out(yl, None, w_out, xl, mod(1, 2), g_ffn, mod(1, 3), mod(1, 4), None,
                            rw_hi, rw_lo, tile_w)
    moe_l, _ = _expert_choice(hl, at_l, None, None, 1, w_gate, w_up, w_down)
    return _final(xl, moe_l, mod(1, 5), final_g.reshape(1, d), tile_w)
```

```python
import functools

import jax
import jax.numpy as jnp
from jax import lax
from jax.experimental import pallas as pl
from jax.experimental.pallas import tpu as pltpu

F32 = jnp.float32
MXU_DTYPE = jnp.bfloat16

D_MODEL = 1024
GRID_W = 64
EPS = 1e-6
NEG_INF = -1e30
HEAD_DIM = 64
N_Q_HEADS = 8
N_KV_HEADS = 2
Q_PER_KV = N_Q_HEADS // N_KV_HEADS
ATTN_WIDTH = N_Q_HEADS * HEAD_DIM
KV_WIDTH = N_KV_HEADS * HEAD_DIM
WINDOW = 128
BLOCK = 128
ROPE_BASE = 10000.0
CONV_B_WIDTH = D_MODEL // 2
EVEN_IN = ATTN_WIDTH + 2 * KV_WIDTH + 3 * CONV_B_WIDTH
LRU_WIDTH = D_MODEL
LRU_HEADS = 8
LRU_BLOCK = LRU_WIDTH // LRU_HEADS
LRU_C = 8.0
N_EXPERTS = 16
CAPACITY_FACTOR = 2
EXPERT_FF = 1408
MOD_ROWS = 16
LANES = 128
SUBLANES = 8
HALO = SUBLANES
CUMSUM_TILE = 256
MIX_OUT_ROWS = 256
ATTN_QBLOCKS = 8
ATTN_HEADS_PER_DOT = 4
DISPATCH_EXPERTS = 4
V7X_VMEM_LIMIT = 56 * 1024 * 1024


def _params(*sem):
    return pltpu.CompilerParams(dimension_semantics=sem, vmem_limit_bytes=V7X_VMEM_LIMIT)


def _split_hi_lo(a):
    hi = a.astype(MXU_DTYPE)
    lo = (a - hi.astype(F32)).astype(MXU_DTYPE)
    return hi, lo


def _sigmoid(z):
    return 0.5 * (1.0 + jnp.tanh(0.5 * z))


def _store_token_tiles(ref, val, row0=0):
    tile = val.shape[0]
    for j in range(SUBLANES):
        ref[0, pl.ds(row0 * SUBLANES + j, tile, stride=SUBLANES), :] = val[:, j * LANES:(j + 1) * LANES]


def _load_token_tiles(ref):
    tile = ref.shape[1] // SUBLANES
    return jnp.concatenate([ref[0, pl.ds(j, tile, stride=SUBLANES), :] for j in range(SUBLANES)], axis=1)


def _slab_spec(rows, index_map, width=LRU_WIDTH):
    return pl.BlockSpec((1, width // LANES, rows, LANES), index_map)


def _store_slabs(ref, val):
    for c in range(val.shape[1] // LANES):
        ref[0, c] = val[:, c * LANES:(c + 1) * LANES]


def _load_slabs(ref):
    return jnp.concatenate([ref[0, c] for c in range(ref.shape[1])], axis=1)


def _norm_mod(x, g, shift, scale):
    y = x * lax.rsqrt(jnp.mean(x * x, axis=-1, keepdims=True) + EPS) * g
    return y * (1.0 + scale) + shift


def _ada_kernel(c_ref, w_ref, b_ref, o_ref):
    c = c_ref[...]
    s_hi, s_lo = _split_hi_lo(c * _sigmoid(c))
    w_hi, w_lo = _split_hi_lo(w_ref[0])
    acc = jnp.dot(s_hi, w_hi, preferred_element_type=F32)
    acc += jnp.dot(s_hi, w_lo, preferred_element_type=F32)
    acc += jnp.dot(s_lo, w_hi, preferred_element_type=F32)
    o_ref[0] = acc + b_ref[0]


def _ada(cvec, ada_w, ada_b):
    depth, d, n6 = ada_w.shape
    tn = 1536
    return pl.pallas_call(
        _ada_kernel,
        grid=(depth, n6 // tn),
        in_specs=[
            pl.BlockSpec((MOD_ROWS, d), lambda l, j: (0, 0)),
            pl.BlockSpec((1, d, tn), lambda l, j: (l, 0, j)),
            pl.BlockSpec((1, 1, tn), lambda l, j: (l, 0, j)),
        ],
        out_specs=pl.BlockSpec((1, MOD_ROWS, tn), lambda l, j: (l, 0, j)),
        out_shape=jax.ShapeDtypeStruct((depth, MOD_ROWS, n6), F32),
        compiler_params=_params("arbitrary", "arbitrary"),
        name="ada",
    )(cvec, ada_w, ada_b.reshape(depth, 1, n6))


def _even_in_kernel(*refs, rope):
    if rope:
        (x_ref, g_ref, sh_ref, sc_ref, w_ref, cos_ref, sa_ref, sb_ref, q_ref, kv_ref, gb_ref, cu_ref) = refs
    else:
        (x_ref, g_ref, sh_ref, sc_ref, w_ref, q_ref, kv_ref, gb_ref, cu_ref) = refs
    h = _norm_mod(x_ref[0], g_ref[...], sh_ref[0], sc_ref[0]).astype(MXU_DTYPE)
    y = jnp.dot(h, w_ref[...], preferred_element_type=F32)
    q = y[:, :ATTN_WIDTH]
    k = y[:, ATTN_WIDTH:ATTN_WIDTH + KV_WIDTH]
    v = y[:, ATTN_WIDTH + KV_WIDTH:ATTN_WIDTH + 2 * KV_WIDTH]
    c0 = ATTN_WIDTH + 2 * KV_WIDTH
    if rope:
        cos, sa, sb = cos_ref[...], sa_ref[...], sb_ref[...]

        def rot(z):
            return z * cos + pltpu.roll(z, 16, 1) * sa + pltpu.roll(z, 112, 1) * sb

        q = jnp.concatenate([rot(q[:, j * 128:(j + 1) * 128]) for j in range(ATTN_WIDTH // 128)], axis=1)
        k = rot(k)
    q_ref[0] = (q * (HEAD_DIM ** -0.5)).astype(q_ref.dtype)
    kv_ref[0] = jnp.concatenate([k, v], axis=1).astype(kv_ref.dtype)
    gb_ref[0] = y[:, c0:c0 + CONV_B_WIDTH]
    cu_ref[0] = y[:, c0 + CONV_B_WIDTH:c0 + 2 * CONV_B_WIDTH] * y[:, c0 + 2 * CONV_B_WIDTH:]


def _even_in(x, g, shift, scale, mod_row, w, tables, tile):
    b, n, d = x.shape
    nt = n // tile
    rope = tables is not None
    row = (lambda bi: bi) if mod_row is None else (lambda bi: mod_row)
    in_specs = [
        pl.BlockSpec((1, tile, d), lambda bi, t: (bi, t, 0)),
        pl.BlockSpec((1, d), lambda bi, t: (0, 0)),
        pl.BlockSpec((1, 1, d), lambda bi, t: (row(bi), 0, 0)),
        pl.BlockSpec((1, 1, d), lambda bi, t: (row(bi), 0, 0)),
        pl.BlockSpec(w.shape, lambda bi, t: (0, 0)),
    ]
    args = [x, g, shift, scale, w]
    if rope:
        in_specs += [pl.BlockSpec((tile, 128), lambda bi, t: (t, 0))] * 3
        args += list(tables)
    out_shape = (
        jax.ShapeDtypeStruct((b, n, ATTN_WIDTH), MXU_DTYPE),
        jax.ShapeDtypeStruct((b, n, 2 * KV_WIDTH), MXU_DTYPE),
        jax.ShapeDtypeStruct((b, n, CONV_B_WIDTH), F32),
        jax.ShapeDtypeStruct((b, n, CONV_B_WIDTH), F32),
    )
    out_specs = tuple(pl.BlockSpec((1, tile, s.shape[-1]), lambda bi, t: (bi, t, 0)) for s in out_shape)
    return pl.pallas_call(
        functools.partial(_even_in_kernel, rope=rope),
        grid=(b, nt),
        in_specs=in_specs,
        out_specs=out_specs,
        out_shape=out_shape,
        compiler_params=_params("arbitrary", "arbitrary"),
        name="even_in_rope" if rope else "even_in",
    )(*args)


def _rope_tables(n):
    nf = HEAD_DIM // 4
    pos = jnp.arange(n)
    rows = (pos // GRID_W).astype(F32)
    cols = (pos % GRID_W).astype(F32)
    lane = jnp.arange(128)
    inv = ROPE_BASE ** (-(lane % nf).astype(F32) / nf)
    use_col = (lane % HEAD_DIM) >= HEAD_DIM // 2
    ang = jnp.where(use_col[None, :], cols[:, None], rows[:, None]) * inv[None, :]
    cos, sin = jnp.cos(ang), jnp.sin(ang)
    second = ((lane % (2 * nf)) >= nf)[None, :]
    return cos, jnp.where(second, sin, 0.0), jnp.where(second, 0.0, -sin)


def _attn_kernel(*refs, n, has_local):
    if has_local:
        sink_ref, q_ref, kv_ref, kvc_ref, bias_ref, o_ref = refs
    else:
        sink_ref, q_ref, kvc_ref, o_ref = refs
    n_loc = 3 * BLOCK
    nb = n // BLOCK
    hpd = ATTN_HEADS_PER_DOT
    grp = lax.broadcasted_iota(jnp.int32, (hpd * BLOCK, 1), 0) // BLOCK
    for sb in range(q_ref.shape[1] // BLOCK):
        i = pl.program_id(1) * (q_ref.shape[1] // BLOCK) + sb
        q = q_ref[0, sb * BLOCK:(sb + 1) * BLOCK, :]
        kvall = kvc_ref[0]
        if has_local:
            start = pl.multiple_of(_local_start(i, n), BLOCK)
            kvall = jnp.concatenate([kv_ref[0, pl.ds(start, n_loc), :], kvall], axis=0)
            case = jnp.where(i == 0, 0, jnp.where(i == nb - 1, 2, 1))
            bias = jnp.concatenate([bias_ref[case]] * hpd, axis=0)
        outs = []
        for h0 in range(0, N_Q_HEADS, hpd):
            hk = h0 // Q_PER_KV
            kh = kvall[:, hk * HEAD_DIM:(hk + 1) * HEAD_DIM]
            vh = kvall[:, KV_WIDTH + hk * HEAD_DIM:KV_WIDTH + (hk + 1) * HEAD_DIM]
            qg = jnp.concatenate([q[:, (h0 + g) * HEAD_DIM:(h0 + g + 1) * HEAD_DIM] for g in range(hpd)], axis=0)
            s = lax.dot_general(qg, kh, (((1,), (1,)), ((), ())), preferred_element_type=F32)
            if has_local:
                s = jnp.concatenate([s[:, :n_loc] + bias, s[:, n_loc:]], axis=1)
            snk = jnp.zeros((hpd * BLOCK, 1), F32)
            for g in range(hpd):
                snk = jnp.where(grp == g, sink_ref[h0 + g], snk)
            m = jnp.maximum(jnp.max(s, axis=1, keepdims=True), snk)
            p = jnp.exp(s - m).astype(MXU_DTYPE)
            v_ones = jnp.concatenate([vh, jnp.ones_like(vh)], axis=1)
            ov = jnp.dot(p, v_ones, preferred_element_type=F32)
            o = ov[:, :HEAD_DIM] / (ov[:, HEAD_DIM:HEAD_DIM + 1] + jnp.exp(snk - m))
            outs += [o[g * BLOCK:(g + 1) * BLOCK] for g in range(hpd)]
        o_ref[0, sb * BLOCK:(sb + 1) * BLOCK, :] = jnp.concatenate(outs, axis=1).astype(o_ref.dtype)


def _local_start(i, n):
    return jnp.clip(i * BLOCK - BLOCK, 0, n - 3 * BLOCK)


def _window_bias(n):
    nb = n // BLOCK
    r = jnp.arange(BLOCK)[:, None]
    c = jnp.arange(3 * BLOCK)[None, :]
    cases = []
    for i in (0, 1, nb - 1):
        diff = (_local_start(i, n) + c) - (i * BLOCK + r)
        cases.append(jnp.where(jnp.abs(diff) <= WINDOW, 0.0, NEG_INF).astype(F32))
    return jnp.stack(cases)


def _attention(sink, q, kv, kvc):
    b, n, _ = q.shape
    lc = kvc.shape[1]
    has_local = kv is not None
    qrows = min(ATTN_QBLOCKS * BLOCK, n)
    assert n % qrows == 0
    in_specs = [pl.BlockSpec(memory_space=pltpu.SMEM), pl.BlockSpec((1, qrows, ATTN_WIDTH), lambda bi, i: (bi, i, 0))]
    args = [sink, q]
    if has_local:
        in_specs.append(pl.BlockSpec((1, n, 2 * KV_WIDTH), lambda bi, i: (bi, 0, 0)))
        args.append(kv)
    in_specs.append(pl.BlockSpec((1, lc, 2 * KV_WIDTH), lambda bi, i: (bi, 0, 0)))
    args.append(kvc)
    if has_local:
        nb = n // BLOCK
        assert nb >= 4
        in_specs.append(pl.BlockSpec((3, BLOCK, 3 * BLOCK), lambda bi, i: (0, 0, 0)))
        args.append(_window_bias(n))
    return pl.pallas_call(
        functools.partial(_attn_kernel, n=n, has_local=has_local),
        grid=(b, n // qrows),
        in_specs=in_specs,
        out_specs=pl.BlockSpec((1, qrows, ATTN_WIDTH), lambda bi, i: (bi, i, 0)),
        out_shape=jax.ShapeDtypeStruct((b, n, ATTN_WIDTH), MXU_DTYPE),
        compiler_params=_params("arbitrary", "arbitrary"),
        name="attn_local" if has_local else "attn_ctx",
    )(*args)


def _mix_out_kernel(*refs, conv, first, last):
    if conv:
        (att_ref, gb_ref, cu_ref, cup_ref, cun_ref, cw_ref, cb_ref,
         w_ref, x_ref, gate_ref, g2_ref, sh2_ref, sc2_ref, rwh_ref, rwl_ref, xo_ref, h_ref, afft_ref) = refs
        t = pl.program_id(1)
        prev = jnp.where(t == first, 0.0, cup_ref[0])
        nxt = jnp.where(t == last, 0.0, cun_ref[0])
        ext = jnp.concatenate([prev, cu_ref[0], nxt], axis=0)
        cw = cw_ref[...]
    else:
        (y_ref, w_ref, x_ref, gate_ref, g2_ref, sh2_ref, sc2_ref, rwh_ref, rwl_ref, xo_ref, h_ref, afft_ref) = refs
    tile = x_ref.shape[1]
    rows = min(MIX_OUT_ROWS, tile)
    for s in range(tile // rows):
        r0 = s * rows
        rs = slice(r0, r0 + rows)
        if conv:
            cv = (cw[0:1] * ext[HALO - 1 + r0:HALO - 1 + r0 + rows] + cw[1:2] * ext[HALO + r0:HALO + r0 + rows]
                  + cw[2:3] * ext[HALO + 1 + r0:HALO + 1 + r0 + rows] + cb_ref[...])
            cat = jnp.concatenate([att_ref[0, rs, :], (gb_ref[0, rs, :] * cv).astype(MXU_DTYPE)], axis=1)
        else:
            cat = jnp.concatenate([y_ref[0, c, rs, :] for c in range(y_ref.shape[1])], axis=1).astype(MXU_DTYPE)
        y = jnp.dot(cat, w_ref[...], preferred_element_type=F32)
        x = x_ref[0, rs, :] + gate_ref[0] * y
        xo_ref[0, rs, :] = x
        h = _norm_mod(x, g2_ref[...], sh2_ref[0], sc2_ref[0])
        _store_token_tiles(h_ref, h, r0)
        h_hi, h_lo = _split_hi_lo(h)
        both = jnp.dot(h_hi, jnp.concatenate([rwh_ref[...], rwl_ref[...]], axis=1), preferred_element_type=F32)
        logits = both[:, :LANES] + both[:, LANES:] + jnp.dot(h_lo, rwh_ref[...], preferred_element_type=F32)
        lane = lax.broadcasted_iota(jnp.int32, logits.shape, 1)
        logits = jnp.where(lane < N_EXPERTS, logits, NEG_INF)
        e = jnp.exp(logits - jnp.max(logits, axis=1, keepdims=True))
        aff = e / jnp.sum(e, axis=1, keepdims=True)
        afft_ref[0, :, rs] = aff.T[:N_EXPERTS]


def _mix_out(mix_in, conv_params, w, x, gate, g2, sh2, sc2, mod_row, rw_hi, rw_lo, tile):
    b, n, d = x.shape
    nt = n // tile
    conv = conv_params is not None
    row = (lambda bi: bi) if mod_row is None else (lambda bi: mod_row)
    tok = lambda width: pl.BlockSpec((1, tile, width), lambda bi, t: (bi, t, 0))
    modspec = pl.BlockSpec((1, 1, d), lambda bi, t: (row(bi), 0, 0))
    full = lambda a: pl.BlockSpec(a.shape, lambda bi, t: (0,) * a.ndim)
    if conv:
        att, gb, cu = mix_in
        cw, cb = conv_params
        per = tile // HALO
        in_specs = [tok(ATTN_WIDTH), tok(CONV_B_WIDTH), tok(CONV_B_WIDTH),
                    pl.BlockSpec((1, HALO, CONV_B_WIDTH), lambda bi, t: (bi, jnp.maximum(t * per - 1, 0), 0)),
                    pl.BlockSpec((1, HALO, CONV_B_WIDTH), lambda bi, t: (bi, jnp.minimum((t + 1) * per, n // HALO - 1), 0)),
                    full(cw), full(cb)]
        args = [att, gb, cu, cu, cu, cw, cb]
    else:
        in_specs = [_slab_spec(tile, lambda bi, t: (bi, 0, t, 0), d)]
        args = [mix_in]
    in_specs += [full(w), tok(d), modspec, full(g2), modspec, modspec, full(rw_hi), full(rw_lo)]
    args += [w, x, gate, g2, sh2, sc2, rw_hi, rw_lo]
    out_shape = (jax.ShapeDtypeStruct((b, n, d), F32), jax.ShapeDtypeStruct((b, n * SUBLANES, LANES), F32),
                 jax.ShapeDtypeStruct((b, N_EXPERTS, n), F32))
    out_specs = (tok(d), pl.BlockSpec((1, tile * SUBLANES, LANES), lambda bi, t: (bi, t, 0)),
                 pl.BlockSpec((1, N_EXPERTS, tile), lambda bi, t: (bi, 0, t)))
    return pl.pallas_call(
        functools.partial(_mix_out_kernel, conv=conv, first=0, last=nt - 1),
        grid=(b, nt),
        in_specs=in_specs,
        out_specs=out_specs,
        out_shape=out_shape,
        compiler_params=_params("arbitrary", "arbitrary"),
        name="even_out" if conv else "odd_out",
    )(*args)


def _cumsum_lanes(x):
    n = x.shape[1]
    r = lax.broadcasted_iota(jnp.int32, (CUMSUM_TILE, CUMSUM_TILE), 0)
    c = lax.broadcasted_iota(jnp.int32, (CUMSUM_TILE, CUMSUM_TILE), 1)
    tri = jnp.where(r <= c, 1.0, 0.0).astype(MXU_DTYPE)
    carry = jnp.zeros((x.shape[0], 1), F32)
    outs = []
    for k in range(n // CUMSUM_TILE):
        blk = x[:, k * CUMSUM_TILE:(k + 1) * CUMSUM_TILE].astype(MXU_DTYPE)
        loc = jnp.dot(blk, tri, preferred_element_type=F32) + carry
        outs.append(loc)
        carry = loc[:, CUMSUM_TILE - 1:CUMSUM_TILE]
    return jnp.concatenate(outs, axis=1)


SLOT_EMPTY = 1 << 20


def _select_kernel(at_ref, idx_ref, g_ref, *, cap):
    bblk, n_e, n = at_ref.shape
    at = at_ref[...].reshape(bblk * n_e, n)
    n_exp = bblk * n_e
    capf = float(cap)

    def count_ge(thr):
        return jnp.sum(jnp.where(at >= thr, 1.0, 0.0), axis=1, keepdims=True)

    def bit_body(_, c):
        lo_i, hi_i = c
        mid = lo_i + ((hi_i - lo_i) >> 1)
        ge = count_ge(lax.bitcast_convert_type(mid, F32)) >= capf
        return jnp.where(ge, mid, lo_i), jnp.where(ge, hi_i, mid)

    lo_i, hi_i = lax.fori_loop(
        0, 31, bit_body, (jnp.zeros((n_exp, 1), jnp.int32), jnp.full((n_exp, 1), 0x3F800001, jnp.int32)))

    def val_body(_, c):
        lo, hi = c
        mid = 0.5 * (lo + hi)
        ge = count_ge(mid) >= capf
        return jnp.where(ge, mid, lo), jnp.where(ge, hi, mid)

    lo, hi = lax.fori_loop(
        0, 24, val_body, (lax.bitcast_convert_type(lo_i, F32), lax.bitcast_convert_type(hi_i, F32)))
    need = capf - count_ge(hi)

    above = jnp.where(at >= hi, 1.0, 0.0)
    band = jnp.where(at >= lo, 1.0, 0.0) - above
    sel = above + band * jnp.where(_cumsum_lanes(band) <= need, 1.0, 0.0)
    rank = _cumsum_lanes(sel)

    lane = lax.broadcasted_iota(jnp.int32, (n_exp, n), 1)
    disp = jnp.where(sel > 0.5, lane + 1 - rank.astype(jnp.int32), SLOT_EMPTY)
    g = at
    for k in range(n.bit_length() - 1):
        step = 1 << k
        moving = ((disp >> k) & 1) == 1
        disp_in = pltpu.roll(disp, n - step, 1)
        arriving = ((disp_in >> k) & 1) == 1
        g = jnp.where(arriving, pltpu.roll(g, n - step, 1), g)
        disp = jnp.where(arriving, disp_in, jnp.where(moving, SLOT_EMPTY, disp))

    capp = pl.cdiv(cap, LANES) * LANES
    slot = lax.broadcasted_iota(jnp.int32, (n_exp, cap), 1)
    idx_ref[...] = jnp.clip(slot + disp[:, :cap], 0, n - 1).reshape(bblk, n_e, cap)
    for s in range(bblk):
        g_pad = jnp.concatenate([g[s * n_e:(s + 1) * n_e, :capp], jnp.zeros((LANES - n_e, capp), F32)], axis=0)
        g_ref[s] = g_pad.T[:cap]


def _select(aff_t, cap):
    b, n_exp, n = aff_t.shape
    assert n & (n - 1) == 0 and n % CUMSUM_TILE == 0 and n < SLOT_EMPTY
    bblk = b
    return pl.pallas_call(
        functools.partial(_select_kernel, cap=cap),
        grid=(b // bblk,),
        in_specs=[pl.BlockSpec((bblk, n_exp, n), lambda bi: (bi, 0, 0))],
        out_specs=(pl.BlockSpec((bblk, n_exp, cap), lambda bi: (bi, 0, 0)),
                   pl.BlockSpec((bblk, cap, LANES), lambda bi: (bi, 0, 0))),
        out_shape=(jax.ShapeDtypeStruct((b, n_exp, cap), jnp.int32), jax.ShapeDtypeStruct((b, cap, LANES), F32)),
        compiler_params=_params("arbitrary"),
        name="moe_select",
    )(aff_t)


def _gather_kernel(idx_ref, src_ref, xe_ref, xcm_ref, *, cap, n_exp):
    bblk = src_ref.shape[0]
    eblk, out_rows = xe_ref.shape[1], xe_ref.shape[2]
    stride = out_rows + SUBLANES
    for k in range(eblk):
        for s in range(bblk):
            base = ((pl.program_id(0) * bblk + s) * n_exp + pl.program_id(1) * eblk + k) * cap
            for r in range(cap):
                t = idx_ref[base + r]
                xcm_ref[k, pl.ds(s * cap + r, SUBLANES, stride=stride), :] = (
                    src_ref[s, pl.ds(pl.multiple_of(t * SUBLANES, SUBLANES), SUBLANES), :])
        used = bblk * cap
        if used < out_rows:
            for j in range(SUBLANES):
                xcm_ref[k, pl.ds(j * stride + used, out_rows - used), :] = jnp.zeros((out_rows - used, LANES), F32)
        xe_ref[0, k] = jnp.concatenate(
            [xcm_ref[k, pl.ds(j * stride, out_rows), :] for j in range(SUBLANES)], axis=1).astype(xe_ref.dtype)


def _gather(idx_flat, src, cap, bblk, out_rows):
    b, rows, _ = src.shape
    d = SUBLANES * LANES
    assert cap % SUBLANES == 0 and bblk * cap <= out_rows and b % bblk == 0 and N_EXPERTS % DISPATCH_EXPERTS == 0
    return pl.pallas_call(
        functools.partial(_gather_kernel, cap=cap, n_exp=N_EXPERTS),
        grid=(b // bblk, N_EXPERTS // DISPATCH_EXPERTS),
        in_specs=[pl.BlockSpec(memory_space=pltpu.SMEM),
                  pl.BlockSpec((bblk, rows, LANES), lambda bi, ei: (bi, 0, 0))],
        out_specs=pl.BlockSpec((1, DISPATCH_EXPERTS, out_rows, d), lambda bi, ei: (bi, ei, 0, 0)),
        out_shape=jax.ShapeDtypeStruct((b // bblk, N_EXPERTS, out_rows, d), MXU_DTYPE),
        scratch_shapes=[pltpu.VMEM((DISPATCH_EXPERTS, SUBLANES * (out_rows + SUBLANES), LANES), F32)],
        compiler_params=_params("arbitrary", "arbitrary"),
        name="moe_gather",
    )(idx_flat, src)


WEIGHT_CHUNKS = 2
WEIGHT_DMA_PRIORITY = 1
FFN_BLOCKS = 4


def _moe_ffn_kernel(*refs, layer, n_steps, has_ctx):
    if has_ctx:
        (x_ref, g_ref, xc_ref, gc_ref, wg_hbm, wu_hbm, wd_hbm, o_ref, oc_ref,
         wgu_s, wd_s, stage_in, stage_out, sem) = refs
    else:
        (x_ref, g_ref, wg_hbm, wu_hbm, wd_hbm, o_ref, wgu_s, wd_s, stage_in, stage_out, sem) = refs
    n_exp = pl.num_programs(0)
    e = pl.program_id(0)
    s = pl.program_id(1)
    slot = e % 2
    ff = wg_hbm.shape[3]
    chunks = []
    for hbm, dst, col0, stage in ((wg_hbm, wgu_s, 0, stage_in), (wu_hbm, wgu_s, ff, stage_in),
                                  (wd_hbm, wd_s, 0, stage_out)):
        n_rows = hbm.shape[2] // WEIGHT_CHUNKS
        chunks += [(hbm, dst, col0, stage, c * n_rows, n_rows) for c in range(WEIGHT_CHUNKS)]
    per_step = pl.cdiv(len(chunks), n_steps)

    def copy(c, expert):
        hbm, _, _, stage, r0, n_rows = chunks[c]
        return pltpu.make_async_copy(hbm.at[layer, expert, pl.ds(r0, n_rows), :], stage, sem.at[0])

    def cast(c, to_slot):
        hbm, dst, col0, stage, r0, n_rows = chunks[c]
        dst[to_slot, pl.ds(r0, n_rows), pl.ds(col0, hbm.shape[3])] = stage[...].astype(dst.dtype)

    @pl.when((e == 0) & (s == 0))
    def _():
        for c in range(len(chunks)):
            copy(c, 0).start()
            copy(c, 0).wait()
            cast(c, 0)

    has_next = e + 1 < n_exp
    for c in range(len(chunks)):
        if c % per_step == 0:
            @pl.when(has_next & (s == c // per_step))
            def _():
                copy(c, e + 1).start(priority=WEIGHT_DMA_PRIORITY)

    def ffn(xr, gr, outr, k):
        x, aff = xr[k, 0], gr[k]
        rows = x.shape[0]
        au = jnp.dot(x, wgu_s[slot], preferred_element_type=F32)
        a, u = au[:, :ff], au[:, ff:]
        mid = (a * _sigmoid(a) * u).astype(MXU_DTYPE)
        y = jnp.dot(mid, wd_s[slot], preferred_element_type=F32)
        lane = lax.broadcasted_iota(jnp.int32, aff.shape, 1)
        y = y * jnp.sum(jnp.where(lane == e, aff, 0.0), axis=1, keepdims=True)
        stride = rows + SUBLANES
        for j in range(SUBLANES):
            outr[k, 0, pl.ds(j * stride, rows), :] = y[:, j * LANES:(j + 1) * LANES]
            outr[k, 0, pl.ds(j * stride + rows, SUBLANES), :] = jnp.zeros((SUBLANES, LANES), F32)

    n_blocks = x_ref.shape[0]
    is_ctx = (s == n_steps - 1) if has_ctx else None
    for k in range(n_blocks):
        if has_ctx:
            if k == 0:
                pl.when(is_ctx)(lambda: ffn(xc_ref, gc_ref, oc_ref, 0))
            pl.when(jnp.logical_not(is_ctx))(lambda k=k: ffn(x_ref, g_ref, o_ref, k))
        else:
            pl.when(s < n_steps)(lambda k=k: ffn(x_ref, g_ref, o_ref, k))
        for c in range(len(chunks)):
            if min(c % per_step, n_blocks - 1) == k:
                @pl.when(has_next & (s == c // per_step))
                def _():
                    copy(c, e + 1).wait()
                    cast(c, 1 - slot)
                    if (c + 1) % per_step != 0 and c + 1 < len(chunks):
                        copy(c + 1, e + 1).start(priority=WEIGHT_DMA_PRIORITY)


def _moe_ffn(xe, ge, ctx_block, layer, wg, wu, wd):
    nb, e, rows, d = xe.shape
    ff = wg.shape[-1]
    assert d % WEIGHT_CHUNKS == 0 and ff % (WEIGHT_CHUNKS * SUBLANES) == 0 and ff % LANES == 0
    has_ctx = ctx_block is not None
    fb = FFN_BLOCKS if nb % FFN_BLOCKS == 0 else 1
    n_steps = nb // fb + int(has_ctx)
    hbm = pl.BlockSpec(memory_space=pl.ANY)
    out_rows = SUBLANES * (rows + SUBLANES)
    blk = lambda bi: jnp.minimum(bi, nb // fb - 1)
    in_specs = [pl.BlockSpec((fb, 1, rows, d), lambda ei, bi: (blk(bi), ei, 0, 0)),
                pl.BlockSpec((fb, rows, LANES), lambda ei, bi: (blk(bi), 0, 0))]
    args = [xe, ge]
    out_specs = [pl.BlockSpec((fb, 1, out_rows, LANES), lambda ei, bi: (blk(bi), ei, 0, 0))]
    out_shape = [jax.ShapeDtypeStruct((nb, e, out_rows, LANES), F32)]
    if has_ctx:
        rows_c = ctx_block[0].shape[2]
        assert ctx_block[0].shape == (1, e, rows_c, d) and ctx_block[1].shape == (1, rows_c, LANES)
        out_rows_c = SUBLANES * (rows_c + SUBLANES)
        in_specs += [pl.BlockSpec((1, 1, rows_c, d), lambda ei, bi: (0, ei, 0, 0)),
                     pl.BlockSpec((1, rows_c, LANES), lambda ei, bi: (0, 0, 0))]
        args += list(ctx_block)
        out_specs.append(pl.BlockSpec((1, 1, out_rows_c, LANES), lambda ei, bi: (0, ei, 0, 0)))
        out_shape.append(jax.ShapeDtypeStruct((1, e, out_rows_c, LANES), F32))
    return pl.pallas_call(
        functools.partial(_moe_ffn_kernel, layer=layer, n_steps=n_steps, has_ctx=has_ctx),
        grid=(e, n_steps),
        in_specs=in_specs + [hbm, hbm, hbm],
        out_specs=tuple(out_specs),
        out_shape=tuple(out_shape),
        scratch_shapes=[pltpu.VMEM((2, d, 2 * ff), MXU_DTYPE), pltpu.VMEM((2, ff, d), MXU_DTYPE),
                        pltpu.VMEM((d // WEIGHT_CHUNKS, ff), F32), pltpu.VMEM((ff // WEIGHT_CHUNKS, d), F32),
                        pltpu.SemaphoreType.DMA((1,))],
        compiler_params=_params("arbitrary", "arbitrary"),
        name="moe_ffn",
    )(*args, wg, wu, wd)


COMBINE_UNROLL = 8


def _combine_kernel(idx_ref, y_ref, acc_ref, *, cap, n_exp):
    bblk = acc_ref.shape[0]
    eblk = y_ref.shape[1]
    ei = pl.program_id(1)
    stride = y_ref.shape[2] // SUBLANES

    @pl.when(ei == 0)
    def _():
        acc_ref[...] = jnp.zeros_like(acc_ref)

    for k in range(eblk):
        for s in range(bblk):
            base = ((pl.program_id(0) * bblk + s) * n_exp + ei * eblk + k) * cap
            for r0 in range(0, cap, COMBINE_UNROLL):
                toks = [pl.multiple_of(idx_ref[base + r0 + i] * SUBLANES, SUBLANES) for i in range(COMBINE_UNROLL)]
                new = [acc_ref[s, pl.ds(toks[i], SUBLANES), :]
                       + y_ref[0, k, pl.ds(s * cap + r0 + i, SUBLANES, stride=stride), :]
                       for i in range(COMBINE_UNROLL)]
                for i in range(COMBINE_UNROLL):
                    acc_ref[s, pl.ds(toks[i], SUBLANES), :] = new[i]


def _combine(idx_flat, y, b, n, cap, bblk):
    nb, n_exp, rows, _ = y.shape
    assert cap % COMBINE_UNROLL == 0 and nb * bblk == b and bblk * cap <= rows // SUBLANES - SUBLANES
    return pl.pallas_call(
        functools.partial(_combine_kernel, cap=cap, n_exp=n_exp),
        grid=(nb, n_exp // DISPATCH_EXPERTS),
        in_specs=[pl.BlockSpec(memory_space=pltpu.SMEM),
                  pl.BlockSpec((1, DISPATCH_EXPERTS, rows, LANES), lambda bi, ei: (bi, ei, 0, 0))],
        out_specs=pl.BlockSpec((bblk, n * SUBLANES, LANES), lambda bi, ei: (bi, 0, 0)),
        out_shape=jax.ShapeDtypeStruct((b, n * SUBLANES, LANES), F32),
        compiler_params=_params("arbitrary", "arbitrary"),
        name="moe_combine",
    )(idx_flat, y)


def _expert_choice(h_lat, aff_lat, h_ctx, aff_ctx, layer, wg, wu, wd):
    b, _, n = aff_lat.shape
    cap = max(1, CAPACITY_FACTOR * n // N_EXPERTS)
    idx, g = _select(aff_lat, cap)
    idx = idx.reshape(-1)
    xe = _gather(idx, h_lat, cap, 1, cap)
    if h_ctx is None:
        (y,) = _moe_ffn(xe, g, None, layer, wg, wu, wd)
        return _combine(idx, y, b, n, cap, 1), None
    lc = aff_ctx.shape[2]
    cap_c = max(1, CAPACITY_FACTOR * lc // N_EXPERTS)
    idx_c, g_c = _select(aff_ctx, cap_c)
    idx_c = idx_c.reshape(-1)
    xe_c = _gather(idx_c, h_ctx, cap_c, b, b * cap_c)
    g_c = g_c.reshape(1, b * cap_c, LANES)
    y, y_c = _moe_ffn(xe, g, (xe_c, g_c), layer, wg, wu, wd)
    return _combine(idx, y, b, n, cap, 1), _combine(idx_c, y_c, b, lc, cap_c, b)


def _odd_in_kernel(x_ref, moe_ref, gate_ref, g_ref, sh_ref, sc_ref, w_ref, xo_ref, gg_ref, u_ref):
    x = x_ref[0] + gate_ref[0] * _load_token_tiles(moe_ref)
    xo_ref[0] = x
    h = _norm_mod(x, g_ref[...], sh_ref[0], sc_ref[0]).astype(MXU_DTYPE)
    y = jnp.dot(h, w_ref[...], preferred_element_type=F32)
    gl = y[:, :LRU_WIDTH]
    _store_slabs(gg_ref, 0.25 * gl * (1.0 + jnp.tanh(0.7978845608028654 * (gl + 0.044715 * gl * gl * gl))))
    _store_slabs(u_ref, y[:, LRU_WIDTH:])


def _odd_in(x, moe, gate, g, shift, scale, mod_row, w, tile):
    b, n, d = x.shape
    row = (lambda bi: bi) if mod_row is None else (lambda bi: mod_row)
    tok = pl.BlockSpec((1, tile, d), lambda bi, t: (bi, t, 0))
    tiles = pl.BlockSpec((1, tile * SUBLANES, LANES), lambda bi, t: (bi, t, 0))
    modspec = pl.BlockSpec((1, 1, d), lambda bi, t: (row(bi), 0, 0))
    return pl.pallas_call(
        _odd_in_kernel,
        grid=(b, n // tile),
        in_specs=[tok, tiles, modspec, pl.BlockSpec((1, d), lambda bi, t: (0, 0)), modspec, modspec,
                  pl.BlockSpec(w.shape, lambda bi, t: (0, 0))],
        out_specs=(tok, _slab_spec(tile, lambda bi, t: (bi, 0, t, 0)), _slab_spec(tile, lambda bi, t: (bi, 0, t, 0))),
        out_shape=(jax.ShapeDtypeStruct((b, n, d), F32),) + (jax.ShapeDtypeStruct((b, d // LANES, n, LANES), F32),) * 2,
        compiler_params=_params("arbitrary", "arbitrary"),
        name="odd_in",
    )(x, moe, gate, g, shift, scale, w)


def _lru_kernel(*refs, reverse, nt):
    if reverse:
        (uc_ref, wax_ref, lam_ref, h0_ref, hf_ref, gg_ref, out_ref, hlast_ref, a_scr, b_scr, uc_scr, carry_scr) = refs
        n_slab, tile = uc_ref.shape[1], uc_ref.shape[2]
    else:
        (u_ref, up_ref, un_ref, cw_ref, cb_ref, wax_ref, lam_ref, h0_ref,
         out_ref, uco_ref, hlast_ref, a_scr, b_scr, uc_scr, carry_scr) = refs
        n_slab, tile = u_ref.shape[1], u_ref.shape[2]
    w = n_slab * LANES
    per = tile // SUBLANES
    rows = lambda j: slice(j * per, (j + 1) * per)
    phase = lambda ref, j: jnp.concatenate(
        [ref[0, c, pl.ds(j, per, stride=SUBLANES), :] for c in range(n_slab)], axis=1)
    phase_major = lambda ref, j: jnp.concatenate([ref[0, c, rows(j), :] for c in range(n_slab)], axis=1)

    def store_phase(ref, j, val, time_order):
        for c in range(n_slab):
            dst = pl.ds(j, per, stride=SUBLANES) if time_order else rows(j)
            ref[0, c, dst, :] = val[:, c * LANES:(c + 1) * LANES]

    t = pl.program_id(1)
    if reverse:
        for j in range(SUBLANES):
            uc_scr[rows(j), :] = phase_major(uc_ref, j)
    else:
        prev = jnp.where(t == 0, 0.0, _load_slabs(up_ref))
        nxt = jnp.where(t == nt - 1, 0.0, _load_slabs(un_ref))
        rowid = lax.broadcasted_iota(jnp.int32, (per, w), 0)
        shift_down = lambda x, first: jnp.where(rowid == 0, first, pltpu.roll(x, 1, 0))
        shift_up = lambda x, last: jnp.where(rowid == per - 1, last, pltpu.roll(x, per - 1, 0))
        u = [phase(u_ref, j) for j in range(SUBLANES)]
        um1 = [shift_down(u[7], prev[7:8])] + u[:7]
        um2 = [shift_down(u[6], prev[6:7]), um1[0]] + u[:6]
        up1 = u[1:] + [shift_up(u[0], nxt[0:1])]
        cw = cw_ref[...]
        for j in range(SUBLANES):
            uc_j = cb_ref[...] + cw[0:1] * um2[j] + cw[1:2] * um1[j] + cw[2:3] * u[j] + cw[3:4] * up1[j]
            uc_scr[rows(j), :] = uc_j
            store_phase(uco_ref, j, uc_j, False)

    lam = lam_ref[0]
    half_decay = (0.5 * LRU_C) * (jnp.maximum(-lam, 0.0) + jnp.log1p(jnp.exp(-jnp.abs(lam))))
    ones = jnp.where(lax.broadcasted_iota(jnp.int32, (tile, LRU_BLOCK), 1) < BIAS_TERMS, 1.0, 0.0).astype(MXU_DTYPE)
    for hd in range(LRU_HEADS):
        sl = slice(hd * LRU_BLOCK, (hd + 1) * LRU_BLOCK)
        uc = uc_scr[:, sl]
        z = jnp.dot(jnp.concatenate([uc.astype(MXU_DTYPE), ones], axis=1), wax_ref[0, hd], preferred_element_type=F32)
        hd_row = half_decay[:, sl]
        neg_log_a = hd_row * jnp.tanh(z[:, :LRU_BLOCK]) + hd_row
        gate2 = 1.0 + jnp.tanh(z[:, LRU_BLOCK:])
        a = jnp.exp(-neg_log_a)
        m2 = jnp.tanh(neg_log_a) * (a * a + 1.0)
        mult = jnp.where(m2 > 0.0, m2 * lax.rsqrt(m2), 0.0)
        a_scr[:, sl] = a
        b_scr[:, sl] = mult * (gate2 * uc)

    @pl.when(t == 0)
    def _():
        carry_scr[...] = jnp.broadcast_to(h0_ref[0], carry_scr.shape)

    order = list(range(SUBLANES))[::-1] if reverse else list(range(SUBLANES))
    hrun = b_scr[rows(order[0]), :]
    prun = a_scr[rows(order[0]), :]
    for j in order[1:]:
        aj = a_scr[rows(j), :]
        hrun = aj * hrun + b_scr[rows(j), :]
        prun = aj * prun
        b_scr[rows(j), :] = hrun
        a_scr[rows(j), :] = prun

    lane_row = lax.broadcasted_iota(jnp.int32, (SUBLANES, w), 0)
    carry = carry_scr[...]
    groups = list(range(per // SUBLANES))
    entering = [None] * len(groups)
    for m in (groups[::-1] if reverse else groups):
        a = prun[m * SUBLANES:(m + 1) * SUBLANES]
        bcoef = hrun[m * SUBLANES:(m + 1) * SUBLANES]
        for dist in (1, 2, 4):
            shift = (SUBLANES - dist) if reverse else dist
            msk = (lane_row < SUBLANES - dist) if reverse else (lane_row >= dist)
            a_s = pltpu.roll(a, shift, 0)
            b_s = pltpu.roll(bcoef, shift, 0)
            bcoef = jnp.where(msk, a * b_s + bcoef, bcoef)
            a = jnp.where(msk, a * a_s, a)
        after = a * carry + bcoef
        if reverse:
            entering[m] = jnp.where(lane_row == SUBLANES - 1, carry, pltpu.roll(after, SUBLANES - 1, 0))
            carry = jnp.broadcast_to(after[0:1], carry.shape)
        else:
            entering[m] = jnp.where(lane_row == 0, carry, pltpu.roll(after, 1, 0))
            carry = jnp.broadcast_to(after[SUBLANES - 1:SUBLANES], carry.shape)
    carry_scr[...] = carry
    hlast_ref[0] = carry[0:1]
    h_in = jnp.concatenate(entering, axis=0)

    for j in range(SUBLANES):
        hcur = b_scr[rows(j), :] + a_scr[rows(j), :] * h_in
        if reverse:
            hcur = phase(gg_ref, j) * (phase_major(hf_ref, j) + hcur)
        store_phase(out_ref, j, hcur, reverse)


BIAS_TERMS = 3


def _gate_weights(wa, wx, ba, bx):
    n_dir, heads, blk, _ = wa.shape
    w = 0.5 * jnp.concatenate([wa, wx], axis=-1)
    bias = 0.5 * jnp.concatenate([ba.reshape(n_dir, heads, blk), bx.reshape(n_dir, heads, blk)], axis=-1)
    terms = []
    for _ in range(BIAS_TERMS):
        term = bias.astype(MXU_DTYPE)
        terms.append(term)
        bias = bias - term.astype(F32)
    rows = jnp.stack(terms, axis=2)
    pad = jnp.zeros((n_dir, heads, blk - BIAS_TERMS, 2 * blk), MXU_DTYPE)
    return jnp.concatenate([w.astype(MXU_DTYPE), rows, pad], axis=2)


def _lru_fwd(u, cw, cb, wax, lam, h0, tile):
    b, n_slab, n, _ = u.shape
    w = n_slab * LANES
    nt = n // tile
    per = tile // HALO
    assert tile % (SUBLANES * SUBLANES) == 0
    tok = _slab_spec(tile, lambda bi, t: (bi, 0, t, 0), w)
    state = pl.BlockSpec((1, 1, w), lambda bi, t: (bi, 0, 0))
    return pl.pallas_call(
        functools.partial(_lru_kernel, reverse=False, nt=nt),
        grid=(b, nt),
        in_specs=[tok,
                  _slab_spec(HALO, lambda bi, t: (bi, 0, jnp.maximum(t * per - 1, 0), 0), w),
                  _slab_spec(HALO, lambda bi, t: (bi, 0, jnp.minimum((t + 1) * per, n // HALO - 1), 0), w),
                  pl.BlockSpec(cw.shape, lambda bi, t: (0, 0)),
                  pl.BlockSpec(cb.shape, lambda bi, t: (0, 0)),
                  pl.BlockSpec((1,) + wax.shape[1:], lambda bi, t: (0, 0, 0, 0)),
                  pl.BlockSpec((1, 1, w), lambda bi, t: (0, 0, 0)),
                  state],
        out_specs=(tok, tok, state),
        out_shape=(jax.ShapeDtypeStruct(u.shape, F32), jax.ShapeDtypeStruct(u.shape, F32),
                   jax.ShapeDtypeStruct((b, 1, w), F32)),
        scratch_shapes=[pltpu.VMEM((tile, w), F32), pltpu.VMEM((tile, w), F32), pltpu.VMEM((tile, w), F32),
                        pltpu.VMEM((SUBLANES, w), F32)],
        compiler_params=_params("arbitrary", "arbitrary"),
        name="lru_fwd",
    )(u, u, u, cw, cb, wax, lam, h0)


def _lru_bwd(uc, wax, lam, h0, hf, gg, tile):
    b, n_slab, n, _ = uc.shape
    w = n_slab * LANES
    nt = n // tile
    assert tile % (SUBLANES * SUBLANES) == 0
    tok = _slab_spec(tile, lambda bi, t: (bi, 0, nt - 1 - t, 0), w)
    state = pl.BlockSpec((1, 1, w), lambda bi, t: (bi, 0, 0))
    return pl.pallas_call(
        functools.partial(_lru_kernel, reverse=True, nt=nt),
        grid=(b, nt),
        in_specs=[tok,
                  pl.BlockSpec((1,) + wax.shape[1:], lambda bi, t: (1, 0, 0, 0)),
                  pl.BlockSpec((1, 1, w), lambda bi, t: (1, 0, 0)),
                  state, tok, tok],
        out_specs=(tok, state),
        out_shape=(jax.ShapeDtypeStruct(uc.shape, F32), jax.ShapeDtypeStruct((b, 1, w), F32)),
        scratch_shapes=[pltpu.VMEM((tile, w), F32), pltpu.VMEM((tile, w), F32), pltpu.VMEM((tile, w), F32),
                        pltpu.VMEM((SUBLANES, w), F32)],
        compiler_params=_params("arbitrary", "arbitrary"),
        name="lru_bwd",
    )(uc, wax, lam, h0, hf, gg)


def _final_kernel(x_ref, moe_ref, gate_ref, g_ref, o_ref):
    x = x_ref[0] + gate_ref[0] * _load_token_tiles(moe_ref)
    o_ref[0] = x * lax.rsqrt(jnp.mean(x * x, axis=-1, keepdims=True) + EPS) * g_ref[...]


def _final(x, moe, gate, g, tile):
    b, n, d = x.shape
    tok = pl.BlockSpec((1, tile, d), lambda bi, t: (bi, t, 0))
    tiles = pl.BlockSpec((1, tile * SUBLANES, LANES), lambda bi, t: (bi, t, 0))
    return pl.pallas_call(
        _final_kernel,
        grid=(b, n // tile),
        in_specs=[tok, tiles, pl.BlockSpec((1, 1, d), lambda bi, t: (bi, 0, 0)), pl.BlockSpec((1, d), lambda bi, t: (0, 0))],
        out_specs=tok,
        out_shape=jax.ShapeDtypeStruct((b, n, d), F32),
        compiler_params=_params("arbitrary", "arbitrary"),
        name="final_norm",
    )(x, moe, gate, g)


def kernel(x, c, ctx, c_ctx, ada_w, ada_b, norm_mix_g, norm_ffn_g, ev_w_in, ev_w_out, ev_sink, ev_conv_w, ev_conv_b, od_w_in, od_w_out, od_conv_w, od_conv_b, od_wa, od_ba, od_wx, od_bx, od_lambda, router_w, w_gate, w_up, w_down, final_g):
    b, n, d = x.shape
    lc = ctx.shape[1]
    depth = ada_w.shape[0]
    assert depth == 2 and d == D_MODEL and b < MOD_ROWS
    tile_l = tile_w = min(1024, n)
    tile_c = lc
    ctx_row = b

    cvec = jnp.zeros((MOD_ROWS, d), F32).at[:b].set(c).at[b].set(c_ctx)
    mods = _ada(cvec, ada_w, ada_b).reshape(depth, MOD_ROWS, 6, 1, d)
    mod = lambda l, j: mods[l, :, j]

    def router_split(l):
        return _split_hi_lo(jnp.pad(router_w[l], ((0, 0), (0, LANES - N_EXPERTS))))

    bf = lambda a: a.astype(MXU_DTYPE)

    g_mix = norm_mix_g[0].reshape(1, d)
    g_ffn = norm_ffn_g[0].reshape(1, d)
    w_in = bf(ev_w_in[0])
    w_out = bf(ev_w_out[0])
    conv_p = (ev_conv_w[0], ev_conv_b[0].reshape(1, -1))
    tables = _rope_tables(n)
    ql, kvl, gbl, cul = _even_in(x, g_mix, mod(0, 0), mod(0, 1), None, w_in, tables, tile_l)
    qc, kvc, gbc, cuc = _even_in(ctx, g_mix, mod(0, 0), mod(0, 1), ctx_row, w_in, None, tile_c)
    att_l = _attention(ev_sink[0], ql, kvl, kvc)
    att_c = _attention(ev_sink[0], qc, None, kvc)
    rw_hi, rw_lo = router_split(0)
    xl, hl, at_l = _mix_out((att_l, gbl, cul), conv_p, w_out, x, mod(0, 2), g_ffn, mod(0, 3), mod(0, 4), None,
                            rw_hi, rw_lo, tile_w)
    xc, hc, at_c = _mix_out((att_c, gbc, cuc), conv_p, w_out, ctx, mod(0, 2), g_ffn, mod(0, 3), mod(0, 4),
                            ctx_row, rw_hi, rw_lo, tile_c)
    moe_l, moe_c = _expert_choice(hl, at_l, hc, at_c, 0, w_gate, w_up, w_down)

    g_mix = norm_mix_g[1].reshape(1, d)
    g_ffn = norm_ffn_g[1].reshape(1, d)
    w_in = bf(od_w_in[0])
    w_out = bf(od_w_out[0])
    xl, ggl, ul = _odd_in(xl, moe_l, mod(0, 5), g_mix, mod(1, 0), mod(1, 1), None, w_in, tile_l)
    _, _, u_ctx = _odd_in(xc, moe_c, mod(0, 5), g_mix, mod(1, 0), mod(1, 1), ctx_row, w_in, tile_c)
    cw, cb = od_conv_w[0], od_conv_b[0].reshape(1, -1)
    wax = _gate_weights(od_wa[0], od_wx[0], od_ba[0], od_bx[0])
    lam = od_lambda[0].reshape(2, 1, -1)
    zero_state = jnp.zeros((b, 1, LRU_WIDTH), F32)
    hf_c, uc_c, h0_f = _lru_fwd(u_ctx, cw, cb, wax, lam, zero_state, tile_c)
    _, h0_b = _lru_bwd(uc_c, wax, lam, zero_state, hf_c, hf_c, tile_c)
    hf_l, uc_l, _ = _lru_fwd(ul, cw, cb, wax, lam, h0_f, tile_w)
    yl, _ = _lru_bwd(uc_l, wax, lam, h0_b, hf_l, ggl, tile_w)
    rw_hi, rw_lo = router_split(1)
    xl, hl, at_l = _mix_out(yl, None, w_out, xl, mod(1, 2), g_ffn, mod(1, 3), mod(1, 4), None,
                            rw_hi, rw_lo, tile_w)
    moe_l, _ = _expert_choice(hl, at_l, None, None, 1, w_gate, w_up, w_down)
    return _final(xl, moe_l, mod(1, 5), final_g.reshape(1, d), tile_w)
```

```python
import functools

import jax
import jax.numpy as jnp
from jax import lax
from jax.experimental import pallas as pl
from jax.experimental.pallas import tpu as pltpu

F32 = jnp.float32
MXU_DTYPE = jnp.bfloat16

D_MODEL = 1024
GRID_W = 64
EPS = 1e-6
NEG_INF = -1e30
HEAD_DIM = 64
N_Q_HEADS = 8
N_KV_HEADS = 2
Q_PER_KV = N_Q_HEADS // N_KV_HEADS
ATTN_WIDTH = N_Q_HEADS * HEAD_DIM
KV_WIDTH = N_KV_HEADS * HEAD_DIM
WINDOW = 128
BLOCK = 128
ROPE_BASE = 10000.0
CONV_B_WIDTH = D_MODEL // 2
EVEN_IN = ATTN_WIDTH + 2 * KV_WIDTH + 3 * CONV_B_WIDTH
LRU_WIDTH = D_MODEL
LRU_HEADS = 8
LRU_BLOCK = LRU_WIDTH // LRU_HEADS
LRU_C = 8.0
N_EXPERTS = 16
CAPACITY_FACTOR = 2
EXPERT_FF = 1408
MOD_ROWS = 16
LANES = 128
SUBLANES = 8
HALO = SUBLANES
CUMSUM_TILE = 256
MIX_OUT_ROWS = 256
ATTN_QBLOCKS = 8
ATTN_HEADS_PER_DOT = 4
DISPATCH_EXPERTS = 4
V7X_VMEM_LIMIT = 56 * 1024 * 1024


def _params(*sem):
    return pltpu.CompilerParams(dimension_semantics=sem, vmem_limit_bytes=V7X_VMEM_LIMIT)


def _split_hi_lo(a):
    hi = a.astype(MXU_DTYPE)
    lo = (a - hi.astype(F32)).astype(MXU_DTYPE)
    return hi, lo


def _sigmoid(z):
    return 0.5 * (1.0 + jnp.tanh(0.5 * z))


def _store_token_tiles(ref, val, row0=0):
    tile = val.shape[0]
    for j in range(SUBLANES):
        ref[0, pl.ds(row0 * SUBLANES + j, tile, stride=SUBLANES), :] = val[:, j * LANES:(j + 1) * LANES]


def _load_token_tiles(ref):
    tile = ref.shape[1] // SUBLANES
    return jnp.concatenate([ref[0, pl.ds(j, tile, stride=SUBLANES), :] for j in range(SUBLANES)], axis=1)


def _slab_spec(rows, index_map, width=LRU_WIDTH):
    return pl.BlockSpec((1, width // LANES, rows, LANES), index_map)


def _store_slabs(ref, val):
    for c in range(val.shape[1] // LANES):
        ref[0, c] = val[:, c * LANES:(c + 1) * LANES]


def _load_slabs(ref):
    return jnp.concatenate([ref[0, c] for c in range(ref.shape[1])], axis=1)


def _norm_mod(x, g, shift, scale):
    y = x * lax.rsqrt(jnp.mean(x * x, axis=-1, keepdims=True) + EPS) * g
    return y * (1.0 + scale) + shift


def _ada_kernel(c_ref, w_ref, b_ref, o_ref):
    c = c_ref[...]
    s_hi, s_lo = _split_hi_lo(c * _sigmoid(c))
    w_hi, w_lo = _split_hi_lo(w_ref[0])
    acc = jnp.dot(s_hi, w_hi, preferred_element_type=F32)
    acc += jnp.dot(s_hi, w_lo, preferred_element_type=F32)
    acc += jnp.dot(s_lo, w_hi, preferred_element_type=F32)
    o_ref[0] = acc + b_ref[0]


def _ada(cvec, ada_w, ada_b):
    depth, d, n6 = ada_w.shape
    tn = 1536
    return pl.pallas_call(
        _ada_kernel,
        grid=(depth, n6 // tn),
        in_specs=[
            pl.BlockSpec((MOD_ROWS, d), lambda l, j: (0, 0)),
            pl.BlockSpec((1, d, tn), lambda l, j: (l, 0, j)),
            pl.BlockSpec((1, 1, tn), lambda l, j: (l, 0, j)),
        ],
        out_specs=pl.BlockSpec((1, MOD_ROWS, tn), lambda l, j: (l, 0, j)),
        out_shape=jax.ShapeDtypeStruct((depth, MOD_ROWS, n6), F32),
        compiler_params=_params("arbitrary", "arbitrary"),
        name="ada",
    )(cvec, ada_w, ada_b.reshape(depth, 1, n6))


def _even_in_kernel(*refs, rope):
    if rope:
        (x_ref, g_ref, sh_ref, sc_ref, w_ref, cos_ref, sa_ref, sb_ref, q_ref, kv_ref, gb_ref, cu_ref) = refs
    else:
        (x_ref, g_ref, sh_ref, sc_ref, w_ref, q_ref, kv_ref, gb_ref, cu_ref) = refs
    h = _norm_mod(x_ref[0], g_ref[...], sh_ref[0], sc_ref[0]).astype(MXU_DTYPE)
    y = jnp.dot(h, w_ref[...], preferred_element_type=F32)
    q = y[:, :ATTN_WIDTH]
    k = y[:, ATTN_WIDTH:ATTN_WIDTH + KV_WIDTH]
    v = y[:, ATTN_WIDTH + KV_WIDTH:ATTN_WIDTH + 2 * KV_WIDTH]
    c0 = ATTN_WIDTH + 2 * KV_WIDTH
    if rope:
        cos, sa, sb = cos_ref[...], sa_ref[...], sb_ref[...]

        def rot(z):
            return z * cos + pltpu.roll(z, 16, 1) * sa + pltpu.roll(z, 112, 1) * sb

        q = jnp.concatenate([rot(q[:, j * 128:(j + 1) * 128]) for j in range(ATTN_WIDTH // 128)], axis=1)
        k = rot(k)
    q_ref[0] = (q * (HEAD_DIM ** -0.5)).astype(q_ref.dtype)
    kv_ref[0] = jnp.concatenate([k, v], axis=1).astype(kv_ref.dtype)
    gb_ref[0] = y[:, c0:c0 + CONV_B_WIDTH]
    cu_ref[0] = y[:, c0 + CONV_B_WIDTH:c0 + 2 * CONV_B_WIDTH] * y[:, c0 + 2 * CONV_B_WIDTH:]


def _even_in(x, g, shift, scale, mod_row, w, tables, tile):
    b, n, d = x.shape
    nt = n // tile
    rope = tables is not None
    row = (lambda bi: bi) if mod_row is None else (lambda bi: mod_row)
    in_specs = [
        pl.BlockSpec((1, tile, d), lambda bi, t: (bi, t, 0)),
        pl.BlockSpec((1, d), lambda bi, t: (0, 0)),
        pl.BlockSpec((1, 1, d), lambda bi, t: (row(bi), 0, 0)),
        pl.BlockSpec((1, 1, d), lambda bi, t: (row(bi), 0, 0)),
        pl.BlockSpec(w.shape, lambda bi, t: (0, 0)),
    ]
    args = [x, g, shift, scale, w]
    if rope:
        in_specs += [pl.BlockSpec((tile, 128), lambda bi, t: (t, 0))] * 3
        args += list(tables)
    out_shape = (
        jax.ShapeDtypeStruct((b, n, ATTN_WIDTH), MXU_DTYPE),
        jax.ShapeDtypeStruct((b, n, 2 * KV_WIDTH), MXU_DTYPE),
        jax.ShapeDtypeStruct((b, n, CONV_B_WIDTH), F32),
        jax.ShapeDtypeStruct((b, n, CONV_B_WIDTH), F32),
    )
    out_specs = tuple(pl.BlockSpec((1, tile, s.shape[-1]), lambda bi, t: (bi, t, 0)) for s in out_shape)
    return pl.pallas_call(
        functools.partial(_even_in_kernel, rope=rope),
        grid=(b, nt),
        in_specs=in_specs,
        out_specs=out_specs,
        out_shape=out_shape,
        compiler_params=_params("arbitrary", "arbitrary"),
        name="even_in_rope" if rope else "even_in",
    )(*args)


def _rope_tables(n):
    nf = HEAD_DIM // 4
    pos = jnp.arange(n)
    rows = (pos // GRID_W).astype(F32)
    cols = (pos % GRID_W).astype(F32)
    lane = jnp.arange(128)
    inv = ROPE_BASE ** (-(lane % nf).astype(F32) / nf)
    use_col = (lane % HEAD_DIM) >= HEAD_DIM // 2
    ang = jnp.where(use_col[None, :], cols[:, None], rows[:, None]) * inv[None, :]
    cos, sin = jnp.cos(ang), jnp.sin(ang)
    second = ((lane % (2 * nf)) >= nf)[None, :]
    return cos, jnp.where(second, sin, 0.0), jnp.where(second, 0.0, -sin)


def _attn_kernel(*refs, n, has_local):
    if has_local:
        sink_ref, q_ref, kv_ref, kvc_ref, bias_ref, o_ref = refs
    else:
        sink_ref, q_ref, kvc_ref, o_ref = refs
    n_loc = 3 * BLOCK
    nb = n // BLOCK
    hpd = ATTN_HEADS_PER_DOT
    grp = lax.broadcasted_iota(jnp.int32, (hpd * BLOCK, 1), 0) // BLOCK
    for sb in range(q_ref.shape[1] // BLOCK):
        i = pl.program_id(1) * (q_ref.shape[1] // BLOCK) + sb
        q = q_ref[0, sb * BLOCK:(sb + 1) * BLOCK, :]
        kvall = kvc_ref[0]
        if has_local:
            start = pl.multiple_of(_local_start(i, n), BLOCK)
            kvall = jnp.concatenate([kv_ref[0, pl.ds(start, n_loc), :], kvall], axis=0)
            case = jnp.where(i == 0, 0, jnp.where(i == nb - 1, 2, 1))
            bias = jnp.concatenate([bias_ref[case]] * hpd, axis=0)
        outs = []
        for h0 in range(0, N_Q_HEADS, hpd):
            hk = h0 // Q_PER_KV
            kh = kvall[:, hk * HEAD_DIM:(hk + 1) * HEAD_DIM]
            vh = kvall[:, KV_WIDTH + hk * HEAD_DIM:KV_WIDTH + (hk + 1) * HEAD_DIM]
            qg = jnp.concatenate([q[:, (h0 + g) * HEAD_DIM:(h0 + g + 1) * HEAD_DIM] for g in range(hpd)], axis=0)
            s = lax.dot_general(qg, kh, (((1,), (1,)), ((), ())), preferred_element_type=F32)
            if has_local:
                s = jnp.concatenate([s[:, :n_loc] + bias, s[:, n_loc:]], axis=1)
            snk = jnp.zeros((hpd * BLOCK, 1), F32)
            for g in range(hpd):
                snk = jnp.where(grp == g, sink_ref[h0 + g], snk)
            m = jnp.maximum(jnp.max(s, axis=1, keepdims=True), snk)
            p = jnp.exp(s - m).astype(MXU_DTYPE)
            v_ones = jnp.concatenate([vh, jnp.ones_like(vh)], axis=1)
            ov = jnp.dot(p, v_ones, preferred_element_type=F32)
            o = ov[:, :HEAD_DIM] / (ov[:, HEAD_DIM:HEAD_DIM + 1] + jnp.exp(snk - m))
            outs += [o[g * BLOCK:(g + 1) * BLOCK] for g in range(hpd)]
        o_ref[0, sb * BLOCK:(sb + 1) * BLOCK, :] = jnp.concatenate(outs, axis=1).astype(o_ref.dtype)


def _local_start(i, n):
    return jnp.clip(i * BLOCK - BLOCK, 0, n - 3 * BLOCK)


def _window_bias(n):
    nb = n // BLOCK
    r = jnp.arange(BLOCK)[:, None]
    c = jnp.arange(3 * BLOCK)[None, :]
    cases = []
    for i in (0, 1, nb - 1):
        diff = (_local_start(i, n) + c) - (i * BLOCK + r)
        cases.append(jnp.where(jnp.abs(diff) <= WINDOW, 0.0, NEG_INF).astype(F32))
    return jnp.stack(cases)


def _attention(sink, q, kv, kvc):
    b, n, _ = q.shape
    lc = kvc.shape[1]
    has_local = kv is not None
    qrows = min(ATTN_QBLOCKS * BLOCK, n)
    assert n % qrows == 0
    in_specs = [pl.BlockSpec(memory_space=pltpu.SMEM), pl.BlockSpec((1, qrows, ATTN_WIDTH), lambda bi, i: (bi, i, 0))]
    args = [sink, q]
    if has_local:
        in_specs.append(pl.BlockSpec((1, n, 2 * KV_WIDTH), lambda bi, i: (bi, 0, 0)))
        args.append(kv)
    in_specs.append(pl.BlockSpec((1, lc, 2 * KV_WIDTH), lambda bi, i: (bi, 0, 0)))
    args.append(kvc)
    if has_local:
        nb = n // BLOCK
        assert nb >= 4
        in_specs.append(pl.BlockSpec((3, BLOCK, 3 * BLOCK), lambda bi, i: (0, 0, 0)))
        args.append(_window_bias(n))
    return pl.pallas_call(
        functools.partial(_attn_kernel, n=n, has_local=has_local),
        grid=(b, n // qrows),
        in_specs=in_specs,
        out_specs=pl.BlockSpec((1, qrows, ATTN_WIDTH), lambda bi, i: (bi, i, 0)),
        out_shape=jax.ShapeDtypeStruct((b, n, ATTN_WIDTH), MXU_DTYPE),
        compiler_params=_params("arbitrary", "arbitrary"),
        name="attn_local" if has_local else "attn_ctx",
    )(*args)


def _mix_out_kernel(*refs, conv, first, last):
    if conv:
        (att_ref, gb_ref, cu_ref, cup_ref, cun_ref, cw_ref, cb_ref,
         w_ref, x_ref, gate_ref, g2_ref, sh2_ref, sc2_ref, rwh_ref, rwl_ref, xo_ref, h_ref, afft_ref) = refs
        t = pl.program_id(1)
        prev = jnp.where(t == first, 0.0, cup_ref[0])
        nxt = jnp.where(t == last, 0.0, cun_ref[0])
        ext = jnp.concatenate([prev, cu_ref[0], nxt], axis=0)
        cw = cw_ref[...]
    else:
        (y_ref, w_ref, x_ref, gate_ref, g2_ref, sh2_ref, sc2_ref, rwh_ref, rwl_ref, xo_ref, h_ref, afft_ref) = refs
    tile = x_ref.shape[1]
    rows = min(MIX_OUT_ROWS, tile)
    for s in range(tile // rows):
        r0 = s * rows
        rs = slice(r0, r0 + rows)
        if conv:
            cv = (cw[0:1] * ext[HALO - 1 + r0:HALO - 1 + r0 + rows] + cw[1:2] * ext[HALO + r0:HALO + r0 + rows]
                  + cw[2:3] * ext[HALO + 1 + r0:HALO + 1 + r0 + rows] + cb_ref[...])
            cat = jnp.concatenate([att_ref[0, rs, :], (gb_ref[0, rs, :] * cv).astype(MXU_DTYPE)], axis=1)
        else:
            cat = jnp.concatenate([y_ref[0, c, rs, :] for c in range(y_ref.shape[1])], axis=1).astype(MXU_DTYPE)
        y = jnp.dot(cat, w_ref[...], preferred_element_type=F32)
        x = x_ref[0, rs, :] + gate_ref[0] * y
        xo_ref[0, rs, :] = x
        h = _norm_mod(x, g2_ref[...], sh2_ref[0], sc2_ref[0])
        _store_token_tiles(h_ref, h, r0)
        h_hi, h_lo = _split_hi_lo(h)
        both = jnp.dot(h_hi, jnp.concatenate([rwh_ref[...], rwl_ref[...]], axis=1), preferred_element_type=F32)
        logits = both[:, :LANES] + both[:, LANES:] + jnp.dot(h_lo, rwh_ref[...], preferred_element_type=F32)
        lane = lax.broadcasted_iota(jnp.int32, logits.shape, 1)
        logits = jnp.where(lane < N_EXPERTS, logits, NEG_INF)
        e = jnp.exp(logits - jnp.max(logits, axis=1, keepdims=True))
        aff = e / jnp.sum(e, axis=1, keepdims=True)
        afft_ref[0, :, rs] = aff.T[:N_EXPERTS]


def _mix_out(mix_in, conv_params, w, x, gate, g2, sh2, sc2, mod_row, rw_hi, rw_lo, tile):
    b, n, d = x.shape
    nt = n // tile
    conv = conv_params is not None
    row = (lambda bi: bi) if mod_row is None else (lambda bi: mod_row)
    tok = lambda width: pl.BlockSpec((1, tile, width), lambda bi, t: (bi, t, 0))
    modspec = pl.BlockSpec((1, 1, d), lambda bi, t: (row(bi), 0, 0))
    full = lambda a: pl.BlockSpec(a.shape, lambda bi, t: (0,) * a.ndim)
    if conv:
        att, gb, cu = mix_in
        cw, cb = conv_params
        per = tile // HALO
        in_specs = [tok(ATTN_WIDTH), tok(CONV_B_WIDTH), tok(CONV_B_WIDTH),
                    pl.BlockSpec((1, HALO, CONV_B_WIDTH), lambda bi, t: (bi, jnp.maximum(t * per - 1, 0), 0)),
                    pl.BlockSpec((1, HALO, CONV_B_WIDTH), lambda bi, t: (bi, jnp.minimum((t + 1) * per, n // HALO - 1), 0)),
                    full(cw), full(cb)]
        args = [att, gb, cu, cu, cu, cw, cb]
    else:
        in_specs = [_slab_spec(tile, lambda bi, t: (bi, 0, t, 0), d)]
        args = [mix_in]
    in_specs += [full(w), tok(d), modspec, full(g2), modspec, modspec, full(rw_hi), full(rw_lo)]
    args += [w, x, gate, g2, sh2, sc2, rw_hi, rw_lo]
    out_shape = (jax.ShapeDtypeStruct((b, n, d), F32), jax.ShapeDtypeStruct((b, n * SUBLANES, LANES), F32),
                 jax.ShapeDtypeStruct((b, N_EXPERTS, n), F32))
    out_specs = (tok(d), pl.BlockSpec((1, tile * SUBLANES, LANES), lambda bi, t: (bi, t, 0)),
                 pl.BlockSpec((1, N_EXPERTS, tile), lambda bi, t: (bi, 0, t)))
    return pl.pallas_call(
        functools.partial(_mix_out_kernel, conv=conv, first=0, last=nt - 1),
        grid=(b, nt),
        in_specs=in_specs,
        out_specs=out_specs,
        out_shape=out_shape,
        compiler_params=_params("arbitrary", "arbitrary"),
        name="even_out" if conv else "odd_out",
    )(*args)


def _cumsum_lanes(x):
    n = x.shape[1]
    r = lax.broadcasted_iota(jnp.int32, (CUMSUM_TILE, CUMSUM_TILE), 0)
    c = lax.broadcasted_iota(jnp.int32, (CUMSUM_TILE, CUMSUM_TILE), 1)
    tri = jnp.where(r <= c, 1.0, 0.0).astype(MXU_DTYPE)
    carry = jnp.zeros((x.shape[0], 1), F32)
    outs = []
    for k in range(n // CUMSUM_TILE):
        blk = x[:, k * CUMSUM_TILE:(k + 1) * CUMSUM_TILE].astype(MXU_DTYPE)
        loc = jnp.dot(blk, tri, preferred_element_type=F32) + carry
        outs.append(loc)
        carry = loc[:, CUMSUM_TILE - 1:CUMSUM_TILE]
    return jnp.concatenate(outs, axis=1)


SLOT_EMPTY = 1 << 20


def _select_kernel(at_ref, idx_ref, g_ref, *, cap):
    bblk, n_e, n = at_ref.shape
    at = at_ref[...].reshape(bblk * n_e, n)
    n_exp = bblk * n_e
    capf = float(cap)

    def count_ge(thr):
        return jnp.sum(jnp.where(at >= thr, 1.0, 0.0), axis=1, keepdims=True)

    def bit_body(_, c):
        lo_i, hi_i = c
        mid = lo_i + ((hi_i - lo_i) >> 1)
        ge = count_ge(lax.bitcast_convert_type(mid, F32)) >= capf
        return jnp.where(ge, mid, lo_i), jnp.where(ge, hi_i, mid)

    lo_i, hi_i = lax.fori_loop(
        0, 31, bit_body, (jnp.zeros((n_exp, 1), jnp.int32), jnp.full((n_exp, 1), 0x3F800001, jnp.int32)))

    def val_body(_, c):
        lo, hi = c
        mid = 0.5 * (lo + hi)
        ge = count_ge(mid) >= capf
        return jnp.where(ge, mid, lo), jnp.where(ge, hi, mid)

    lo, hi = lax.fori_loop(
        0, 24, val_body, (lax.bitcast_convert_type(lo_i, F32), lax.bitcast_convert_type(hi_i, F32)))
    need = capf - count_ge(hi)

    above = jnp.where(at >= hi, 1.0, 0.0)
    band = jnp.where(at >= lo, 1.0, 0.0) - above
    sel = above + band * jnp.where(_cumsum_lanes(band) <= need, 1.0, 0.0)
    rank = _cumsum_lanes(sel)

    lane = lax.broadcasted_iota(jnp.int32, (n_exp, n), 1)
    disp = jnp.where(sel > 0.5, lane + 1 - rank.astype(jnp.int32), SLOT_EMPTY)
    g = at
    for k in range(n.bit_length() - 1):
        step = 1 << k
        moving = ((disp >> k) & 1) == 1
        disp_in = pltpu.roll(disp, n - step, 1)
        arriving = ((disp_in >> k) & 1) == 1
        g = jnp.where(arriving, pltpu.roll(g, n - step, 1), g)
        disp = jnp.where(arriving, disp_in, jnp.where(moving, SLOT_EMPTY, disp))

    capp = pl.cdiv(cap, LANES) * LANES
    slot = lax.broadcasted_iota(jnp.int32, (n_exp, cap), 1)
    idx_ref[...] = jnp.clip(slot + disp[:, :cap], 0, n - 1).reshape(bblk, n_e, cap)
    for s in range(bblk):
        g_pad = jnp.concatenate([g[s * n_e:(s + 1) * n_e, :capp], jnp.zeros((LANES - n_e, capp), F32)], axis=0)
        g_ref[s] = g_pad.T[:cap]


def _select(aff_t, cap):
    b, n_exp, n = aff_t.shape
    assert n & (n - 1) == 0 and n % CUMSUM_TILE == 0 and n < SLOT_EMPTY
    bblk = b
    return pl.pallas_call(
        functools.partial(_select_kernel, cap=cap),
        grid=(b // bblk,),
        in_specs=[pl.BlockSpec((bblk, n_exp, n), lambda bi: (bi, 0, 0))],
        out_specs=(pl.BlockSpec((bblk, n_exp, cap), lambda bi: (bi, 0, 0)),
                   pl.BlockSpec((bblk, cap, LANES), lambda bi: (bi, 0, 0))),
        out_shape=(jax.ShapeDtypeStruct((b, n_exp, cap), jnp.int32), jax.ShapeDtypeStruct((b, cap, LANES), F32)),
        compiler_params=_params("arbitrary"),
        name="moe_select",
    )(aff_t)


def _gather_kernel(idx_ref, src_ref, xe_ref, xcm_ref, *, cap, n_exp):
    bblk = src_ref.shape[0]
    eblk, out_rows = xe_ref.shape[1], xe_ref.shape[2]
    stride = out_rows + SUBLANES
    for k in range(eblk):
        for s in range(bblk):
            base = ((pl.program_id(0) * bblk + s) * n_exp + pl.program_id(1) * eblk + k) * cap
            for r in range(cap):
                t = idx_ref[base + r]
                xcm_ref[k, pl.ds(s * cap + r, SUBLANES, stride=stride), :] = (
                    src_ref[s, pl.ds(pl.multiple_of(t * SUBLANES, SUBLANES), SUBLANES), :])
        used = bblk * cap
        if used < out_rows:
            for j in range(SUBLANES):
                xcm_ref[k, pl.ds(j * stride + used, out_rows - used), :] = jnp.zeros((out_rows - used, LANES), F32)
        xe_ref[0, k] = jnp.concatenate(
            [xcm_ref[k, pl.ds(j * stride, out_rows), :] for j in range(SUBLANES)], axis=1).astype(xe_ref.dtype)


def _gather(idx_flat, src, cap, bblk, out_rows):
    b, rows, _ = src.shape
    d = SUBLANES * LANES
    assert cap % SUBLANES == 0 and bblk * cap <= out_rows and b % bblk == 0 and N_EXPERTS % DISPATCH_EXPERTS == 0
    return pl.pallas_call(
        functools.partial(_gather_kernel, cap=cap, n_exp=N_EXPERTS),
        grid=(b // bblk, N_EXPERTS // DISPATCH_EXPERTS),
        in_specs=[pl.BlockSpec(memory_space=pltpu.SMEM),
                  pl.BlockSpec((bblk, rows, LANES), lambda bi, ei: (bi, 0, 0))],
        out_specs=pl.BlockSpec((1, DISPATCH_EXPERTS, out_rows, d), lambda bi, ei: (bi, ei, 0, 0)),
        out_shape=jax.ShapeDtypeStruct((b // bblk, N_EXPERTS, out_rows, d), MXU_DTYPE),
        scratch_shapes=[pltpu.VMEM((DISPATCH_EXPERTS, SUBLANES * (out_rows + SUBLANES), LANES), F32)],
        compiler_params=_params("arbitrary", "arbitrary"),
        name="moe_gather",
    )(idx_flat, src)


WEIGHT_CHUNKS = 2
WEIGHT_DMA_PRIORITY = 1
FFN_BLOCKS = 2


def _moe_ffn_kernel(*refs, layer, n_steps, has_ctx):
    if has_ctx:
        (x_ref, g_ref, xc_ref, gc_ref, wg_hbm, wu_hbm, wd_hbm, o_ref, oc_ref,
         wgu_s, wd_s, stage_in, stage_out, sem) = refs
    else:
        (x_ref, g_ref, wg_hbm, wu_hbm, wd_hbm, o_ref, wgu_s, wd_s, stage_in, stage_out, sem) = refs
    n_exp = pl.num_programs(0)
    e = pl.program_id(0)
    s = pl.program_id(1)
    slot = e % 2
    ff = wg_hbm.shape[3]
    chunks = []
    for hbm, dst, col0, stage in ((wg_hbm, wgu_s, 0, stage_in), (wu_hbm, wgu_s, ff, stage_in),
                                  (wd_hbm, wd_s, 0, stage_out)):
        n_rows = hbm.shape[2] // WEIGHT_CHUNKS
        chunks += [(hbm, dst, col0, stage, c * n_rows, n_rows) for c in range(WEIGHT_CHUNKS)]
    per_step = pl.cdiv(len(chunks), n_steps)

    def copy(c, expert):
        hbm, _, _, stage, r0, n_rows = chunks[c]
        return pltpu.make_async_copy(hbm.at[layer, expert, pl.ds(r0, n_rows), :], stage, sem.at[0])

    def cast(c, to_slot):
        hbm, dst, col0, stage, r0, n_rows = chunks[c]
        dst[to_slot, pl.ds(r0, n_rows), pl.ds(col0, hbm.shape[3])] = stage[...].astype(dst.dtype)

    @pl.when((e == 0) & (s == 0))
    def _():
        for c in range(len(chunks)):
            copy(c, 0).start()
            copy(c, 0).wait()
            cast(c, 0)

    has_next = e + 1 < n_exp
    for c in range(len(chunks)):
        if c % per_step == 0:
            @pl.when(has_next & (s == c // per_step))
            def _():
                copy(c, e + 1).start(priority=WEIGHT_DMA_PRIORITY)

    def ffn(xr, gr, outr, k):
        x, aff = xr[k, 0], gr[k]
        rows = x.shape[0]
        au = jnp.dot(x, wgu_s[slot], preferred_element_type=F32)
        a, u = au[:, :ff], au[:, ff:]
        mid = (a * _sigmoid(a) * u).astype(MXU_DTYPE)
        y = jnp.dot(mid, wd_s[slot], preferred_element_type=F32)
        lane = lax.broadcasted_iota(jnp.int32, aff.shape, 1)
        y = y * jnp.sum(jnp.where(lane == e, aff, 0.0), axis=1, keepdims=True)
        stride = rows + SUBLANES
        for j in range(SUBLANES):
            outr[k, 0, pl.ds(j * stride, rows), :] = y[:, j * LANES:(j + 1) * LANES]
            outr[k, 0, pl.ds(j * stride + rows, SUBLANES), :] = jnp.zeros((SUBLANES, LANES), F32)

    n_blocks = x_ref.shape[0]
    is_ctx = (s == n_steps - 1) if has_ctx else None
    for k in range(n_blocks):
        if has_ctx:
            if k == 0:
                pl.when(is_ctx)(lambda: ffn(xc_ref, gc_ref, oc_ref, 0))
            pl.when(jnp.logical_not(is_ctx))(lambda k=k: ffn(x_ref, g_ref, o_ref, k))
        else:
            pl.when(s < n_steps)(lambda k=k: ffn(x_ref, g_ref, o_ref, k))
        for c in range(len(chunks)):
            if min(c % per_step, n_blocks - 1) == k:
                @pl.when(has_next & (s == c // per_step))
                def _():
                    copy(c, e + 1).wait()
                    cast(c, 1 - slot)
                    if (c + 1) % per_step != 0 and c + 1 < len(chunks):
                        copy(c + 1, e + 1).start(priority=WEIGHT_DMA_PRIORITY)


def _moe_ffn(xe, ge, ctx_block, layer, wg, wu, wd):
    nb, e, rows, d = xe.shape
    ff = wg.shape[-1]
    assert d % WEIGHT_CHUNKS == 0 and ff % (WEIGHT_CHUNKS * SUBLANES) == 0 and ff % LANES == 0
    has_ctx = ctx_block is not None
    fb = FFN_BLOCKS if nb % FFN_BLOCKS == 0 else 1
    n_steps = nb // fb + int(has_ctx)
    hbm = pl.BlockSpec(memory_space=pl.ANY)
    out_rows = SUBLANES * (rows + SUBLANES)
    blk = lambda bi: jnp.minimum(bi, nb // fb - 1)
    in_specs = [pl.BlockSpec((fb, 1, rows, d), lambda ei, bi: (blk(bi), ei, 0, 0)),
                pl.BlockSpec((fb, rows, LANES), lambda ei, bi: (blk(bi), 0, 0))]
    args = [xe, ge]
    out_specs = [pl.BlockSpec((fb, 1, out_rows, LANES), lambda ei, bi: (blk(bi), ei, 0, 0))]
    out_shape = [jax.ShapeDtypeStruct((nb, e, out_rows, LANES), F32)]
    if has_ctx:
        rows_c = ctx_block[0].shape[2]
        assert ctx_block[0].shape == (1, e, rows_c, d) and ctx_block[1].shape == (1, rows_c, LANES)
        out_rows_c = SUBLANES * (rows_c + SUBLANES)
        in_specs += [pl.BlockSpec((1, 1, rows_c, d), lambda ei, bi: (0, ei, 0, 0)),
                     pl.BlockSpec((1, rows_c, LANES), lambda ei, bi: (0, 0, 0))]
        args += list(ctx_block)
        out_specs.append(pl.BlockSpec((1, 1, out_rows_c, LANES), lambda ei, bi: (0, ei, 0, 0)))
        out_shape.append(jax.ShapeDtypeStruct((1, e, out_rows_c, LANES), F32))
    return pl.pallas_call(
        functools.partial(_moe_ffn_kernel, layer=layer, n_steps=n_steps, has_ctx=has_ctx),
        grid=(e, n_steps),
        in_specs=in_specs + [hbm, hbm, hbm],
        out_specs=tuple(out_specs),
        out_shape=tuple(out_shape),
        scratch_shapes=[pltpu.VMEM((2, d, 2 * ff), MXU_DTYPE), pltpu.VMEM((2, ff, d), MXU_DTYPE),
                        pltpu.VMEM((d // WEIGHT_CHUNKS, ff), F32), pltpu.VMEM((ff // WEIGHT_CHUNKS, d), F32),
                        pltpu.SemaphoreType.DMA((1,))],
        compiler_params=_params("arbitrary", "arbitrary"),
        name="moe_ffn",
    )(*args, wg, wu, wd)


COMBINE_UNROLL = 8


def _combine_kernel(idx_ref, y_ref, acc_ref, *, cap, n_exp):
    bblk = acc_ref.shape[0]
    eblk = y_ref.shape[1]
    ei = pl.program_id(1)
    stride = y_ref.shape[2] // SUBLANES

    @pl.when(ei == 0)
    def _():
        acc_ref[...] = jnp.zeros_like(acc_ref)

    for k in range(eblk):
        for s in range(bblk):
            base = ((pl.program_id(0) * bblk + s) * n_exp + ei * eblk + k) * cap
            for r0 in range(0, cap, COMBINE_UNROLL):
                toks = [pl.multiple_of(idx_ref[base + r0 + i] * SUBLANES, SUBLANES) for i in range(COMBINE_UNROLL)]
                new = [acc_ref[s, pl.ds(toks[i], SUBLANES), :]
                       + y_ref[0, k, pl.ds(s * cap + r0 + i, SUBLANES, stride=stride), :]
                       for i in range(COMBINE_UNROLL)]
                for i in range(COMBINE_UNROLL):
                    acc_ref[s, pl.ds(toks[i], SUBLANES), :] = new[i]


def _combine(idx_flat, y, b, n, cap, bblk):
    nb, n_exp, rows, _ = y.shape
    assert cap % COMBINE_UNROLL == 0 and nb * bblk == b and bblk * cap <= rows // SUBLANES - SUBLANES
    return pl.pallas_call(
        functools.partial(_combine_kernel, cap=cap, n_exp=n_exp),
        grid=(nb, n_exp // DISPATCH_EXPERTS),
        in_specs=[pl.BlockSpec(memory_space=pltpu.SMEM),
                  pl.BlockSpec((1, DISPATCH_EXPERTS, rows, LANES), lambda bi, ei: (bi, ei, 0, 0))],
        out_specs=pl.BlockSpec((bblk, n * SUBLANES, LANES), lambda bi, ei: (bi, 0, 0)),
        out_shape=jax.ShapeDtypeStruct((b, n * SUBLANES, LANES), F32),
        compiler_params=_params("arbitrary", "arbitrary"),
        name="moe_combine",
    )(idx_flat, y)


def _expert_choice(h_lat, aff_lat, h_ctx, aff_ctx, layer, wg, wu, wd):
    b, _, n = aff_lat.shape
    cap = max(1, CAPACITY_FACTOR * n // N_EXPERTS)
    idx, g = _select(aff_lat, cap)
    idx = idx.reshape(-1)
    xe = _gather(idx, h_lat, cap, 1, cap)
    if h_ctx is None:
        (y,) = _moe_ffn(xe, g, None, layer, wg, wu, wd)
        return _combine(idx, y, b, n, cap, 1), None
    lc = aff_ctx.shape[2]
    cap_c = max(1, CAPACITY_FACTOR * lc // N_EXPERTS)
    idx_c, g_c = _select(aff_ctx, cap_c)
    idx_c = idx_c.reshape(-1)
    xe_c = _gather(idx_c, h_ctx, cap_c, b, b * cap_c)
    g_c = g_c.reshape(1, b * cap_c, LANES)
    y, y_c = _moe_ffn(xe, g, (xe_c, g_c), layer, wg, wu, wd)
    return _combine(idx, y, b, n, cap, 1), _combine(idx_c, y_c, b, lc, cap_c, b)


def _odd_in_kernel(x_ref, moe_ref, gate_ref, g_ref, sh_ref, sc_ref, w_ref, xo_ref, gg_ref, u_ref):
    x = x_ref[0] + gate_ref[0] * _load_token_tiles(moe_ref)
    xo_ref[0] = x
    h = _norm_mod(x, g_ref[...], sh_ref[0], sc_ref[0]).astype(MXU_DTYPE)
    y = jnp.dot(h, w_ref[...], preferred_element_type=F32)
    gl = y[:, :LRU_WIDTH]
    _store_slabs(gg_ref, 0.25 * gl * (1.0 + jnp.tanh(0.7978845608028654 * (gl + 0.044715 * gl * gl * gl))))
    _store_slabs(u_ref, y[:, LRU_WIDTH:])


def _odd_in(x, moe, gate, g, shift, scale, mod_row, w, tile):
    b, n, d = x.shape
    row = (lambda bi: bi) if mod_row is None else (lambda bi: mod_row)
    tok = pl.BlockSpec((1, tile, d), lambda bi, t: (bi, t, 0))
    tiles = pl.BlockSpec((1, tile * SUBLANES, LANES), lambda bi, t: (bi, t, 0))
    modspec = pl.BlockSpec((1, 1, d), lambda bi, t: (row(bi), 0, 0))
    return pl.pallas_call(
        _odd_in_kernel,
        grid=(b, n // tile),
        in_specs=[tok, tiles, modspec, pl.BlockSpec((1, d), lambda bi, t: (0, 0)), modspec, modspec,
                  pl.BlockSpec(w.shape, lambda bi, t: (0, 0))],
        out_specs=(tok, _slab_spec(tile, lambda bi, t: (bi, 0, t, 0)), _slab_spec(tile, lambda bi, t: (bi, 0, t, 0))),
        out_shape=(jax.ShapeDtypeStruct((b, n, d), F32),) + (jax.ShapeDtypeStruct((b, d // LANES, n, LANES), F32),) * 2,
        compiler_params=_params("arbitrary", "arbitrary"),
        name="odd_in",
    )(x, moe, gate, g, shift, scale, w)


def _lru_kernel(*refs, reverse, nt):
    if reverse:
        (uc_ref, wax_ref, lam_ref, h0_ref, hf_ref, gg_ref, out_ref, hlast_ref, a_scr, b_scr, uc_scr, carry_scr) = refs
        n_slab, tile = uc_ref.shape[1], uc_ref.shape[2]
    else:
        (u_ref, up_ref, un_ref, cw_ref, cb_ref, wax_ref, lam_ref, h0_ref,
         out_ref, uco_ref, hlast_ref, a_scr, b_scr, uc_scr, carry_scr) = refs
        n_slab, tile = u_ref.shape[1], u_ref.shape[2]
    w = n_slab * LANES
    per = tile // SUBLANES
    rows = lambda j: slice(j * per, (j + 1) * per)
    phase = lambda ref, j: jnp.concatenate(
        [ref[0, c, pl.ds(j, per, stride=SUBLANES), :] for c in range(n_slab)], axis=1)
    phase_major = lambda ref, j: jnp.concatenate([ref[0, c, rows(j), :] for c in range(n_slab)], axis=1)

    def store_phase(ref, j, val, time_order):
        for c in range(n_slab):
            dst = pl.ds(j, per, stride=SUBLANES) if time_order else rows(j)
            ref[0, c, dst, :] = val[:, c * LANES:(c + 1) * LANES]

    t = pl.program_id(1)
    if reverse:
        for j in range(SUBLANES):
            uc_scr[rows(j), :] = phase_major(uc_ref, j)
    else:
        prev = jnp.where(t == 0, 0.0, _load_slabs(up_ref))
        nxt = jnp.where(t == nt - 1, 0.0, _load_slabs(un_ref))
        rowid = lax.broadcasted_iota(jnp.int32, (per, w), 0)
        shift_down = lambda x, first: jnp.where(rowid == 0, first, pltpu.roll(x, 1, 0))
        shift_up = lambda x, last: jnp.where(rowid == per - 1, last, pltpu.roll(x, per - 1, 0))
        u = [phase(u_ref, j) for j in range(SUBLANES)]
        um1 = [shift_down(u[7], prev[7:8])] + u[:7]
        um2 = [shift_down(u[6], prev[6:7]), um1[0]] + u[:6]
        up1 = u[1:] + [shift_up(u[0], nxt[0:1])]
        cw = cw_ref[...]
        for j in range(SUBLANES):
            uc_j = cb_ref[...] + cw[0:1] * um2[j] + cw[1:2] * um1[j] + cw[2:3] * u[j] + cw[3:4] * up1[j]
            uc_scr[rows(j), :] = uc_j
            store_phase(uco_ref, j, uc_j, False)

    lam = lam_ref[0]
    half_decay = (0.5 * LRU_C) * (jnp.maximum(-lam, 0.0) + jnp.log1p(jnp.exp(-jnp.abs(lam))))
    ones = jnp.where(lax.broadcasted_iota(jnp.int32, (tile, LRU_BLOCK), 1) < BIAS_TERMS, 1.0, 0.0).astype(MXU_DTYPE)
    for hd in range(LRU_HEADS):
        sl = slice(hd * LRU_BLOCK, (hd + 1) * LRU_BLOCK)
        uc = uc_scr[:, sl]
        z = jnp.dot(jnp.concatenate([uc.astype(MXU_DTYPE), ones], axis=1), wax_ref[0, hd], preferred_element_type=F32)
        hd_row = half_decay[:, sl]
        neg_log_a = hd_row * jnp.tanh(z[:, :LRU_BLOCK]) + hd_row
        gate2 = 1.0 + jnp.tanh(z[:, LRU_BLOCK:])
        a = jnp.exp(-neg_log_a)
        m2 = jnp.tanh(neg_log_a) * (a * a + 1.0)
        mult = jnp.where(m2 > 0.0, m2 * lax.rsqrt(m2), 0.0)
        a_scr[:, sl] = a
        b_scr[:, sl] = mult * (gate2 * uc)

    @pl.when(t == 0)
    def _():
        carry_scr[...] = jnp.broadcast_to(h0_ref[0], carry_scr.shape)

    order = list(range(SUBLANES))[::-1] if reverse else list(range(SUBLANES))
    hrun = b_scr[rows(order[0]), :]
    prun = a_scr[rows(order[0]), :]
    for j in order[1:]:
        aj = a_scr[rows(j), :]
        hrun = aj * hrun + b_scr[rows(j), :]
        prun = aj * prun
        b_scr[rows(j), :] = hrun
        a_scr[rows(j), :] = prun

    lane_row = lax.broadcasted_iota(jnp.int32, (SUBLANES, w), 0)
    carry = carry_scr[...]
    groups = list(range(per // SUBLANES))
    entering = [None] * len(groups)
    for m in (groups[::-1] if reverse else groups):
        a = prun[m * SUBLANES:(m + 1) * SUBLANES]
        bcoef = hrun[m * SUBLANES:(m + 1) * SUBLANES]
        for dist in (1, 2, 4):
            shift = (SUBLANES - dist) if reverse else dist
            msk = (lane_row < SUBLANES - dist) if reverse else (lane_row >= dist)
            a_s = pltpu.roll(a, shift, 0)
            b_s = pltpu.roll(bcoef, shift, 0)
            bcoef = jnp.where(msk, a * b_s + bcoef, bcoef)
            a = jnp.where(msk, a * a_s, a)
        after = a * carry + bcoef
        if reverse:
            entering[m] = jnp.where(lane_row == SUBLANES - 1, carry, pltpu.roll(after, SUBLANES - 1, 0))
            carry = jnp.broadcast_to(after[0:1], carry.shape)
        else:
            entering[m] = jnp.where(lane_row == 0, carry, pltpu.roll(after, 1, 0))
            carry = jnp.broadcast_to(after[SUBLANES - 1:SUBLANES], carry.shape)
    carry_scr[...] = carry
    hlast_ref[0] = carry[0:1]
    h_in = jnp.concatenate(entering, axis=0)

    for j in range(SUBLANES):
        hcur = b_scr[rows(j), :] + a_scr[rows(j), :] * h_in
        if reverse:
            hcur = phase(gg_ref, j) * (phase_major(hf_ref, j) + hcur)
        store_phase(out_ref, j, hcur, reverse)


BIAS_TERMS = 3


def _gate_weights(wa, wx, ba, bx):
    n_dir, heads, blk, _ = wa.shape
    w = 0.5 * jnp.concatenate([wa, wx], axis=-1)
    bias = 0.5 * jnp.concatenate([ba.reshape(n_dir, heads, blk), bx.reshape(n_dir, heads, blk)], axis=-1)
    terms = []
    for _ in range(BIAS_TERMS):
        term = bias.astype(MXU_DTYPE)
        terms.append(term)
        bias = bias - term.astype(F32)
    rows = jnp.stack(terms, axis=2)
    pad = jnp.zeros((n_dir, heads, blk - BIAS_TERMS, 2 * blk), MXU_DTYPE)
    return jnp.concatenate([w.astype(MXU_DTYPE), rows, pad], axis=2)


def _lru_fwd(u, cw, cb, wax, lam, h0, tile):
    b, n_slab, n, _ = u.shape
    w = n_slab * LANES
    nt = n // tile
    per = tile // HALO
    assert tile % (SUBLANES * SUBLANES) == 0
    tok = _slab_spec(tile, lambda bi, t: (bi, 0, t, 0), w)
    state = pl.BlockSpec((1, 1, w), lambda bi, t: (bi, 0, 0))
    return pl.pallas_call(
        functools.partial(_lru_kernel, reverse=False, nt=nt),
        grid=(b, nt),
        in_specs=[tok,
                  _slab_spec(HALO, lambda bi, t: (bi, 0, jnp.maximum(t * per - 1, 0), 0), w),
                  _slab_spec(HALO, lambda bi, t: (bi, 0, jnp.minimum((t + 1) * per, n // HALO - 1), 0), w),
                  pl.BlockSpec(cw.shape, lambda bi, t: (0, 0)),
                  pl.BlockSpec(cb.shape, lambda bi, t: (0, 0)),
                  pl.BlockSpec((1,) + wax.shape[1:], lambda bi, t: (0, 0, 0, 0)),
                  pl.BlockSpec((1, 1, w), lambda bi, t: (0, 0, 0)),
                  state],
        out_specs=(tok, tok, state),
        out_shape=(jax.ShapeDtypeStruct(u.shape, F32), jax.ShapeDtypeStruct(u.shape, F32),
                   jax.ShapeDtypeStruct((b, 1, w), F32)),
        scratch_shapes=[pltpu.VMEM((tile, w), F32), pltpu.VMEM((tile, w), F32), pltpu.VMEM((tile, w), F32),
                        pltpu.VMEM((SUBLANES, w), F32)],
        compiler_params=_params("arbitrary", "arbitrary"),
        name="lru_fwd",
    )(u, u, u, cw, cb, wax, lam, h0)


def _lru_bwd(uc, wax, lam, h0, hf, gg, tile):
    b, n_slab, n, _ = uc.shape
    w = n_slab * LANES
    nt = n // tile
    assert tile % (SUBLANES * SUBLANES) == 0
    tok = _slab_spec(tile, lambda bi, t: (bi, 0, nt - 1 - t, 0), w)
    state = pl.BlockSpec((1, 1, w), lambda bi, t: (bi, 0, 0))
    return pl.pallas_call(
        functools.partial(_lru_kernel, reverse=True, nt=nt),
        grid=(b, nt),
        in_specs=[tok,
                  pl.BlockSpec((1,) + wax.shape[1:], lambda bi, t: (1, 0, 0, 0)),
                  pl.BlockSpec((1, 1, w), lambda bi, t: (1, 0, 0)),
                  state, tok, tok],
        out_specs=(tok, state),
        out_shape=(jax.ShapeDtypeStruct(uc.shape, F32), jax.ShapeDtypeStruct((b, 1, w), F32)),
        scratch_shapes=[pltpu.VMEM((tile, w), F32), pltpu.VMEM((tile, w), F32), pltpu.VMEM((tile, w), F32),
                        pltpu.VMEM((SUBLANES, w), F32)],
        compiler_params=_params("arbitrary", "arbitrary"),
        name="lru_bwd",
    )(uc, wax, lam, h0, hf, gg)


def _final_kernel(x_ref, moe_ref, gate_ref, g_ref, o_ref):
    x = x_ref[0] + gate_ref[0] * _load_token_tiles(moe_ref)
    o_ref[0] = x * lax.rsqrt(jnp.mean(x * x, axis=-1, keepdims=True) + EPS) * g_ref[...]


def _final(x, moe, gate, g, tile):
    b, n, d = x.shape
    tok = pl.BlockSpec((1, tile, d), lambda bi, t: (bi, t, 0))
    tiles = pl.BlockSpec((1, tile * SUBLANES, LANES), lambda bi, t: (bi, t, 0))
    return pl.pallas_call(
        _final_kernel,
        grid=(b, n // tile),
        in_specs=[tok, tiles, pl.BlockSpec((1, 1, d), lambda bi, t: (bi, 0, 0)), pl.BlockSpec((1, d), lambda bi, t: (0, 0))],
        out_specs=tok,
        out_shape=jax.ShapeDtypeStruct((b, n, d), F32),
        compiler_params=_params("arbitrary", "arbitrary"),
        name="final_norm",
    )(x, moe, gate, g)


def kernel(x, c, ctx, c_ctx, ada_w, ada_b, norm_mix_g, norm_ffn_g, ev_w_in, ev_w_out, ev_sink, ev_conv_w, ev_conv_b, od_w_in, od_w_out, od_conv_w, od_conv_b, od_wa, od_ba, od_wx, od_bx, od_lambda, router_w, w_gate, w_up, w_down, final_g):
    b, n, d = x.shape
    lc = ctx.shape[1]
    depth = ada_w.shape[0]
    assert depth == 2 and d == D_MODEL and b < MOD_ROWS
    tile_l = tile_w = min(1024, n)
    tile_c = lc
    ctx_row = b

    cvec = jnp.zeros((MOD_ROWS, d), F32).at[:b].set(c).at[b].set(c_ctx)
    mods = _ada(cvec, ada_w, ada_b).reshape(depth, MOD_ROWS, 6, 1, d)
    mod = lambda l, j: mods[l, :, j]

    def router_split(l):
        return _split_hi_lo(jnp.pad(router_w[l], ((0, 0), (0, LANES - N_EXPERTS))))

    bf = lambda a: a.astype(MXU_DTYPE)

    g_mix = norm_mix_g[0].reshape(1, d)
    g_ffn = norm_ffn_g[0].reshape(1, d)
    w_in = bf(ev_w_in[0])
    w_out = bf(ev_w_out[0])
    conv_p = (ev_conv_w[0], ev_conv_b[0].reshape(1, -1))
    tables = _rope_tables(n)
    ql, kvl, gbl, cul = _even_in(x, g_mix, mod(0, 0), mod(0, 1), None, w_in, tables, tile_l)
    qc, kvc, gbc, cuc = _even_in(ctx, g_mix, mod(0, 0), mod(0, 1), ctx_row, w_in, None, tile_c)
    att_l = _attention(ev_sink[0], ql, kvl, kvc)
    att_c = _attention(ev_sink[0], qc, None, kvc)
    rw_hi, rw_lo = router_split(0)
    xl, hl, at_l = _mix_out((att_l, gbl, cul), conv_p, w_out, x, mod(0, 2), g_ffn, mod(0, 3), mod(0, 4), None,
                            rw_hi, rw_lo, tile_w)
    xc, hc, at_c = _mix_out((att_c, gbc, cuc), conv_p, w_out, ctx, mod(0, 2), g_ffn, mod(0, 3), mod(0, 4),
                            ctx_row, rw_hi, rw_lo, tile_c)
    moe_l, moe_c = _expert_choice(hl, at_l, hc, at_c, 0, w_gate, w_up, w_down)

    g_mix = norm_mix_g[1].reshape(1, d)
    g_ffn = norm_ffn_g[1].reshape(1, d)
    w_in = bf(od_w_in[0])
    w_out = bf(od_w_out[0])
    xl, ggl, ul = _odd_in(xl, moe_l, mod(0, 5), g_mix, mod(1, 0), mod(1, 1), None, w_in, tile_l)
    _, _, u_ctx = _odd_in(xc, moe_c, mod(0, 5), g_mix, mod(1, 0), mod(1, 1), ctx_row, w_in, tile_c)
    cw, cb = od_conv_w[0], od_conv_b[0].reshape(1, -1)
    wax = _gate_weights(od_wa[0], od_wx[0], od_ba[0], od_bx[0])
    lam = od_lambda[0].reshape(2, 1, -1)
    zero_state = jnp.zeros((b, 1, LRU_WIDTH), F32)
    hf_c, uc_c, h0_f = _lru_fwd(u_ctx, cw, cb, wax, lam, zero_state, tile_c)
    _, h0_b = _lru_bwd(uc_c, wax, lam, zero_state, hf_c, hf_c, tile_c)
    hf_l, uc_l, _ = _lru_fwd(ul, cw, cb, wax, lam, h0_f, tile_w)
    yl, _ = _lru_bwd(uc_l, wax, lam, h0_b, hf_l, ggl, tile_w)
    rw_hi, rw_lo = router_split(1)
    xl, hl, at_l = _mix_out(yl, None, w_out, xl, mod(1, 2), g_ffn, mod(1, 3), mod(1, 4), None,
                            rw_hi, rw_lo, tile_w)
    moe_l, _ = _expert_choice(hl, at_l, None, None, 1, w_gate, w_up, w_down)
    return _final(xl, moe_l, mod(1, 5), final_g.reshape(1, d), tile_w)
```
